```python
import math
import jax
import jax.numpy as jnp
from jax import lax
import numpy as np

D_MODEL = 1024
BATCH = 16
SEQ = 256
DEPTH = 4
DEC_BATCH = 2
DEC_SEQ = 1024
PAST_LEN = 512

GRID_W = 64
HEAD_DIM = 64
MIX_W = D_MODEL
H_A = 6
A_W = H_A * HEAD_DIM
H_B = 6
KV_B = 2
B_W = H_B * HEAD_DIM
H_C = 4
DIFF_HD = 32
C_W = H_C * 2 * DIFF_HD
CONV_W = 3
CHUNK = 64
Q_BLOCK = 128
ROPE_THETA = 10000.0
D_FF = 2816
N_EXP = 8
TOP_K = 2
D_FF_E = 1408
N_DENSE = (DEPTH + 1) // 2
N_MOE = DEPTH // 2
EPS = 1e-6
SPLIT_SIZES = (3 * A_W, A_W, 2 * H_A, 2 * H_A, B_W, KV_B * HEAD_DIM, KV_B * HEAD_DIM, C_W, C_W, C_W)
IN_COLS = 3 * A_W + A_W + 4 * H_A + B_W + 2 * KV_B * HEAD_DIM + 3 * C_W

kernel_name = 'hymba_deltanet_gqa_diff_dit_step'


def rmsnorm(x, g):
    xf = x.astype(jnp.float32)
    y = xf * lax.rsqrt(jnp.mean(xf * xf, axis=-1, keepdims=True) + EPS)
    return (y * g.astype(jnp.float32)).astype(x.dtype)


def l2norm(x):
    xf = x.astype(jnp.float32)
    return xf * lax.rsqrt(jnp.sum(xf * xf, axis=-1, keepdims=True) + EPS)


def modulate(x, g, shift, scale):
    return rmsnorm(x, g) * (1 + scale) + shift


def axial_rope_tables(rows, rot_dim):
    n_freq = rot_dim // 4
    inv = ROPE_THETA ** (-jnp.arange(n_freq, dtype=jnp.float32) / n_freq)
    row = jnp.repeat(jnp.arange(rows, dtype=jnp.float32), GRID_W)
    col = jnp.tile(jnp.arange(GRID_W, dtype=jnp.float32), rows)
    ang = jnp.stack([row[:, None] * inv, col[:, None] * inv], axis=1)
    return jnp.cos(ang), jnp.sin(ang)


def apply_axial_rope(x, cos, sin):
    shp = x.shape
    nf = shp[-1] // 4
    xr = x.astype(jnp.float32).reshape(*shp[:-1], 2, 2, nf)
    bshape = (shp[1],) + (1,) * (x.ndim - 3) + (2, nf)
    c, s = cos.reshape(bshape), sin.reshape(bshape)
    x1, x2 = xr[..., 0, :], xr[..., 1, :]
    out = jnp.stack([x1 * c - x2 * s, x2 * c + x1 * s], axis=-2)
    return out.reshape(shp).astype(x.dtype)


def short_conv(x, w):
    return lax.conv_general_dilated(
        x, w[:, None, :].astype(x.dtype), window_strides=(1,),
        padding=[(CONV_W // 2, CONV_W // 2)], dimension_numbers=('NWC', 'WIO', 'NWC'),
        feature_group_count=x.shape[-1])


def chunk_gated_delta(q, k, v, g, beta, s0):
    B, T, H, DK = q.shape
    DV = v.shape[-1]
    n_chunk = T // CHUNK

    def chunks(a):
        a = a.astype(jnp.float32).reshape(B, n_chunk, CHUNK, H, *a.shape[3:])
        return jnp.moveaxis(jnp.swapaxes(a, 2, 3), 1, 0)

    qc = chunks(q) * (DK ** -0.5)
    kc, vc, gc, bc = chunks(k), chunks(v), chunks(g), chunks(beta)
    gcum = jnp.cumsum(gc, axis=-1)
    incl = jnp.tril(jnp.ones((CHUNK, CHUNK), dtype=bool))
    strict = jnp.tril(jnp.ones((CHUNK, CHUNK), dtype=bool), -1)
    diff = gcum[..., :, None] - gcum[..., None, :]
    decay = jnp.where(incl, jnp.exp(jnp.where(incl, diff, 0.0)), 0.0)
    kb = kc * bc[..., None]
    lmat = jnp.where(strict, jnp.einsum('nbhcd,nbhed->nbhce', kb, kc) * decay, 0.0)
    eye = jnp.eye(CHUNK, dtype=jnp.float32)
    rhs = jnp.concatenate([vc * bc[..., None], kb * jnp.exp(gcum)[..., None]], axis=-1)
    sol = lax.linalg.triangular_solve(eye + lmat, rhs, left_side=True, lower=True, unit_diagonal=True)
    u, w = sol[..., :DV], sol[..., DV:]
    a_intra = jnp.where(incl, jnp.einsum('nbhcd,nbhed->nbhce', qc, kc) * decay, 0.0)
    q_dec = qc * jnp.exp(gcum)[..., None]
    k_dec = kc * jnp.exp(gcum[..., -1:] - gcum)[..., None]
    g_last = jnp.exp(gcum[..., -1])

    def step(S, xs):
        q_i, k_i, u_i, w_i, a_i, gl = xs
        v_new = u_i - jnp.einsum('bhcd,bhde->bhce', w_i, S)
        o = jnp.einsum('bhcd,bhde->bhce', q_i, S) + jnp.einsum('bhce,bhef->bhcf', a_i, v_new)
        S = S * gl[..., None, None] + jnp.einsum('bhcd,bhce->bhde', k_i, v_new)
        return S, o

    S, out = lax.scan(step, s0.astype(jnp.float32), (q_dec, k_dec, u, w, a_intra, g_last))
    out = jnp.swapaxes(jnp.moveaxis(out, 0, 1), 2, 3).reshape(B, T, H, DV)
    return out, S


def deltanet_mixer(qkv, z, beta_raw, a_raw, a_log, dt_bias, norm_g, s0):
    B, T, _ = qkv.shape
    qkv = qkv.reshape(B, T, 3, H_A, HEAD_DIM)
    q, k, v = l2norm(qkv[:, :, 0]), l2norm(qkv[:, :, 1]), qkv[:, :, 2]
    beta = jax.nn.sigmoid(beta_raw.astype(jnp.float32)).reshape(B, T, 2, H_A)
    g = -jnp.exp(a_log.astype(jnp.float32)) * jax.nn.softplus(
        a_raw.astype(jnp.float32).reshape(B, T, 2, H_A) + dt_bias.astype(jnp.float32))
    o_f, s_f = chunk_gated_delta(q, k, v, g[:, :, 0], beta[:, :, 0], s0[:, 0])
    o_b, s_b = chunk_gated_delta(q[:, ::-1], k[:, ::-1], v[:, ::-1], g[:, ::-1, 1], beta[:, ::-1, 1], s0[:, 1])
    o = o_f + o_b[:, ::-1]
    o = rmsnorm(o, norm_g) * jax.nn.silu(z.astype(jnp.float32).reshape(B, T, H_A, HEAD_DIM))
    return o.reshape(B, T, A_W).astype(z.dtype), jnp.stack([s_f, s_b], axis=1)


def sweep_query_blocks(fn, q):
    B, T = q.shape[:2]
    nb = T // Q_BLOCK
    qb = jnp.moveaxis(q.reshape(B, nb, Q_BLOCK, *q.shape[2:]), 1, 0)
    out = lax.map(fn, qb)
    return jnp.moveaxis(out, 0, 1).reshape(B, T, *out.shape[3:])


def gqa_attend(q, k, v):
    B, T = q.shape[:2]
    qg = q.reshape(B, T, KV_B, H_B // KV_B, HEAD_DIM)

    def blk(qi):
        s = jnp.einsum('bqkgd,bskd->bkgqs', qi, k).astype(jnp.float32) * (HEAD_DIM ** -0.5)
        p = jax.nn.softmax(s, axis=-1).astype(v.dtype)
        return jnp.einsum('bkgqs,bskd->bqkgd', p, v)

    return sweep_query_blocks(blk, qg).reshape(B, T, B_W)


def diff_attend(q, k, v, lam):
    def blk(qi):
        s = jnp.einsum('bqhmd,bshmd->bhmqs', qi, k).astype(jnp.float32) * (DIFF_HD ** -0.5)
        p = jax.nn.softmax(s, axis=-1)
        a = (p[:, :, 0] - lam * p[:, :, 1]).astype(v.dtype)
        return jnp.einsum('bhqs,bshe->bqhe', a, v)

    return sweep_query_blocks(blk, q)


def token_mixer(h, p, l, ctx, rope):
    B, T, _ = h.shape
    idx = np.cumsum(SPLIT_SIZES)[:-1].tolist()
    qkv_a, z_a, beta_raw, a_raw, q_b, k_b, v_b, q_c, k_c, v_c = jnp.split(h @ p['w_in'][l], idx, axis=-1)
    qkv_a = jax.nn.silu(short_conv(qkv_a, p['conv_w'][l]))
    s0 = jnp.zeros((B, 2, H_A, HEAD_DIM, HEAD_DIM), jnp.float32) if ctx is None else ctx[0]
    o_a, s_fin = deltanet_mixer(qkv_a, z_a, beta_raw, a_raw, p['dn_a_log'][l], p['dn_dt_bias'][l],
                                p['dn_norm'][l], s0)
    q_b = rmsnorm(q_b.reshape(B, T, H_B, HEAD_DIM), p['gqa_q_norm'][l])
    k_b = rmsnorm(k_b.reshape(B, T, KV_B, HEAD_DIM), p['gqa_k_norm'][l])
    v_b = v_b.reshape(B, T, KV_B, HEAD_DIM)
    q_c = q_c.reshape(B, T, H_C, 2, DIFF_HD)
    k_c = k_c.reshape(B, T, H_C, 2, DIFF_HD)
    v_c = v_c.reshape(B, T, H_C, 2 * DIFF_HD)
    if ctx is None:
        new_ctx = (s_fin, k_b, v_b, k_c, v_c)
        kb_all, vb_all, kc_all, vc_all = k_b, v_b, k_c, v_c
    else:
        cos_b, sin_b, cos_c, sin_c = rope
        q_b, k_b = apply_axial_rope(q_b, cos_b, sin_b), apply_axial_rope(k_b, cos_b, sin_b)
        q_c, k_c = apply_axial_rope(q_c, cos_c, sin_c), apply_axial_rope(k_c, cos_c, sin_c)
        kb_all = jnp.concatenate([ctx[1].astype(k_b.dtype), k_b], axis=1)
        vb_all = jnp.concatenate([ctx[2].astype(v_b.dtype), v_b], axis=1)
        kc_all = jnp.concatenate([ctx[3].astype(k_c.dtype), k_c], axis=1)
        vc_all = jnp.concatenate([ctx[4].astype(v_c.dtype), v_c], axis=1)
        new_ctx = None
    o_b = gqa_attend(q_b, kb_all, vb_all)
    lam_init = 0.8 - 0.6 * math.exp(-0.3 * l)
    lv = p['diff_lambda'][l].astype(jnp.float32)
    lam = jnp.exp(jnp.sum(lv[0] * lv[1])) - jnp.exp(jnp.sum(lv[2] * lv[3])) + lam_init
    o_c = rmsnorm(diff_attend(q_c, kc_all, vc_all, lam), p['diff_norm'][l]) * (1.0 - lam_init)
    mixed = jnp.concatenate([o_a, o_b, o_c.reshape(B, T, C_W)], axis=-1)
    return mixed @ p['w_out'][l], new_ctx


def swiglu(h, w_gu, w_down):
    g, u = jnp.split(h @ w_gu, 2, axis=-1)
    return (jax.nn.silu(g) * u) @ w_down


def moe_swiglu(h, router, w_gu, w_down):
    B, T, D = h.shape
    hf = h.reshape(B * T, D)
    logits = (hf @ router).astype(jnp.float32)
    top_v, top_i = lax.top_k(logits, TOP_K)
    wts = jax.nn.softmax(top_v, axis=-1)
    gate = jnp.sum(jax.nn.one_hot(top_i, N_EXP, dtype=jnp.float32) * wts[..., None], axis=-2)
    g, u = jnp.split(jnp.einsum('nd,edf->nef', hf, w_gu), 2, axis=-1)
    act = jax.nn.silu(g) * u * gate[..., None].astype(h.dtype)
    return jnp.einsum('nef,efd->nd', act, w_down).reshape(B, T, D)


def trunk_layer(x, cond, l, p, ctx, rope):
    mod = jax.nn.silu(cond) @ p['ada_w'][l] + p['ada_b'][l]
    sh_m, sc_m, g_m, sh_f, sc_f, g_f = jnp.split(mod[:, None, :], 6, axis=-1)
    mix, new_ctx = token_mixer(modulate(x, p['norm_mix'][l], sh_m, sc_m), p, l, ctx, rope)
    x = x + g_m * mix
    h = modulate(x, p['norm_ffn'][l], sh_f, sc_f)
    if l % 2 == 0:
        f = swiglu(h, p['ffn_w_gu'][l // 2], p['ffn_w_down'][l // 2])
    else:
        f = moe_swiglu(h, p['moe_router'][l // 2], p['moe_w_gu'][l // 2], p['moe_w_down'][l // 2])
    return x + g_f * f, new_ctx


def setup_inputs(seed: int = 0) -> dict:
    key = jax.random.key(seed)
    ks = jax.random.split(key, 32)
    f32 = jnp.float32

    def nrm(k, shape, s=1.0):
        return jax.random.normal(k, shape, f32) * s

    def gain(k, shape):
        return 1.0 + 0.05 * jax.random.normal(k, shape, f32)

    dt = jnp.exp(jax.random.uniform(ks[14], (DEPTH, 2, H_A), f32, minval=math.log(1e-3), maxval=math.log(1e-1)))
    return {
        'x_prompt': nrm(ks[0], (BATCH, SEQ, D_MODEL)),
        'x_sample': nrm(ks[1], (DEC_BATCH, DEC_SEQ, D_MODEL)),
        'state_delta': nrm(ks[2], (DEC_BATCH, DEPTH, 2, H_A, HEAD_DIM, HEAD_DIM), HEAD_DIM ** -0.5),
        'cache_gqa_k': nrm(ks[3], (DEC_BATCH, DEPTH, PAST_LEN, KV_B, HEAD_DIM)),
        'cache_gqa_v': nrm(ks[4], (DEC_BATCH, DEPTH, PAST_LEN, KV_B, HEAD_DIM)),
        'cache_diff_k': nrm(ks[5], (DEC_BATCH, DEPTH, PAST_LEN, H_C, 2, DIFF_HD)),
        'cache_diff_v': nrm(ks[6], (DEC_BATCH, DEPTH, PAST_LEN, H_C, 2 * DIFF_HD)),
        'c': nrm(ks[7], (DEC_BATCH, D_MODEL)),
        'c_ctx': nrm(ks[8], (D_MODEL,)),
        'ada_w': nrm(ks[9], (DEPTH, D_MODEL, 6 * D_MODEL), 0.5 * D_MODEL ** -0.5),
        'ada_b': nrm(ks[10], (DEPTH, 6 * D_MODEL), 0.02),
        'norm_mix': gain(ks[11], (DEPTH, D_MODEL)),
        'norm_ffn': gain(ks[12], (DEPTH, D_MODEL)),
        'w_in': nrm(ks[13], (DEPTH, D_MODEL, IN_COLS), D_MODEL ** -0.5),
        'conv_w': nrm(ks[15], (DEPTH, CONV_W, 3 * A_W), CONV_W ** -0.5),
        'dn_a_log': jnp.log(jax.random.uniform(ks[16], (DEPTH, 2, H_A), f32, minval=1.0, maxval=16.0)),
        'dn_dt_bias': jnp.log(jnp.expm1(dt)),
        'dn_norm': gain(ks[17], (DEPTH, HEAD_DIM)),
        'gqa_q_norm': gain(ks[18], (DEPTH, HEAD_DIM)),
        'gqa_k_norm': gain(ks[19], (DEPTH, HEAD_DIM)),
        'diff_lambda': nrm(ks[20], (DEPTH, 4, DIFF_HD), 0.1),
        'diff_norm': gain(ks[21], (DEPTH, 2 * DIFF_HD)),
        'w_out': nrm(ks[22], (DEPTH, MIX_W, D_MODEL), MIX_W ** -0.5),
        'ffn_w_gu': nrm(ks[23], (N_DENSE, D_MODEL, 2 * D_FF), D_MODEL ** -0.5),
        'ffn_w_down': nrm(ks[24], (N_DENSE, D_FF, D_MODEL), D_FF ** -0.5),
        'moe_router': nrm(ks[25], (N_MOE, D_MODEL, N_EXP), D_MODEL ** -0.5),
        'moe_w_gu': nrm(ks[26], (N_MOE, N_EXP, D_MODEL, 2 * D_FF_E), D_MODEL ** -0.5),
        'moe_w_down': nrm(ks[27], (N_MOE, N_EXP, D_FF_E, D_MODEL), D_FF_E ** -0.5),
        'final_norm': gain(ks[28], (D_MODEL,)),
    }


def reference(x_prompt, x_sample, state_delta, cache_gqa_k, cache_gqa_v, cache_diff_k, cache_diff_v, c, c_ctx,
              ada_w, ada_b, norm_mix, norm_ffn, w_in, conv_w, dn_a_log, dn_dt_bias, dn_norm, gqa_q_norm,
              gqa_k_norm, diff_lambda, diff_norm, w_out, ffn_w_gu, ffn_w_down, moe_router, moe_w_gu, moe_w_down,
              final_norm):
    p = {
        'ada_w': ada_w, 'ada_b': ada_b, 'norm_mix': norm_mix, 'norm_ffn': norm_ffn, 'w_in': w_in,
        'conv_w': conv_w, 'dn_a_log': dn_a_log, 'dn_dt_bias': dn_dt_bias, 'dn_norm': dn_norm,
        'gqa_q_norm': gqa_q_norm, 'gqa_k_norm': gqa_k_norm, 'diff_lambda': diff_lambda, 'diff_norm': diff_norm,
        'w_out': w_out, 'ffn_w_gu': ffn_w_gu, 'ffn_w_down': ffn_w_down, 'moe_router': moe_router,
        'moe_w_gu': moe_w_gu, 'moe_w_down': moe_w_down,
    }
    x = x_prompt
    cond_ctx = c_ctx[None, :]
    ctx_out = []
    for l in range(DEPTH):
        x, ctx_l = trunk_layer(x, cond_ctx, l, p, None, None)
        ctx_out.append(ctx_l)
    y_prompt = rmsnorm(x, final_norm)
    new_state_delta = jnp.stack([t[0] for t in ctx_out], axis=1).astype(x_prompt.dtype)
    new_gqa_k = jnp.stack([t[1] for t in ctx_out], axis=1)
    new_gqa_v = jnp.stack([t[2] for t in ctx_out], axis=1)
    new_diff_k = jnp.stack([t[3] for t in ctx_out], axis=1)
    new_diff_v = jnp.stack([t[4] for t in ctx_out], axis=1)
    rows = x_sample.shape[1] // GRID_W
    cos_b, sin_b = axial_rope_tables(rows, HEAD_DIM)
    cos_c, sin_c = axial_rope_tables(rows, DIFF_HD)
    rope = (cos_b, sin_b, cos_c, sin_c)
    xs = x_sample
    for l in range(DEPTH):
        cache_l = (state_delta[:, l], cache_gqa_k[:, l], cache_gqa_v[:, l], cache_diff_k[:, l], cache_diff_v[:, l])
        xs, _ = trunk_layer(xs, c, l, p, cache_l, rope)
    y_sample = rmsnorm(xs, final_norm)
    return (y_prompt, y_sample, new_state_delta, new_gqa_k, new_gqa_v, new_diff_k, new_diff_v)
```

```python
import functools
import math

import numpy as np
import jax
import jax.numpy as jnp
from jax import lax
from jax.experimental import pallas as pl
from jax.experimental.pallas import tpu as pltpu

D_MODEL = 1024
BATCH = 16
SEQ = 256
DEPTH = 4
DEC_BATCH = 2
DEC_SEQ = 1024
PAST_LEN = 512
GRID_W = 64
HEAD_DIM = 64
H_A = 6
A_W = H_A * HEAD_DIM
H_B = 6
KV_B = 2
B_W = H_B * HEAD_DIM
H_C = 4
DIFF_HD = 32
C_W = H_C * 2 * DIFF_HD
CONV_W = 3
CHUNK = 64
ROPE_THETA = 10000.0
D_FF = 2816
N_EXP = 8
D_FF_E = 1408
EPS = 1e-6

N_CTX = BATCH * SEQ
N_LAT = DEC_BATCH * DEC_SEQ
N_TOK = N_CTX + N_LAT
N_COND = 1 + DEC_BATCH
IN_PAD = 3072
FF_CHUNK = D_FF_E
LANES = 128
VMEM_LIMIT = 56 * 1024 * 1024

OFF_QKV, OFF_Z, OFF_QB, OFF_KB, OFF_VB, OFF_QC, OFF_KC, OFF_VC, OFF_GATE = (
    0, 1152, 1536, 1920, 2048, 2176, 2432, 2688, 2944)

F32 = jnp.float32
BF16 = jnp.bfloat16
HI = lax.Precision.HIGHEST


def _cparams(sem):
    return pltpu.CompilerParams(dimension_semantics=sem, vmem_limit_bytes=VMEM_LIMIT)


def _dot(a, b):
    return jnp.dot(a, b, preferred_element_type=F32)


def _dot_hi(a, b):
    return jnp.dot(a, b, preferred_element_type=F32, precision=HI)


def _dot_nt(a, b):
    return lax.dot_general(a, b, (((1,), (1,)), ((), ())), preferred_element_type=F32)


def _dot_nt_hi(a, b):
    return lax.dot_general(a, b, (((1,), (1,)), ((), ())), preferred_element_type=F32, precision=HI)


def _silu(x):
    return x * jax.nn.sigmoid(x)


def _softplus(x):
    return jnp.maximum(x, 0.0) + jnp.log1p(jnp.exp(-jnp.abs(x)))


def _rms(x, g):
    return x * lax.rsqrt(jnp.mean(x * x, axis=-1, keepdims=True) + EPS) * g


def _cond_row(i, tm):
    return jnp.maximum((i * tm) // DEC_SEQ - (N_CTX // DEC_SEQ - 1), 0)


def _ada_kernel(c_ref, w_ref, b_ref, o_ref):
    o_ref[...] = _dot_hi(_silu(c_ref[...]), w_ref[...]) + b_ref[...]


def ada_modulation(cond8, ada_w, ada_b):
    tn = 1024
    return pl.pallas_call(
        _ada_kernel,
        grid=(DEPTH, 6 * D_MODEL // tn),
        in_specs=[pl.BlockSpec((8, D_MODEL), lambda l, j: (0, 0)),
                  pl.BlockSpec((None, D_MODEL, tn), lambda l, j: (l, 0, j)),
                  pl.BlockSpec((None, 1, tn), lambda l, j: (l, 0, j))],
        out_specs=pl.BlockSpec((None, 8, tn), lambda l, j: (l, 0, j)),
        out_shape=jax.ShapeDtypeStruct((DEPTH, 8, 6 * D_MODEL), F32),
        compiler_params=_cparams(("arbitrary", "arbitrary")),
        name="ada_modulation",
    )(cond8, ada_w, ada_b.reshape(DEPTH, 1, 6 * D_MODEL))


def _inproj_kernel(x_ref, g_ref, sh_ref, sc_ref, w_ref, o_ref):
    h = _rms(x_ref[...], g_ref[...]) * (1.0 + sc_ref[...]) + sh_ref[...]
    o_ref[...] = _dot(h.astype(BF16), w_ref[...])


def in_projection(x, mod, norm_g, w_in_p, l):
    tm = 512
    return pl.pallas_call(
        _inproj_kernel,
        grid=(N_TOK // tm,),
        in_specs=[pl.BlockSpec((tm, D_MODEL), lambda i: (i, 0)),
                  pl.BlockSpec((None, 1, D_MODEL), lambda i: (l, 0, 0)),
                  pl.BlockSpec((None, None, 1, D_MODEL), lambda i: (l, _cond_row(i, tm), 0, 0)),
                  pl.BlockSpec((None, None, 1, D_MODEL), lambda i: (l, _cond_row(i, tm), 0, 1)),
                  pl.BlockSpec((None, D_MODEL, IN_PAD), lambda i: (l, 0, 0))],
        out_specs=pl.BlockSpec((tm, IN_PAD), lambda i: (i, 0)),
        out_shape=jax.ShapeDtypeStruct((N_TOK, IN_PAD), F32),
        compiler_params=_cparams(("arbitrary",)),
        name="in_projection",
    )(x, norm_g, mod, mod, w_in_p)


def _delta_kernel(qkv_ref, z_ref, gate_ref, arow_ref, cw_ref, alog_ref, dtb_ref, ng_ref, s0_ref,
                  o_ref, sfin_ref, q_s, k_s, v_s, o_s, st_s, pad_s, *, seq_len, heads):
    n_chunk = seq_len // CHUNK
    row = lax.broadcasted_iota(jnp.int32, (CHUNK, CHUNK), 0)
    col = lax.broadcasted_iota(jnp.int32, (CHUNK, CHUNK), 1)
    incl = (row >= col, row <= col)
    strict = (row > col, row < col)
    incl_f = tuple(m.astype(F32) for m in incl)
    incl_ft = (incl_f[1], incl_f[0])

    pad_s[0:8, :] = jnp.zeros((8, HEAD_DIM), F32)
    pad_s[8 + seq_len:16 + seq_len, :] = jnp.zeros((8, HEAD_DIM), F32)

    def conv_silu(hh, part):
        pad_s[8:8 + seq_len, :] = qkv_ref[hh, part]
        w = cw_ref[hh, part]
        y = (pad_s[7:7 + seq_len, :] * w[0:1, :] + pad_s[8:8 + seq_len, :] * w[1:2, :]
             + pad_s[9:9 + seq_len, :] * w[2:3, :])
        return _silu(y)

    def l2n(x):
        return x * lax.rsqrt(jnp.sum(x * x, axis=-1, keepdims=True) + EPS)

    for hh in range(heads):
        q_s[hh] = l2n(conv_silu(hh, 0)) * (HEAD_DIM ** -0.5)
        k_s[hh] = l2n(conv_silu(hh, 1))
        v_s[hh] = conv_silu(hh, 2)
        o_s[hh] = jnp.zeros((seq_len, HEAD_DIM), F32)
        st_s[0, hh] = s0_ref[0, hh]
        st_s[1, hh] = s0_ref[1, hh]

    def chain(c, d, hh):
        rows = pl.ds(pl.multiple_of(c * CHUNK, CHUNK), CHUNK)
        qc, kc, vc = q_s[hh, rows, :], k_s[hh, rows, :], v_s[hh, rows, :]
        gate = gate_ref[hh, rows, :]
        beta = jax.nn.sigmoid(gate[:, d:d + 1])
        neg_a = -jnp.exp(alog_ref[hh, d])
        dtb = dtb_ref[hh, d]
        g_col = neg_a * _softplus(gate[:, 2 + d:3 + d] + dtb)
        g_row = neg_a * _softplus(arow_ref[hh, c, d:d + 1, :] + dtb)
        gc_col = _dot_hi(incl_f[d], g_col)
        gc_row = _dot_hi(g_row, incl_ft[d])
        decay = jnp.where(incl[d], jnp.exp(jnp.where(incl[d], gc_col - gc_row, 0.0)), 0.0)
        kb = kc * beta
        eg = jnp.exp(gc_col)
        lmat = jnp.where(strict[d], _dot_nt_hi(kb, kc) * decay, 0.0)
        x = jnp.concatenate([vc * beta, kb * eg], axis=1)
        p = -lmat
        n_fac = int(math.log2(CHUNK))
        for j in range(n_fac):
            if j + 1 < n_fac:
                px = _dot_hi(p, jnp.concatenate([x, p], axis=1))
                x = x + px[:, :2 * HEAD_DIM]
                p = px[:, 2 * HEAD_DIM:]
            else:
                x = x + _dot_hi(p, x)
        u, w = x[:, :HEAD_DIM], x[:, HEAD_DIM:]
        a_intra = jnp.where(incl[d], _dot_nt_hi(qc, kc) * decay, 0.0)
        g_tot = jnp.sum(g_col, axis=0, keepdims=True)
        q_dec = qc * eg
        k_dec = kc * jnp.exp(g_tot - gc_col)
        s_old = st_s[d, hh]
        wq = _dot_hi(jnp.concatenate([w, q_dec], axis=0), s_old)
        v_new = u - wq[:CHUNK]
        o = wq[CHUNK:] + _dot_hi(a_intra, v_new)
        st_s[d, hh] = s_old * jnp.exp(g_tot) + _dot_hi(k_dec.T, v_new)
        o_s[hh, rows, :] = o_s[hh, rows, :] + o

    def step(s, carry):
        for hh in range(heads):
            chain(s, 0, hh)
            chain(n_chunk - 1 - s, 1, hh)
        return carry

    lax.fori_loop(0, n_chunk, step, 0)

    for hh in range(heads):
        o_ref[hh] = _rms(o_s[hh], ng_ref[...]) * _silu(z_ref[hh])
        sfin_ref[0, hh] = st_s[0, hh]
        sfin_ref[1, hh] = st_s[1, hh]


def delta_mixer(qkv_h, z_h, gate_h, arow_h, cw_h, alog_h, dtb_h, norm_g, s0, *, n_seq, seq_len, tok0):
    heads = 2
    n_chunk = seq_len // CHUNK
    b0 = tok0 // seq_len
    kern = functools.partial(_delta_kernel, seq_len=seq_len, heads=heads)
    return pl.pallas_call(
        kern,
        grid=(n_seq, H_A // heads),
        in_specs=[pl.BlockSpec((heads, 3, seq_len, HEAD_DIM), lambda b, g: (g, 0, b0 + b, 0)),
                  pl.BlockSpec((heads, seq_len, HEAD_DIM), lambda b, g: (g, b0 + b, 0)),
                  pl.BlockSpec((heads, seq_len, 4), lambda b, g: (g, b0 + b, 0)),
                  pl.BlockSpec((heads, n_chunk, 2, CHUNK), lambda b, g: (g, b0 + b, 0, 0)),
                  pl.BlockSpec((heads, 3, CONV_W, HEAD_DIM), lambda b, g: (g, 0, 0, 0)),
                  pl.BlockSpec((heads, 2, 1, 1), lambda b, g: (g, 0, 0, 0)),
                  pl.BlockSpec((heads, 2, 1, 1), lambda b, g: (g, 0, 0, 0)),
                  pl.BlockSpec((1, HEAD_DIM), lambda b, g: (0, 0)),
                  pl.BlockSpec((None, 2, heads, HEAD_DIM, HEAD_DIM), lambda b, g: (b, 0, g, 0, 0))],
        out_specs=[pl.BlockSpec((heads, seq_len, HEAD_DIM), lambda b, g: (g, b, 0)),
                   pl.BlockSpec((None, 2, heads, HEAD_DIM, HEAD_DIM), lambda b, g: (b, 0, g, 0, 0))],
        out_shape=[jax.ShapeDtypeStruct((H_A, n_seq * seq_len, HEAD_DIM), F32),
                   jax.ShapeDtypeStruct((n_seq, 2, H_A, HEAD_DIM, HEAD_DIM), F32)],
        scratch_shapes=[pltpu.VMEM((heads, seq_len, HEAD_DIM), F32),
                        pltpu.VMEM((heads, seq_len, HEAD_DIM), F32),
                        pltpu.VMEM((heads, seq_len, HEAD_DIM), F32),
                        pltpu.VMEM((heads, seq_len, HEAD_DIM), F32),
                        pltpu.VMEM((2, heads, HEAD_DIM, HEAD_DIM), F32),
                        pltpu.VMEM((seq_len + 16, HEAD_DIM), F32)],
        compiler_params=_cparams(("arbitrary", "arbitrary")),
        name=f"delta_mixer_{seq_len}",
    )(qkv_h, z_h, gate_h, arow_h, cw_h, alog_h, dtb_h, norm_g, s0)


def _swap_matrix(dim):
    quarter = dim // 4
    i = lax.broadcasted_iota(jnp.int32, (dim, dim), 0)
    j = lax.broadcasted_iota(jnp.int32, (dim, dim), 1)
    partner = jnp.where((j % (2 * quarter)) < quarter, j + quarter, j - quarter)
    return (i == partner).astype(F32)


def _rope(x, cos, sin_signed, swap):
    return x * cos + _dot_hi(x, swap) * sin_signed


def _rope_tables(rows, dim):
    nf = dim // 4
    inv = ROPE_THETA ** (-jnp.arange(nf, dtype=F32) / nf)
    r = jnp.repeat(jnp.arange(rows, dtype=F32), GRID_W)
    c = jnp.tile(jnp.arange(GRID_W, dtype=F32), rows)
    ar, ac = r[:, None] * inv, c[:, None] * inv
    cos = jnp.concatenate([jnp.cos(ar), jnp.cos(ar), jnp.cos(ac), jnp.cos(ac)], axis=1)
    sin = jnp.concatenate([-jnp.sin(ar), jnp.sin(ar), -jnp.sin(ac), jnp.sin(ac)], axis=1)
    return cos, sin


def _gqa_kernel(*refs, cached):
    if cached:
        (q_ref, k_ref, v_ref, gq_ref, gk_ref, ck_ref, cv_ref, cosq_ref, sinq_ref, cosk_ref, sink_ref,
         o_ref) = refs
        swap = _swap_matrix(HEAD_DIM)
    else:
        q_ref, k_ref, v_ref, gq_ref, gk_ref, o_ref, kn_ref = refs
    group = H_B // KV_B
    for kv in range(KV_B):
        k = _rms(k_ref[kv], gk_ref[...])
        v = v_ref[kv]
        if cached:
            k = _rope(k, cosk_ref[...], sink_ref[...], swap)
            k = jnp.concatenate([ck_ref[kv], k], axis=0)
            v = jnp.concatenate([cv_ref[kv], v], axis=0)
        else:
            kn_ref[kv] = k
        k = k.astype(BF16)
        v = v.astype(BF16)
        for g in range(group):
            h = kv * group + g
            q = _rms(q_ref[h], gq_ref[...])
            if cached:
                q = _rope(q, cosq_ref[...], sinq_ref[...], swap)
            s = _dot_nt(q.astype(BF16), k) * (HEAD_DIM ** -0.5)
            p = jnp.exp(s - jnp.max(s, axis=-1, keepdims=True))
            denom = jnp.sum(p, axis=-1, keepdims=True)
            o_ref[h] = _dot((p / denom).astype(BF16), v)


def gqa_mixer(q_h, k_h, v_h, gq, gk, *, n_seq, seq_len, tok0, cache=None, rope=None):
    tq = 256
    nq = seq_len // tq
    b0 = tok0 // seq_len
    q0 = tok0 // tq
    cached = cache is not None
    in_specs = [pl.BlockSpec((H_B, tq, HEAD_DIM), lambda b, i: (0, q0 + b * nq + i, 0)),
                pl.BlockSpec((KV_B, seq_len, HEAD_DIM), lambda b, i: (0, b0 + b, 0)),
                pl.BlockSpec((KV_B, seq_len, HEAD_DIM), lambda b, i: (0, b0 + b, 0)),
                pl.BlockSpec((1, HEAD_DIM), lambda b, i: (0, 0)),
                pl.BlockSpec((1, HEAD_DIM), lambda b, i: (0, 0))]
    args = [q_h, k_h, v_h, gq, gk]
    out_specs = [pl.BlockSpec((H_B, tq, HEAD_DIM), lambda b, i: (0, b * nq + i, 0))]
    out_shape = [jax.ShapeDtypeStruct((H_B, n_seq * seq_len, HEAD_DIM), F32)]
    if cached:
        cos, sin = rope
        in_specs += [pl.BlockSpec((None, KV_B, PAST_LEN, HEAD_DIM), lambda b, i: (b, 0, 0, 0)),
                     pl.BlockSpec((None, KV_B, PAST_LEN, HEAD_DIM), lambda b, i: (b, 0, 0, 0)),
                     pl.BlockSpec((tq, HEAD_DIM), lambda b, i: (i, 0)),
                     pl.BlockSpec((tq, HEAD_DIM), lambda b, i: (i, 0)),
                     pl.BlockSpec((seq_len, HEAD_DIM), lambda b, i: (0, 0)),
                     pl.BlockSpec((seq_len, HEAD_DIM), lambda b, i: (0, 0))]
        args += [cache[0], cache[1], cos, sin, cos, sin]
    else:
        out_specs.append(pl.BlockSpec((KV_B, seq_len, HEAD_DIM), lambda b, i: (0, b, 0)))
        out_shape.append(jax.ShapeDtypeStruct((KV_B, n_seq * seq_len, HEAD_DIM), F32))
    return pl.pallas_call(
        functools.partial(_gqa_kernel, cached=cached),
        grid=(n_seq, nq),
        in_specs=in_specs, out_specs=out_specs, out_shape=out_shape,
        compiler_params=_cparams(("arbitrary", "arbitrary")),
        name=f"gqa_mixer_{seq_len}",
    )(*args)


def _diff_kernel(*refs, cached, lam_init):
    if cached:
        (q_ref, k_ref, v_ref, lam_ref, gn_ref, ck_ref, cv_ref, cosq_ref, sinq_ref, cosk_ref, sink_ref,
         o_ref) = refs
        swap = _swap_matrix(DIFF_HD)
    else:
        q_ref, k_ref, v_ref, lam_ref, gn_ref, o_ref = refs
    lv = lam_ref[...]
    lam = (jnp.exp(jnp.sum(lv[0:1] * lv[1:2], axis=-1, keepdims=True))
           - jnp.exp(jnp.sum(lv[2:3] * lv[3:4], axis=-1, keepdims=True)) + lam_init)
    for h in range(H_C):
        v = v_ref[h]
        if cached:
            v = jnp.concatenate([cv_ref[h], v], axis=0)
        v = v.astype(BF16)
        probs = []
        for m in range(2):
            q, k = q_ref[2 * h + m], k_ref[2 * h + m]
            if cached:
                q = _rope(q, cosq_ref[...], sinq_ref[...], swap)
                k = _rope(k, cosk_ref[...], sink_ref[...], swap)
                k = jnp.concatenate([ck_ref[2 * h + m], k], axis=0)
            s = _dot_nt(q.astype(BF16), k.astype(BF16)) * (DIFF_HD ** -0.5)
            p = jnp.exp(s - jnp.max(s, axis=-1, keepdims=True))
            probs.append(p / jnp.sum(p, axis=-1, keepdims=True))
        a = probs[0] - lam * probs[1]
        o = _dot(a.astype(BF16), v)
        o_ref[h] = _rms(o, gn_ref[...]) * (1.0 - lam_init)


def diff_mixer(q_h, k_h, v_h, lam_p, gn, lam_init, *, n_seq, seq_len, tok0, cache=None, rope=None):
    tq = 256
    nq = seq_len // tq
    b0 = tok0 // seq_len
    q0 = tok0 // tq
    cached = cache is not None
    in_specs = [pl.BlockSpec((2 * H_C, tq, DIFF_HD), lambda b, i: (0, q0 + b * nq + i, 0)),
                pl.BlockSpec((2 * H_C, seq_len, DIFF_HD), lambda b, i: (0, b0 + b, 0)),
                pl.BlockSpec((H_C, seq_len, 2 * DIFF_HD), lambda b, i: (0, b0 + b, 0)),
                pl.BlockSpec((4, DIFF_HD), lambda b, i: (0, 0)),
                pl.BlockSpec((1, 2 * DIFF_HD), lambda b, i: (0, 0))]
    args = [q_h, k_h, v_h, lam_p, gn]
    if cached:
        cos, sin = rope
        in_specs += [pl.BlockSpec((None, 2 * H_C, PAST_LEN, DIFF_HD), lambda b, i: (b, 0, 0, 0)),
                     pl.BlockSpec((None, H_C, PAST_LEN, 2 * DIFF_HD), lambda b, i: (b, 0, 0, 0)),
                     pl.BlockSpec((tq, DIFF_HD), lambda b, i: (i, 0)),
                     pl.BlockSpec((tq, DIFF_HD), lambda b, i: (i, 0)),
                     pl.BlockSpec((seq_len, DIFF_HD), lambda b, i: (0, 0)),
                     pl.BlockSpec((seq_len, DIFF_HD), lambda b, i: (0, 0))]
        args += [cache[0], cache[1], cos, sin, cos, sin]
    return pl.pallas_call(
        functools.partial(_diff_kernel, cached=cached, lam_init=lam_init),
        grid=(n_seq, nq),
        in_specs=in_specs,
        out_specs=pl.BlockSpec((H_C, tq, 2 * DIFF_HD), lambda b, i: (0, b * nq + i, 0)),
        out_shape=jax.ShapeDtypeStruct((H_C, n_seq * seq_len, 2 * DIFF_HD), F32),
        compiler_params=_cparams(("arbitrary", "arbitrary")),
        name=f"diff_mixer_{seq_len}",
    )(*args)


def _outproj_kernel(*refs, routed):
    if routed:
        x_ref, m_ref, w_ref, gm_ref, g_ref, sh_ref, sc_ref, r_ref, x1_ref, h_ref, lg_ref = refs
    else:
        x_ref, m_ref, w_ref, gm_ref, g_ref, sh_ref, sc_ref, x1_ref, h_ref = refs
    x1 = x_ref[...] + gm_ref[...] * _dot(m_ref[...], w_ref[...])
    x1_ref[...] = x1
    h = _rms(x1, g_ref[...]) * (1.0 + sc_ref[...]) + sh_ref[...]
    h_ref[...] = h.astype(BF16)
    if routed:
        lg_ref[...] = _dot_hi(h, r_ref[...])


def out_projection(x, mixed, w_out_b, mod, norm_g, l, router_p=None):
    tm = 512
    routed = router_p is not None
    mod_spec = lambda k: pl.BlockSpec((None, None, 1, D_MODEL), lambda i: (l, _cond_row(i, tm), 0, k))
    in_specs = [pl.BlockSpec((tm, D_MODEL), lambda i: (i, 0)),
                pl.BlockSpec((tm, D_MODEL), lambda i: (i, 0)),
                pl.BlockSpec((None, D_MODEL, D_MODEL), lambda i: (l, 0, 0)),
                mod_spec(2),
                pl.BlockSpec((None, 1, D_MODEL), lambda i: (l, 0, 0)),
                mod_spec(3), mod_spec(4)]
    args = [x, mixed, w_out_b, mod, norm_g, mod, mod]
    out_specs = [pl.BlockSpec((tm, D_MODEL), lambda i: (i, 0)),
                 pl.BlockSpec((tm, D_MODEL), lambda i: (i, 0))]
    out_shape = [jax.ShapeDtypeStruct((N_TOK, D_MODEL), F32),
                 jax.ShapeDtypeStruct((N_TOK, D_MODEL), BF16)]
    if routed:
        in_specs.append(pl.BlockSpec((D_MODEL, LANES), lambda i: (0, 0)))
        args.append(router_p)
        out_specs.append(pl.BlockSpec((tm, LANES), lambda i: (i, 0)))
        out_shape.append(jax.ShapeDtypeStruct((N_TOK, LANES), F32))
    return pl.pallas_call(
        functools.partial(_outproj_kernel, routed=routed),
        grid=(N_TOK // tm,),
        in_specs=in_specs, out_specs=out_specs, out_shape=out_shape,
        compiler_params=_cparams(("arbitrary",)),
        name="out_projection",
    )(*args)


def _top2_gate(logits):
    lane = lax.broadcasted_iota(jnp.int32, logits.shape, 1)
    neg = jnp.float32(-jnp.inf)
    lg = jnp.where(lane < N_EXP, logits, neg)
    t1 = jnp.max(lg, axis=-1, keepdims=True)
    i1 = jnp.min(jnp.where(lg == t1, lane, LANES), axis=-1, keepdims=True)
    lg2 = jnp.where(lane == i1, neg, lg)
    t2 = jnp.max(lg2, axis=-1, keepdims=True)
    i2 = jnp.min(jnp.where(lg2 == t2, lane, LANES), axis=-1, keepdims=True)
    e2 = jnp.exp(t2 - t1)
    w1 = 1.0 / (1.0 + e2)
    w2 = e2 / (1.0 + e2)
    return jnp.where(lane == i1, w1, 0.0) + jnp.where(lane == i2, w2, 0.0)


def _ffn_kernel(*refs, routed):
    if routed:
        h_ref, wg_ref, wu_ref, wd_ref, x_ref, gf_ref, lg_ref, o_ref, acc_ref = refs
    else:
        h_ref, wg_ref, wu_ref, wd_ref, x_ref, gf_ref, o_ref, acc_ref = refs
    j = pl.program_id(1)

    @pl.when(j == 0)
    def _():
        acc_ref[...] = jnp.zeros_like(acc_ref)

    h = h_ref[...]
    act = _silu(_dot(h, wg_ref[...])) * _dot(h, wu_ref[...])
    if routed:
        gate = _top2_gate(lg_ref[...])
        lane = lax.broadcasted_iota(jnp.int32, gate.shape, 1)
        act = act * jnp.sum(jnp.where(lane == j, gate, 0.0), axis=-1, keepdims=True)
    acc_ref[...] += _dot(act.astype(BF16), wd_ref[...])

    @pl.when(j == pl.num_programs(1) - 1)
    def _():
        o_ref[...] = x_ref[...] + gf_ref[...] * acc_ref[...]


def ffn(h, wg, wu, wd, x1, mod, l, logits=None):
    tm = 512
    n_chunk = wg.shape[0]
    routed = logits is not None
    in_specs = [pl.BlockSpec((tm, D_MODEL), lambda i, j: (i, 0)),
                pl.BlockSpec((None, D_MODEL, FF_CHUNK), lambda i, j: (j, 0, 0)),
                pl.BlockSpec((None, D_MODEL, FF_CHUNK), lambda i, j: (j, 0, 0)),
                pl.BlockSpec((None, FF_CHUNK, D_MODEL), lambda i, j: (j, 0, 0)),
                pl.BlockSpec((tm, D_MODEL), lambda i, j: (i, 0)),
                pl.BlockSpec((None, None, 1, D_MODEL), lambda i, j: (l, _cond_row(i, tm), 0, 5))]
    args = [h, wg, wu, wd, x1, mod]
    if routed:
        in_specs.append(pl.BlockSpec((tm, LANES), lambda i, j: (i, 0)))
        args.append(logits)
    return pl.pallas_call(
        functools.partial(_ffn_kernel, routed=routed),
        grid=(N_TOK // tm, n_chunk),
        in_specs=in_specs,
        out_specs=pl.BlockSpec((tm, D_MODEL), lambda i, j: (i, 0)),
        out_shape=jax.ShapeDtypeStruct((N_TOK, D_MODEL), F32),
        scratch_shapes=[pltpu.VMEM((tm, D_MODEL), F32)],
        compiler_params=_cparams(("arbitrary", "arbitrary")),
        name="ffn_routed" if routed else "ffn_dense",
    )(*args)


def _final_norm_kernel(x_ref, g_ref, o_ref):
    o_ref[...] = _rms(x_ref[...], g_ref[...])


def final_norm_call(x, g):
    tm = 1024
    return pl.pallas_call(
        _final_norm_kernel,
        grid=(N_TOK // tm,),
        in_specs=[pl.BlockSpec((tm, D_MODEL), lambda i: (i, 0)),
                  pl.BlockSpec((1, D_MODEL), lambda i: (0, 0))],
        out_specs=pl.BlockSpec((tm, D_MODEL), lambda i: (i, 0)),
        out_shape=jax.ShapeDtypeStruct((N_TOK, D_MODEL), F32),
        compiler_params=_cparams(("arbitrary",)),
        name="final_norm",
    )(x, g)


def _heads(a, n, d):
    return a.reshape(a.shape[0], n, d).transpose(1, 0, 2)


def _unheads(a):
    return a.transpose(1, 0, 2).reshape(a.shape[1], -1)


def kernel(x_prompt, x_sample, state_delta, cache_gqa_k, cache_gqa_v, cache_diff_k, cache_diff_v, c, c_ctx,
           ada_w, ada_b, norm_mix, norm_ffn, w_in, conv_w, dn_a_log, dn_dt_bias, dn_norm, gqa_q_norm,
           gqa_k_norm, diff_lambda, diff_norm, w_out, ffn_w_gu, ffn_w_down, moe_router, moe_w_gu, moe_w_down,
           final_norm):
    cond8 = jnp.concatenate([c_ctx[None, :], c, jnp.zeros((8 - N_COND, D_MODEL), F32)], axis=0)
    gate_cols = w_in[:, :, 1536:1560]
    w_in_p = jnp.concatenate(
        [w_in[:, :, :1536], w_in[:, :, 1560:], gate_cols,
         jnp.zeros((DEPTH, D_MODEL, IN_PAD - OFF_GATE - 24), F32)], axis=-1).astype(BF16)
    w_out_b = w_out.astype(BF16)
    norm_mix3 = norm_mix.reshape(DEPTH, 1, D_MODEL)
    norm_ffn3 = norm_ffn.reshape(DEPTH, 1, D_MODEL)
    cw_h = conv_w.reshape(DEPTH, CONV_W, 3, H_A, HEAD_DIM).transpose(0, 3, 2, 1, 4)
    alog_h = dn_a_log.transpose(0, 2, 1).reshape(DEPTH, H_A, 2, 1, 1)
    dtb_h = dn_dt_bias.transpose(0, 2, 1).reshape(DEPTH, H_A, 2, 1, 1)
    ffn_g = ffn_w_gu[:, :, :D_FF].reshape(-1, D_MODEL, D_FF // FF_CHUNK, FF_CHUNK).transpose(0, 2, 1, 3).astype(BF16)
    ffn_u = ffn_w_gu[:, :, D_FF:].reshape(-1, D_MODEL, D_FF // FF_CHUNK, FF_CHUNK).transpose(0, 2, 1, 3).astype(BF16)
    ffn_d = ffn_w_down.reshape(-1, D_FF // FF_CHUNK, FF_CHUNK, D_MODEL).astype(BF16)
    moe_g = moe_w_gu[:, :, :, :D_FF_E].astype(BF16)
    moe_u = moe_w_gu[:, :, :, D_FF_E:].astype(BF16)
    moe_d = moe_w_down.astype(BF16)
    router_p = jnp.pad(moe_router, ((0, 0), (0, 0), (0, LANES - N_EXP)))
    rope_b = _rope_tables(DEC_SEQ // GRID_W, HEAD_DIM)
    rope_c = _rope_tables(DEC_SEQ // GRID_W, DIFF_HD)
    s0_ctx = jnp.zeros((BATCH, 2, H_A, HEAD_DIM, HEAD_DIM), F32)

    mod = ada_modulation(cond8, ada_w, ada_b)[:, :N_COND].reshape(DEPTH, N_COND, 1, 6 * D_MODEL)

    x = jnp.concatenate([x_prompt.reshape(N_CTX, D_MODEL), x_sample.reshape(N_LAT, D_MODEL)], axis=0)
    states, gk_out, gv_out, dk_out, dv_out = [], [], [], [], []
    ctx = dict(n_seq=BATCH, seq_len=SEQ, tok0=0)
    lat = dict(n_seq=DEC_BATCH, seq_len=DEC_SEQ, tok0=N_CTX)
    for l in range(DEPTH):
        proj = in_projection(x, mod, norm_mix3, w_in_p, l)
        qkv_h = proj[:, OFF_QKV:OFF_Z].reshape(N_TOK, 3, H_A, HEAD_DIM).transpose(2, 1, 0, 3)
        z_h = _heads(proj[:, OFF_Z:OFF_QB], H_A, HEAD_DIM)
        qb_h = _heads(proj[:, OFF_QB:OFF_KB], H_B, HEAD_DIM)
        kb_h = _heads(proj[:, OFF_KB:OFF_VB], KV_B, HEAD_DIM)
        vb_h = _heads(proj[:, OFF_VB:OFF_QC], KV_B, HEAD_DIM)
        qc_h = _heads(proj[:, OFF_QC:OFF_KC], 2 * H_C, DIFF_HD)
        kc_h = _heads(proj[:, OFF_KC:OFF_VC], 2 * H_C, DIFF_HD)
        vc_h = _heads(proj[:, OFF_VC:OFF_GATE], H_C, 2 * DIFF_HD)
        gates = proj[:, OFF_GATE:OFF_GATE + 4 * H_A].reshape(N_TOK, 2, 2, H_A)
        gate_h = gates.transpose(3, 0, 1, 2).reshape(H_A, N_TOK, 4)
        arow_h = gates[:, 1].reshape(N_TOK // CHUNK, CHUNK, 2, H_A).transpose(3, 0, 2, 1)

        dn_args = (qkv_h, z_h, gate_h, arow_h, cw_h[l], alog_h[l], dtb_h[l], dn_norm[l][None, :])
        oa_ctx, s_fin = delta_mixer(*dn_args, s0_ctx, **ctx)
        oa_lat, _ = delta_mixer(*dn_args, state_delta[:, l], **lat)

        gq, gk = gqa_q_norm[l][None, :], gqa_k_norm[l][None, :]
        ob_ctx, kn_ctx = gqa_mixer(qb_h, kb_h, vb_h, gq, gk, **ctx)
        cache_b = (cache_gqa_k[:, l].transpose(0, 2, 1, 3), cache_gqa_v[:, l].transpose(0, 2, 1, 3))
        (ob_lat,) = gqa_mixer(qb_h, kb_h, vb_h, gq, gk, cache=cache_b, rope=rope_b, **lat)

        lam_init = 0.8 - 0.6 * math.exp(-0.3 * l)
        gn = diff_norm[l][None, :]
        oc_ctx = diff_mixer(qc_h, kc_h, vc_h, diff_lambda[l], gn, lam_init, **ctx)
        cache_c = (cache_diff_k[:, l].reshape(DEC_BATCH, PAST_LEN, 2 * H_C, DIFF_HD).transpose(0, 2, 1, 3),
                   cache_diff_v[:, l].transpose(0, 2, 1, 3))
        oc_lat = diff_mixer(qc_h, kc_h, vc_h, diff_lambda[l], gn, lam_init, cache=cache_c, rope=rope_c, **lat)

        mixed = jnp.concatenate([
            _unheads(jnp.concatenate([oa_ctx, oa_lat], axis=1)),
            _unheads(jnp.concatenate([ob_ctx, ob_lat], axis=1)),
            _unheads(jnp.concatenate([oc_ctx, oc_lat], axis=1))], axis=-1).astype(BF16)

        if l % 2 == 0:
            x1, h2 = out_projection(x, mixed, w_out_b, mod, norm_ffn3, l)
            x = ffn(h2, ffn_g[l // 2], ffn_u[l // 2], ffn_d[l // 2], x1, mod, l)
        else:
            x1, h2, logits = out_projection(x, mixed, w_out_b, mod, norm_ffn3, l, router_p[l // 2])
            x = ffn(h2, moe_g[l // 2], moe_u[l // 2], moe_d[l // 2], x1, mod, l, logits)

        states.append(s_fin)
        gk_out.append(_unheads(kn_ctx).reshape(BATCH, SEQ, KV_B, HEAD_DIM))
        gv_out.append(proj[:N_CTX, OFF_VB:OFF_QC].reshape(BATCH, SEQ, KV_B, HEAD_DIM))
        dk_out.append(proj[:N_CTX, OFF_KC:OFF_VC].reshape(BATCH, SEQ, H_C, 2, DIFF_HD))
        dv_out.append(proj[:N_CTX, OFF_VC:OFF_GATE].reshape(BATCH, SEQ, H_C, 2 * DIFF_HD))

    y = final_norm_call(x, final_norm[None, :])
    y_prompt = y[:N_CTX].reshape(BATCH, SEQ, D_MODEL)
    y_sample = y[N_CTX:].reshape(DEC_BATCH, DEC_SEQ, D_MODEL)
    return (y_prompt, y_sample, jnp.stack(states, axis=1), jnp.stack(gk_out, axis=1),
            jnp.stack(gv_out, axis=1), jnp.stack(dk_out, axis=1), jnp.stack(dv_out, axis=1))
```

```python
import functools
import math

import jax
import jax.numpy as jnp
from jax import lax
from jax.experimental import pallas as pl
from jax.experimental.pallas import tpu as pltpu

D_MODEL = 1024
BATCH = 16
SEQ = 256
DEPTH = 4
DEC_BATCH = 2
DEC_SEQ = 1024
PAST_LEN = 512
GRID_W = 64
HEAD_DIM = 64
H_A = 6
A_W = H_A * HEAD_DIM
H_B = 6
KV_B = 2
B_W = H_B * HEAD_DIM
H_C = 4
DIFF_HD = 32
C_W = H_C * 2 * DIFF_HD
CONV_W = 3
CHUNK = 64
ROPE_THETA = 10000.0
D_FF = 2816
N_EXP = 8
D_FF_E = 1408
EPS = 1e-6

N_CTX = BATCH * SEQ
N_LAT = DEC_BATCH * DEC_SEQ
N_TOK = N_CTX + N_LAT
N_COND = 1 + DEC_BATCH
IN_PAD = 3072
FF_CHUNK = D_FF_E
LANES = 128
VMEM_LIMIT = 56 * 1024 * 1024

IN_SPLIT = (("qkv_a", 3 * A_W), ("z_a", A_W), ("q_b", B_W), ("k_b", KV_B * HEAD_DIM), ("v_b", KV_B * HEAD_DIM),
            ("q_c", C_W), ("k_c", C_W), ("v_c", C_W), ("gate", LANES))
GATE_BETA, GATE_A = 0, 2 * H_A

F32 = jnp.float32
BF16 = jnp.bfloat16
HI = lax.Precision.HIGHEST


def _cparams(sem):
    return pltpu.CompilerParams(dimension_semantics=sem, vmem_limit_bytes=VMEM_LIMIT)


def _dot(a, b):
    return jnp.dot(a, b, preferred_element_type=F32)


def _dot_hi(a, b):
    return jnp.dot(a, b, preferred_element_type=F32, precision=HI)


def _dot_nt(a, b):
    return lax.dot_general(a, b, (((1,), (1,)), ((), ())), preferred_element_type=F32)


def _silu(x):
    return x * jax.nn.sigmoid(x)


def _softplus(x):
    return jnp.maximum(x, 0.0) + jnp.log1p(jnp.exp(-jnp.abs(x)))


def _rms(x, g):
    return x * lax.rsqrt(jnp.mean(x * x, axis=-1, keepdims=True) + EPS) * g


def _lane(shape):
    return lax.broadcasted_iota(jnp.int32, shape, len(shape) - 1)


def _group_mean_matrix(width, group):
    i = lax.broadcasted_iota(jnp.int32, (width, width), 0)
    j = lax.broadcasted_iota(jnp.int32, (width, width), 1)
    return jnp.where(i // group == j // group, 1.0 / group, 0.0).astype(F32)


def _group_rms(x, gmat, g):
    return x * lax.rsqrt(_dot_hi(x * x, gmat) + EPS) * g


def _cond_row(i, tm):
    return jnp.maximum((i * tm) // DEC_SEQ - (N_CTX // DEC_SEQ - 1), 0)


def _ada_kernel(c_ref, w_ref, b_ref, o_ref):
    o_ref[...] = _dot_hi(_silu(c_ref[...]), w_ref[...]) + b_ref[...]


def ada_modulation(cond8, ada_w, ada_b):
    tn = 1024
    return pl.pallas_call(
        _ada_kernel,
        grid=(DEPTH, 6 * D_MODEL // tn),
        in_specs=[pl.BlockSpec((8, D_MODEL), lambda l, j: (0, 0)),
                  pl.BlockSpec((None, D_MODEL, tn), lambda l, j: (l, 0, j)),
                  pl.BlockSpec((None, 1, tn), lambda l, j: (l, 0, j))],
        out_specs=pl.BlockSpec((None, 8, tn), lambda l, j: (l, 0, j)),
        out_shape=jax.ShapeDtypeStruct((DEPTH, 8, 6 * D_MODEL), F32),
        compiler_params=_cparams(("arbitrary", "arbitrary")),
        name="ada_modulation",
    )(cond8, ada_w, ada_b.reshape(DEPTH, 1, 6 * D_MODEL))


def _inproj_kernel(x_ref, g_ref, sh_ref, sc_ref, w_ref, *o_refs):
    h = _rms(x_ref[...], g_ref[...]) * (1.0 + sc_ref[...]) + sh_ref[...]
    acc = _dot(h.astype(BF16), w_ref[...])
    off = 0
    for o_ref, (_, width) in zip(o_refs, IN_SPLIT):
        o_ref[...] = acc[:, off:off + width]
        off += width


def in_projection(x, mod, norm_g, w_in_p, l):
    tm = 512
    return pl.pallas_call(
        _inproj_kernel,
        grid=(N_TOK // tm,),
        in_specs=[pl.BlockSpec((tm, D_MODEL), lambda i: (i, 0)),
                  pl.BlockSpec((None, 1, D_MODEL), lambda i: (l, 0, 0)),
                  pl.BlockSpec((None, None, 1, D_MODEL), lambda i: (l, _cond_row(i, tm), 0, 0)),
                  pl.BlockSpec((None, None, 1, D_MODEL), lambda i: (l, _cond_row(i, tm), 0, 1)),
                  pl.BlockSpec((None, D_MODEL, IN_PAD), lambda i: (l, 0, 0))],
        out_specs=[pl.BlockSpec((tm, w), lambda i: (i, 0)) for _, w in IN_SPLIT],
        out_shape=[jax.ShapeDtypeStruct((N_TOK, w), F32) for _, w in IN_SPLIT],
        compiler_params=_cparams(("arbitrary",)),
        name="in_projection",
    )(x, norm_g, mod, mod, w_in_p)


SOLVE_BLOCK = 16


def _solve_unit_triangular(lmat, rhs, lo, diag_blk, eye):
    def both(a):
        return jnp.concatenate([jnp.where(lo, a, 0.0), jnp.where(lo, 0.0, a)], axis=0)

    neg_d = jnp.where(diag_blk, -lmat, 0.0)
    pd = both(neg_d)
    po = both(-lmat - neg_d)
    t = eye + pd
    p = _dot(pd.astype(BF16), pd.astype(BF16))
    n_fac = int(math.log2(SOLVE_BLOCK))
    for j in range(1, n_fac):
        if j + 1 < n_fac:
            tp = _dot(p.astype(BF16), jnp.concatenate([t, p], axis=1).astype(BF16))
            t = t + tp[:, :LANES]
            p = tp[:, LANES:]
        else:
            t = t + _dot(p.astype(BF16), t.astype(BF16))
    cm = _dot(t.astype(BF16), jnp.concatenate([rhs, po], axis=1).astype(BF16))
    c, m = cm[:, :LANES], cm[:, LANES:].astype(BF16)
    x = c
    for _ in range(CHUNK // SOLVE_BLOCK - 1):
        x = c + _dot(m, x.astype(BF16))
    return x


def _delta_kernel(qkv_ref, z_ref, gate_ref, cw_ref, alog_ref, dtb_ref, ng_ref, s0_ref, o_ref, sfin_ref,
                  pad_s, kk_s, qq_s, vk_s, gc_s, beta_s, u_s, w_s, qd_s, a_s, kdt_s, egt_s, sa_s, o_s,
                  *, seq_len):
    n_chunk = seq_len // CHUNK
    n_pair = H_A // 2
    half = HEAD_DIM
    lo = _lane((CHUNK, LANES)) < half
    lo_row = _lane((1, LANES)) < half
    lo16 = _lane((CHUNK, LANES)).astype(F32).astype(BF16) < half
    row = lax.broadcasted_iota(jnp.int32, (CHUNK, LANES), 0)
    col = _lane((CHUNK, LANES)) % half
    ahead = jnp.where(lo, row - col, col - row)
    incl = ahead >= 0
    strict = ahead > 0
    diag_blk = row // SOLVE_BLOCK == col // SOLVE_BLOCK
    big_row = lax.broadcasted_iota(jnp.int32, (LANES, LANES), 0)
    anti = big_row // half + _lane((LANES, LANES)) // half == 1
    eye = (big_row == _lane((LANES, LANES))).astype(F32)

    rb = 128
    zero8 = jnp.zeros((8, 3 * A_W), F32)
    pad_s[0:8, :] = zero8
    pad_s[8 + seq_len:16 + seq_len, :] = zero8
    pad_s[8:8 + seq_len, :] = qkv_ref[...]
    lo_rb = _lane((rb, LANES)) < half
    for r0 in range(0, seq_len, rb):
        y = _silu(pad_s[7 + r0:7 + r0 + rb, :] * cw_ref[0:1, :] + pad_s[8 + r0:8 + r0 + rb, :] * cw_ref[1:2, :]
                  + pad_s[9 + r0:9 + r0 + rb, :] * cw_ref[2:3, :])

        def dup(part, p):
            c = y[:, part * A_W + p * LANES:part * A_W + (p + 1) * LANES]
            r = pltpu.roll(c, half, axis=1)
            return jnp.where(lo_rb, c, r), jnp.where(lo_rb, r, c)

        def l2n(x):
            return x * lax.rsqrt(0.5 * jnp.sum(x * x, axis=-1, keepdims=True) + EPS)

        for p in range(n_pair):
            qs, ks, vs = dup(0, p), dup(1, p), dup(2, p)
            for e in range(2):
                h = 2 * p + e
                kk = l2n(ks[e])
                kk_s[h, r0:r0 + rb, :] = kk
                qq_s[h, r0:r0 + rb, :] = l2n(qs[e]) * (HEAD_DIM ** -0.5)
                vk_s[h, r0:r0 + rb, :] = jnp.where(lo_rb, vs[e], kk)

    gt = gate_ref[...]
    beta_s[...] = jax.nn.sigmoid(gt)
    g = -jnp.exp(alog_ref[...]) * _softplus(gt + dtb_ref[...])
    g1 = g.astype(BF16)
    r1 = g - g1.astype(F32)
    g2 = r1.astype(BF16)
    g3 = (r1 - g2.astype(F32)).astype(BF16)
    g_split = jnp.concatenate([g1, g2, g3], axis=1)
    cb = 256
    ci = lax.broadcasted_iota(jnp.int32, (cb, cb), 0)
    cj = lax.broadcasted_iota(jnp.int32, (cb, cb), 1)
    same = ci // CHUNK == cj // CHUNK
    m_pre = (same & (ci >= cj)).astype(BF16)
    m_suf = (same & (ci <= cj)).astype(BF16)
    fwd_cols = (_lane((cb, LANES)) >= GATE_A) & (_lane((cb, LANES)) < GATE_A + H_A)
    for r0 in range(0, seq_len, cb):
        blk = g_split[r0:r0 + cb, :]
        pre, suf = _dot(m_pre, blk), _dot(m_suf, blk)
        pre = pre[:, :LANES] + pre[:, LANES:2 * LANES] + pre[:, 2 * LANES:]
        suf = suf[:, :LANES] + suf[:, LANES:2 * LANES] + suf[:, 2 * LANES:]
        gc_s[r0:r0 + cb, :] = jnp.where(fwd_cols, pre, suf)

    for h in range(H_A):
        zero = jnp.zeros((half, half), F32)
        sa_s[h] = jnp.concatenate([jnp.concatenate([zero, s0_ref[1, h]], axis=1),
                                   jnp.concatenate([s0_ref[0, h], zero], axis=1)], axis=0)

    def prep(c, carry):
        rows = pl.ds(pl.multiple_of(c * CHUNK, CHUNK), CHUNK)
        gcb = gc_s[rows, :]
        gct = gcb.T
        bb = beta_s[rows, :]
        for h in range(H_A):
            cf = gcb[:, GATE_A + h:GATE_A + h + 1]
            cbk = gcb[:, GATE_A + H_A + h:GATE_A + H_A + h + 1]
            gc_fb = jnp.where(lo, cf, cbk)
            gr_fb = jnp.concatenate([gct[GATE_A + h:GATE_A + h + 1, :],
                                     gct[GATE_A + H_A + h:GATE_A + H_A + h + 1, :]], axis=1)
            decay = jnp.where(incl, jnp.exp(jnp.where(incl, gc_fb - gr_fb, 0.0)), 0.0)
            bf = bb[:, GATE_BETA + h:GATE_BETA + h + 1]
            bbk = bb[:, GATE_BETA + H_A + h:GATE_BETA + H_A + h + 1]
            egf, egb = jnp.exp(cf), jnp.exp(cbk)
            tot_f = gcb[CHUNK - 1:CHUNK, GATE_A + h:GATE_A + h + 1]
            tot_b = gcb[0:1, GATE_A + H_A + h:GATE_A + H_A + h + 1]
            kk, qq, vk = kk_s[h, rows, :], qq_s[h, rows, :], vk_s[h, rows, :]
            k16 = kk[:, :half].astype(BF16)
            kq = jnp.concatenate([k16, qq[:, :half].astype(BF16)], axis=0)
            gram = _dot_nt(kq, jnp.concatenate([k16, k16], axis=0))
            lmat = jnp.where(strict, gram[:CHUNK] * jnp.where(lo, bf, bbk) * decay, 0.0)
            a_s[h, rows, :] = jnp.where(incl, gram[CHUNK:] * decay, 0.0).astype(BF16)
            x_f = vk * jnp.where(lo, bf, bf * egf)
            x_b = vk * jnp.where(lo, bbk, bbk * egb)
            x = _solve_unit_triangular(lmat, jnp.concatenate([x_f, x_b], axis=0), lo, diag_blk, eye)
            x_f, x_b = x[:CHUNK], pltpu.roll(x[CHUNK:], half, axis=1)
            u_s[h, rows, :] = jnp.where(lo, x_f, x_b)
            w_s[h, rows, :] = jnp.where(lo, x_b, x_f).astype(BF16)
            qd_s[h, rows, :] = (qq * jnp.where(lo, egb, egf)).astype(BF16)
            kd = kk * jnp.where(lo, jnp.exp(tot_b - cbk), jnp.exp(tot_f - cf))
            kdt_s[h, c] = kd.T.astype(BF16)
            egt_s[h, c] = jnp.exp(jnp.where(lo_row, tot_f, tot_b))
        return carry

    lax.fori_loop(0, n_chunk, prep, 0)

    def scan(s, carry):
        rf = pl.ds(pl.multiple_of(s * CHUNK, CHUNK), CHUNK)
        sb = n_chunk - 1 - s
        rbk = pl.ds(pl.multiple_of(sb * CHUNK, CHUNK), CHUNK)
        for h in range(H_A):
            u = jnp.where(lo, u_s[h, rf, :], u_s[h, rbk, :])
            w = jnp.where(lo16, w_s[h, rbk, :], w_s[h, rf, :])
            qd = jnp.where(lo16, qd_s[h, rbk, :], qd_s[h, rf, :])
            a = jnp.where(lo16, a_s[h, rf, :], a_s[h, rbk, :])
            kdt = jnp.concatenate([kdt_s[h, sb][:half], kdt_s[h, s][half:]], axis=0)
            egt = jnp.where(lo_row, egt_s[h, s], egt_s[h, sb])
            st = sa_s[h]
            wq = _dot(jnp.concatenate([w, qd], axis=0), st.astype(BF16))
            v = u - wq[:CHUNK]
            vbd = jnp.concatenate([jnp.where(lo, v, 0.0), jnp.where(lo, 0.0, v)], axis=0).astype(BF16)
            o = wq[CHUNK:] + _dot(a, vbd)
            sa_s[h] = st * egt + jnp.where(anti, _dot(kdt, v.astype(BF16)), 0.0)
            o_s[h, rf, 0:half] = o[:, :half]
            o_s[h, rbk, half:LANES] = o[:, half:]
        return carry

    lax.fori_loop(0, n_chunk, scan, 0)

    for r0 in range(0, seq_len, rb):
        for p in range(n_pair):
            nrm = []
            for e in range(2):
                ofb = o_s[2 * p + e, r0:r0 + rb, :]
                oo = ofb + pltpu.roll(ofb, half, axis=1)
                nrm.append(oo * lax.rsqrt(jnp.sum(oo * oo, axis=-1, keepdims=True) * (0.5 / HEAD_DIM) + EPS))
            o_ref[r0:r0 + rb, p * LANES:(p + 1) * LANES] = (
                jnp.where(lo_rb, nrm[0], nrm[1]) * ng_ref[...] * _silu(z_ref[r0:r0 + rb, p * LANES:(p + 1) * LANES]))

    for h in range(H_A):
        st = sa_s[h]
        sfin_ref[0, h] = st[half:, :half]
        sfin_ref[1, h] = st[:half, half:]


def delta_mixer(qkv, z, gate, cw, alog_row, dtb_row, ng2, s0, *, n_seq, seq_len, tok0):
    n_chunk = seq_len // CHUNK
    b0 = tok0 // seq_len
    tok = lambda w: pl.BlockSpec((seq_len, w), lambda b: (b0 + b, 0))
    full = lambda a: pl.BlockSpec(a.shape, lambda b: (0,) * a.ndim)
    state = pl.BlockSpec((None, 2, H_A, HEAD_DIM, HEAD_DIM), lambda b: (b, 0, 0, 0, 0))
    per_head = lambda dt: pltpu.VMEM((H_A, seq_len, LANES), dt)
    return pl.pallas_call(
        functools.partial(_delta_kernel, seq_len=seq_len),
        grid=(n_seq,),
        in_specs=[tok(3 * A_W), tok(A_W), tok(LANES), full(cw), full(alog_row), full(dtb_row), full(ng2), state],
        out_specs=[pl.BlockSpec((seq_len, A_W), lambda b: (b, 0)), state],
        out_shape=[jax.ShapeDtypeStruct((n_seq * seq_len, A_W), F32),
                   jax.ShapeDtypeStruct((n_seq, 2, H_A, HEAD_DIM, HEAD_DIM), F32)],
        scratch_shapes=[pltpu.VMEM((seq_len + 16, 3 * A_W), F32),
                        per_head(F32), per_head(F32), per_head(F32),
                        pltpu.VMEM((seq_len, LANES), F32),
                        pltpu.VMEM((seq_len, LANES), F32),
                        per_head(F32), per_head(BF16), per_head(BF16), per_head(BF16),
                        pltpu.VMEM((H_A, n_chunk, LANES, HEAD_DIM), BF16),
                        pltpu.VMEM((H_A, n_chunk, 1, LANES), F32),
                        pltpu.VMEM((H_A, LANES, LANES), F32),
                        per_head(F32)],
        compiler_params=_cparams(("arbitrary",)),
        name=f"delta_mixer_{seq_len}",
    )(qkv, z, gate, cw, alog_row, dtb_row, ng2, s0)


def _rope(x, cos, sin_signed, quarter):
    width = x.shape[-1]
    swapped = jnp.where((_lane(x.shape) % (2 * quarter)) < quarter,
                        pltpu.roll(x, width - quarter, axis=1), pltpu.roll(x, quarter, axis=1))
    return x * cos + swapped * sin_signed


def _rope_tables(rows, dim):
    nf = dim // 4
    inv = ROPE_THETA ** (-jnp.arange(nf, dtype=F32) / nf)
    r = jnp.repeat(jnp.arange(rows, dtype=F32), GRID_W)
    c = jnp.tile(jnp.arange(GRID_W, dtype=F32), rows)
    ar, ac = r[:, None] * inv, c[:, None] * inv
    cos = jnp.concatenate([jnp.cos(ar), jnp.cos(ar), jnp.cos(ac), jnp.cos(ac)], axis=1)
    sin = jnp.concatenate([-jnp.sin(ar), jnp.sin(ar), -jnp.sin(ac), jnp.sin(ac)], axis=1)
    return jnp.tile(cos, (1, LANES // dim)), jnp.tile(sin, (1, LANES // dim))


def _softmax_rows(s):
    p = jnp.exp(s - jnp.max(s, axis=-1, keepdims=True))
    return p / jnp.sum(p, axis=-1, keepdims=True)


def _gqa_kernel(*refs, cached):
    if cached:
        (q_ref, k_ref, v_ref, gq_ref, gk_ref, ck_ref, cv_ref, cosq_ref, sinq_ref, cosk_ref, sink_ref,
         o_ref) = refs
    else:
        q_ref, k_ref, v_ref, gq_ref, gk_ref, o_ref, kn_ref = refs
    gmat = _group_mean_matrix(LANES, HEAD_DIM)
    quarter = HEAD_DIM // 4
    k = _group_rms(k_ref[...], gmat, gk_ref[...])
    v = v_ref[...]
    if cached:
        k = _rope(k, cosk_ref[...], sink_ref[...], quarter)
        k = jnp.concatenate([ck_ref[...], k], axis=0)
        v = jnp.concatenate([cv_ref[...], v], axis=0)
    else:
        kn_ref[...] = k
    k = k.astype(BF16)
    v = v.astype(BF16)
    tq = q_ref.shape[0]
    lo = _lane((tq, LANES)) < HEAD_DIM
    group = H_B // KV_B
    for j in range(H_B // 2):
        q = _group_rms(q_ref[:, j * LANES:(j + 1) * LANES], gmat, gq_ref[...])
        if cached:
            q = _rope(q, cosq_ref[...], sinq_ref[...], quarter)
        q_sw = pltpu.roll(q, HEAD_DIM, axis=1)
        outs = []
        for e in range(2):
            kv = (2 * j + e) // group
            qh = jnp.where(lo if kv == 0 else ~lo, q if e == kv else q_sw, 0.0)
            p = _softmax_rows(_dot_nt(qh.astype(BF16), k) * (HEAD_DIM ** -0.5))
            r = _dot(p.astype(BF16), v)
            outs.append(r if e == kv else pltpu.roll(r, HEAD_DIM, axis=1))
        o_ref[:, j * LANES:(j + 1) * LANES] = jnp.where(lo, outs[0], outs[1])


def gqa_mixer(q, k, v, gq, gk, *, n_seq, seq_len, tok0, cache=None, rope=None, layer=0):
    tq = 256
    nq = seq_len // tq
    b0 = tok0 // seq_len
    q0 = tok0 // tq
    cached = cache is not None
    kvw = KV_B * HEAD_DIM
    in_specs = [pl.BlockSpec((tq, B_W), lambda b, i: (q0 + b * nq + i, 0)),
                pl.BlockSpec((seq_len, kvw), lambda b, i: (b0 + b, 0)),
                pl.BlockSpec((seq_len, kvw), lambda b, i: (b0 + b, 0)),
                pl.BlockSpec((1, LANES), lambda b, i: (0, 0)),
                pl.BlockSpec((1, LANES), lambda b, i: (0, 0))]
    args = [q, k, v, gq, gk]
    out_specs = [pl.BlockSpec((tq, B_W), lambda b, i: (b * nq + i, 0))]
    out_shape = [jax.ShapeDtypeStruct((n_seq * seq_len, B_W), F32)]
    if cached:
        cos, sin = rope
        cache_spec = pl.BlockSpec((None, None, PAST_LEN, kvw), lambda b, i: (b, layer, 0, 0))
        in_specs += [cache_spec, cache_spec,
                     pl.BlockSpec((tq, LANES), lambda b, i: (i, 0)),
                     pl.BlockSpec((tq, LANES), lambda b, i: (i, 0)),
                     pl.BlockSpec((seq_len, LANES), lambda b, i: (0, 0)),
                     pl.BlockSpec((seq_len, LANES), lambda b, i: (0, 0))]
        args += [cache[0], cache[1], cos, sin, cos, sin]
    else:
        out_specs.append(pl.BlockSpec((seq_len, kvw), lambda b, i: (b, 0)))
        out_shape.append(jax.ShapeDtypeStruct((n_seq * seq_len, kvw), F32))
    return pl.pallas_call(
        functools.partial(_gqa_kernel, cached=cached),
        grid=(n_seq, nq),
        in_specs=in_specs, out_specs=out_specs, out_shape=out_shape,
        compiler_params=_cparams(("arbitrary", "arbitrary")),
        name=f"gqa_mixer_{seq_len}",
    )(*args)


def _diff_kernel(*refs, cached, lam_init):
    if cached:
        (q_ref, k_ref, v_ref, lam_ref, gn_ref, ck_ref, cv_ref, cosq_ref, sinq_ref, cosk_ref, sink_ref,
         o_ref) = refs
    else:
        q_ref, k_ref, v_ref, lam_ref, gn_ref, o_ref = refs
    quarter = DIFF_HD // 4
    lv = lam_ref[...]
    lam = (jnp.exp(jnp.sum(lv[0:1] * lv[1:2], axis=-1, keepdims=True))
           - jnp.exp(jnp.sum(lv[2:3] * lv[3:4], axis=-1, keepdims=True)) + lam_init)
    q, k, v = q_ref[...], k_ref[...], v_ref[...]
    if cached:
        cq = jnp.concatenate([cosq_ref[...]] * 2, axis=1)
        sq = jnp.concatenate([sinq_ref[...]] * 2, axis=1)
        ck = jnp.concatenate([cosk_ref[...]] * 2, axis=1)
        sk = jnp.concatenate([sink_ref[...]] * 2, axis=1)
        q = _rope(q, cq, sq, quarter)
        k = _rope(k, ck, sk, quarter)
        k = jnp.concatenate([ck_ref[...], k], axis=0)
        v = jnp.concatenate([cv_ref[...], v], axis=0)
    k = k.astype(BF16)
    v = v.astype(BF16)
    lane = _lane(q.shape)
    o = jnp.zeros(q.shape, F32)
    for h in range(H_C):
        probs = []
        for m in range(2):
            qm = jnp.where(lane // DIFF_HD == 2 * h + m, q, 0.0)
            probs.append(_softmax_rows(_dot_nt(qm.astype(BF16), k) * (DIFF_HD ** -0.5)))
        a = probs[0] - lam * probs[1]
        o = jnp.where(lane // (2 * DIFF_HD) == h, _dot(a.astype(BF16), v), o)
    o_ref[...] = _group_rms(o, _group_mean_matrix(C_W, 2 * DIFF_HD), gn_ref[...]) * (1.0 - lam_init)


def diff_mixer(q, k, v, lam_p, gn, lam_init, *, n_seq, seq_len, tok0, cache=None, rope=None, layer=0):
    tq = 256
    nq = seq_len // tq
    b0 = tok0 // seq_len
    q0 = tok0 // tq
    cached = cache is not None
    in_specs = [pl.BlockSpec((tq, C_W), lambda b, i: (q0 + b * nq + i, 0)),
                pl.BlockSpec((seq_len, C_W), lambda b, i: (b0 + b, 0)),
                pl.BlockSpec((seq_len, C_W), lambda b, i: (b0 + b, 0)),
                pl.BlockSpec((4, DIFF_HD), lambda b, i: (0, 0)),
                pl.BlockSpec((1, C_W), lambda b, i: (0, 0))]
    args = [q, k, v, lam_p, gn]
    if cached:
        cos, sin = rope
        cache_spec = pl.BlockSpec((None, None, PAST_LEN, C_W), lambda b, i: (b, layer, 0, 0))
        in_specs += [cache_spec, cache_spec,
                     pl.BlockSpec((tq, LANES), lambda b, i: (i, 0)),
                     pl.BlockSpec((tq, LANES), lambda b, i: (i, 0)),
                     pl.BlockSpec((seq_len, LANES), lambda b, i: (0, 0)),
                     pl.BlockSpec((seq_len, LANES), lambda b, i: (0, 0))]
        args += [cache[0], cache[1], cos, sin, cos, sin]
    return pl.pallas_call(
        functools.partial(_diff_kernel, cached=cached, lam_init=lam_init),
        grid=(n_seq, nq),
        in_specs=in_specs,
        out_specs=pl.BlockSpec((tq, C_W), lambda b, i: (b * nq + i, 0)),
        out_shape=jax.ShapeDtypeStruct((n_seq * seq_len, C_W), F32),
        compiler_params=_cparams(("arbitrary", "arbitrary")),
        name=f"diff_mixer_{seq_len}",
    )(*args)


def _outproj_kernel(*refs, routed, n_mix):
    x_ref = refs[0]
    mix_refs = refs[1:1 + 2 * n_mix]
    rest = refs[1 + 2 * n_mix:]
    if routed:
        w_ref, gm_ref, g_ref, sh_ref, sc_ref, r_ref, x1_ref, h_ref, lg_ref = rest
    else:
        w_ref, gm_ref, g_ref, sh_ref, sc_ref, x1_ref, h_ref = rest
    is_ctx = pl.program_id(0) < N_CTX // x_ref.shape[0]
    mixed = jnp.concatenate(
        [jnp.where(is_ctx, mix_refs[2 * m][...], mix_refs[2 * m + 1][...]) for m in range(n_mix)], axis=1)
    x1 = x_ref[...] + gm_ref[...] * _dot(mixed.astype(BF16), w_ref[...])
    x1_ref[...] = x1
    h = _rms(x1, g_ref[...]) * (1.0 + sc_ref[...]) + sh_ref[...]
    h_ref[...] = h.astype(BF16)
    if routed:
        lg_ref[...] = _dot_hi(h, r_ref[...])


def out_projection(x, mixes, w_out_b, mod, norm_g, l, router_p=None):
    tm = 512
    n_ctx_tiles = N_CTX // tm
    n_lat_tiles = N_LAT // tm
    routed = router_p is not None
    mod_spec = lambda k: pl.BlockSpec((None, None, 1, D_MODEL), lambda i: (l, _cond_row(i, tm), 0, k))
    in_specs = [pl.BlockSpec((tm, D_MODEL), lambda i: (i, 0))]
    args = [x]
    for ctx_arr, lat_arr in mixes:
        w = ctx_arr.shape[1]
        in_specs.append(pl.BlockSpec((tm, w), lambda i: (jnp.minimum(i, n_ctx_tiles - 1), 0)))
        in_specs.append(pl.BlockSpec((tm, w), lambda i: (jnp.clip(i - n_ctx_tiles, 0, n_lat_tiles - 1), 0)))
        args += [ctx_arr, lat_arr]
    in_specs += [pl.BlockSpec((None, D_MODEL, D_MODEL), lambda i: (l, 0, 0)),
                 mod_spec(2),
                 pl.BlockSpec((None, 1, D_MODEL), lambda i: (l, 0, 0)),
                 mod_spec(3), mod_spec(4)]
    args += [w_out_b, mod, norm_g, mod, mod]
    out_specs = [pl.BlockSpec((tm, D_MODEL), lambda i: (i, 0)),
                 pl.BlockSpec((tm, D_MODEL), lambda i: (i, 0))]
    out_shape = [jax.ShapeDtypeStruct((N_TOK, D_MODEL), F32),
                 jax.ShapeDtypeStruct((N_TOK, D_MODEL), BF16)]
    if routed:
        in_specs.append(pl.BlockSpec((D_MODEL, LANES), lambda i: (0, 0)))
        args.append(router_p)
        out_specs.append(pl.BlockSpec((tm, LANES), lambda i: (i, 0)))
        out_shape.append(jax.ShapeDtypeStruct((N_TOK, LANES), F32))
    return pl.pallas_call(
        functools.partial(_outproj_kernel, routed=routed, n_mix=len(mixes)),
        grid=(N_TOK // tm,),
        in_specs=in_specs, out_specs=out_specs, out_shape=out_shape,
        compiler_params=_cparams(("arbitrary",)),
        name="out_projection",
    )(*args)


def _top2_gate(logits):
    lane = _lane(logits.shape)
    neg = jnp.float32(-jnp.inf)
    lg = jnp.where(lane < N_EXP, logits, neg)
    t1 = jnp.max(lg, axis=-1, keepdims=True)
    i1 = jnp.min(jnp.where(lg == t1, lane, LANES), axis=-1, keepdims=True)
    lg2 = jnp.where(lane == i1, neg, lg)
    t2 = jnp.max(lg2, axis=-1, keepdims=True)
    i2 = jnp.min(jnp.where(lg2 == t2, lane, LANES), axis=-1, keepdims=True)
    e2 = jnp.exp(t2 - t1)
    w1 = 1.0 / (1.0 + e2)
    w2 = e2 / (1.0 + e2)
    return jnp.where(lane == i1, w1, 0.0) + jnp.where(lane == i2, w2, 0.0)


def _ffn_kernel(*refs, routed):
    if routed:
        h_ref, wg_ref, wu_ref, wd_ref, x_ref, gf_ref, lg_ref, o_ref, acc_ref = refs
    else:
        h_ref, wg_ref, wu_ref, wd_ref, x_ref, gf_ref, o_ref, acc_ref = refs
    j = pl.program_id(1)

    @pl.when(j == 0)
    def _():
        acc_ref[...] = jnp.zeros_like(acc_ref)

    h = h_ref[...]
    act = _silu(_dot(h, wg_ref[...])) * _dot(h, wu_ref[...])
    if routed:
        gate = _top2_gate(lg_ref[...])
        act = act * jnp.sum(jnp.where(_lane(gate.shape) == j, gate, 0.0), axis=-1, keepdims=True)
    acc_ref[...] += _dot(act.astype(BF16), wd_ref[...])

    @pl.when(j == pl.num_programs(1) - 1)
    def _():
        o_ref[...] = x_ref[...] + gf_ref[...] * acc_ref[...]


def ffn(h, wg, wu, wd, x1, mod, l, logits=None):
    tm = 512
    n_chunk = wg.shape[0]
    routed = logits is not None
    in_specs = [pl.BlockSpec((tm, D_MODEL), lambda i, j: (i, 0)),
                pl.BlockSpec((None, D_MODEL, FF_CHUNK), lambda i, j: (j, 0, 0)),
                pl.BlockSpec((None, D_MODEL, FF_CHUNK), lambda i, j: (j, 0, 0)),
                pl.BlockSpec((None, FF_CHUNK, D_MODEL), lambda i, j: (j, 0, 0)),
                pl.BlockSpec((tm, D_MODEL), lambda i, j: (i, 0)),
                pl.BlockSpec((None, None, 1, D_MODEL), lambda i, j: (l, _cond_row(i, tm), 0, 5))]
    args = [h, wg, wu, wd, x1, mod]
    if routed:
        in_specs.append(pl.BlockSpec((tm, LANES), lambda i, j: (i, 0)))
        args.append(logits)
    return pl.pallas_call(
        functools.partial(_ffn_kernel, routed=routed),
        grid=(N_TOK // tm, n_chunk),
        in_specs=in_specs,
        out_specs=pl.BlockSpec((tm, D_MODEL), lambda i, j: (i, 0)),
        out_shape=jax.ShapeDtypeStruct((N_TOK, D_MODEL), F32),
        scratch_shapes=[pltpu.VMEM((tm, D_MODEL), F32)],
        compiler_params=_cparams(("arbitrary", "arbitrary")),
        name="ffn_routed" if routed else "ffn_dense",
    )(*args)


def _final_norm_kernel(x_ref, g_ref, o_ref):
    o_ref[...] = _rms(x_ref[...], g_ref[...])


def final_norm_call(x, g):
    tm = 1024
    return pl.pallas_call(
        _final_norm_kernel,
        grid=(N_TOK // tm,),
        in_specs=[pl.BlockSpec((tm, D_MODEL), lambda i: (i, 0)),
                  pl.BlockSpec((1, D_MODEL), lambda i: (0, 0))],
        out_specs=pl.BlockSpec((tm, D_MODEL), lambda i: (i, 0)),
        out_shape=jax.ShapeDtypeStruct((N_TOK, D_MODEL), F32),
        compiler_params=_cparams(("arbitrary",)),
        name="final_norm",
    )(x, g)


def _gate_row(p):
    return jnp.zeros((1, LANES), F32).at[0, GATE_A:GATE_A + 2 * H_A].set(p.reshape(-1))


def _tile_lanes(g, width):
    return jnp.tile(g, width // g.shape[0])[None, :]


def kernel(x_prompt, x_sample, state_delta, cache_gqa_k, cache_gqa_v, cache_diff_k, cache_diff_v, c, c_ctx,
           ada_w, ada_b, norm_mix, norm_ffn, w_in, conv_w, dn_a_log, dn_dt_bias, dn_norm, gqa_q_norm,
           gqa_k_norm, diff_lambda, diff_norm, w_out, ffn_w_gu, ffn_w_down, moe_router, moe_w_gu, moe_w_down,
           final_norm):
    cond8 = jnp.concatenate([c_ctx[None, :], c, jnp.zeros((8 - N_COND, D_MODEL), F32)], axis=0)
    gate_lo = 4 * A_W
    gate_hi = gate_lo + 4 * H_A
    w_in_p = jnp.concatenate(
        [w_in[:, :, :gate_lo], w_in[:, :, gate_hi:], w_in[:, :, gate_lo:gate_hi],
         jnp.zeros((DEPTH, D_MODEL, LANES - 4 * H_A), F32)], axis=-1).astype(BF16)
    w_out_b = w_out.astype(BF16)
    norm_mix3 = norm_mix.reshape(DEPTH, 1, D_MODEL)
    norm_ffn3 = norm_ffn.reshape(DEPTH, 1, D_MODEL)
    n_ff = D_FF // FF_CHUNK
    ffn_g = ffn_w_gu[:, :, :D_FF].reshape(-1, D_MODEL, n_ff, FF_CHUNK).transpose(0, 2, 1, 3).astype(BF16)
    ffn_u = ffn_w_gu[:, :, D_FF:].reshape(-1, D_MODEL, n_ff, FF_CHUNK).transpose(0, 2, 1, 3).astype(BF16)
    ffn_d = ffn_w_down.reshape(-1, n_ff, FF_CHUNK, D_MODEL).astype(BF16)
    moe_g = moe_w_gu[:, :, :, :D_FF_E].astype(BF16)
    moe_u = moe_w_gu[:, :, :, D_FF_E:].astype(BF16)
    moe_d = moe_w_down.astype(BF16)
    router_p = jnp.pad(moe_router, ((0, 0), (0, 0), (0, LANES - N_EXP)))
    rope_b = _rope_tables(DEC_SEQ // GRID_W, HEAD_DIM)
    rope_c = _rope_tables(DEC_SEQ // GRID_W, DIFF_HD)
    s0_ctx = jnp.zeros((BATCH, 2, H_A, HEAD_DIM, HEAD_DIM), F32)
    cache_b = (cache_gqa_k.reshape(DEC_BATCH, DEPTH, PAST_LEN, KV_B * HEAD_DIM),
               cache_gqa_v.reshape(DEC_BATCH, DEPTH, PAST_LEN, KV_B * HEAD_DIM))
    cache_c = (cache_diff_k.reshape(DEC_BATCH, DEPTH, PAST_LEN, C_W),
               cache_diff_v.reshape(DEC_BATCH, DEPTH, PAST_LEN, C_W))

    mod = ada_modulation(cond8, ada_w, ada_b)[:, :N_COND].reshape(DEPTH, N_COND, 1, 6 * D_MODEL)

    x = jnp.concatenate([x_prompt.reshape(N_CTX, D_MODEL), x_sample.reshape(N_LAT, D_MODEL)], axis=0)
    states, gk_out, gv_out, dk_out, dv_out = [], [], [], [], []
    ctx = dict(n_seq=BATCH, seq_len=SEQ, tok0=0)
    lat = dict(n_seq=DEC_BATCH, seq_len=DEC_SEQ, tok0=N_CTX)
    for l in range(DEPTH):
        qkv_a, z_a, q_b, k_b, v_b, q_c, k_c, v_c, gate = in_projection(x, mod, norm_mix3, w_in_p, l)

        dn_args = (qkv_a, z_a, gate, conv_w[l], _gate_row(dn_a_log[l]), _gate_row(dn_dt_bias[l]),
                   _tile_lanes(dn_norm[l], LANES))
        oa_ctx, s_fin = delta_mixer(*dn_args, s0_ctx, **ctx)
        oa_lat, _ = delta_mixer(*dn_args, state_delta[:, l], **lat)

        gq, gk = _tile_lanes(gqa_q_norm[l], LANES), _tile_lanes(gqa_k_norm[l], LANES)
        ob_ctx, kn_ctx = gqa_mixer(q_b, k_b, v_b, gq, gk, **ctx)
        (ob_lat,) = gqa_mixer(q_b, k_b, v_b, gq, gk, cache=cache_b, rope=rope_b, layer=l, **lat)

        lam_init = 0.8 - 0.6 * math.exp(-0.3 * l)
        gn = _tile_lanes(diff_norm[l], C_W)
        oc_ctx = diff_mixer(q_c, k_c, v_c, diff_lambda[l], gn, lam_init, **ctx)
        oc_lat = diff_mixer(q_c, k_c, v_c, diff_lambda[l], gn, lam_init, cache=cache_c, rope=rope_c, layer=l, **lat)

        mixes = ((oa_ctx, oa_lat), (ob_ctx, ob_lat), (oc_ctx, oc_lat))
        if l % 2 == 0:
            x1, h2 = out_projection(x, mixes, w_out_b, mod, norm_ffn3, l)
            x = ffn(h2, ffn_g[l // 2], ffn_u[l // 2], ffn_d[l // 2], x1, mod, l)
        else:
            x1, h2, logits = out_projection(x, mixes, w_out_b, mod, norm_ffn3, l, router_p[l // 2])
            x = ffn(h2, moe_g[l // 2], moe_u[l // 2], moe_d[l // 2], x1, mod, l, logits)

        states.append(s_fin)
        gk_out.append(kn_ctx.reshape(BATCH, SEQ, KV_B, HEAD_DIM))
        gv_out.append(v_b[:N_CTX].reshape(BATCH, SEQ, KV_B, HEAD_DIM))
        dk_out.append(k_c[:N_CTX].reshape(BATCH, SEQ, H_C, 2, DIFF_HD))
        dv_out.append(v_c[:N_CTX].reshape(BATCH, SEQ, H_C, 2 * DIFF_HD))

    y = final_norm_call(x, final_norm[None, :])
    y_prompt = y[:N_CTX].reshape(BATCH, SEQ, D_MODEL)
    y_sample = y[N_CTX:].reshape(DEC_BATCH, DEC_SEQ, D_MODEL)
    return (y_prompt, y_sample, jnp.stack(states, axis=1), jnp.stack(gk_out, axis=1),
            jnp.stack(gv_out, axis=1), jnp.stack(dk_out, axis=1), jnp.stack(dv_out, axis=1))
```

```python
import functools
import math

import jax
import jax.numpy as jnp
from jax import lax
from jax.experimental import pallas as pl
from jax.experimental.pallas import tpu as pltpu

D_MODEL = 1024
BATCH = 16
SEQ = 256
DEPTH = 4
DEC_BATCH = 2
DEC_SEQ = 1024
PAST_LEN = 512
GRID_W = 64
HEAD_DIM = 64
H_A = 6
A_W = H_A * HEAD_DIM
H_B = 6
KV_B = 2
B_W = H_B * HEAD_DIM
H_C = 4
DIFF_HD = 32
C_W = H_C * 2 * DIFF_HD
CONV_W = 3
CHUNK = 64
ROPE_THETA = 10000.0
D_FF = 2816
N_EXP = 8
D_FF_E = 1408
EPS = 1e-6

N_CTX = BATCH * SEQ
N_LAT = DEC_BATCH * DEC_SEQ
N_TOK = N_CTX + N_LAT
N_COND = 1 + DEC_BATCH
IN_PAD = 3072
FF_CHUNK = D_FF_E
LANES = 128
VMEM_LIMIT = 56 * 1024 * 1024

IN_SPLIT = (("qkv_a", 3 * A_W), ("z_a", A_W), ("q_b", B_W), ("k_b", KV_B * HEAD_DIM), ("v_b", KV_B * HEAD_DIM),
            ("q_c", C_W), ("k_c", C_W), ("v_c", C_W), ("gate", LANES))
GATE_BETA, GATE_A = 0, 2 * H_A

F32 = jnp.float32
BF16 = jnp.bfloat16
HI = lax.Precision.HIGHEST


def _cparams(sem):
    return pltpu.CompilerParams(dimension_semantics=sem, vmem_limit_bytes=VMEM_LIMIT)


def _dot(a, b):
    return jnp.dot(a, b, preferred_element_type=F32)


def _dot_hi(a, b):
    return jnp.dot(a, b, preferred_element_type=F32, precision=HI)


def _dot_nt(a, b):
    return lax.dot_general(a, b, (((1,), (1,)), ((), ())), preferred_element_type=F32)


def _silu(x):
    return x * jax.nn.sigmoid(x)


def _softplus(x):
    return jnp.maximum(x, 0.0) + jnp.log1p(jnp.exp(-jnp.abs(x)))


def _rms(x, g):
    return x * lax.rsqrt(jnp.mean(x * x, axis=-1, keepdims=True) + EPS) * g


def _lane(shape):
    return lax.broadcasted_iota(jnp.int32, shape, len(shape) - 1)


def _group_mean_matrix(width, group):
    i = lax.broadcasted_iota(jnp.int32, (width, width), 0)
    j = lax.broadcasted_iota(jnp.int32, (width, width), 1)
    return jnp.where(i // group == j // group, 1.0 / group, 0.0).astype(F32)


def _group_rms(x, gmat, g):
    return x * lax.rsqrt(_dot_hi(x * x, gmat) + EPS) * g


def _cond_row(i, tm):
    return jnp.maximum((i * tm) // DEC_SEQ - (N_CTX // DEC_SEQ - 1), 0)


def _ada_kernel(c_ref, w_ref, b_ref, o_ref):
    o_ref[...] = _dot_hi(_silu(c_ref[...]), w_ref[...]) + b_ref[...]


def ada_modulation(cond8, ada_w, ada_b):
    tn = 1024
    return pl.pallas_call(
        _ada_kernel,
        grid=(DEPTH, 6 * D_MODEL // tn),
        in_specs=[pl.BlockSpec((8, D_MODEL), lambda l, j: (0, 0)),
                  pl.BlockSpec((None, D_MODEL, tn), lambda l, j: (l, 0, j)),
                  pl.BlockSpec((None, 1, tn), lambda l, j: (l, 0, j))],
        out_specs=pl.BlockSpec((None, 8, tn), lambda l, j: (l, 0, j)),
        out_shape=jax.ShapeDtypeStruct((DEPTH, 8, 6 * D_MODEL), F32),
        compiler_params=_cparams(("arbitrary", "arbitrary")),
        name="ada_modulation",
    )(cond8, ada_w, ada_b.reshape(DEPTH, 1, 6 * D_MODEL))


def _inproj_kernel(x_ref, g_ref, sh_ref, sc_ref, w_ref, *o_refs):
    h = _rms(x_ref[...], g_ref[...]) * (1.0 + sc_ref[...]) + sh_ref[...]
    acc = _dot(h.astype(BF16), w_ref[...])
    off = 0
    for o_ref, (_, width) in zip(o_refs, IN_SPLIT):
        o_ref[...] = acc[:, off:off + width]
        off += width


def in_projection(x, mod, norm_g, w_in_p, l):
    tm = 512
    return pl.pallas_call(
        _inproj_kernel,
        grid=(N_TOK // tm,),
        in_specs=[pl.BlockSpec((tm, D_MODEL), lambda i: (i, 0)),
                  pl.BlockSpec((None, 1, D_MODEL), lambda i: (l, 0, 0)),
                  pl.BlockSpec((None, None, 1, D_MODEL), lambda i: (l, _cond_row(i, tm), 0, 0)),
                  pl.BlockSpec((None, None, 1, D_MODEL), lambda i: (l, _cond_row(i, tm), 0, 1)),
                  pl.BlockSpec((None, D_MODEL, IN_PAD), lambda i: (l, 0, 0))],
        out_specs=[pl.BlockSpec((tm, w), lambda i: (i, 0)) for _, w in IN_SPLIT],
        out_shape=[jax.ShapeDtypeStruct((N_TOK, w), F32) for _, w in IN_SPLIT],
        compiler_params=_cparams(("arbitrary",)),
        name="in_projection",
    )(x, norm_g, mod, mod, w_in_p)


SOLVE_BLOCK = 16


def _solve_unit_triangular(lmat, rhs, lo, diag_blk, eye):
    def both(a):
        return jnp.concatenate([jnp.where(lo, a, 0.0), jnp.where(lo, 0.0, a)], axis=0)

    neg_d = jnp.where(diag_blk, -lmat, 0.0)
    pd = both(neg_d)
    po = both(-lmat - neg_d)
    t = eye + pd
    p = _dot(pd.astype(BF16), pd.astype(BF16))
    n_fac = int(math.log2(SOLVE_BLOCK))
    for j in range(1, n_fac):
        if j + 1 < n_fac:
            tp = _dot(p.astype(BF16), jnp.concatenate([t, p], axis=1).astype(BF16))
            t = t + tp[:, :LANES]
            p = tp[:, LANES:]
        else:
            t = t + _dot(p.astype(BF16), t.astype(BF16))
    cm = _dot(t.astype(BF16), jnp.concatenate([rhs, po], axis=1).astype(BF16))
    c, m = cm[:, :LANES], cm[:, LANES:].astype(BF16)
    x = c
    for _ in range(CHUNK // SOLVE_BLOCK - 1):
        x = c + _dot(m, x.astype(BF16))
    return x


def _delta_kernel(qkv_ref, z_ref, gate_ref, cw_ref, alog_ref, dtb_ref, ng_ref, s0_ref, o_ref, sfin_ref,
                  pad_s, kk_s, qq_s, vk_s, gc_s, beta_s, u_s, w_s, qd_s, a_s, kdt_s, egt_s, sa_s, o_s,
                  *, seq_len):
    n_chunk = seq_len // CHUNK
    n_pair = H_A // 2
    half = HEAD_DIM
    lo = _lane((CHUNK, LANES)) < half
    lo_row = _lane((1, LANES)) < half
    lo16 = _lane((CHUNK, LANES)).astype(F32).astype(BF16) < half
    row = lax.broadcasted_iota(jnp.int32, (CHUNK, LANES), 0)
    col = _lane((CHUNK, LANES)) % half
    ahead = jnp.where(lo, row - col, col - row)
    incl = ahead >= 0
    strict = ahead > 0
    diag_blk = row // SOLVE_BLOCK == col // SOLVE_BLOCK
    big_row = lax.broadcasted_iota(jnp.int32, (LANES, LANES), 0)
    anti = big_row // half + _lane((LANES, LANES)) // half == 1
    eye = (big_row == _lane((LANES, LANES))).astype(F32)

    rb = 128
    zero8 = jnp.zeros((8, 3 * A_W), F32)
    pad_s[0:8, :] = zero8
    pad_s[8 + seq_len:16 + seq_len, :] = zero8
    pad_s[8:8 + seq_len, :] = qkv_ref[...]
    lo_rb = _lane((rb, LANES)) < half
    for r0 in range(0, seq_len, rb):
        y = _silu(pad_s[7 + r0:7 + r0 + rb, :] * cw_ref[0:1, :] + pad_s[8 + r0:8 + r0 + rb, :] * cw_ref[1:2, :]
                  + pad_s[9 + r0:9 + r0 + rb, :] * cw_ref[2:3, :])

        def dup(part, p):
            c = y[:, part * A_W + p * LANES:part * A_W + (p + 1) * LANES]
            r = pltpu.roll(c, half, axis=1)
            return jnp.where(lo_rb, c, r), jnp.where(lo_rb, r, c)

        def l2n(x):
            return x * lax.rsqrt(0.5 * jnp.sum(x * x, axis=-1, keepdims=True) + EPS)

        for p in range(n_pair):
            qs, ks, vs = dup(0, p), dup(1, p), dup(2, p)
            for e in range(2):
                h = 2 * p + e
                kk = l2n(ks[e])
                kk_s[h, r0:r0 + rb, :] = kk
                qq_s[h, r0:r0 + rb, :] = l2n(qs[e]) * (HEAD_DIM ** -0.5)
                vk_s[h, r0:r0 + rb, :] = jnp.where(lo_rb, vs[e], kk)

    gt = gate_ref[...]
    beta_s[...] = jax.nn.sigmoid(gt)
    g = -jnp.exp(alog_ref[...]) * _softplus(gt + dtb_ref[...])
    g1 = g.astype(BF16)
    r1 = g - g1.astype(F32)
    g2 = r1.astype(BF16)
    g3 = (r1 - g2.astype(F32)).astype(BF16)
    g_split = jnp.concatenate([g1, g2, g3], axis=1)
    cb = 256
    ci = lax.broadcasted_iota(jnp.int32, (cb, cb), 0)
    cj = lax.broadcasted_iota(jnp.int32, (cb, cb), 1)
    same = ci // CHUNK == cj // CHUNK
    m_pre = (same & (ci >= cj)).astype(BF16)
    m_suf = (same & (ci <= cj)).astype(BF16)
    fwd_cols = (_lane((cb, LANES)) >= GATE_A) & (_lane((cb, LANES)) < GATE_A + H_A)
    for r0 in range(0, seq_len, cb):
        blk = g_split[r0:r0 + cb, :]
        pre, suf = _dot(m_pre, blk), _dot(m_suf, blk)
        pre = pre[:, :LANES] + pre[:, LANES:2 * LANES] + pre[:, 2 * LANES:]
        suf = suf[:, :LANES] + suf[:, LANES:2 * LANES] + suf[:, 2 * LANES:]
        gc_s[r0:r0 + cb, :] = jnp.where(fwd_cols, pre, suf)

    for h in range(H_A):
        zero = jnp.zeros((half, half), F32)
        sa_s[h] = jnp.concatenate([jnp.concatenate([zero, s0_ref[1, h]], axis=1),
                                   jnp.concatenate([s0_ref[0, h], zero], axis=1)], axis=0)

    def prep(c, carry):
        rows = pl.ds(pl.multiple_of(c * CHUNK, CHUNK), CHUNK)
        gcb = gc_s[rows, :]
        gct = gcb.T
        bb = beta_s[rows, :]
        for h in range(H_A):
            cf = gcb[:, GATE_A + h:GATE_A + h + 1]
            cbk = gcb[:, GATE_A + H_A + h:GATE_A + H_A + h + 1]
            gc_fb = jnp.where(lo, cf, cbk)
            gr_fb = jnp.concatenate([gct[GATE_A + h:GATE_A + h + 1, :],
                                     gct[GATE_A + H_A + h:GATE_A + H_A + h + 1, :]], axis=1)
            decay = jnp.where(incl, jnp.exp(jnp.where(incl, gc_fb - gr_fb, 0.0)), 0.0)
            bf = bb[:, GATE_BETA + h:GATE_BETA + h + 1]
            bbk = bb[:, GATE_BETA + H_A + h:GATE_BETA + H_A + h + 1]
            egf, egb = jnp.exp(cf), jnp.exp(cbk)
            tot_f = gcb[CHUNK - 1:CHUNK, GATE_A + h:GATE_A + h + 1]
            tot_b = gcb[0:1, GATE_A + H_A + h:GATE_A + H_A + h + 1]
            kk, qq, vk = kk_s[h, rows, :], qq_s[h, rows, :], vk_s[h, rows, :]
            k16 = kk[:, :half].astype(BF16)
            kq = jnp.concatenate([k16, qq[:, :half].astype(BF16)], axis=0)
            gram = _dot_nt(kq, jnp.concatenate([k16, k16], axis=0))
            lmat = jnp.where(strict, gram[:CHUNK] * jnp.where(lo, bf, bbk) * decay, 0.0)
            a_s[h, rows, :] = jnp.where(incl, gram[CHUNK:] * decay, 0.0).astype(BF16)
            x_f = vk * jnp.where(lo, bf, bf * egf)
            x_b = vk * jnp.where(lo, bbk, bbk * egb)
            x = _solve_unit_triangular(lmat, jnp.concatenate([x_f, x_b], axis=0), lo, diag_blk, eye)
            x_f, x_b = x[:CHUNK], pltpu.roll(x[CHUNK:], half, axis=1)
            u_s[h, rows, :] = jnp.where(lo, x_f, x_b)
            w_s[h, rows, :] = jnp.where(lo, x_b, x_f).astype(BF16)
            qd_s[h, rows, :] = (qq * jnp.where(lo, egb, egf)).astype(BF16)
            kd = kk * jnp.where(lo, jnp.exp(tot_b - cbk), jnp.exp(tot_f - cf))
            kdt_s[h, c] = kd.T.astype(BF16)
            egt_s[h, c] = jnp.exp(jnp.where(lo_row, tot_f, tot_b))
        return carry

    lax.fori_loop(0, n_chunk, prep, 0)

    def scan(s, carry):
        rf = pl.ds(pl.multiple_of(s * CHUNK, CHUNK), CHUNK)
        sb = n_chunk - 1 - s
        rbk = pl.ds(pl.multiple_of(sb * CHUNK, CHUNK), CHUNK)
        for h in range(H_A):
            u = jnp.where(lo, u_s[h, rf, :], u_s[h, rbk, :])
            w = jnp.where(lo16, w_s[h, rbk, :], w_s[h, rf, :])
            qd = jnp.where(lo16, qd_s[h, rbk, :], qd_s[h, rf, :])
            a = jnp.where(lo16, a_s[h, rf, :], a_s[h, rbk, :])
            kdt = jnp.concatenate([kdt_s[h, sb][:half], kdt_s[h, s][half:]], axis=0)
            egt = jnp.where(lo_row, egt_s[h, s], egt_s[h, sb])
            st = sa_s[h]
            wq = _dot(jnp.concatenate([w, qd], axis=0), st.astype(BF16))
            v = u - wq[:CHUNK]
            vbd = jnp.concatenate([jnp.where(lo, v, 0.0), jnp.where(lo, 0.0, v)], axis=0).astype(BF16)
            o = wq[CHUNK:] + _dot(a, vbd)
            sa_s[h] = st * egt + jnp.where(anti, _dot(kdt, v.astype(BF16)), 0.0)
            o_s[h, rf, 0:half] = o[:, :half]
            o_s[h, rbk, half:LANES] = o[:, half:]
        return carry

    lax.fori_loop(0, n_chunk, scan, 0)

    for r0 in range(0, seq_len, rb):
        for p in range(n_pair):
            nrm = []
            for e in range(2):
                ofb = o_s[2 * p + e, r0:r0 + rb, :]
                oo = ofb + pltpu.roll(ofb, half, axis=1)
                nrm.append(oo * lax.rsqrt(jnp.sum(oo * oo, axis=-1, keepdims=True) * (0.5 / HEAD_DIM) + EPS))
            o_ref[r0:r0 + rb, p * LANES:(p + 1) * LANES] = (
                jnp.where(lo_rb, nrm[0], nrm[1]) * ng_ref[...] * _silu(z_ref[r0:r0 + rb, p * LANES:(p + 1) * LANES]))

    for h in range(H_A):
        st = sa_s[h]
        sfin_ref[0, h] = st[half:, :half]
        sfin_ref[1, h] = st[:half, half:]


def delta_mixer(qkv, z, gate, cw, alog_row, dtb_row, ng2, s0, *, n_seq, seq_len, tok0):
    n_chunk = seq_len // CHUNK
    b0 = tok0 // seq_len
    tok = lambda w: pl.BlockSpec((seq_len, w), lambda b: (b0 + b, 0))
    full = lambda a: pl.BlockSpec(a.shape, lambda b: (0,) * a.ndim)
    state = pl.BlockSpec((None, 2, H_A, HEAD_DIM, HEAD_DIM), lambda b: (b, 0, 0, 0, 0))
    per_head = lambda dt: pltpu.VMEM((H_A, seq_len, LANES), dt)
    return pl.pallas_call(
        functools.partial(_delta_kernel, seq_len=seq_len),
        grid=(n_seq,),
        in_specs=[tok(3 * A_W), tok(A_W), tok(LANES), full(cw), full(alog_row), full(dtb_row), full(ng2), state],
        out_specs=[pl.BlockSpec((seq_len, A_W), lambda b: (b, 0)), state],
        out_shape=[jax.ShapeDtypeStruct((n_seq * seq_len, A_W), F32),
                   jax.ShapeDtypeStruct((n_seq, 2, H_A, HEAD_DIM, HEAD_DIM), F32)],
        scratch_shapes=[pltpu.VMEM((seq_len + 16, 3 * A_W), F32),
                        per_head(F32), per_head(F32), per_head(F32),
                        pltpu.VMEM((seq_len, LANES), F32),
                        pltpu.VMEM((seq_len, LANES), F32),
                        per_head(F32), per_head(BF16), per_head(BF16), per_head(BF16),
                        pltpu.VMEM((H_A, n_chunk, LANES, HEAD_DIM), BF16),
                        pltpu.VMEM((H_A, n_chunk, 1, LANES), F32),
                        pltpu.VMEM((H_A, LANES, LANES), F32),
                        per_head(F32)],
        compiler_params=_cparams(("arbitrary",)),
        name=f"delta_mixer_{seq_len}",
    )(qkv, z, gate, cw, alog_row, dtb_row, ng2, s0)


def _delta_kernel(qkv_ref, z_ref, gate_ref, cw_ref, alog_ref, dtb_ref, ng_ref, s0_ref, o_ref, sfin_ref,
                  pad_s, kk_s, qq_s, vk_s, gc_s, beta_s, u_s, w_s, qd_s, a_s, kdt_s, egt_s, sa_s, o_s,
                  tp_s, t32_s, rp_s, c32_s, m_s, x_s, vbd_s, v_s, op_s, *, seq_len, n_sub):
    n_chunk = seq_len // CHUNK
    n_pair = H_A // 2
    n_unit = n_sub * H_A
    half = HEAD_DIM
    lo = _lane((CHUNK, LANES)) < half
    lo_row = _lane((1, LANES)) < half
    lo16 = _lane((CHUNK, LANES)).astype(F32).astype(BF16) < half
    row = lax.broadcasted_iota(jnp.int32, (CHUNK, LANES), 0)
    col = _lane((CHUNK, LANES)) % half
    ahead = jnp.where(lo, row - col, col - row)
    incl = ahead >= 0
    strict = ahead > 0
    diag_blk = row // SOLVE_BLOCK == col // SOLVE_BLOCK
    big_row = lax.broadcasted_iota(jnp.int32, (LANES, LANES), 0)
    anti = big_row // half + _lane((LANES, LANES)) // half == 1
    eye = (big_row == _lane((LANES, LANES))).astype(F32)

    def both(a):
        return jnp.concatenate([jnp.where(lo, a, 0.0), jnp.where(lo, 0.0, a)], axis=0)

    rb = 128
    lo_rb = _lane((rb, LANES)) < half
    zero8 = jnp.zeros((8, 3 * A_W), F32)
    pad_s[0:8, :] = zero8
    pad_s[8 + seq_len:16 + seq_len, :] = zero8

    def l2n(x):
        return x * lax.rsqrt(0.5 * jnp.sum(x * x, axis=-1, keepdims=True) + EPS)

    for sq in range(n_sub):
        pad_s[8:8 + seq_len, :] = qkv_ref[sq * seq_len:(sq + 1) * seq_len, :]
        for r0 in range(0, seq_len, rb):
            y = _silu(pad_s[7 + r0:7 + r0 + rb, :] * cw_ref[0:1, :] + pad_s[8 + r0:8 + r0 + rb, :] * cw_ref[1:2, :]
                      + pad_s[9 + r0:9 + r0 + rb, :] * cw_ref[2:3, :])

            def dup(part, p):
                c = y[:, part * A_W + p * LANES:part * A_W + (p + 1) * LANES]
                r = pltpu.roll(c, half, axis=1)
                return jnp.where(lo_rb, c, r), jnp.where(lo_rb, r, c)

            for p in range(n_pair):
                qs, ks, vs = dup(0, p), dup(1, p), dup(2, p)
                for e in range(2):
                    u = sq * H_A + 2 * p + e
                    kk = l2n(ks[e])
                    kk_s[u, r0:r0 + rb, :] = kk
                    qq_s[u, r0:r0 + rb, :] = l2n(qs[e]) * (HEAD_DIM ** -0.5)
                    vk_s[u, r0:r0 + rb, :] = jnp.where(lo_rb, vs[e], kk)

    cb = 256
    ci = lax.broadcasted_iota(jnp.int32, (cb, cb), 0)
    cj = lax.broadcasted_iota(jnp.int32, (cb, cb), 1)
    same = ci // CHUNK == cj // CHUNK
    m_pre = jnp.where(same & (ci >= cj), 1.0, 0.0).astype(BF16)
    m_suf = jnp.where(same & (ci <= cj), 1.0, 0.0).astype(BF16)
    fwd_cols = (_lane((cb, LANES)) >= GATE_A) & (_lane((cb, LANES)) < GATE_A + H_A)
    for r0 in range(0, n_sub * seq_len, cb):
        gt = gate_ref[r0:r0 + cb, :]
        beta_s[r0:r0 + cb, :] = jax.nn.sigmoid(gt)
        g = -jnp.exp(alog_ref[...]) * _softplus(gt + dtb_ref[...])
        g1 = g.astype(BF16)
        r1 = g - g1.astype(F32)
        g2 = r1.astype(BF16)
        g3 = (r1 - g2.astype(F32)).astype(BF16)
        blk = jnp.concatenate([g1, g2, g3], axis=1)
        pre, suf = _dot(m_pre, blk), _dot(m_suf, blk)
        pre = pre[:, :LANES] + pre[:, LANES:2 * LANES] + pre[:, 2 * LANES:]
        suf = suf[:, :LANES] + suf[:, LANES:2 * LANES] + suf[:, 2 * LANES:]
        gc_s[r0:r0 + cb, :] = jnp.where(fwd_cols, pre, suf)

    zero = jnp.zeros((half, half), F32)
    for sq in range(n_sub):
        for h in range(H_A):
            sa_s[sq * H_A + h] = jnp.concatenate([jnp.concatenate([zero, s0_ref[sq, 1, h]], axis=1),
                                                  jnp.concatenate([s0_ref[sq, 0, h], zero], axis=1)], axis=0)

    n_fac = int(math.log2(SOLVE_BLOCK))

    def prep(c, carry):
        rows = pl.ds(pl.multiple_of(c * CHUNK, CHUNK), CHUNK)
        for sq in range(n_sub):
            grow = pl.ds(pl.multiple_of(sq * seq_len + c * CHUNK, CHUNK), CHUNK)
            gcb = gc_s[grow, :]
            gct = gcb.T
            bb = beta_s[grow, :]
            for h in range(H_A):
                u = sq * H_A + h
                full = (CHUNK, LANES)
                cff = jnp.broadcast_to(gcb[:, GATE_A + h:GATE_A + h + 1], full)
                cbb = jnp.broadcast_to(gcb[:, GATE_A + H_A + h:GATE_A + H_A + h + 1], full)
                bff = jnp.broadcast_to(bb[:, GATE_BETA + h:GATE_BETA + h + 1], full)
                bbb = jnp.broadcast_to(bb[:, GATE_BETA + H_A + h:GATE_BETA + H_A + h + 1], full)
                gr_fb = jnp.concatenate([gct[GATE_A + h:GATE_A + h + 1, :],
                                         gct[GATE_A + H_A + h:GATE_A + H_A + h + 1, :]], axis=1)
                decay = jnp.where(incl, jnp.exp(jnp.where(incl, jnp.where(lo, cff, cbb) - gr_fb, 0.0)), 0.0)
                egf, egb = jnp.exp(cff), jnp.exp(cbb)
                tot_f = gcb[CHUNK - 1:CHUNK, GATE_A + h:GATE_A + h + 1]
                tot_b = gcb[0:1, GATE_A + H_A + h:GATE_A + H_A + h + 1]
                kk, qq, vk = kk_s[u, rows, :], qq_s[u, rows, :], vk_s[u, rows, :]
                k16 = kk[:, :half].astype(BF16)
                kq = jnp.concatenate([k16, qq[:, :half].astype(BF16)], axis=0)
                gram = _dot_nt(kq, jnp.concatenate([k16, k16], axis=0))
                neg_l = jnp.where(strict, gram[:CHUNK] * jnp.where(lo, -bff, -bbb) * decay, 0.0)
                a_s[u, rows, :] = jnp.where(incl, gram[CHUNK:] * decay, 0.0).astype(BF16)
                neg_d = jnp.where(diag_blk, neg_l, 0.0)
                pd = both(neg_d)
                t32_s[u] = eye + pd
                tp_s[u, :, :LANES] = (eye + pd).astype(BF16)
                tp_s[u, :, LANES:] = pd.astype(BF16)
                x_f = vk * jnp.where(lo, bff, bff * egf)
                x_b = vk * jnp.where(lo, bbb, bbb * egb)
                rp_s[u, :, :LANES] = jnp.concatenate([x_f, x_b], axis=0).astype(BF16)
                rp_s[u, :, LANES:] = both(neg_l - neg_d).astype(BF16)
                qd_s[u, rows, :] = (qq * jnp.where(lo, egb, egf)).astype(BF16)
                kd = kk * jnp.exp(jnp.where(lo, tot_b - cbb, tot_f - cff))
                kdt_s[u, c] = kd.T.astype(BF16)
                egt_s[u, c] = jnp.exp(jnp.where(lo_row, tot_f, tot_b))
        for u in range(n_unit):
            tp_s[u, :, LANES:] = _dot(tp_s[u, :, LANES:], tp_s[u, :, LANES:]).astype(BF16)
        for j in range(1, n_fac):
            for u in range(n_unit):
                if j + 1 < n_fac:
                    tp = _dot(tp_s[u, :, LANES:], tp_s[u])
                    t = t32_s[u] + tp[:, :LANES]
                    t32_s[u] = t
                    tp_s[u, :, :LANES] = t.astype(BF16)
                    tp_s[u, :, LANES:] = tp[:, LANES:].astype(BF16)
                else:
                    t = t32_s[u] + _dot(tp_s[u, :, LANES:], tp_s[u, :, :LANES])
                    tp_s[u, :, :LANES] = t.astype(BF16)
        for u in range(n_unit):
            cm = _dot(tp_s[u, :, :LANES], rp_s[u])
            c32_s[u] = cm[:, :LANES]
            x_s[u] = cm[:, :LANES].astype(BF16)
            m_s[u] = cm[:, LANES:].astype(BF16)
        n_sweep = CHUNK // SOLVE_BLOCK - 1
        for it in range(n_sweep):
            for u in range(n_unit):
                x = c32_s[u] + _dot(m_s[u], x_s[u])
                if it + 1 < n_sweep:
                    x_s[u] = x.astype(BF16)
                else:
                    x_f, x_b = x[:CHUNK], pltpu.roll(x[CHUNK:], half, axis=1)
                    u_s[u, rows, :] = jnp.where(lo, x_f, x_b)
                    w_s[u, rows, :] = jnp.where(lo, x_b, x_f).astype(BF16)
        return carry

    lax.fori_loop(0, n_chunk, prep, 0)

    def scan(s, carry):
        rf = pl.ds(pl.multiple_of(s * CHUNK, CHUNK), CHUNK)
        sb = n_chunk - 1 - s
        rbk = pl.ds(pl.multiple_of(sb * CHUNK, CHUNK), CHUNK)
        for u in range(n_unit):
            w = jnp.where(lo16, w_s[u, rbk, :], w_s[u, rf, :])
            qd = jnp.where(lo16, qd_s[u, rbk, :], qd_s[u, rf, :])
            wq = _dot(jnp.concatenate([w, qd], axis=0), sa_s[u].astype(BF16))
            v = jnp.where(lo, u_s[u, rf, :], u_s[u, rbk, :]) - wq[:CHUNK]
            v_s[u] = v.astype(BF16)
            vbd_s[u] = both(v).astype(BF16)
            op_s[u] = wq[CHUNK:]
        for u in range(n_unit):
            a = jnp.where(lo16, a_s[u, rf, :], a_s[u, rbk, :])
            o = op_s[u] + _dot(a, vbd_s[u])
            o_s[u, rf, 0:half] = o[:, :half]
            o_s[u, rbk, half:LANES] = o[:, half:]
            kdt = jnp.concatenate([kdt_s[u, sb][:half], kdt_s[u, s][half:]], axis=0)
            egt = jnp.where(lo_row, egt_s[u, s], egt_s[u, sb])
            sa_s[u] = sa_s[u] * egt + jnp.where(anti, _dot(kdt, v_s[u]), 0.0)
        return carry

    lax.fori_loop(0, n_chunk, scan, 0)

    for sq in range(n_sub):
        for r0 in range(0, seq_len, rb):
            tr = sq * seq_len + r0
            for p in range(n_pair):
                nrm = []
                for e in range(2):
                    ofb = o_s[sq * H_A + 2 * p + e, r0:r0 + rb, :]
                    oo = ofb + pltpu.roll(ofb, half, axis=1)
                    nrm.append(oo * lax.rsqrt(jnp.sum(oo * oo, axis=-1, keepdims=True) * (0.5 / HEAD_DIM) + EPS))
                o_ref[tr:tr + rb, p * LANES:(p + 1) * LANES] = (
                    jnp.where(lo_rb, nrm[0], nrm[1]) * ng_ref[...]
                    * _silu(z_ref[tr:tr + rb, p * LANES:(p + 1) * LANES]))
        for h in range(H_A):
            st = sa_s[sq * H_A + h]
            sfin_ref[sq, 0, h] = st[half:, :half]
            sfin_ref[sq, 1, h] = st[:half, half:]


def delta_mixer(qkv, z, gate, cw, alog_row, dtb_row, ng2, s0, *, n_seq, seq_len, tok0, n_sub):
    n_chunk = seq_len // CHUNK
    n_unit = n_sub * H_A
    blk = n_sub * seq_len
    b0 = tok0 // blk
    tok = lambda w: pl.BlockSpec((blk, w), lambda b: (b0 + b, 0))
    full = lambda a: pl.BlockSpec(a.shape, lambda b: (0,) * a.ndim)
    state = pl.BlockSpec((n_sub, 2, H_A, HEAD_DIM, HEAD_DIM), lambda b: (b, 0, 0, 0, 0))
    per_unit = lambda dt: pltpu.VMEM((n_unit, seq_len, LANES), dt)
    stage = lambda rows, cols, dt: pltpu.VMEM((n_unit, rows, cols), dt)
    return pl.pallas_call(
        functools.partial(_delta_kernel, seq_len=seq_len, n_sub=n_sub),
        grid=(n_seq // n_sub,),
        in_specs=[tok(3 * A_W), tok(A_W), tok(LANES), full(cw), full(alog_row), full(dtb_row), full(ng2), state],
        out_specs=[pl.BlockSpec((blk, A_W), lambda b: (b, 0)), state],
        out_shape=[jax.ShapeDtypeStruct((n_seq * seq_len, A_W), F32),
                   jax.ShapeDtypeStruct((n_seq, 2, H_A, HEAD_DIM, HEAD_DIM), F32)],
        scratch_shapes=[pltpu.VMEM((seq_len + 16, 3 * A_W), F32),
                        per_unit(F32), per_unit(F32), per_unit(F32),
                        pltpu.VMEM((blk, LANES), F32),
                        pltpu.VMEM((blk, LANES), F32),
                        per_unit(F32), per_unit(BF16), per_unit(BF16), per_unit(BF16),
                        pltpu.VMEM((n_unit, n_chunk, LANES, HEAD_DIM), BF16),
                        pltpu.VMEM((n_unit, n_chunk, 1, LANES), F32),
                        stage(LANES, LANES, F32),
                        per_unit(F32),
                        stage(LANES, 2 * LANES, BF16),
                        stage(LANES, LANES, F32),
                        stage(LANES, 2 * LANES, BF16),
                        stage(LANES, LANES, F32),
                        stage(LANES, LANES, BF16),
                        stage(LANES, LANES, BF16),
                        stage(LANES, LANES, BF16),
                        stage(CHUNK, LANES, BF16),
                        stage(CHUNK, LANES, F32)],
        compiler_params=_cparams(("arbitrary",)),
        name=f"delta_mixer_{seq_len}",
    )(qkv, z, gate, cw, alog_row, dtb_row, ng2, s0)


def _rope(x, cos, sin_signed, quarter):
    width = x.shape[-1]
    swapped = jnp.where((_lane(x.shape) % (2 * quarter)) < quarter,
                        pltpu.roll(x, width - quarter, axis=1), pltpu.roll(x, quarter, axis=1))
    return x * cos + swapped * sin_signed


def _rope_tables(rows, dim):
    nf = dim // 4
    inv = ROPE_THETA ** (-jnp.arange(nf, dtype=F32) / nf)
    r = jnp.repeat(jnp.arange(rows, dtype=F32), GRID_W)
    c = jnp.tile(jnp.arange(GRID_W, dtype=F32), rows)
    ar, ac = r[:, None] * inv, c[:, None] * inv
    cos = jnp.concatenate([jnp.cos(ar), jnp.cos(ar), jnp.cos(ac), jnp.cos(ac)], axis=1)
    sin = jnp.concatenate([-jnp.sin(ar), jnp.sin(ar), -jnp.sin(ac), jnp.sin(ac)], axis=1)
    return jnp.tile(cos, (1, LANES // dim)), jnp.tile(sin, (1, LANES // dim))


def _softmax_rows(s):
    p = jnp.exp(s - jnp.max(s, axis=-1, keepdims=True))
    return p / jnp.sum(p, axis=-1, keepdims=True)


def _gqa_kernel(*refs, cached):
    if cached:
        (q_ref, k_ref, v_ref, gq_ref, gk_ref, ck_ref, cv_ref, cosq_ref, sinq_ref, cosk_ref, sink_ref,
         o_ref) = refs
    else:
        q_ref, k_ref, v_ref, gq_ref, gk_ref, o_ref, kn_ref = refs
    gmat = _group_mean_matrix(LANES, HEAD_DIM)
    quarter = HEAD_DIM // 4
    k = _group_rms(k_ref[...], gmat, gk_ref[...])
    v = v_ref[...]
    if cached:
        k = _rope(k, cosk_ref[...], sink_ref[...], quarter)
        k = jnp.concatenate([ck_ref[...], k], axis=0)
        v = jnp.concatenate([cv_ref[...], v], axis=0)
    else:
        kn_ref[...] = k
    k = k.astype(BF16)
    v = v.astype(BF16)
    tq = q_ref.shape[0]
    lo = _lane((tq, LANES)) < HEAD_DIM
    group = H_B // KV_B
    for j in range(H_B // 2):
        q = _group_rms(q_ref[:, j * LANES:(j + 1) * LANES], gmat, gq_ref[...])
        if cached:
            q = _rope(q, cosq_ref[...], sinq_ref[...], quarter)
        q_sw = pltpu.roll(q, HEAD_DIM, axis=1)
        outs = []
        for e in range(2):
            kv = (2 * j + e) // group
            qh = jnp.where(lo if kv == 0 else ~lo, q if e == kv else q_sw, 0.0)
            p = _softmax_rows(_dot_nt(qh.astype(BF16), k) * (HEAD_DIM ** -0.5))
            r = _dot(p.astype(BF16), v)
            outs.append(r if e == kv else pltpu.roll(r, HEAD_DIM, axis=1))
        o_ref[:, j * LANES:(j + 1) * LANES] = jnp.where(lo, outs[0], outs[1])


def gqa_mixer(q, k, v, gq, gk, *, n_seq, seq_len, tok0, cache=None, rope=None, layer=0):
    tq = 256
    nq = seq_len // tq
    b0 = tok0 // seq_len
    q0 = tok0 // tq
    cached = cache is not None
    kvw = KV_B * HEAD_DIM
    in_specs = [pl.BlockSpec((tq, B_W), lambda b, i: (q0 + b * nq + i, 0)),
                pl.BlockSpec((seq_len, kvw), lambda b, i: (b0 + b, 0)),
                pl.BlockSpec((seq_len, kvw), lambda b, i: (b0 + b, 0)),
                pl.BlockSpec((1, LANES), lambda b, i: (0, 0)),
                pl.BlockSpec((1, LANES), lambda b, i: (0, 0))]
    args = [q, k, v, gq, gk]
    out_specs = [pl.BlockSpec((tq, B_W), lambda b, i: (b * nq + i, 0))]
    out_shape = [jax.ShapeDtypeStruct((n_seq * seq_len, B_W), F32)]
    if cached:
        cos, sin = rope
        cache_spec = pl.BlockSpec((None, None, PAST_LEN, kvw), lambda b, i: (b, layer, 0, 0))
        in_specs += [cache_spec, cache_spec,
                     pl.BlockSpec((tq, LANES), lambda b, i: (i, 0)),
                     pl.BlockSpec((tq, LANES), lambda b, i: (i, 0)),
                     pl.BlockSpec((seq_len, LANES), lambda b, i: (0, 0)),
                     pl.BlockSpec((seq_len, LANES), lambda b, i: (0, 0))]
        args += [cache[0], cache[1], cos, sin, cos, sin]
    else:
        out_specs.append(pl.BlockSpec((seq_len, kvw), lambda b, i: (b, 0)))
        out_shape.append(jax.ShapeDtypeStruct((n_seq * seq_len, kvw), F32))
    return pl.pallas_call(
        functools.partial(_gqa_kernel, cached=cached),
        grid=(n_seq, nq),
        in_specs=in_specs, out_specs=out_specs, out_shape=out_shape,
        compiler_params=_cparams(("arbitrary", "arbitrary")),
        name=f"gqa_mixer_{seq_len}",
    )(*args)


def _diff_kernel(*refs, cached, lam_init):
    if cached:
        (q_ref, k_ref, v_ref, lam_ref, gn_ref, ck_ref, cv_ref, cosq_ref, sinq_ref, cosk_ref, sink_ref,
         o_ref) = refs
    else:
        q_ref, k_ref, v_ref, lam_ref, gn_ref, o_ref = refs
    quarter = DIFF_HD // 4
    lv = lam_ref[...]
    lam = (jnp.exp(jnp.sum(lv[0:1] * lv[1:2], axis=-1, keepdims=True))
           - jnp.exp(jnp.sum(lv[2:3] * lv[3:4], axis=-1, keepdims=True)) + lam_init)
    q, k, v = q_ref[...], k_ref[...], v_ref[...]
    if cached:
        cq = jnp.concatenate([cosq_ref[...]] * 2, axis=1)
        sq = jnp.concatenate([sinq_ref[...]] * 2, axis=1)
        ck = jnp.concatenate([cosk_ref[...]] * 2, axis=1)
        sk = jnp.concatenate([sink_ref[...]] * 2, axis=1)
        q = _rope(q, cq, sq, quarter)
        k = _rope(k, ck, sk, quarter)
        k = jnp.concatenate([ck_ref[...], k], axis=0)
        v = jnp.concatenate([cv_ref[...], v], axis=0)
    k = k.astype(BF16)
    v = v.astype(BF16)
    lane = _lane(q.shape)
    o = jnp.zeros(q.shape, F32)
    for h in range(H_C):
        probs = []
        for m in range(2):
            qm = jnp.where(lane // DIFF_HD == 2 * h + m, q, 0.0)
            probs.append(_softmax_rows(_dot_nt(qm.astype(BF16), k) * (DIFF_HD ** -0.5)))
        a = probs[0] - lam * probs[1]
        o = jnp.where(lane // (2 * DIFF_HD) == h, _dot(a.astype(BF16), v), o)
    o_ref[...] = _group_rms(o, _group_mean_matrix(C_W, 2 * DIFF_HD), gn_ref[...]) * (1.0 - lam_init)


def diff_mixer(q, k, v, lam_p, gn, lam_init, *, n_seq, seq_len, tok0, cache=None, rope=None, layer=0):
    tq = 256
    nq = seq_len // tq
    b0 = tok0 // seq_len
    q0 = tok0 // tq
    cached = cache is not None
    in_specs = [pl.BlockSpec((tq, C_W), lambda b, i: (q0 + b * nq + i, 0)),
                pl.BlockSpec((seq_len, C_W), lambda b, i: (b0 + b, 0)),
                pl.BlockSpec((seq_len, C_W), lambda b, i: (b0 + b, 0)),
                pl.BlockSpec((4, DIFF_HD), lambda b, i: (0, 0)),
                pl.BlockSpec((1, C_W), lambda b, i: (0, 0))]
    args = [q, k, v, lam_p, gn]
    if cached:
        cos, sin = rope
        cache_spec = pl.BlockSpec((None, None, PAST_LEN, C_W), lambda b, i: (b, layer, 0, 0))
        in_specs += [cache_spec, cache_spec,
                     pl.BlockSpec((tq, LANES), lambda b, i: (i, 0)),
                     pl.BlockSpec((tq, LANES), lambda b, i: (i, 0)),
                     pl.BlockSpec((seq_len, LANES), lambda b, i: (0, 0)),
                     pl.BlockSpec((seq_len, LANES), lambda b, i: (0, 0))]
        args += [cache[0], cache[1], cos, sin, cos, sin]
    return pl.pallas_call(
        functools.partial(_diff_kernel, cached=cached, lam_init=lam_init),
        grid=(n_seq, nq),
        in_specs=in_specs,
        out_specs=pl.BlockSpec((tq, C_W), lambda b, i: (b * nq + i, 0)),
        out_shape=jax.ShapeDtypeStruct((n_seq * seq_len, C_W), F32),
        compiler_params=_cparams(("arbitrary", "arbitrary")),
        name=f"diff_mixer_{seq_len}",
    )(*args)


def _outproj_kernel(*refs, routed, n_mix):
    x_ref = refs[0]
    mix_refs = refs[1:1 + 2 * n_mix]
    rest = refs[1 + 2 * n_mix:]
    if routed:
        w_ref, gm_ref, g_ref, sh_ref, sc_ref, r_ref, x1_ref, h_ref, lg_ref = rest
    else:
        w_ref, gm_ref, g_ref, sh_ref, sc_ref, x1_ref, h_ref = rest
    is_ctx = pl.program_id(0) < N_CTX // x_ref.shape[0]
    mixed = jnp.concatenate(
        [jnp.where(is_ctx, mix_refs[2 * m][...], mix_refs[2 * m + 1][...]) for m in range(n_mix)], axis=1)
    x1 = x_ref[...] + gm_ref[...] * _dot(mixed.astype(BF16), w_ref[...])
    x1_ref[...] = x1
    h = _rms(x1, g_ref[...]) * (1.0 + sc_ref[...]) + sh_ref[...]
    h_ref[...] = h.astype(BF16)
    if routed:
        lg_ref[...] = _dot_hi(h, r_ref[...])


def out_projection(x, mixes, w_out_b, mod, norm_g, l, router_p=None):
    tm = 512
    n_ctx_tiles = N_CTX // tm
    n_lat_tiles = N_LAT // tm
    routed = router_p is not None
    mod_spec = lambda k: pl.BlockSpec((None, None, 1, D_MODEL), lambda i: (l, _cond_row(i, tm), 0, k))
    in_specs = [pl.BlockSpec((tm, D_MODEL), lambda i: (i, 0))]
    args = [x]
    for ctx_arr, lat_arr in mixes:
        w = ctx_arr.shape[1]
        in_specs.append(pl.BlockSpec((tm, w), lambda i: (jnp.minimum(i, n_ctx_tiles - 1), 0)))
        in_specs.append(pl.BlockSpec((tm, w), lambda i: (jnp.clip(i - n_ctx_tiles, 0, n_lat_tiles - 1), 0)))
        args += [ctx_arr, lat_arr]
    in_specs += [pl.BlockSpec((None, D_MODEL, D_MODEL), lambda i: (l, 0, 0)),
                 mod_spec(2),
                 pl.BlockSpec((None, 1, D_MODEL), lambda i: (l, 0, 0)),
                 mod_spec(3), mod_spec(4)]
    args += [w_out_b, mod, norm_g, mod, mod]
    out_specs = [pl.BlockSpec((tm, D_MODEL), lambda i: (i, 0)),
                 pl.BlockSpec((tm, D_MODEL), lambda i: (i, 0))]
    out_shape = [jax.ShapeDtypeStruct((N_TOK, D_MODEL), F32),
                 jax.ShapeDtypeStruct((N_TOK, D_MODEL), BF16)]
    if routed:
        in_specs.append(pl.BlockSpec((D_MODEL, LANES), lambda i: (0, 0)))
        args.append(router_p)
        out_specs.append(pl.BlockSpec((tm, LANES), lambda i: (i, 0)))
        out_shape.append(jax.ShapeDtypeStruct((N_TOK, LANES), F32))
    return pl.pallas_call(
        functools.partial(_outproj_kernel, routed=routed, n_mix=len(mixes)),
        grid=(N_TOK // tm,),
        in_specs=in_specs, out_specs=out_specs, out_shape=out_shape,
        compiler_params=_cparams(("arbitrary",)),
        name="out_projection",
    )(*args)


def _top2_gate(logits):
    lane = _lane(logits.shape)
    neg = jnp.float32(-jnp.inf)
    lg = jnp.where(lane < N_EXP, logits, neg)
    t1 = jnp.max(lg, axis=-1, keepdims=True)
    i1 = jnp.min(jnp.where(lg == t1, lane, LANES), axis=-1, keepdims=True)
    lg2 = jnp.where(lane == i1, neg, lg)
    t2 = jnp.max(lg2, axis=-1, keepdims=True)
    i2 = jnp.min(jnp.where(lg2 == t2, lane, LANES), axis=-1, keepdims=True)
    e2 = jnp.exp(t2 - t1)
    w1 = 1.0 / (1.0 + e2)
    w2 = e2 / (1.0 + e2)
    return jnp.where(lane == i1, w1, 0.0) + jnp.where(lane == i2, w2, 0.0)


def _ffn_kernel(*refs, routed):
    if routed:
        h_ref, wg_ref, wu_ref, wd_ref, x_ref, gf_ref, lg_ref, o_ref, acc_ref = refs
    else:
        h_ref, wg_ref, wu_ref, wd_ref, x_ref, gf_ref, o_ref, acc_ref = refs
    j = pl.program_id(1)

    @pl.when(j == 0)
    def _():
        acc_ref[...] = jnp.zeros_like(acc_ref)

    h = h_ref[...]
    act = _silu(_dot(h, wg_ref[...])) * _dot(h, wu_ref[...])
    if routed:
        gate = _top2_gate(lg_ref[...])
        act = act * jnp.sum(jnp.where(_lane(gate.shape) == j, gate, 0.0), axis=-1, keepdims=True)
    acc_ref[...] += _dot(act.astype(BF16), wd_ref[...])

    @pl.when(j == pl.num_programs(1) - 1)
    def _():
        o_ref[...] = x_ref[...] + gf_ref[...] * acc_ref[...]


def ffn(h, wg, wu, wd, x1, mod, l, logits=None):
    tm = 512
    n_chunk = wg.shape[0]
    routed = logits is not None
    in_specs = [pl.BlockSpec((tm, D_MODEL), lambda i, j: (i, 0)),
                pl.BlockSpec((None, D_MODEL, FF_CHUNK), lambda i, j: (j, 0, 0)),
                pl.BlockSpec((None, D_MODEL, FF_CHUNK), lambda i, j: (j, 0, 0)),
                pl.BlockSpec((None, FF_CHUNK, D_MODEL), lambda i, j: (j, 0, 0)),
                pl.BlockSpec((tm, D_MODEL), lambda i, j: (i, 0)),
                pl.BlockSpec((None, None, 1, D_MODEL), lambda i, j: (l, _cond_row(i, tm), 0, 5))]
    args = [h, wg, wu, wd, x1, mod]
    if routed:
        in_specs.append(pl.BlockSpec((tm, LANES), lambda i, j: (i, 0)))
        args.append(logits)
    return pl.pallas_call(
        functools.partial(_ffn_kernel, routed=routed),
        grid=(N_TOK // tm, n_chunk),
        in_specs=in_specs,
        out_specs=pl.BlockSpec((tm, D_MODEL), lambda i, j: (i, 0)),
        out_shape=jax.ShapeDtypeStruct((N_TOK, D_MODEL), F32),
        scratch_shapes=[pltpu.VMEM((tm, D_MODEL), F32)],
        compiler_params=_cparams(("arbitrary", "arbitrary")),
        name="ffn_routed" if routed else "ffn_dense",
    )(*args)


def _final_norm_kernel(x_ref, g_ref, o_ref):
    o_ref[...] = _rms(x_ref[...], g_ref[...])


def final_norm_call(x, g):
    tm = 1024
    return pl.pallas_call(
        _final_norm_kernel,
        grid=(N_TOK // tm,),
        in_specs=[pl.BlockSpec((tm, D_MODEL), lambda i: (i, 0)),
                  pl.BlockSpec((1, D_MODEL), lambda i: (0, 0))],
        out_specs=pl.BlockSpec((tm, D_MODEL), lambda i: (i, 0)),
        out_shape=jax.ShapeDtypeStruct((N_TOK, D_MODEL), F32),
        compiler_params=_cparams(("arbitrary",)),
        name="final_norm",
    )(x, g)


def _gate_row(p):
    return jnp.zeros((1, LANES), F32).at[0, GATE_A:GATE_A + 2 * H_A].set(p.reshape(-1))


def _tile_lanes(g, width):
    return jnp.tile(g, width // g.shape[0])[None, :]


def kernel(x_prompt, x_sample, state_delta, cache_gqa_k, cache_gqa_v, cache_diff_k, cache_diff_v, c, c_ctx,
           ada_w, ada_b, norm_mix, norm_ffn, w_in, conv_w, dn_a_log, dn_dt_bias, dn_norm, gqa_q_norm,
           gqa_k_norm, diff_lambda, diff_norm, w_out, ffn_w_gu, ffn_w_down, moe_router, moe_w_gu, moe_w_down,
           final_norm):
    cond8 = jnp.concatenate([c_ctx[None, :], c, jnp.zeros((8 - N_COND, D_MODEL), F32)], axis=0)
    gate_lo = 4 * A_W
    gate_hi = gate_lo + 4 * H_A
    w_in_p = jnp.concatenate(
        [w_in[:, :, :gate_lo], w_in[:, :, gate_hi:], w_in[:, :, gate_lo:gate_hi],
         jnp.zeros((DEPTH, D_MODEL, LANES - 4 * H_A), F32)], axis=-1).astype(BF16)
    w_out_b = w_out.astype(BF16)
    norm_mix3 = norm_mix.reshape(DEPTH, 1, D_MODEL)
    norm_ffn3 = norm_ffn.reshape(DEPTH, 1, D_MODEL)
    n_ff = D_FF // FF_CHUNK
    ffn_g = ffn_w_gu[:, :, :D_FF].reshape(-1, D_MODEL, n_ff, FF_CHUNK).transpose(0, 2, 1, 3).astype(BF16)
    ffn_u = ffn_w_gu[:, :, D_FF:].reshape(-1, D_MODEL, n_ff, FF_CHUNK).transpose(0, 2, 1, 3).astype(BF16)
    ffn_d = ffn_w_down.reshape(-1, n_ff, FF_CHUNK, D_MODEL).astype(BF16)
    moe_g = moe_w_gu[:, :, :, :D_FF_E].astype(BF16)
    moe_u = moe_w_gu[:, :, :, D_FF_E:].astype(BF16)
    moe_d = moe_w_down.astype(BF16)
    router_p = jnp.pad(moe_router, ((0, 0), (0, 0), (0, LANES - N_EXP)))
    rope_b = _rope_tables(DEC_SEQ // GRID_W, HEAD_DIM)
    rope_c = _rope_tables(DEC_SEQ // GRID_W, DIFF_HD)
    s0_ctx = jnp.zeros((BATCH, 2, H_A, HEAD_DIM, HEAD_DIM), F32)
    cache_b = (cache_gqa_k.reshape(DEC_BATCH, DEPTH, PAST_LEN, KV_B * HEAD_DIM),
               cache_gqa_v.reshape(DEC_BATCH, DEPTH, PAST_LEN, KV_B * HEAD_DIM))
    cache_c = (cache_diff_k.reshape(DEC_BATCH, DEPTH, PAST_LEN, C_W),
               cache_diff_v.reshape(DEC_BATCH, DEPTH, PAST_LEN, C_W))

    mod = ada_modulation(cond8, ada_w, ada_b)[:, :N_COND].reshape(DEPTH, N_COND, 1, 6 * D_MODEL)

    x = jnp.concatenate([x_prompt.reshape(N_CTX, D_MODEL), x_sample.reshape(N_LAT, D_MODEL)], axis=0)
    states, gk_out, gv_out, dk_out, dv_out = [], [], [], [], []
    ctx = dict(n_seq=BATCH, seq_len=SEQ, tok0=0)
    lat = dict(n_seq=DEC_BATCH, seq_len=DEC_SEQ, tok0=N_CTX)
    for l in range(DEPTH):
        qkv_a, z_a, q_b, k_b, v_b, q_c, k_c, v_c, gate = in_projection(x, mod, norm_mix3, w_in_p, l)

        dn_args = (qkv_a, z_a, gate, conv_w[l], _gate_row(dn_a_log[l]), _gate_row(dn_dt_bias[l]),
                   _tile_lanes(dn_norm[l], LANES))
        oa_ctx, s_fin = delta_mixer(*dn_args, s0_ctx, n_sub=2, **ctx)
        oa_lat, _ = delta_mixer(*dn_args, state_delta[:, l], n_sub=1, **lat)

        gq, gk = _tile_lanes(gqa_q_norm[l], LANES), _tile_lanes(gqa_k_norm[l], LANES)
        ob_ctx, kn_ctx = gqa_mixer(q_b, k_b, v_b, gq, gk, **ctx)
        (ob_lat,) = gqa_mixer(q_b, k_b, v_b, gq, gk, cache=cache_b, rope=rope_b, layer=l, **lat)

        lam_init = 0.8 - 0.6 * math.exp(-0.3 * l)
        gn = _tile_lanes(diff_norm[l], C_W)
        oc_ctx = diff_mixer(q_c, k_c, v_c, diff_lambda[l], gn, lam_init, **ctx)
        oc_lat = diff_mixer(q_c, k_c, v_c, diff_lambda[l], gn, lam_init, cache=cache_c, rope=rope_c, layer=l, **lat)

        mixes = ((oa_ctx, oa_lat), (ob_ctx, ob_lat), (oc_ctx, oc_lat))
        if l % 2 == 0:
            x1, h2 = out_projection(x, mixes, w_out_b, mod, norm_ffn3, l)
            x = ffn(h2, ffn_g[l // 2], ffn_u[l // 2], ffn_d[l // 2], x1, mod, l)
        else:
            x1, h2, logits = out_projection(x, mixes, w_out_b, mod, norm_ffn3, l, router_p[l // 2])
            x = ffn(h2, moe_g[l // 2], moe_u[l // 2], moe_d[l // 2], x1, mod, l, logits)

        states.append(s_fin)
        gk_out.append(kn_ctx.reshape(BATCH, SEQ, KV_B, HEAD_DIM))
        gv_out.append(v_b[:N_CTX].reshape(BATCH, SEQ, KV_B, HEAD_DIM))
        dk_out.append(k_c[:N_CTX].reshape(BATCH, SEQ, H_C, 2, DIFF_HD))
        dv_out.append(v_c[:N_CTX].reshape(BATCH, SEQ, H_C, 2 * DIFF_HD))

    y = final_norm_call(x, final_norm[None, :])
    y_prompt = y[:N_CTX].reshape(BATCH, SEQ, D_MODEL)
    y_sample = y[N_CTX:].reshape(DEC_BATCH, DEC_SEQ, D_MODEL)
    return (y_prompt, y_sample, jnp.stack(states, axis=1), jnp.stack(gk_out, axis=1),
            jnp.stack(gv_out, axis=1), jnp.stack(dk_out, axis=1), jnp.stack(dv_out, axis=1))
```

```python
import functools
import math

import jax
import jax.numpy as jnp
from jax import lax
from jax.experimental import pallas as pl
from jax.experimental.pallas import tpu as pltpu

D_MODEL = 1024
BATCH = 16
SEQ = 256
DEPTH = 4
DEC_BATCH = 2
DEC_SEQ = 1024
PAST_LEN = 512
GRID_W = 64
HEAD_DIM = 64
H_A = 6
A_W = H_A * HEAD_DIM
H_B = 6
KV_B = 2
B_W = H_B * HEAD_DIM
H_C = 4
DIFF_HD = 32
C_W = H_C * 2 * DIFF_HD
CONV_W = 3
CHUNK = 64
ROPE_THETA = 10000.0
D_FF = 2816
N_EXP = 8
D_FF_E = 1408
EPS = 1e-6

N_CTX = BATCH * SEQ
N_LAT = DEC_BATCH * DEC_SEQ
N_TOK = N_CTX + N_LAT
N_COND = 1 + DEC_BATCH
IN_PAD = 3072
FF_CHUNK = D_FF_E
LANES = 128
VMEM_LIMIT = 56 * 1024 * 1024

IN_SPLIT = (("qkv_a", 3 * A_W), ("z_a", A_W), ("q_b", B_W), ("k_b", KV_B * HEAD_DIM), ("v_b", KV_B * HEAD_DIM),
            ("q_c", C_W), ("k_c", C_W), ("v_c", C_W), ("gate", LANES))
GATE_BETA, GATE_A = 0, 2 * H_A

F32 = jnp.float32
BF16 = jnp.bfloat16
HI = lax.Precision.HIGHEST


def _cparams(sem):
    return pltpu.CompilerParams(dimension_semantics=sem, vmem_limit_bytes=VMEM_LIMIT)


def _dot(a, b):
    return jnp.dot(a, b, preferred_element_type=F32)


def _dot_hi(a, b):
    return jnp.dot(a, b, preferred_element_type=F32, precision=HI)


def _dot_nt(a, b):
    return lax.dot_general(a, b, (((1,), (1,)), ((), ())), preferred_element_type=F32)


def _silu(x):
    return x * jax.nn.sigmoid(x)


def _softplus(x):
    return jnp.maximum(x, 0.0) + jnp.log1p(jnp.exp(-jnp.abs(x)))


def _rms(x, g):
    return x * lax.rsqrt(jnp.mean(x * x, axis=-1, keepdims=True) + EPS) * g


def _lane(shape):
    return lax.broadcasted_iota(jnp.int32, shape, len(shape) - 1)


def _group_mean_matrix(width, group):
    i = lax.broadcasted_iota(jnp.int32, (width, width), 0)
    j = lax.broadcasted_iota(jnp.int32, (width, width), 1)
    return jnp.where(i // group == j // group, 1.0 / group, 0.0).astype(BF16)


def _group_rms(x, gmat, g):
    sq = x * x
    hi = sq.astype(BF16)
    lo = (sq - hi.astype(F32)).astype(BF16)
    ms = _dot(jnp.concatenate([hi, lo], axis=1), jnp.concatenate([gmat, gmat], axis=0))
    return x * lax.rsqrt(ms + EPS) * g


def _cond_row(i, tm):
    return jnp.maximum((i * tm) // DEC_SEQ - (N_CTX // DEC_SEQ - 1), 0)


def _ada_kernel(c_ref, w_ref, b_ref, o_ref):
    o_ref[...] = _dot_hi(_silu(c_ref[...]), w_ref[...]) + b_ref[...]


def ada_modulation(cond8, ada_w, ada_b):
    tn = 1024
    return pl.pallas_call(
        _ada_kernel,
        grid=(DEPTH, 6 * D_MODEL // tn),
        in_specs=[pl.BlockSpec((8, D_MODEL), lambda l, j: (0, 0)),
                  pl.BlockSpec((None, D_MODEL, tn), lambda l, j: (l, 0, j)),
                  pl.BlockSpec((None, 1, tn), lambda l, j: (l, 0, j))],
        out_specs=pl.BlockSpec((None, 8, tn), lambda l, j: (l, 0, j)),
        out_shape=jax.ShapeDtypeStruct((DEPTH, 8, 6 * D_MODEL), F32),
        compiler_params=_cparams(("arbitrary", "arbitrary")),
        name="ada_modulation",
    )(cond8, ada_w, ada_b.reshape(DEPTH, 1, 6 * D_MODEL))


def _inproj_kernel(x_ref, g_ref, sh_ref, sc_ref, w_ref, *o_refs):
    h = _rms(x_ref[...], g_ref[...]) * (1.0 + sc_ref[...]) + sh_ref[...]
    acc = _dot(h.astype(BF16), w_ref[...])
    off = 0
    for o_ref, (_, width) in zip(o_refs, IN_SPLIT):
        o_ref[...] = acc[:, off:off + width]
        off += width


def in_projection(x, mod, norm_g, w_in_p, l):
    tm = 512
    return pl.pallas_call(
        _inproj_kernel,
        grid=(N_TOK // tm,),
        in_specs=[pl.BlockSpec((tm, D_MODEL), lambda i: (i, 0)),
                  pl.BlockSpec((None, 1, D_MODEL), lambda i: (l, 0, 0)),
                  pl.BlockSpec((None, None, 1, D_MODEL), lambda i: (l, _cond_row(i, tm), 0, 0)),
                  pl.BlockSpec((None, None, 1, D_MODEL), lambda i: (l, _cond_row(i, tm), 0, 1)),
                  pl.BlockSpec((None, D_MODEL, IN_PAD), lambda i: (l, 0, 0))],
        out_specs=[pl.BlockSpec((tm, w), lambda i: (i, 0)) for _, w in IN_SPLIT],
        out_shape=[jax.ShapeDtypeStruct((N_TOK, w), F32) for _, w in IN_SPLIT],
        compiler_params=_cparams(("arbitrary",)),
        name="in_projection",
    )(x, norm_g, mod, mod, w_in_p)


SOLVE_BLOCK = 16


def _solve_unit_triangular(lmat, rhs, lo, diag_blk, eye):
    def both(a):
        return jnp.concatenate([jnp.where(lo, a, 0.0), jnp.where(lo, 0.0, a)], axis=0)

    neg_d = jnp.where(diag_blk, -lmat, 0.0)
    pd = both(neg_d)
    po = both(-lmat - neg_d)
    t = eye + pd
    p = _dot(pd.astype(BF16), pd.astype(BF16))
    n_fac = int(math.log2(SOLVE_BLOCK))
    for j in range(1, n_fac):
        if j + 1 < n_fac:
            tp = _dot(p.astype(BF16), jnp.concatenate([t, p], axis=1).astype(BF16))
            t = t + tp[:, :LANES]
            p = tp[:, LANES:]
        else:
            t = t + _dot(p.astype(BF16), t.astype(BF16))
    cm = _dot(t.astype(BF16), jnp.concatenate([rhs, po], axis=1).astype(BF16))
    c, m = cm[:, :LANES], cm[:, LANES:].astype(BF16)
    x = c
    for _ in range(CHUNK // SOLVE_BLOCK - 1):
        x = c + _dot(m, x.astype(BF16))
    return x


def _delta_kernel(qkv_ref, z_ref, gate_ref, cw_ref, alog_ref, dtb_ref, ng_ref, s0_ref, o_ref, sfin_ref,
                  pad_s, kk_s, qq_s, vk_s, gc_s, beta_s, u_s, w_s, qd_s, a_s, kdt_s, egt_s, sa_s, o_s,
                  *, seq_len):
    n_chunk = seq_len // CHUNK
    n_pair = H_A // 2
    half = HEAD_DIM
    lo = _lane((CHUNK, LANES)) < half
    lo_row = _lane((1, LANES)) < half
    lo16 = _lane((CHUNK, LANES)).astype(F32).astype(BF16) < half
    row = lax.broadcasted_iota(jnp.int32, (CHUNK, LANES), 0)
    col = _lane((CHUNK, LANES)) % half
    ahead = jnp.where(lo, row - col, col - row)
    incl = ahead >= 0
    strict = ahead > 0
    diag_blk = row // SOLVE_BLOCK == col // SOLVE_BLOCK
    big_row = lax.broadcasted_iota(jnp.int32, (LANES, LANES), 0)
    anti = big_row // half + _lane((LANES, LANES)) // half == 1
    eye = (big_row == _lane((LANES, LANES))).astype(F32)

    rb = 128
    zero8 = jnp.zeros((8, 3 * A_W), F32)
    pad_s[0:8, :] = zero8
    pad_s[8 + seq_len:16 + seq_len, :] = zero8
    pad_s[8:8 + seq_len, :] = qkv_ref[...]
    lo_rb = _lane((rb, LANES)) < half
    for r0 in range(0, seq_len, rb):
        y = _silu(pad_s[7 + r0:7 + r0 + rb, :] * cw_ref[0:1, :] + pad_s[8 + r0:8 + r0 + rb, :] * cw_ref[1:2, :]
                  + pad_s[9 + r0:9 + r0 + rb, :] * cw_ref[2:3, :])

        def dup(part, p):
            c = y[:, part * A_W + p * LANES:part * A_W + (p + 1) * LANES]
            r = pltpu.roll(c, half, axis=1)
            return jnp.where(lo_rb, c, r), jnp.where(lo_rb, r, c)

        def l2n(x):
            return x * lax.rsqrt(0.5 * jnp.sum(x * x, axis=-1, keepdims=True) + EPS)

        for p in range(n_pair):
            qs, ks, vs = dup(0, p), dup(1, p), dup(2, p)
            for e in range(2):
                h = 2 * p + e
                kk = l2n(ks[e])
                kk_s[h, r0:r0 + rb, :] = kk
                qq_s[h, r0:r0 + rb, :] = l2n(qs[e]) * (HEAD_DIM ** -0.5)
                vk_s[h, r0:r0 + rb, :] = jnp.where(lo_rb, vs[e], kk)

    gt = gate_ref[...]
    beta_s[...] = jax.nn.sigmoid(gt)
    g = -jnp.exp(alog_ref[...]) * _softplus(gt + dtb_ref[...])
    g1 = g.astype(BF16)
    r1 = g - g1.astype(F32)
    g2 = r1.astype(BF16)
    g3 = (r1 - g2.astype(F32)).astype(BF16)
    g_split = jnp.concatenate([g1, g2, g3], axis=1)
    cb = 256
    ci = lax.broadcasted_iota(jnp.int32, (cb, cb), 0)
    cj = lax.broadcasted_iota(jnp.int32, (cb, cb), 1)
    same = ci // CHUNK == cj // CHUNK
    m_pre = (same & (ci >= cj)).astype(BF16)
    m_suf = (same & (ci <= cj)).astype(BF16)
    fwd_cols = (_lane((cb, LANES)) >= GATE_A) & (_lane((cb, LANES)) < GATE_A + H_A)
    for r0 in range(0, seq_len, cb):
        blk = g_split[r0:r0 + cb, :]
        pre, suf = _dot(m_pre, blk), _dot(m_suf, blk)
        pre = pre[:, :LANES] + pre[:, LANES:2 * LANES] + pre[:, 2 * LANES:]
        suf = suf[:, :LANES] + suf[:, LANES:2 * LANES] + suf[:, 2 * LANES:]
        gc_s[r0:r0 + cb, :] = jnp.where(fwd_cols, pre, suf)

    for h in range(H_A):
        zero = jnp.zeros((half, half), F32)
        sa_s[h] = jnp.concatenate([jnp.concatenate([zero, s0_ref[1, h]], axis=1),
                                   jnp.concatenate([s0_ref[0, h], zero], axis=1)], axis=0)

    def prep(c, carry):
        rows = pl.ds(pl.multiple_of(c * CHUNK, CHUNK), CHUNK)
        gcb = gc_s[rows, :]
        gct = gcb.T
        bb = beta_s[rows, :]
        for h in range(H_A):
            cf = gcb[:, GATE_A + h:GATE_A + h + 1]
            cbk = gcb[:, GATE_A + H_A + h:GATE_A + H_A + h + 1]
            gc_fb = jnp.where(lo, cf, cbk)
            gr_fb = jnp.concatenate([gct[GATE_A + h:GATE_A + h + 1, :],
                                     gct[GATE_A + H_A + h:GATE_A + H_A + h + 1, :]], axis=1)
            decay = jnp.where(incl, jnp.exp(jnp.where(incl, gc_fb - gr_fb, 0.0)), 0.0)
            bf = bb[:, GATE_BETA + h:GATE_BETA + h + 1]
            bbk = bb[:, GATE_BETA + H_A + h:GATE_BETA + H_A + h + 1]
            egf, egb = jnp.exp(cf), jnp.exp(cbk)
            tot_f = gcb[CHUNK - 1:CHUNK, GATE_A + h:GATE_A + h + 1]
            tot_b = gcb[0:1, GATE_A + H_A + h:GATE_A + H_A + h + 1]
            kk, qq, vk = kk_s[h, rows, :], qq_s[h, rows, :], vk_s[h, rows, :]
            k16 = kk[:, :half].astype(BF16)
            kq = jnp.concatenate([k16, qq[:, :half].astype(BF16)], axis=0)
            gram = _dot_nt(kq, jnp.concatenate([k16, k16], axis=0))
            lmat = jnp.where(strict, gram[:CHUNK] * jnp.where(lo, bf, bbk) * decay, 0.0)
            a_s[h, rows, :] = jnp.where(incl, gram[CHUNK:] * decay, 0.0).astype(BF16)
            x_f = vk * jnp.where(lo, bf, bf * egf)
            x_b = vk * jnp.where(lo, bbk, bbk * egb)
            x = _solve_unit_triangular(lmat, jnp.concatenate([x_f, x_b], axis=0), lo, diag_blk, eye)
            x_f, x_b = x[:CHUNK], pltpu.roll(x[CHUNK:], half, axis=1)
            u_s[h, rows, :] = jnp.where(lo, x_f, x_b)
            w_s[h, rows, :] = jnp.where(lo, x_b, x_f).astype(BF16)
            qd_s[h, rows, :] = (qq * jnp.where(lo, egb, egf)).astype(BF16)
            kd = kk * jnp.where(lo, jnp.exp(tot_b - cbk), jnp.exp(tot_f - cf))
            kdt_s[h, c] = kd.T.astype(BF16)
            egt_s[h, c] = jnp.exp(jnp.where(lo_row, tot_f, tot_b))
        return carry

    lax.fori_loop(0, n_chunk, prep, 0)

    def scan(s, carry):
        rf = pl.ds(pl.multiple_of(s * CHUNK, CHUNK), CHUNK)
        sb = n_chunk - 1 - s
        rbk = pl.ds(pl.multiple_of(sb * CHUNK, CHUNK), CHUNK)
        for h in range(H_A):
            u = jnp.where(lo, u_s[h, rf, :], u_s[h, rbk, :])
            w = jnp.where(lo16, w_s[h, rbk, :], w_s[h, rf, :])
            qd = jnp.where(lo16, qd_s[h, rbk, :], qd_s[h, rf, :])
            a = jnp.where(lo16, a_s[h, rf, :], a_s[h, rbk, :])
            kdt = jnp.concatenate([kdt_s[h, sb][:half], kdt_s[h, s][half:]], axis=0)
            egt = jnp.where(lo_row, egt_s[h, s], egt_s[h, sb])
            st = sa_s[h]
            wq = _dot(jnp.concatenate([w, qd], axis=0), st.astype(BF16))
            v = u - wq[:CHUNK]
            vbd = jnp.concatenate([jnp.where(lo, v, 0.0), jnp.where(lo, 0.0, v)], axis=0).astype(BF16)
            o = wq[CHUNK:] + _dot(a, vbd)
            sa_s[h] = st * egt + jnp.where(anti, _dot(kdt, v.astype(BF16)), 0.0)
            o_s[h, rf, 0:half] = o[:, :half]
            o_s[h, rbk, half:LANES] = o[:, half:]
        return carry

    lax.fori_loop(0, n_chunk, scan, 0)

    for r0 in range(0, seq_len, rb):
        for p in range(n_pair):
            nrm = []
            for e in range(2):
                ofb = o_s[2 * p + e, r0:r0 + rb, :]
                oo = ofb + pltpu.roll(ofb, half, axis=1)
                nrm.append(oo * lax.rsqrt(jnp.sum(oo * oo, axis=-1, keepdims=True) * (0.5 / HEAD_DIM) + EPS))
            o_ref[r0:r0 + rb, p * LANES:(p + 1) * LANES] = (
                jnp.where(lo_rb, nrm[0], nrm[1]) * ng_ref[...] * _silu(z_ref[r0:r0 + rb, p * LANES:(p + 1) * LANES]))

    for h in range(H_A):
        st = sa_s[h]
        sfin_ref[0, h] = st[half:, :half]
        sfin_ref[1, h] = st[:half, half:]


def delta_mixer(qkv, z, gate, cw, alog_row, dtb_row, ng2, s0, *, n_seq, seq_len, tok0):
    n_chunk = seq_len // CHUNK
    b0 = tok0 // seq_len
    tok = lambda w: pl.BlockSpec((seq_len, w), lambda b: (b0 + b, 0))
    full = lambda a: pl.BlockSpec(a.shape, lambda b: (0,) * a.ndim)
    state = pl.BlockSpec((None, 2, H_A, HEAD_DIM, HEAD_DIM), lambda b: (b, 0, 0, 0, 0))
    per_head = lambda dt: pltpu.VMEM((H_A, seq_len, LANES), dt)
    return pl.pallas_call(
        functools.partial(_delta_kernel, seq_len=seq_len),
        grid=(n_seq,),
        in_specs=[tok(3 * A_W), tok(A_W), tok(LANES), full(cw), full(alog_row), full(dtb_row), full(ng2), state],
        out_specs=[pl.BlockSpec((seq_len, A_W), lambda b: (b, 0)), state],
        out_shape=[jax.ShapeDtypeStruct((n_seq * seq_len, A_W), F32),
                   jax.ShapeDtypeStruct((n_seq, 2, H_A, HEAD_DIM, HEAD_DIM), F32)],
        scratch_shapes=[pltpu.VMEM((seq_len + 16, 3 * A_W), F32),
                        per_head(F32), per_head(F32), per_head(F32),
                        pltpu.VMEM((seq_len, LANES), F32),
                        pltpu.VMEM((seq_len, LANES), F32),
                        per_head(F32), per_head(BF16), per_head(BF16), per_head(BF16),
                        pltpu.VMEM((H_A, n_chunk, LANES, HEAD_DIM), BF16),
                        pltpu.VMEM((H_A, n_chunk, 1, LANES), F32),
                        pltpu.VMEM((H_A, LANES, LANES), F32),
                        per_head(F32)],
        compiler_params=_cparams(("arbitrary",)),
        name=f"delta_mixer_{seq_len}",
    )(qkv, z, gate, cw, alog_row, dtb_row, ng2, s0)


def _delta_kernel(*refs, seq_len, n_sub, aliased):
    qkv_ref, z_ref, gate_ref, cw_ref, alog_ref, dtb_ref, ng_ref, s0_ref = refs[:8]
    (o_ref, sfin_ref, pad_s, kk_s, qq_s, vk_s, gc_s, beta_s, u_s, w_s, qd_s, a_s, kdt_s, egt_s, sa_s, o_s,
     tp_s, t32_s, rp_s, c32_s, m_s, x_s, vbd_s, v_s, op_s) = refs[9 if aliased else 8:]
    n_chunk = seq_len // CHUNK
    n_pair = H_A // 2
    n_unit = n_sub * H_A
    half = HEAD_DIM
    lo = _lane((CHUNK, LANES)) < half
    lo_row = _lane((1, LANES)) < half
    lo16 = _lane((CHUNK, LANES)).astype(F32).astype(BF16) < half
    row = lax.broadcasted_iota(jnp.int32, (CHUNK, LANES), 0)
    col = _lane((CHUNK, LANES)) % half
    ahead = jnp.where(lo, row - col, col - row)
    incl = ahead >= 0
    strict = ahead > 0
    diag_blk = row // SOLVE_BLOCK == col // SOLVE_BLOCK
    big_row = lax.broadcasted_iota(jnp.int32, (LANES, LANES), 0)
    anti = big_row // half + _lane((LANES, LANES)) // half == 1
    eye = (big_row == _lane((LANES, LANES))).astype(F32)

    def both(a):
        return jnp.concatenate([jnp.where(lo, a, 0.0), jnp.where(lo, 0.0, a)], axis=0)

    rb = 128
    lo_rb = _lane((rb, LANES)) < half
    zero8 = jnp.zeros((8, 3 * A_W), F32)
    pad_s[0:8, :] = zero8
    pad_s[8 + seq_len:16 + seq_len, :] = zero8

    def l2n(x):
        return x * lax.rsqrt(0.5 * jnp.sum(x * x, axis=-1, keepdims=True) + EPS)

    for sq in range(n_sub):
        pad_s[8:8 + seq_len, :] = qkv_ref[sq * seq_len:(sq + 1) * seq_len, :]
        for r0 in range(0, seq_len, rb):
            y = _silu(pad_s[7 + r0:7 + r0 + rb, :] * cw_ref[0:1, :] + pad_s[8 + r0:8 + r0 + rb, :] * cw_ref[1:2, :]
                      + pad_s[9 + r0:9 + r0 + rb, :] * cw_ref[2:3, :])

            def dup(part, p):
                c = y[:, part * A_W + p * LANES:part * A_W + (p + 1) * LANES]
                r = pltpu.roll(c, half, axis=1)
                return jnp.where(lo_rb, c, r), jnp.where(lo_rb, r, c)

            for p in range(n_pair):
                qs, ks, vs = dup(0, p), dup(1, p), dup(2, p)
                for e in range(2):
                    u = sq * H_A + 2 * p + e
                    kk = l2n(ks[e])
                    kk_s[u, r0:r0 + rb, :] = kk
                    qq_s[u, r0:r0 + rb, :] = l2n(qs[e]) * (HEAD_DIM ** -0.5)
                    vk_s[u, r0:r0 + rb, :] = jnp.where(lo_rb, vs[e], kk)

    cb = 256
    ci = lax.broadcasted_iota(jnp.int32, (cb, cb), 0)
    cj = lax.broadcasted_iota(jnp.int32, (cb, cb), 1)
    same = ci // CHUNK == cj // CHUNK
    m_pre = jnp.where(same & (ci >= cj), 1.0, 0.0).astype(BF16)
    m_suf = jnp.where(same & (ci <= cj), 1.0, 0.0).astype(BF16)
    fwd_cols = (_lane((cb, LANES)) >= GATE_A) & (_lane((cb, LANES)) < GATE_A + H_A)
    for r0 in range(0, n_sub * seq_len, cb):
        gt = gate_ref[r0:r0 + cb, :]
        beta_s[r0:r0 + cb, :] = jax.nn.sigmoid(gt)
        g = -jnp.exp(alog_ref[...]) * _softplus(gt + dtb_ref[...])
        g1 = g.astype(BF16)
        r1 = g - g1.astype(F32)
        g2 = r1.astype(BF16)
        g3 = (r1 - g2.astype(F32)).astype(BF16)
        blk = jnp.concatenate([g1, g2, g3], axis=1)
        pre, suf = _dot(m_pre, blk), _dot(m_suf, blk)
        pre = pre[:, :LANES] + pre[:, LANES:2 * LANES] + pre[:, 2 * LANES:]
        suf = suf[:, :LANES] + suf[:, LANES:2 * LANES] + suf[:, 2 * LANES:]
        gc_s[r0:r0 + cb, :] = jnp.where(fwd_cols, pre, suf)

    zero = jnp.zeros((half, half), F32)
    for sq in range(n_sub):
        for h in range(H_A):
            sa_s[sq * H_A + h] = jnp.concatenate([jnp.concatenate([zero, s0_ref[sq, 1, h]], axis=1),
                                                  jnp.concatenate([s0_ref[sq, 0, h], zero], axis=1)], axis=0)

    n_fac = int(math.log2(SOLVE_BLOCK))

    def prep(c, carry):
        rows = pl.ds(pl.multiple_of(c * CHUNK, CHUNK), CHUNK)
        for sq in range(n_sub):
            grow = pl.ds(pl.multiple_of(sq * seq_len + c * CHUNK, CHUNK), CHUNK)
            gcb = gc_s[grow, :]
            gct = gcb.T
            bb = beta_s[grow, :]
            for h in range(H_A):
                u = sq * H_A + h
                full = (CHUNK, LANES)
                cff = jnp.broadcast_to(gcb[:, GATE_A + h:GATE_A + h + 1], full)
                cbb = jnp.broadcast_to(gcb[:, GATE_A + H_A + h:GATE_A + H_A + h + 1], full)
                bff = jnp.broadcast_to(bb[:, GATE_BETA + h:GATE_BETA + h + 1], full)
                bbb = jnp.broadcast_to(bb[:, GATE_BETA + H_A + h:GATE_BETA + H_A + h + 1], full)
                gr_fb = jnp.concatenate([gct[GATE_A + h:GATE_A + h + 1, :],
                                         gct[GATE_A + H_A + h:GATE_A + H_A + h + 1, :]], axis=1)
                decay = jnp.where(incl, jnp.exp(jnp.where(incl, jnp.where(lo, cff, cbb) - gr_fb, 0.0)), 0.0)
                egf, egb = jnp.exp(cff), jnp.exp(cbb)
                tot_f = gcb[CHUNK - 1:CHUNK, GATE_A + h:GATE_A + h + 1]
                tot_b = gcb[0:1, GATE_A + H_A + h:GATE_A + H_A + h + 1]
                kk, qq, vk = kk_s[u, rows, :], qq_s[u, rows, :], vk_s[u, rows, :]
                k16 = kk[:, :half].astype(BF16)
                kq = jnp.concatenate([k16, qq[:, :half].astype(BF16)], axis=0)
                gram = _dot_nt(kq, jnp.concatenate([k16, k16], axis=0))
                neg_l = jnp.where(strict, gram[:CHUNK] * jnp.where(lo, -bff, -bbb) * decay, 0.0)
                a_s[u, rows, :] = jnp.where(incl, gram[CHUNK:] * decay, 0.0).astype(BF16)
                neg_d = jnp.where(diag_blk, neg_l, 0.0)
                pd = both(neg_d)
                t32_s[u] = eye + pd
                tp_s[u, :, :LANES] = (eye + pd).astype(BF16)
                tp_s[u, :, LANES:] = pd.astype(BF16)
                x_f = vk * jnp.where(lo, bff, bff * egf)
                x_b = vk * jnp.where(lo, bbb, bbb * egb)
                rp_s[u, :, :LANES] = jnp.concatenate([x_f, x_b], axis=0).astype(BF16)
                rp_s[u, :, LANES:] = both(neg_l - neg_d).astype(BF16)
                qd_s[u, rows, :] = (qq * jnp.where(lo, egb, egf)).astype(BF16)
                kd = kk * jnp.exp(jnp.where(lo, tot_b - cbb, tot_f - cff))
                kdt_s[u, c] = kd.T.astype(BF16)
                egt_s[u, c] = jnp.exp(jnp.where(lo_row, tot_f, tot_b))
        for u in range(n_unit):
            tp_s[u, :, LANES:] = _dot(tp_s[u, :, LANES:], tp_s[u, :, LANES:]).astype(BF16)
        for j in range(1, n_fac):
            for u in range(n_unit):
                if j + 1 < n_fac:
                    tp = _dot(tp_s[u, :, LANES:], tp_s[u])
                    t = t32_s[u] + tp[:, :LANES]
                    t32_s[u] = t
                    tp_s[u, :, :LANES] = t.astype(BF16)
                    tp_s[u, :, LANES:] = tp[:, LANES:].astype(BF16)
                else:
                    t = t32_s[u] + _dot(tp_s[u, :, LANES:], tp_s[u, :, :LANES])
                    tp_s[u, :, :LANES] = t.astype(BF16)
        for u in range(n_unit):
            cm = _dot(tp_s[u, :, :LANES], rp_s[u])
            c32_s[u] = cm[:, :LANES]
            x_s[u] = cm[:, :LANES].astype(BF16)
            m_s[u] = cm[:, LANES:].astype(BF16)
        n_sweep = CHUNK // SOLVE_BLOCK - 1
        for it in range(n_sweep):
            for u in range(n_unit):
                x = c32_s[u] + _dot(m_s[u], x_s[u])
                if it + 1 < n_sweep:
                    x_s[u] = x.astype(BF16)
                else:
                    x_f, x_b = x[:CHUNK], pltpu.roll(x[CHUNK:], half, axis=1)
                    u_s[u, rows, :] = jnp.where(lo, x_f, x_b)
                    w_s[u, rows, :] = jnp.where(lo, x_b, x_f).astype(BF16)
        return carry

    lax.fori_loop(0, n_chunk, prep, 0)

    def scan(s, carry):
        rf = pl.ds(pl.multiple_of(s * CHUNK, CHUNK), CHUNK)
        sb = n_chunk - 1 - s
        rbk = pl.ds(pl.multiple_of(sb * CHUNK, CHUNK), CHUNK)
        for u in range(n_unit):
            w = jnp.where(lo16, w_s[u, rbk, :], w_s[u, rf, :])
            qd = jnp.where(lo16, qd_s[u, rbk, :], qd_s[u, rf, :])
            wq = _dot(jnp.concatenate([w, qd], axis=0), sa_s[u].astype(BF16))
            v = jnp.where(lo, u_s[u, rf, :], u_s[u, rbk, :]) - wq[:CHUNK]
            v_s[u] = v.astype(BF16)
            vbd_s[u] = both(v).astype(BF16)
            op_s[u] = wq[CHUNK:]
        for u in range(n_unit):
            a = jnp.where(lo16, a_s[u, rf, :], a_s[u, rbk, :])
            o = op_s[u] + _dot(a, vbd_s[u])
            o_s[u, rf, 0:half] = o[:, :half]
            o_s[u, rbk, half:LANES] = o[:, half:]
            kdt = jnp.concatenate([kdt_s[u, sb][:half], kdt_s[u, s][half:]], axis=0)
            egt = jnp.where(lo_row, egt_s[u, s], egt_s[u, sb])
            sa_s[u] = sa_s[u] * egt + jnp.where(anti, _dot(kdt, v_s[u]), 0.0)
        return carry

    lax.fori_loop(0, n_chunk, scan, 0)

    for sq in range(n_sub):
        for r0 in range(0, seq_len, rb):
            tr = sq * seq_len + r0
            for p in range(n_pair):
                nrm = []
                for e in range(2):
                    ofb = o_s[sq * H_A + 2 * p + e, r0:r0 + rb, :]
                    oo = ofb + pltpu.roll(ofb, half, axis=1)
                    nrm.append(oo * lax.rsqrt(jnp.sum(oo * oo, axis=-1, keepdims=True) * (0.5 / HEAD_DIM) + EPS))
                o_ref[tr:tr + rb, p * LANES:(p + 1) * LANES] = (
                    jnp.where(lo_rb, nrm[0], nrm[1]) * ng_ref[...]
                    * _silu(z_ref[tr:tr + rb, p * LANES:(p + 1) * LANES]))
        for h in range(H_A):
            st = sa_s[sq * H_A + h]
            sfin_ref[sq, 0, h] = st[half:, :half]
            sfin_ref[sq, 1, h] = st[:half, half:]


def delta_mixer(qkv, z, gate, cw, alog_row, dtb_row, ng2, s0, *, n_seq, seq_len, tok0, n_sub,
                all_states=None, layer=0):
    n_chunk = seq_len // CHUNK
    n_unit = n_sub * H_A
    blk = n_sub * seq_len
    b0 = tok0 // blk
    tok = lambda w: pl.BlockSpec((blk, w), lambda b: (b0 + b, 0))
    full = lambda a: pl.BlockSpec(a.shape, lambda b: (0,) * a.ndim)
    state = pl.BlockSpec((n_sub, 2, H_A, HEAD_DIM, HEAD_DIM), lambda b: (b, 0, 0, 0, 0))
    per_unit = lambda dt: pltpu.VMEM((n_unit, seq_len, LANES), dt)
    stage = lambda rows, cols, dt: pltpu.VMEM((n_unit, rows, cols), dt)
    aliased = all_states is not None
    in_specs = [tok(3 * A_W), tok(A_W), tok(LANES), full(cw), full(alog_row), full(dtb_row), full(ng2), state]
    args = [qkv, z, gate, cw, alog_row, dtb_row, ng2, s0]
    if aliased:
        in_specs.append(pl.BlockSpec(memory_space=pl.ANY))
        args.append(all_states)
        state_out = pl.BlockSpec((n_sub, None, 2, H_A, HEAD_DIM, HEAD_DIM), lambda b: (b, layer, 0, 0, 0, 0))
        state_shape = jax.ShapeDtypeStruct(all_states.shape, F32)
    else:
        state_out = state
        state_shape = jax.ShapeDtypeStruct((n_seq, 2, H_A, HEAD_DIM, HEAD_DIM), F32)
    return pl.pallas_call(
        functools.partial(_delta_kernel, seq_len=seq_len, n_sub=n_sub, aliased=aliased),
        grid=(n_seq // n_sub,),
        in_specs=in_specs,
        out_specs=[pl.BlockSpec((blk, A_W), lambda b: (b, 0)), state_out],
        out_shape=[jax.ShapeDtypeStruct((n_seq * seq_len, A_W), F32), state_shape],
        input_output_aliases={len(args) - 1: 1} if aliased else {},
        scratch_shapes=[pltpu.VMEM((seq_len + 16, 3 * A_W), F32),
                        per_unit(F32), per_unit(F32), per_unit(F32),
                        pltpu.VMEM((blk, LANES), F32),
                        pltpu.VMEM((blk, LANES), F32),
                        per_unit(F32), per_unit(BF16), per_unit(BF16), per_unit(BF16),
                        pltpu.VMEM((n_unit, n_chunk, LANES, HEAD_DIM), BF16),
                        pltpu.VMEM((n_unit, n_chunk, 1, LANES), F32),
                        stage(LANES, LANES, F32),
                        per_unit(F32),
                        stage(LANES, 2 * LANES, BF16),
                        stage(LANES, LANES, F32),
                        stage(LANES, 2 * LANES, BF16),
                        stage(LANES, LANES, F32),
                        stage(LANES, LANES, BF16),
                        stage(LANES, LANES, BF16),
                        stage(LANES, LANES, BF16),
                        stage(CHUNK, LANES, BF16),
                        stage(CHUNK, LANES, F32)],
        compiler_params=_cparams(("arbitrary",)),
        name=f"delta_mixer_{seq_len}",
    )(*args)


def _rope(x, cos, sin_signed, quarter):
    width = x.shape[-1]
    swapped = jnp.where((_lane(x.shape) % (2 * quarter)) < quarter,
                        pltpu.roll(x, width - quarter, axis=1), pltpu.roll(x, quarter, axis=1))
    return x * cos + swapped * sin_signed


def _rope_tables(rows, dim):
    nf = dim // 4
    inv = ROPE_THETA ** (-jnp.arange(nf, dtype=F32) / nf)
    r = jnp.repeat(jnp.arange(rows, dtype=F32), GRID_W)
    c = jnp.tile(jnp.arange(GRID_W, dtype=F32), rows)
    ar, ac = r[:, None] * inv, c[:, None] * inv
    cos = jnp.concatenate([jnp.cos(ar), jnp.cos(ar), jnp.cos(ac), jnp.cos(ac)], axis=1)
    sin = jnp.concatenate([-jnp.sin(ar), jnp.sin(ar), -jnp.sin(ac), jnp.sin(ac)], axis=1)
    return jnp.tile(cos, (1, LANES // dim)), jnp.tile(sin, (1, LANES // dim))


def _gqa_kernel(*refs, cached):
    if cached:
        (q_ref, k_ref, v_ref, gq_ref, gk_ref, ck_ref, cv_ref, cosq_ref, sinq_ref, cosk_ref, sink_ref,
         o_ref, k_s, v_s) = refs
    else:
        q_ref, k_ref, v_ref, gq_ref, gk_ref, o_ref, kn_ref, k_s, v_s = refs
    gmat = _group_mean_matrix(LANES, HEAD_DIM)
    quarter = HEAD_DIM // 4

    @pl.when(pl.program_id(1) == 0)
    def _():
        k = _group_rms(k_ref[...], gmat, gk_ref[...])
        v = v_ref[...]
        if cached:
            k = _rope(k, cosk_ref[...], sink_ref[...], quarter)
            k = jnp.concatenate([ck_ref[...], k], axis=0)
            v = jnp.concatenate([cv_ref[...], v], axis=0)
        else:
            kn_ref[...] = k
        k_s[...] = k.astype(BF16)
        v_s[...] = v.astype(BF16)

    k, v = k_s[...], v_s[...]
    tq = q_ref.shape[0]
    lo = _lane((tq, LANES)) < HEAD_DIM
    group = H_B // KV_B
    q_scale = (HEAD_DIM ** -0.5) * math.log2(math.e)
    qs = []
    for j in range(H_B // 2):
        q = _group_rms(q_ref[:, j * LANES:(j + 1) * LANES], gmat, gq_ref[...])
        if cached:
            q = _rope(q, cosq_ref[...], sinq_ref[...], quarter)
        q = q * q_scale
        qs.append((q, pltpu.roll(q, HEAD_DIM, axis=1)))
    outs = [None] * H_B
    for kv in range(KV_B):
        keep = lo if kv == 0 else ~lo
        heads = range(kv * group, (kv + 1) * group)
        q3 = jnp.concatenate([jnp.where(keep, qs[h // 2][0 if h % 2 == kv else 1], 0.0).astype(BF16)
                              for h in heads], axis=0)
        s = _dot_nt(q3, k)
        p = jnp.exp2(s - jnp.max(s, axis=-1, keepdims=True))
        r = _dot(p.astype(BF16), v) / jnp.sum(p, axis=-1, keepdims=True)
        for i, h in enumerate(heads):
            rh = r[i * tq:(i + 1) * tq]
            outs[h] = rh if h % 2 == kv else pltpu.roll(rh, HEAD_DIM, axis=1)
    for j in range(H_B // 2):
        o_ref[:, j * LANES:(j + 1) * LANES] = jnp.where(lo, outs[2 * j], outs[2 * j + 1])


def gqa_mixer(q, k, v, gq, gk, *, n_seq, seq_len, tok0, cache=None, rope=None, layer=0):
    tq = 256
    nq = seq_len // tq
    b0 = tok0 // seq_len
    q0 = tok0 // tq
    cached = cache is not None
    kvw = KV_B * HEAD_DIM
    in_specs = [pl.BlockSpec((tq, B_W), lambda b, i: (q0 + b * nq + i, 0)),
                pl.BlockSpec((seq_len, kvw), lambda b, i: (b0 + b, 0)),
                pl.BlockSpec((seq_len, kvw), lambda b, i: (b0 + b, 0)),
                pl.BlockSpec((1, LANES), lambda b, i: (0, 0)),
                pl.BlockSpec((1, LANES), lambda b, i: (0, 0))]
    args = [q, k, v, gq, gk]
    out_specs = [pl.BlockSpec((tq, B_W), lambda b, i: (b * nq + i, 0))]
    out_shape = [jax.ShapeDtypeStruct((n_seq * seq_len, B_W), F32)]
    if cached:
        cos, sin = rope
        cache_spec = pl.BlockSpec((None, None, PAST_LEN, kvw), lambda b, i: (b, layer, 0, 0))
        in_specs += [cache_spec, cache_spec,
                     pl.BlockSpec((tq, LANES), lambda b, i: (i, 0)),
                     pl.BlockSpec((tq, LANES), lambda b, i: (i, 0)),
                     pl.BlockSpec((seq_len, LANES), lambda b, i: (0, 0)),
                     pl.BlockSpec((seq_len, LANES), lambda b, i: (0, 0))]
        args += [cache[0], cache[1], cos, sin, cos, sin]
    else:
        out_specs.append(pl.BlockSpec((seq_len, kvw), lambda b, i: (b, 0)))
        out_shape.append(jax.ShapeDtypeStruct((n_seq * seq_len, kvw), F32))
    n_keys = seq_len + (PAST_LEN if cached else 0)
    return pl.pallas_call(
        functools.partial(_gqa_kernel, cached=cached),
        grid=(n_seq, nq),
        in_specs=in_specs, out_specs=out_specs, out_shape=out_shape,
        scratch_shapes=[pltpu.VMEM((n_keys, kvw), BF16), pltpu.VMEM((n_keys, kvw), BF16)],
        compiler_params=_cparams(("arbitrary", "arbitrary")),
        name=f"gqa_mixer_{seq_len}",
    )(*args)


def _diff_kernel(*refs, cached, lam_init):
    if cached:
        (q_ref, k_ref, v_ref, lam_ref, gn_ref, ck_ref, cv_ref, cosq_ref, sinq_ref, cosk_ref, sink_ref,
         o_ref) = refs
    else:
        q_ref, k_ref, v_ref, lam_ref, gn_ref, o_ref = refs
    quarter = DIFF_HD // 4
    lv = lam_ref[...]
    lam = (jnp.exp(jnp.sum(lv[0:1] * lv[1:2], axis=-1, keepdims=True))
           - jnp.exp(jnp.sum(lv[2:3] * lv[3:4], axis=-1, keepdims=True)) + lam_init)
    q, k, v = q_ref[...], k_ref[...], v_ref[...]
    if cached:
        cq = jnp.concatenate([cosq_ref[...]] * 2, axis=1)
        sq = jnp.concatenate([sinq_ref[...]] * 2, axis=1)
        ck = jnp.concatenate([cosk_ref[...]] * 2, axis=1)
        sk = jnp.concatenate([sink_ref[...]] * 2, axis=1)
        q = _rope(q, cq, sq, quarter)
        k = _rope(k, ck, sk, quarter)
        k = jnp.concatenate([ck_ref[...], k], axis=0)
        v = jnp.concatenate([cv_ref[...], v], axis=0)
    k = k.astype(BF16)
    v = v.astype(BF16)
    q = q * ((DIFF_HD ** -0.5) * math.log2(math.e))
    tq = q.shape[0]
    lane = _lane(q.shape)
    o = jnp.zeros(q.shape, F32)
    for h in range(H_C):
        ps, cs = [], []
        for m in range(2):
            qm = jnp.where(lane // DIFF_HD == 2 * h + m, q, 0.0).astype(BF16)
            s = _dot_nt(qm, k)
            p = jnp.exp2(s - jnp.max(s, axis=-1, keepdims=True))
            ps.append(p)
            cs.append((1.0 if m == 0 else lam) / jnp.sum(p, axis=-1, keepdims=True))
        a = ps[0] * cs[0] - ps[1] * cs[1]
        o = jnp.where(lane // (2 * DIFF_HD) == h, _dot(a.astype(BF16), v), o)
    o_ref[...] = _group_rms(o, _group_mean_matrix(C_W, 2 * DIFF_HD), gn_ref[...]) * (1.0 - lam_init)


def diff_mixer(q, k, v, lam_p, gn, lam_init, *, n_seq, seq_len, tok0, cache=None, rope=None, layer=0):
    tq = 256
    nq = seq_len // tq
    b0 = tok0 // seq_len
    q0 = tok0 // tq
    cached = cache is not None
    in_specs = [pl.BlockSpec((tq, C_W), lambda b, i: (q0 + b * nq + i, 0)),
                pl.BlockSpec((seq_len, C_W), lambda b, i: (b0 + b, 0)),
                pl.BlockSpec((seq_len, C_W), lambda b, i: (b0 + b, 0)),
                pl.BlockSpec((4, DIFF_HD), lambda b, i: (0, 0)),
                pl.BlockSpec((1, C_W), lambda b, i: (0, 0))]
    args = [q, k, v, lam_p, gn]
    if cached:
        cos, sin = rope
        cache_spec = pl.BlockSpec((None, None, PAST_LEN, C_W), lambda b, i: (b, layer, 0, 0))
        in_specs += [cache_spec, cache_spec,
                     pl.BlockSpec((tq, LANES), lambda b, i: (i, 0)),
                     pl.BlockSpec((tq, LANES), lambda b, i: (i, 0)),
                     pl.BlockSpec((seq_len, LANES), lambda b, i: (0, 0)),
                     pl.BlockSpec((seq_len, LANES), lambda b, i: (0, 0))]
        args += [cache[0], cache[1], cos, sin, cos, sin]
    return pl.pallas_call(
        functools.partial(_diff_kernel, cached=cached, lam_init=lam_init),
        grid=(n_seq, nq),
        in_specs=in_specs,
        out_specs=pl.BlockSpec((tq, C_W), lambda b, i: (b * nq + i, 0)),
        out_shape=jax.ShapeDtypeStruct((n_seq * seq_len, C_W), F32),
        compiler_params=_cparams(("arbitrary", "arbitrary")),
        name=f"diff_mixer_{seq_len}",
    )(*args)


def _outproj_kernel(*refs, routed, n_mix):
    x_ref = refs[0]
    mix_refs = refs[1:1 + 2 * n_mix]
    rest = refs[1 + 2 * n_mix:]
    if routed:
        w_ref, gm_ref, g_ref, sh_ref, sc_ref, r_ref, x1_ref, h_ref, lg_ref = rest
    else:
        w_ref, gm_ref, g_ref, sh_ref, sc_ref, x1_ref, h_ref = rest
    is_ctx = pl.program_id(0) < N_CTX // x_ref.shape[0]
    mixed = jnp.concatenate(
        [jnp.where(is_ctx, mix_refs[2 * m][...], mix_refs[2 * m + 1][...]) for m in range(n_mix)], axis=1)
    x1 = x_ref[...] + gm_ref[...] * _dot(mixed.astype(BF16), w_ref[...])
    x1_ref[...] = x1
    h = _rms(x1, g_ref[...]) * (1.0 + sc_ref[...]) + sh_ref[...]
    h_ref[...] = h.astype(BF16)
    if routed:
        lg_ref[...] = _dot_hi(h, r_ref[...])


def out_projection(x, mixes, w_out_b, mod, norm_g, l, router_p=None):
    tm = 512
    n_ctx_tiles = N_CTX // tm
    n_lat_tiles = N_LAT // tm
    routed = router_p is not None
    mod_spec = lambda k: pl.BlockSpec((None, None, 1, D_MODEL), lambda i: (l, _cond_row(i, tm), 0, k))
    in_specs = [pl.BlockSpec((tm, D_MODEL), lambda i: (i, 0))]
    args = [x]
    for ctx_arr, lat_arr in mixes:
        w = ctx_arr.shape[1]
        in_specs.append(pl.BlockSpec((tm, w), lambda i: (jnp.minimum(i, n_ctx_tiles - 1), 0)))
        in_specs.append(pl.BlockSpec((tm, w), lambda i: (jnp.clip(i - n_ctx_tiles, 0, n_lat_tiles - 1), 0)))
        args += [ctx_arr, lat_arr]
    in_specs += [pl.BlockSpec((None, D_MODEL, D_MODEL), lambda i: (l, 0, 0)),
                 mod_spec(2),
                 pl.BlockSpec((None, 1, D_MODEL), lambda i: (l, 0, 0)),
                 mod_spec(3), mod_spec(4)]
    args += [w_out_b, mod, norm_g, mod, mod]
    out_specs = [pl.BlockSpec((tm, D_MODEL), lambda i: (i, 0)),
                 pl.BlockSpec((tm, D_MODEL), lambda i: (i, 0))]
    out_shape = [jax.ShapeDtypeStruct((N_TOK, D_MODEL), F32),
                 jax.ShapeDtypeStruct((N_TOK, D_MODEL), BF16)]
    if routed:
        in_specs.append(pl.BlockSpec((D_MODEL, LANES), lambda i: (0, 0)))
        args.append(router_p)
        out_specs.append(pl.BlockSpec((tm, LANES), lambda i: (i, 0)))
        out_shape.append(jax.ShapeDtypeStruct((N_TOK, LANES), F32))
    return pl.pallas_call(
        functools.partial(_outproj_kernel, routed=routed, n_mix=len(mixes)),
        grid=(N_TOK // tm,),
        in_specs=in_specs, out_specs=out_specs, out_shape=out_shape,
        compiler_params=_cparams(("arbitrary",)),
        name="out_projection",
    )(*args)


def _top2_gate(logits):
    lane = _lane(logits.shape)
    neg = jnp.float32(-jnp.inf)
    lg = jnp.where(lane < N_EXP, logits, neg)
    t1 = jnp.max(lg, axis=-1, keepdims=True)
    i1 = jnp.min(jnp.where(lg == t1, lane, LANES), axis=-1, keepdims=True)
    lg2 = jnp.where(lane == i1, neg, lg)
    t2 = jnp.max(lg2, axis=-1, keepdims=True)
    i2 = jnp.min(jnp.where(lg2 == t2, lane, LANES), axis=-1, keepdims=True)
    e2 = jnp.exp(t2 - t1)
    w1 = 1.0 / (1.0 + e2)
    w2 = e2 / (1.0 + e2)
    return jnp.where(lane == i1, w1, 0.0) + jnp.where(lane == i2, w2, 0.0)


def _ffn_kernel(*refs, routed):
    if routed:
        h_ref, wg_ref, wu_ref, wd_ref, x_ref, gf_ref, lg_ref, o_ref, acc_ref = refs
    else:
        h_ref, wg_ref, wu_ref, wd_ref, x_ref, gf_ref, o_ref, acc_ref = refs
    j = pl.program_id(1)

    @pl.when(j == 0)
    def _():
        acc_ref[...] = jnp.zeros_like(acc_ref)

    h = h_ref[...]
    act = _silu(_dot(h, wg_ref[...])) * _dot(h, wu_ref[...])
    if routed:
        gate = _top2_gate(lg_ref[...])
        act = act * jnp.sum(jnp.where(_lane(gate.shape) == j, gate, 0.0), axis=-1, keepdims=True)
    acc_ref[...] += _dot(act.astype(BF16), wd_ref[...])

    @pl.when(j == pl.num_programs(1) - 1)
    def _():
        o_ref[...] = x_ref[...] + gf_ref[...] * acc_ref[...]


def ffn(h, w_gu, w_d, x1, mod, l, logits=None):
    tm = 512
    lyr = l // 2
    routed = logits is not None
    if routed:
        n_chunk = N_EXP
        gu_spec = lambda half: pl.BlockSpec((None, None, D_MODEL, FF_CHUNK), lambda i, j: (lyr, j, 0, half))
        d_spec = pl.BlockSpec((None, None, FF_CHUNK, D_MODEL), lambda i, j: (lyr, j, 0, 0))
    else:
        n_chunk = D_FF // FF_CHUNK
        gu_spec = lambda half: pl.BlockSpec((None, D_MODEL, FF_CHUNK), lambda i, j: (lyr, 0, half * n_chunk + j))
        d_spec = pl.BlockSpec((None, FF_CHUNK, D_MODEL), lambda i, j: (lyr, j, 0))
    in_specs = [pl.BlockSpec((tm, D_MODEL), lambda i, j: (i, 0)),
                gu_spec(0), gu_spec(1), d_spec,
                pl.BlockSpec((tm, D_MODEL), lambda i, j: (i, 0)),
                pl.BlockSpec((None, None, 1, D_MODEL), lambda i, j: (l, _cond_row(i, tm), 0, 5))]
    args = [h, w_gu, w_gu, w_d, x1, mod]
    if routed:
        in_specs.append(pl.BlockSpec((tm, LANES), lambda i, j: (i, 0)))
        args.append(logits)
    return pl.pallas_call(
        functools.partial(_ffn_kernel, routed=routed),
        grid=(N_TOK // tm, n_chunk),
        in_specs=in_specs,
        out_specs=pl.BlockSpec((tm, D_MODEL), lambda i, j: (i, 0)),
        out_shape=jax.ShapeDtypeStruct((N_TOK, D_MODEL), F32),
        scratch_shapes=[pltpu.VMEM((tm, D_MODEL), F32)],
        compiler_params=_cparams(("arbitrary", "arbitrary")),
        name="ffn_routed" if routed else "ffn_dense",
    )(*args)


def _final_norm_kernel(x_ref, g_ref, o_ref):
    o_ref[...] = _rms(x_ref[...], g_ref[...])


def final_norm_call(x, g, tok0, n_tok):
    tm = 1024
    i0 = tok0 // tm
    return pl.pallas_call(
        _final_norm_kernel,
        grid=(n_tok // tm,),
        in_specs=[pl.BlockSpec((tm, D_MODEL), lambda i: (i0 + i, 0)),
                  pl.BlockSpec((1, D_MODEL), lambda i: (0, 0))],
        out_specs=pl.BlockSpec((tm, D_MODEL), lambda i: (i, 0)),
        out_shape=jax.ShapeDtypeStruct((n_tok, D_MODEL), F32),
        compiler_params=_cparams(("arbitrary",)),
        name="final_norm",
    )(x, g)


def _cache_store_kernel(*refs):
    n = len(refs) // 3
    for src, dst in zip(refs[:n], refs[2 * n:]):
        dst[...] = src[...].reshape(dst.shape)


def cache_store(sources, all_layers, l):
    n_sq = 2
    tm = n_sq * SEQ
    return pl.pallas_call(
        _cache_store_kernel,
        grid=(N_CTX // tm,),
        in_specs=([pl.BlockSpec((tm, s.shape[1]), lambda i: (i, 0)) for s in sources]
                  + [pl.BlockSpec(memory_space=pl.ANY)] * len(all_layers)),
        out_specs=[pl.BlockSpec((n_sq, None, SEQ, a.shape[-1]), lambda i: (i, l, 0, 0)) for a in all_layers],
        out_shape=[jax.ShapeDtypeStruct(a.shape, F32) for a in all_layers],
        input_output_aliases={len(sources) + k: k for k in range(len(all_layers))},
        compiler_params=_cparams(("arbitrary",)),
        name="cache_store",
    )(*sources, *all_layers)


def _gate_row(p):
    return jnp.zeros((1, LANES), F32).at[0, GATE_A:GATE_A + 2 * H_A].set(p.reshape(-1))


def _tile_lanes(g, width):
    return jnp.tile(g, width // g.shape[0])[None, :]


def kernel(x_prompt, x_sample, state_delta, cache_gqa_k, cache_gqa_v, cache_diff_k, cache_diff_v, c, c_ctx,
           ada_w, ada_b, norm_mix, norm_ffn, w_in, conv_w, dn_a_log, dn_dt_bias, dn_norm, gqa_q_norm,
           gqa_k_norm, diff_lambda, diff_norm, w_out, ffn_w_gu, ffn_w_down, moe_router, moe_w_gu, moe_w_down,
           final_norm):
    cond8 = jnp.concatenate([c_ctx[None, :], c, jnp.zeros((8 - N_COND, D_MODEL), F32)], axis=0)
    gate_lo = 4 * A_W
    gate_hi = gate_lo + 4 * H_A
    w_in_p = jnp.concatenate(
        [w_in[:, :, :gate_lo], w_in[:, :, gate_hi:], w_in[:, :, gate_lo:gate_hi],
         jnp.zeros((DEPTH, D_MODEL, LANES - 4 * H_A), F32)], axis=-1).astype(BF16)
    w_out_b = w_out.astype(BF16)
    norm_mix3 = norm_mix.reshape(DEPTH, 1, D_MODEL)
    norm_ffn3 = norm_ffn.reshape(DEPTH, 1, D_MODEL)
    ffn_gu_b, ffn_d_b = ffn_w_gu.astype(BF16), ffn_w_down.astype(BF16)
    moe_gu_b, moe_d_b = moe_w_gu.astype(BF16), moe_w_down.astype(BF16)
    router_p = jnp.pad(moe_router, ((0, 0), (0, 0), (0, LANES - N_EXP)))
    rope_b = _rope_tables(DEC_SEQ // GRID_W, HEAD_DIM)
    rope_c = _rope_tables(DEC_SEQ // GRID_W, DIFF_HD)
    s0_ctx = jnp.zeros((BATCH, 2, H_A, HEAD_DIM, HEAD_DIM), F32)
    cache_b = (cache_gqa_k.reshape(DEC_BATCH, DEPTH, PAST_LEN, KV_B * HEAD_DIM),
               cache_gqa_v.reshape(DEC_BATCH, DEPTH, PAST_LEN, KV_B * HEAD_DIM))
    cache_c = (cache_diff_k.reshape(DEC_BATCH, DEPTH, PAST_LEN, C_W),
               cache_diff_v.reshape(DEC_BATCH, DEPTH, PAST_LEN, C_W))

    mod = ada_modulation(cond8, ada_w, ada_b)[:, :N_COND].reshape(DEPTH, N_COND, 1, 6 * D_MODEL)

    x = jnp.concatenate([x_prompt.reshape(N_CTX, D_MODEL), x_sample.reshape(N_LAT, D_MODEL)], axis=0)
    all_states = jnp.zeros((BATCH, DEPTH, 2, H_A, HEAD_DIM, HEAD_DIM), F32)
    ctx_layers = tuple(jnp.zeros((BATCH, DEPTH, SEQ, w), F32)
                       for w in (KV_B * HEAD_DIM, KV_B * HEAD_DIM, C_W, C_W))
    ctx = dict(n_seq=BATCH, seq_len=SEQ, tok0=0)
    lat = dict(n_seq=DEC_BATCH, seq_len=DEC_SEQ, tok0=N_CTX)
    for l in range(DEPTH):
        qkv_a, z_a, q_b, k_b, v_b, q_c, k_c, v_c, gate = in_projection(x, mod, norm_mix3, w_in_p, l)

        dn_args = (qkv_a, z_a, gate, conv_w[l], _gate_row(dn_a_log[l]), _gate_row(dn_dt_bias[l]),
                   _tile_lanes(dn_norm[l], LANES))
        oa_ctx, all_states = delta_mixer(*dn_args, s0_ctx, n_sub=2, all_states=all_states, layer=l, **ctx)
        oa_lat, _ = delta_mixer(*dn_args, state_delta[:, l], n_sub=1, **lat)

        gq, gk = _tile_lanes(gqa_q_norm[l], LANES), _tile_lanes(gqa_k_norm[l], LANES)
        ob_ctx, kn_ctx = gqa_mixer(q_b, k_b, v_b, gq, gk, **ctx)
        (ob_lat,) = gqa_mixer(q_b, k_b, v_b, gq, gk, cache=cache_b, rope=rope_b, layer=l, **lat)

        lam_init = 0.8 - 0.6 * math.exp(-0.3 * l)
        gn = _tile_lanes(diff_norm[l], C_W)
        oc_ctx = diff_mixer(q_c, k_c, v_c, diff_lambda[l], gn, lam_init, **ctx)
        oc_lat = diff_mixer(q_c, k_c, v_c, diff_lambda[l], gn, lam_init, cache=cache_c, rope=rope_c, layer=l, **lat)

        mixes = ((oa_ctx, oa_lat), (ob_ctx, ob_lat), (oc_ctx, oc_lat))
        if l % 2 == 0:
            x1, h2 = out_projection(x, mixes, w_out_b, mod, norm_ffn3, l)
            x = ffn(h2, ffn_gu_b, ffn_d_b, x1, mod, l)
        else:
            x1, h2, logits = out_projection(x, mixes, w_out_b, mod, norm_ffn3, l, router_p[l // 2])
            x = ffn(h2, moe_gu_b, moe_d_b, x1, mod, l, logits)

        ctx_layers = cache_store((kn_ctx, v_b, k_c, v_c), ctx_layers, l)

    y_prompt = final_norm_call(x, final_norm[None, :], 0, N_CTX).reshape(BATCH, SEQ, D_MODEL)
    y_sample = final_norm_call(x, final_norm[None, :], N_CTX, N_LAT).reshape(DEC_BATCH, DEC_SEQ, D_MODEL)
    new_gk, new_gv, new_dk, new_dv = ctx_layers
    return (y_prompt, y_sample, all_states,
            new_gk.reshape(BATCH, DEPTH, SEQ, KV_B, HEAD_DIM), new_gv.reshape(BATCH, DEPTH, SEQ, KV_B, HEAD_DIM),
            new_dk.reshape(BATCH, DEPTH, SEQ, H_C, 2, DIFF_HD), new_dv.reshape(BATCH, DEPTH, SEQ, H_C, 2 * DIFF_HD))
```

```python
import functools
import math

import jax
import jax.numpy as jnp
from jax import lax
from jax.experimental import pallas as pl
from jax.experimental.pallas import tpu as pltpu

D_MODEL = 1024
BATCH = 16
SEQ = 256
DEPTH = 4
DEC_BATCH = 2
DEC_SEQ = 1024
PAST_LEN = 512
GRID_W = 64
HEAD_DIM = 64
H_A = 6
A_W = H_A * HEAD_DIM
H_B = 6
KV_B = 2
B_W = H_B * HEAD_DIM
H_C = 4
DIFF_HD = 32
C_W = H_C * 2 * DIFF_HD
CONV_W = 3
CHUNK = 64
ROPE_THETA = 10000.0
D_FF = 2816
N_EXP = 8
TOP_K = 2
D_FF_E = 1408
EPS = 1e-6

N_CTX = BATCH * SEQ
N_LAT = DEC_BATCH * DEC_SEQ
N_TOK = N_CTX + N_LAT
N_COND = 1 + DEC_BATCH
IN_PAD = 3072
FF_CHUNK = D_FF_E
LANES = 128
VMEM_LIMIT = 56 * 1024 * 1024

IN_SPLIT = (("qkv_a", 3 * A_W), ("z_a", A_W), ("q_b", B_W), ("k_b", KV_B * HEAD_DIM), ("v_b", KV_B * HEAD_DIM),
            ("q_c", C_W), ("k_c", C_W), ("v_c", C_W), ("gate", LANES))
GATE_BETA, GATE_A = 0, 2 * H_A

F32 = jnp.float32
BF16 = jnp.bfloat16
HI = lax.Precision.HIGHEST


def _cparams(sem):
    return pltpu.CompilerParams(dimension_semantics=sem, vmem_limit_bytes=VMEM_LIMIT)


def _dot(a, b):
    return jnp.dot(a, b, preferred_element_type=F32)


def _dot_hi(a, b):
    return jnp.dot(a, b, preferred_element_type=F32, precision=HI)


def _dot_nt(a, b):
    return lax.dot_general(a, b, (((1,), (1,)), ((), ())), preferred_element_type=F32)


def _silu(x):
    return x * jax.nn.sigmoid(x)


def _softplus(x):
    return jnp.maximum(x, 0.0) + jnp.log1p(jnp.exp(-jnp.abs(x)))


def _rms(x, g):
    return x * lax.rsqrt(jnp.mean(x * x, axis=-1, keepdims=True) + EPS) * g


def _lane(shape):
    return lax.broadcasted_iota(jnp.int32, shape, len(shape) - 1)


def _group_mean_matrix(width, group):
    i = lax.broadcasted_iota(jnp.int32, (width, width), 0)
    j = lax.broadcasted_iota(jnp.int32, (width, width), 1)
    return jnp.where(i // group == j // group, 1.0 / group, 0.0).astype(BF16)


def _group_rms(x, gmat, g):
    sq = x * x
    hi = sq.astype(BF16)
    lo = (sq - hi.astype(F32)).astype(BF16)
    ms = _dot(jnp.concatenate([hi, lo], axis=1), jnp.concatenate([gmat, gmat], axis=0))
    return x * lax.rsqrt(ms + EPS) * g


def _cond_row(i, tm):
    return jnp.maximum((i * tm) // DEC_SEQ - (N_CTX // DEC_SEQ - 1), 0)


def _ada_kernel(c_ref, w_ref, b_ref, o_ref):
    o_ref[...] = _dot_hi(_silu(c_ref[...]), w_ref[...]) + b_ref[...]


def ada_modulation(cond8, ada_w, ada_b):
    tn = 1024
    return pl.pallas_call(
        _ada_kernel,
        grid=(DEPTH, 6 * D_MODEL // tn),
        in_specs=[pl.BlockSpec((8, D_MODEL), lambda l, j: (0, 0)),
                  pl.BlockSpec((None, D_MODEL, tn), lambda l, j: (l, 0, j)),
                  pl.BlockSpec((None, 1, tn), lambda l, j: (l, 0, j))],
        out_specs=pl.BlockSpec((None, 8, tn), lambda l, j: (l, 0, j)),
        out_shape=jax.ShapeDtypeStruct((DEPTH, 8, 6 * D_MODEL), F32),
        compiler_params=_cparams(("arbitrary", "arbitrary")),
        name="ada_modulation",
    )(cond8, ada_w, ada_b.reshape(DEPTH, 1, 6 * D_MODEL))


def _inproj_kernel(x_ref, g_ref, sh_ref, sc_ref, w_ref, *o_refs):
    h = _rms(x_ref[...], g_ref[...]) * (1.0 + sc_ref[...]) + sh_ref[...]
    acc = _dot(h.astype(BF16), w_ref[...])
    off = 0
    for o_ref, (_, width) in zip(o_refs, IN_SPLIT):
        o_ref[...] = acc[:, off:off + width]
        off += width


def in_projection(x, mod, norm_g, w_in_p, l):
    tm = 512
    return pl.pallas_call(
        _inproj_kernel,
        grid=(N_TOK // tm,),
        in_specs=[pl.BlockSpec((tm, D_MODEL), lambda i: (i, 0)),
                  pl.BlockSpec((None, 1, D_MODEL), lambda i: (l, 0, 0)),
                  pl.BlockSpec((None, None, 1, D_MODEL), lambda i: (l, _cond_row(i, tm), 0, 0)),
                  pl.BlockSpec((None, None, 1, D_MODEL), lambda i: (l, _cond_row(i, tm), 0, 1)),
                  pl.BlockSpec((None, D_MODEL, IN_PAD), lambda i: (l, 0, 0))],
        out_specs=[pl.BlockSpec((tm, w), lambda i: (i, 0)) for _, w in IN_SPLIT],
        out_shape=[jax.ShapeDtypeStruct((N_TOK, w), F32) for _, w in IN_SPLIT],
        compiler_params=_cparams(("arbitrary",)),
        name="in_projection",
    )(x, norm_g, mod, mod, w_in_p)


SOLVE_BLOCK = 16


def _solve_unit_triangular(lmat, rhs, lo, diag_blk, eye):
    def both(a):
        return jnp.concatenate([jnp.where(lo, a, 0.0), jnp.where(lo, 0.0, a)], axis=0)

    neg_d = jnp.where(diag_blk, -lmat, 0.0)
    pd = both(neg_d)
    po = both(-lmat - neg_d)
    t = eye + pd
    p = _dot(pd.astype(BF16), pd.astype(BF16))
    n_fac = int(math.log2(SOLVE_BLOCK))
    for j in range(1, n_fac):
        if j + 1 < n_fac:
            tp = _dot(p.astype(BF16), jnp.concatenate([t, p], axis=1).astype(BF16))
            t = t + tp[:, :LANES]
            p = tp[:, LANES:]
        else:
            t = t + _dot(p.astype(BF16), t.astype(BF16))
    cm = _dot(t.astype(BF16), jnp.concatenate([rhs, po], axis=1).astype(BF16))
    c, m = cm[:, :LANES], cm[:, LANES:].astype(BF16)
    x = c
    for _ in range(CHUNK // SOLVE_BLOCK - 1):
        x = c + _dot(m, x.astype(BF16))
    return x


def _delta_kernel(qkv_ref, z_ref, gate_ref, cw_ref, alog_ref, dtb_ref, ng_ref, s0_ref, o_ref, sfin_ref,
                  pad_s, kk_s, qq_s, vk_s, gc_s, beta_s, u_s, w_s, qd_s, a_s, kdt_s, egt_s, sa_s, o_s,
                  *, seq_len):
    n_chunk = seq_len // CHUNK
    n_pair = H_A // 2
    half = HEAD_DIM
    lo = _lane((CHUNK, LANES)) < half
    lo_row = _lane((1, LANES)) < half
    lo16 = _lane((CHUNK, LANES)).astype(F32).astype(BF16) < half
    row = lax.broadcasted_iota(jnp.int32, (CHUNK, LANES), 0)
    col = _lane((CHUNK, LANES)) % half
    ahead = jnp.where(lo, row - col, col - row)
    incl = ahead >= 0
    strict = ahead > 0
    diag_blk = row // SOLVE_BLOCK == col // SOLVE_BLOCK
    big_row = lax.broadcasted_iota(jnp.int32, (LANES, LANES), 0)
    anti = big_row // half + _lane((LANES, LANES)) // half == 1
    eye = (big_row == _lane((LANES, LANES))).astype(F32)

    rb = 128
    zero8 = jnp.zeros((8, 3 * A_W), F32)
    pad_s[0:8, :] = zero8
    pad_s[8 + seq_len:16 + seq_len, :] = zero8
    pad_s[8:8 + seq_len, :] = qkv_ref[...]
    lo_rb = _lane((rb, LANES)) < half
    for r0 in range(0, seq_len, rb):
        y = _silu(pad_s[7 + r0:7 + r0 + rb, :] * cw_ref[0:1, :] + pad_s[8 + r0:8 + r0 + rb, :] * cw_ref[1:2, :]
                  + pad_s[9 + r0:9 + r0 + rb, :] * cw_ref[2:3, :])

        def dup(part, p):
            c = y[:, part * A_W + p * LANES:part * A_W + (p + 1) * LANES]
            r = pltpu.roll(c, half, axis=1)
            return jnp.where(lo_rb, c, r), jnp.where(lo_rb, r, c)

        def l2n(x):
            return x * lax.rsqrt(0.5 * jnp.sum(x * x, axis=-1, keepdims=True) + EPS)

        for p in range(n_pair):
            qs, ks, vs = dup(0, p), dup(1, p), dup(2, p)
            for e in range(2):
                h = 2 * p + e
                kk = l2n(ks[e])
                kk_s[h, r0:r0 + rb, :] = kk
                qq_s[h, r0:r0 + rb, :] = l2n(qs[e]) * (HEAD_DIM ** -0.5)
                vk_s[h, r0:r0 + rb, :] = jnp.where(lo_rb, vs[e], kk)

    gt = gate_ref[...]
    beta_s[...] = jax.nn.sigmoid(gt)
    g = -jnp.exp(alog_ref[...]) * _softplus(gt + dtb_ref[...])
    g1 = g.astype(BF16)
    r1 = g - g1.astype(F32)
    g2 = r1.astype(BF16)
    g3 = (r1 - g2.astype(F32)).astype(BF16)
    g_split = jnp.concatenate([g1, g2, g3], axis=1)
    cb = 256
    ci = lax.broadcasted_iota(jnp.int32, (cb, cb), 0)
    cj = lax.broadcasted_iota(jnp.int32, (cb, cb), 1)
    same = ci // CHUNK == cj // CHUNK
    m_pre = (same & (ci >= cj)).astype(BF16)
    m_suf = (same & (ci <= cj)).astype(BF16)
    fwd_cols = (_lane((cb, LANES)) >= GATE_A) & (_lane((cb, LANES)) < GATE_A + H_A)
    for r0 in range(0, seq_len, cb):
        blk = g_split[r0:r0 + cb, :]
        pre, suf = _dot(m_pre, blk), _dot(m_suf, blk)
        pre = pre[:, :LANES] + pre[:, LANES:2 * LANES] + pre[:, 2 * LANES:]
        suf = suf[:, :LANES] + suf[:, LANES:2 * LANES] + suf[:, 2 * LANES:]
        gc_s[r0:r0 + cb, :] = jnp.where(fwd_cols, pre, suf)

    for h in range(H_A):
        zero = jnp.zeros((half, half), F32)
        sa_s[h] = jnp.concatenate([jnp.concatenate([zero, s0_ref[1, h]], axis=1),
                                   jnp.concatenate([s0_ref[0, h], zero], axis=1)], axis=0)

    def prep(c, carry):
        rows = pl.ds(pl.multiple_of(c * CHUNK, CHUNK), CHUNK)
        gcb = gc_s[rows, :]
        gct = gcb.T
        bb = beta_s[rows, :]
        for h in range(H_A):
            cf = gcb[:, GATE_A + h:GATE_A + h + 1]
            cbk = gcb[:, GATE_A + H_A + h:GATE_A + H_A + h + 1]
            gc_fb = jnp.where(lo, cf, cbk)
            gr_fb = jnp.concatenate([gct[GATE_A + h:GATE_A + h + 1, :],
                                     gct[GATE_A + H_A + h:GATE_A + H_A + h + 1, :]], axis=1)
            decay = jnp.where(incl, jnp.exp(jnp.where(incl, gc_fb - gr_fb, 0.0)), 0.0)
            bf = bb[:, GATE_BETA + h:GATE_BETA + h + 1]
            bbk = bb[:, GATE_BETA + H_A + h:GATE_BETA + H_A + h + 1]
            egf, egb = jnp.exp(cf), jnp.exp(cbk)
            tot_f = gcb[CHUNK - 1:CHUNK, GATE_A + h:GATE_A + h + 1]
            tot_b = gcb[0:1, GATE_A + H_A + h:GATE_A + H_A + h + 1]
            kk, qq, vk = kk_s[h, rows, :], qq_s[h, rows, :], vk_s[h, rows, :]
            k16 = kk[:, :half].astype(BF16)
            kq = jnp.concatenate([k16, qq[:, :half].astype(BF16)], axis=0)
            gram = _dot_nt(kq, jnp.concatenate([k16, k16], axis=0))
            lmat = jnp.where(strict, gram[:CHUNK] * jnp.where(lo, bf, bbk) * decay, 0.0)
            a_s[h, rows, :] = jnp.where(incl, gram[CHUNK:] * decay, 0.0).astype(BF16)
            x_f = vk * jnp.where(lo, bf, bf * egf)
            x_b = vk * jnp.where(lo, bbk, bbk * egb)
            x = _solve_unit_triangular(lmat, jnp.concatenate([x_f, x_b], axis=0), lo, diag_blk, eye)
            x_f, x_b = x[:CHUNK], pltpu.roll(x[CHUNK:], half, axis=1)
            u_s[h, rows, :] = jnp.where(lo, x_f, x_b)
            w_s[h, rows, :] = jnp.where(lo, x_b, x_f).astype(BF16)
            qd_s[h, rows, :] = (qq * jnp.where(lo, egb, egf)).astype(BF16)
            kd = kk * jnp.where(lo, jnp.exp(tot_b - cbk), jnp.exp(tot_f - cf))
            kdt_s[h, c] = kd.T.astype(BF16)
            egt_s[h, c] = jnp.exp(jnp.where(lo_row, tot_f, tot_b))
        return carry

    lax.fori_loop(0, n_chunk, prep, 0)

    def scan(s, carry):
        rf = pl.ds(pl.multiple_of(s * CHUNK, CHUNK), CHUNK)
        sb = n_chunk - 1 - s
        rbk = pl.ds(pl.multiple_of(sb * CHUNK, CHUNK), CHUNK)
        for h in range(H_A):
            u = jnp.where(lo, u_s[h, rf, :], u_s[h, rbk, :])
            w = jnp.where(lo16, w_s[h, rbk, :], w_s[h, rf, :])
            qd = jnp.where(lo16, qd_s[h, rbk, :], qd_s[h, rf, :])
            a = jnp.where(lo16, a_s[h, rf, :], a_s[h, rbk, :])
            kdt = jnp.concatenate([kdt_s[h, sb][:half], kdt_s[h, s][half:]], axis=0)
            egt = jnp.where(lo_row, egt_s[h, s], egt_s[h, sb])
            st = sa_s[h]
            wq = _dot(jnp.concatenate([w, qd], axis=0), st.astype(BF16))
            v = u - wq[:CHUNK]
            vbd = jnp.concatenate([jnp.where(lo, v, 0.0), jnp.where(lo, 0.0, v)], axis=0).astype(BF16)
            o = wq[CHUNK:] + _dot(a, vbd)
            sa_s[h] = st * egt + jnp.where(anti, _dot(kdt, v.astype(BF16)), 0.0)
            o_s[h, rf, 0:half] = o[:, :half]
            o_s[h, rbk, half:LANES] = o[:, half:]
        return carry

    lax.fori_loop(0, n_chunk, scan, 0)

    for r0 in range(0, seq_len, rb):
        for p in range(n_pair):
            nrm = []
            for e in range(2):
                ofb = o_s[2 * p + e, r0:r0 + rb, :]
                oo = ofb + pltpu.roll(ofb, half, axis=1)
                nrm.append(oo * lax.rsqrt(jnp.sum(oo * oo, axis=-1, keepdims=True) * (0.5 / HEAD_DIM) + EPS))
            o_ref[r0:r0 + rb, p * LANES:(p + 1) * LANES] = (
                jnp.where(lo_rb, nrm[0], nrm[1]) * ng_ref[...] * _silu(z_ref[r0:r0 + rb, p * LANES:(p + 1) * LANES]))

    for h in range(H_A):
        st = sa_s[h]
        sfin_ref[0, h] = st[half:, :half]
        sfin_ref[1, h] = st[:half, half:]


def delta_mixer(qkv, z, gate, cw, alog_row, dtb_row, ng2, s0, *, n_seq, seq_len, tok0):
    n_chunk = seq_len // CHUNK
    b0 = tok0 // seq_len
    tok = lambda w: pl.BlockSpec((seq_len, w), lambda b: (b0 + b, 0))
    full = lambda a: pl.BlockSpec(a.shape, lambda b: (0,) * a.ndim)
    state = pl.BlockSpec((None, 2, H_A, HEAD_DIM, HEAD_DIM), lambda b: (b, 0, 0, 0, 0))
    per_head = lambda dt: pltpu.VMEM((H_A, seq_len, LANES), dt)
    return pl.pallas_call(
        functools.partial(_delta_kernel, seq_len=seq_len),
        grid=(n_seq,),
        in_specs=[tok(3 * A_W), tok(A_W), tok(LANES), full(cw), full(alog_row), full(dtb_row), full(ng2), state],
        out_specs=[pl.BlockSpec((seq_len, A_W), lambda b: (b, 0)), state],
        out_shape=[jax.ShapeDtypeStruct((n_seq * seq_len, A_W), F32),
                   jax.ShapeDtypeStruct((n_seq, 2, H_A, HEAD_DIM, HEAD_DIM), F32)],
        scratch_shapes=[pltpu.VMEM((seq_len + 16, 3 * A_W), F32),
                        per_head(F32), per_head(F32), per_head(F32),
                        pltpu.VMEM((seq_len, LANES), F32),
                        pltpu.VMEM((seq_len, LANES), F32),
                        per_head(F32), per_head(BF16), per_head(BF16), per_head(BF16),
                        pltpu.VMEM((H_A, n_chunk, LANES, HEAD_DIM), BF16),
                        pltpu.VMEM((H_A, n_chunk, 1, LANES), F32),
                        pltpu.VMEM((H_A, LANES, LANES), F32),
                        per_head(F32)],
        compiler_params=_cparams(("arbitrary",)),
        name=f"delta_mixer_{seq_len}",
    )(qkv, z, gate, cw, alog_row, dtb_row, ng2, s0)


def _delta_kernel(*refs, seq_len, n_sub, aliased):
    qkv_ref, z_ref, gate_ref, cw_ref, alog_ref, dtb_ref, ng_ref, s0_ref = refs[:8]
    (o_ref, sfin_ref, pad_s, kk_s, qq_s, vk_s, gc_s, beta_s, u_s, w_s, qd_s, a_s, kdt_s, egt_s, sa_s, o_s,
     tp_s, t32_s, rp_s, c32_s, m_s, x_s, vbd_s, v_s, op_s) = refs[9 if aliased else 8:]
    n_chunk = seq_len // CHUNK
    n_pair = H_A // 2
    n_unit = n_sub * H_A
    half = HEAD_DIM
    lo = _lane((CHUNK, LANES)) < half
    lo_row = _lane((1, LANES)) < half
    lo16 = _lane((CHUNK, LANES)).astype(F32).astype(BF16) < half
    row = lax.broadcasted_iota(jnp.int32, (CHUNK, LANES), 0)
    col = _lane((CHUNK, LANES)) % half
    ahead = jnp.where(lo, row - col, col - row)
    incl = ahead >= 0
    strict = ahead > 0
    diag_blk = row // SOLVE_BLOCK == col // SOLVE_BLOCK
    big_row = lax.broadcasted_iota(jnp.int32, (LANES, LANES), 0)
    anti = big_row // half + _lane((LANES, LANES)) // half == 1
    eye = (big_row == _lane((LANES, LANES))).astype(F32)

    def both(a):
        return jnp.concatenate([jnp.where(lo, a, 0.0), jnp.where(lo, 0.0, a)], axis=0)

    rb = 128
    lo_rb = _lane((rb, LANES)) < half
    zero8 = jnp.zeros((8, 3 * A_W), F32)
    pad_s[0:8, :] = zero8
    pad_s[8 + seq_len:16 + seq_len, :] = zero8

    def l2n(x):
        return x * lax.rsqrt(0.5 * jnp.sum(x * x, axis=-1, keepdims=True) + EPS)

    for sq in range(n_sub):
        pad_s[8:8 + seq_len, :] = qkv_ref[sq * seq_len:(sq + 1) * seq_len, :]
        for r0 in range(0, seq_len, rb):
            y = _silu(pad_s[7 + r0:7 + r0 + rb, :] * cw_ref[0:1, :] + pad_s[8 + r0:8 + r0 + rb, :] * cw_ref[1:2, :]
                      + pad_s[9 + r0:9 + r0 + rb, :] * cw_ref[2:3, :])

            def dup(part, p):
                c = y[:, part * A_W + p * LANES:part * A_W + (p + 1) * LANES]
                r = pltpu.roll(c, half, axis=1)
                return jnp.where(lo_rb, c, r), jnp.where(lo_rb, r, c)

            for p in range(n_pair):
                qs, ks, vs = dup(0, p), dup(1, p), dup(2, p)
                for e in range(2):
                    u = sq * H_A + 2 * p + e
                    kk = l2n(ks[e])
                    kk_s[u, r0:r0 + rb, :] = kk
                    qq_s[u, r0:r0 + rb, :] = l2n(qs[e]) * (HEAD_DIM ** -0.5)
                    vk_s[u, r0:r0 + rb, :] = jnp.where(lo_rb, vs[e], kk)

    cb = 256
    ci = lax.broadcasted_iota(jnp.int32, (cb, cb), 0)
    cj = lax.broadcasted_iota(jnp.int32, (cb, cb), 1)
    same = ci // CHUNK == cj // CHUNK
    m_pre = jnp.where(same & (ci >= cj), 1.0, 0.0).astype(BF16)
    m_suf = jnp.where(same & (ci <= cj), 1.0, 0.0).astype(BF16)
    fwd_cols = (_lane((cb, LANES)) >= GATE_A) & (_lane((cb, LANES)) < GATE_A + H_A)
    for r0 in range(0, n_sub * seq_len, cb):
        gt = gate_ref[r0:r0 + cb, :]
        beta_s[r0:r0 + cb, :] = jax.nn.sigmoid(gt)
        g = -jnp.exp(alog_ref[...]) * _softplus(gt + dtb_ref[...])
        g1 = g.astype(BF16)
        r1 = g - g1.astype(F32)
        g2 = r1.astype(BF16)
        g3 = (r1 - g2.astype(F32)).astype(BF16)
        blk = jnp.concatenate([g1, g2, g3], axis=1)
        pre, suf = _dot(m_pre, blk), _dot(m_suf, blk)
        pre = pre[:, :LANES] + pre[:, LANES:2 * LANES] + pre[:, 2 * LANES:]
        suf = suf[:, :LANES] + suf[:, LANES:2 * LANES] + suf[:, 2 * LANES:]
        gc_s[r0:r0 + cb, :] = jnp.where(fwd_cols, pre, suf)

    zero = jnp.zeros((half, half), F32)
    for sq in range(n_sub):
        for h in range(H_A):
            sa_s[sq * H_A + h] = jnp.concatenate([jnp.concatenate([zero, s0_ref[sq, 1, h]], axis=1),
                                                  jnp.concatenate([s0_ref[sq, 0, h], zero], axis=1)], axis=0)

    n_fac = int(math.log2(SOLVE_BLOCK))

    def prep(c, carry):
        rows = pl.ds(pl.multiple_of(c * CHUNK, CHUNK), CHUNK)
        for sq in range(n_sub):
            grow = pl.ds(pl.multiple_of(sq * seq_len + c * CHUNK, CHUNK), CHUNK)
            gcb = gc_s[grow, :]
            gct = gcb.T
            bb = beta_s[grow, :]
            for h in range(H_A):
                u = sq * H_A + h
                full = (CHUNK, LANES)
                cff = jnp.broadcast_to(gcb[:, GATE_A + h:GATE_A + h + 1], full)
                cbb = jnp.broadcast_to(gcb[:, GATE_A + H_A + h:GATE_A + H_A + h + 1], full)
                bff = jnp.broadcast_to(bb[:, GATE_BETA + h:GATE_BETA + h + 1], full)
                bbb = jnp.broadcast_to(bb[:, GATE_BETA + H_A + h:GATE_BETA + H_A + h + 1], full)
                gr_fb = jnp.concatenate([gct[GATE_A + h:GATE_A + h + 1, :],
                                         gct[GATE_A + H_A + h:GATE_A + H_A + h + 1, :]], axis=1)
                decay = jnp.where(incl, jnp.exp(jnp.where(incl, jnp.where(lo, cff, cbb) - gr_fb, 0.0)), 0.0)
                egf, egb = jnp.exp(cff), jnp.exp(cbb)
                tot_f = gcb[CHUNK - 1:CHUNK, GATE_A + h:GATE_A + h + 1]
                tot_b = gcb[0:1, GATE_A + H_A + h:GATE_A + H_A + h + 1]
                kk, qq, vk = kk_s[u, rows, :], qq_s[u, rows, :], vk_s[u, rows, :]
                k16 = kk[:, :half].astype(BF16)
                kq = jnp.concatenate([k16, qq[:, :half].astype(BF16)], axis=0)
                gram = _dot_nt(kq, jnp.concatenate([k16, k16], axis=0))
                neg_l = jnp.where(strict, gram[:CHUNK] * jnp.where(lo, -bff, -bbb) * decay, 0.0)
                a_s[u, rows, :] = jnp.where(incl, gram[CHUNK:] * decay, 0.0).astype(BF16)
                neg_d = jnp.where(diag_blk, neg_l, 0.0)
                pd = both(neg_d)
                t32_s[u] = eye + pd
                tp_s[u, :, :LANES] = (eye + pd).astype(BF16)
                tp_s[u, :, LANES:] = pd.astype(BF16)
                x_f = vk * jnp.where(lo, bff, bff * egf)
                x_b = vk * jnp.where(lo, bbb, bbb * egb)
                rp_s[u, :, :LANES] = jnp.concatenate([x_f, x_b], axis=0).astype(BF16)
                rp_s[u, :, LANES:] = both(neg_l - neg_d).astype(BF16)
                qd_s[u, rows, :] = (qq * jnp.where(lo, egb, egf)).astype(BF16)
                kd = kk * jnp.exp(jnp.where(lo, tot_b - cbb, tot_f - cff))
                kdt_s[u, c] = kd.T.astype(BF16)
                egt_s[u, c] = jnp.exp(jnp.where(lo_row, tot_f, tot_b))
        for u in range(n_unit):
            tp_s[u, :, LANES:] = _dot(tp_s[u, :, LANES:], tp_s[u, :, LANES:]).astype(BF16)
        for j in range(1, n_fac):
            for u in range(n_unit):
                if j + 1 < n_fac:
                    tp = _dot(tp_s[u, :, LANES:], tp_s[u])
                    t = t32_s[u] + tp[:, :LANES]
                    t32_s[u] = t
                    tp_s[u, :, :LANES] = t.astype(BF16)
                    tp_s[u, :, LANES:] = tp[:, LANES:].astype(BF16)
                else:
                    t = t32_s[u] + _dot(tp_s[u, :, LANES:], tp_s[u, :, :LANES])
                    tp_s[u, :, :LANES] = t.astype(BF16)
        for u in range(n_unit):
            cm = _dot(tp_s[u, :, :LANES], rp_s[u])
            c32_s[u] = cm[:, :LANES]
            x_s[u] = cm[:, :LANES].astype(BF16)
            m_s[u] = cm[:, LANES:].astype(BF16)
        n_sweep = CHUNK // SOLVE_BLOCK - 1
        for it in range(n_sweep):
            for u in range(n_unit):
                x = c32_s[u] + _dot(m_s[u], x_s[u])
                if it + 1 < n_sweep:
                    x_s[u] = x.astype(BF16)
                else:
                    x_f, x_b = x[:CHUNK], pltpu.roll(x[CHUNK:], half, axis=1)
                    u_s[u, rows, :] = jnp.where(lo, x_f, x_b)
                    w_s[u, rows, :] = jnp.where(lo, x_b, x_f).astype(BF16)
        return carry

    lax.fori_loop(0, n_chunk, prep, 0)

    def scan(s, carry):
        rf = pl.ds(pl.multiple_of(s * CHUNK, CHUNK), CHUNK)
        sb = n_chunk - 1 - s
        rbk = pl.ds(pl.multiple_of(sb * CHUNK, CHUNK), CHUNK)
        for u in range(n_unit):
            w = jnp.where(lo16, w_s[u, rbk, :], w_s[u, rf, :])
            qd = jnp.where(lo16, qd_s[u, rbk, :], qd_s[u, rf, :])
            wq = _dot(jnp.concatenate([w, qd], axis=0), sa_s[u].astype(BF16))
            v = jnp.where(lo, u_s[u, rf, :], u_s[u, rbk, :]) - wq[:CHUNK]
            v_s[u] = v.astype(BF16)
            vbd_s[u] = both(v).astype(BF16)
            op_s[u] = wq[CHUNK:]
        for u in range(n_unit):
            a = jnp.where(lo16, a_s[u, rf, :], a_s[u, rbk, :])
            o = op_s[u] + _dot(a, vbd_s[u])
            o_s[u, rf, 0:half] = o[:, :half]
            o_s[u, rbk, half:LANES] = o[:, half:]
            kdt = jnp.concatenate([kdt_s[u, sb][:half], kdt_s[u, s][half:]], axis=0)
            egt = jnp.where(lo_row, egt_s[u, s], egt_s[u, sb])
            sa_s[u] = sa_s[u] * egt + jnp.where(anti, _dot(kdt, v_s[u]), 0.0)
        return carry

    lax.fori_loop(0, n_chunk, scan, 0)

    for sq in range(n_sub):
        for r0 in range(0, seq_len, rb):
            tr = sq * seq_len + r0
            for p in range(n_pair):
                nrm = []
                for e in range(2):
                    ofb = o_s[sq * H_A + 2 * p + e, r0:r0 + rb, :]
                    oo = ofb + pltpu.roll(ofb, half, axis=1)
                    nrm.append(oo * lax.rsqrt(jnp.sum(oo * oo, axis=-1, keepdims=True) * (0.5 / HEAD_DIM) + EPS))
                o_ref[tr:tr + rb, p * LANES:(p + 1) * LANES] = (
                    jnp.where(lo_rb, nrm[0], nrm[1]) * ng_ref[...]
                    * _silu(z_ref[tr:tr + rb, p * LANES:(p + 1) * LANES]))
        for h in range(H_A):
            st = sa_s[sq * H_A + h]
            sfin_ref[sq, 0, h] = st[half:, :half]
            sfin_ref[sq, 1, h] = st[:half, half:]


def delta_mixer(qkv, z, gate, cw, alog_row, dtb_row, ng2, s0, *, n_seq, seq_len, tok0, n_sub,
                all_states=None, layer=0):
    n_chunk = seq_len // CHUNK
    n_unit = n_sub * H_A
    blk = n_sub * seq_len
    b0 = tok0 // blk
    tok = lambda w: pl.BlockSpec((blk, w), lambda b: (b0 + b, 0))
    full = lambda a: pl.BlockSpec(a.shape, lambda b: (0,) * a.ndim)
    state = pl.BlockSpec((n_sub, 2, H_A, HEAD_DIM, HEAD_DIM), lambda b: (b, 0, 0, 0, 0))
    per_unit = lambda dt: pltpu.VMEM((n_unit, seq_len, LANES), dt)
    stage = lambda rows, cols, dt: pltpu.VMEM((n_unit, rows, cols), dt)
    aliased = all_states is not None
    in_specs = [tok(3 * A_W), tok(A_W), tok(LANES), full(cw), full(alog_row), full(dtb_row), full(ng2), state]
    args = [qkv, z, gate, cw, alog_row, dtb_row, ng2, s0]
    if aliased:
        in_specs.append(pl.BlockSpec(memory_space=pl.ANY))
        args.append(all_states)
        state_out = pl.BlockSpec((n_sub, None, 2, H_A, HEAD_DIM, HEAD_DIM), lambda b: (b, layer, 0, 0, 0, 0))
        state_shape = jax.ShapeDtypeStruct(all_states.shape, F32)
    else:
        state_out = state
        state_shape = jax.ShapeDtypeStruct((n_seq, 2, H_A, HEAD_DIM, HEAD_DIM), F32)
    return pl.pallas_call(
        functools.partial(_delta_kernel, seq_len=seq_len, n_sub=n_sub, aliased=aliased),
        grid=(n_seq // n_sub,),
        in_specs=in_specs,
        out_specs=[pl.BlockSpec((blk, A_W), lambda b: (b, 0)), state_out],
        out_shape=[jax.ShapeDtypeStruct((n_seq * seq_len, A_W), F32), state_shape],
        input_output_aliases={len(args) - 1: 1} if aliased else {},
        scratch_shapes=[pltpu.VMEM((seq_len + 16, 3 * A_W), F32),
                        per_unit(F32), per_unit(F32), per_unit(F32),
                        pltpu.VMEM((blk, LANES), F32),
                        pltpu.VMEM((blk, LANES), F32),
                        per_unit(F32), per_unit(BF16), per_unit(BF16), per_unit(BF16),
                        pltpu.VMEM((n_unit, n_chunk, LANES, HEAD_DIM), BF16),
                        pltpu.VMEM((n_unit, n_chunk, 1, LANES), F32),
                        stage(LANES, LANES, F32),
                        per_unit(F32),
                        stage(LANES, 2 * LANES, BF16),
                        stage(LANES, LANES, F32),
                        stage(LANES, 2 * LANES, BF16),
                        stage(LANES, LANES, F32),
                        stage(LANES, LANES, BF16),
                        stage(LANES, LANES, BF16),
                        stage(LANES, LANES, BF16),
                        stage(CHUNK, LANES, BF16),
                        stage(CHUNK, LANES, F32)],
        compiler_params=_cparams(("arbitrary",)),
        name=f"delta_mixer_{seq_len}",
    )(*args)


def _rope(x, cos, sin_signed, quarter):
    width = x.shape[-1]
    swapped = jnp.where((_lane(x.shape) % (2 * quarter)) < quarter,
                        pltpu.roll(x, width - quarter, axis=1), pltpu.roll(x, quarter, axis=1))
    return x * cos + swapped * sin_signed


def _rope_tables(rows, dim):
    nf = dim // 4
    inv = ROPE_THETA ** (-jnp.arange(nf, dtype=F32) / nf)
    r = jnp.repeat(jnp.arange(rows, dtype=F32), GRID_W)
    c = jnp.tile(jnp.arange(GRID_W, dtype=F32), rows)
    ar, ac = r[:, None] * inv, c[:, None] * inv
    cos = jnp.concatenate([jnp.cos(ar), jnp.cos(ar), jnp.cos(ac), jnp.cos(ac)], axis=1)
    sin = jnp.concatenate([-jnp.sin(ar), jnp.sin(ar), -jnp.sin(ac), jnp.sin(ac)], axis=1)
    return jnp.tile(cos, (1, LANES // dim)), jnp.tile(sin, (1, LANES // dim))


def _gqa_kernel(*refs, cached):
    if cached:
        (q_ref, k_ref, v_ref, gq_ref, gk_ref, ck_ref, cv_ref, cosq_ref, sinq_ref, cosk_ref, sink_ref,
         o_ref, k_s, v_s) = refs
    else:
        q_ref, k_ref, v_ref, gq_ref, gk_ref, o_ref, kn_ref, k_s, v_s = refs
    gmat = _group_mean_matrix(LANES, HEAD_DIM)
    quarter = HEAD_DIM // 4

    @pl.when(pl.program_id(1) == 0)
    def _():
        k = _group_rms(k_ref[...], gmat, gk_ref[...])
        v = v_ref[...]
        if cached:
            k = _rope(k, cosk_ref[...], sink_ref[...], quarter)
            k = jnp.concatenate([ck_ref[...], k], axis=0)
            v = jnp.concatenate([cv_ref[...], v], axis=0)
        else:
            kn_ref[...] = k
        k_s[...] = k.astype(BF16)
        v_s[...] = v.astype(BF16)

    k, v = k_s[...], v_s[...]
    tq = q_ref.shape[0]
    lo = _lane((tq, LANES)) < HEAD_DIM
    group = H_B // KV_B
    q_scale = (HEAD_DIM ** -0.5) * math.log2(math.e)
    qs = []
    for j in range(H_B // 2):
        q = _group_rms(q_ref[:, j * LANES:(j + 1) * LANES], gmat, gq_ref[...])
        if cached:
            q = _rope(q, cosq_ref[...], sinq_ref[...], quarter)
        q = q * q_scale
        qs.append((q, pltpu.roll(q, HEAD_DIM, axis=1)))
    outs = [None] * H_B
    for kv in range(KV_B):
        keep = lo if kv == 0 else ~lo
        heads = range(kv * group, (kv + 1) * group)
        q3 = jnp.concatenate([jnp.where(keep, qs[h // 2][0 if h % 2 == kv else 1], 0.0).astype(BF16)
                              for h in heads], axis=0)
        s = _dot_nt(q3, k)
        p = jnp.exp2(s - jnp.max(s, axis=-1, keepdims=True))
        r = _dot(p.astype(BF16), v) / jnp.sum(p, axis=-1, keepdims=True)
        for i, h in enumerate(heads):
            rh = r[i * tq:(i + 1) * tq]
            outs[h] = rh if h % 2 == kv else pltpu.roll(rh, HEAD_DIM, axis=1)
    for j in range(H_B // 2):
        o_ref[:, j * LANES:(j + 1) * LANES] = jnp.where(lo, outs[2 * j], outs[2 * j + 1])


def gqa_mixer(q, k, v, gq, gk, *, n_seq, seq_len, tok0, cache=None, rope=None, layer=0):
    tq = 256
    nq = seq_len // tq
    b0 = tok0 // seq_len
    q0 = tok0 // tq
    cached = cache is not None
    kvw = KV_B * HEAD_DIM
    in_specs = [pl.BlockSpec((tq, B_W), lambda b, i: (q0 + b * nq + i, 0)),
                pl.BlockSpec((seq_len, kvw), lambda b, i: (b0 + b, 0)),
                pl.BlockSpec((seq_len, kvw), lambda b, i: (b0 + b, 0)),
                pl.BlockSpec((1, LANES), lambda b, i: (0, 0)),
                pl.BlockSpec((1, LANES), lambda b, i: (0, 0))]
    args = [q, k, v, gq, gk]
    out_specs = [pl.BlockSpec((tq, B_W), lambda b, i: (b * nq + i, 0))]
    out_shape = [jax.ShapeDtypeStruct((n_seq * seq_len, B_W), F32)]
    if cached:
        cos, sin = rope
        cache_spec = pl.BlockSpec((None, None, PAST_LEN, kvw), lambda b, i: (b, layer, 0, 0))
        in_specs += [cache_spec, cache_spec,
                     pl.BlockSpec((tq, LANES), lambda b, i: (i, 0)),
                     pl.BlockSpec((tq, LANES), lambda b, i: (i, 0)),
                     pl.BlockSpec((seq_len, LANES), lambda b, i: (0, 0)),
                     pl.BlockSpec((seq_len, LANES), lambda b, i: (0, 0))]
        args += [cache[0], cache[1], cos, sin, cos, sin]
    else:
        out_specs.append(pl.BlockSpec((seq_len, kvw), lambda b, i: (b, 0)))
        out_shape.append(jax.ShapeDtypeStruct((n_seq * seq_len, kvw), F32))
    n_keys = seq_len + (PAST_LEN if cached else 0)
    return pl.pallas_call(
        functools.partial(_gqa_kernel, cached=cached),
        grid=(n_seq, nq),
        in_specs=in_specs, out_specs=out_specs, out_shape=out_shape,
        scratch_shapes=[pltpu.VMEM((n_keys, kvw), BF16), pltpu.VMEM((n_keys, kvw), BF16)],
        compiler_params=_cparams(("arbitrary", "arbitrary")),
        name=f"gqa_mixer_{seq_len}",
    )(*args)


def _diff_kernel(*refs, cached, lam_init):
    if cached:
        (q_ref, k_ref, v_ref, lam_ref, gn_ref, ck_ref, cv_ref, cosq_ref, sinq_ref, cosk_ref, sink_ref,
         o_ref) = refs
    else:
        q_ref, k_ref, v_ref, lam_ref, gn_ref, o_ref = refs
    quarter = DIFF_HD // 4
    lv = lam_ref[...]
    lam = (jnp.exp(jnp.sum(lv[0:1] * lv[1:2], axis=-1, keepdims=True))
           - jnp.exp(jnp.sum(lv[2:3] * lv[3:4], axis=-1, keepdims=True)) + lam_init)
    q, k, v = q_ref[...], k_ref[...], v_ref[...]
    if cached:
        cq = jnp.concatenate([cosq_ref[...]] * 2, axis=1)
        sq = jnp.concatenate([sinq_ref[...]] * 2, axis=1)
        ck = jnp.concatenate([cosk_ref[...]] * 2, axis=1)
        sk = jnp.concatenate([sink_ref[...]] * 2, axis=1)
        q = _rope(q, cq, sq, quarter)
        k = _rope(k, ck, sk, quarter)
        k = jnp.concatenate([ck_ref[...], k], axis=0)
        v = jnp.concatenate([cv_ref[...], v], axis=0)
    k = k.astype(BF16)
    v = v.astype(BF16)
    q = q * ((DIFF_HD ** -0.5) * math.log2(math.e))
    tq = q.shape[0]
    lane = _lane(q.shape)
    o = jnp.zeros(q.shape, F32)
    for h in range(H_C):
        ps, cs = [], []
        for m in range(2):
            qm = jnp.where(lane // DIFF_HD == 2 * h + m, q, 0.0).astype(BF16)
            s = _dot_nt(qm, k)
            p = jnp.exp2(s - jnp.max(s, axis=-1, keepdims=True))
            ps.append(p)
            cs.append((1.0 if m == 0 else lam) / jnp.sum(p, axis=-1, keepdims=True))
        a = ps[0] * cs[0] - ps[1] * cs[1]
        o = jnp.where(lane // (2 * DIFF_HD) == h, _dot(a.astype(BF16), v), o)
    o_ref[...] = _group_rms(o, _group_mean_matrix(C_W, 2 * DIFF_HD), gn_ref[...]) * (1.0 - lam_init)


def diff_mixer(q, k, v, lam_p, gn, lam_init, *, n_seq, seq_len, tok0, cache=None, rope=None, layer=0):
    tq = 256
    nq = seq_len // tq
    b0 = tok0 // seq_len
    q0 = tok0 // tq
    cached = cache is not None
    in_specs = [pl.BlockSpec((tq, C_W), lambda b, i: (q0 + b * nq + i, 0)),
                pl.BlockSpec((seq_len, C_W), lambda b, i: (b0 + b, 0)),
                pl.BlockSpec((seq_len, C_W), lambda b, i: (b0 + b, 0)),
                pl.BlockSpec((4, DIFF_HD), lambda b, i: (0, 0)),
                pl.BlockSpec((1, C_W), lambda b, i: (0, 0))]
    args = [q, k, v, lam_p, gn]
    if cached:
        cos, sin = rope
        cache_spec = pl.BlockSpec((None, None, PAST_LEN, C_W), lambda b, i: (b, layer, 0, 0))
        in_specs += [cache_spec, cache_spec,
                     pl.BlockSpec((tq, LANES), lambda b, i: (i, 0)),
                     pl.BlockSpec((tq, LANES), lambda b, i: (i, 0)),
                     pl.BlockSpec((seq_len, LANES), lambda b, i: (0, 0)),
                     pl.BlockSpec((seq_len, LANES), lambda b, i: (0, 0))]
        args += [cache[0], cache[1], cos, sin, cos, sin]
    return pl.pallas_call(
        functools.partial(_diff_kernel, cached=cached, lam_init=lam_init),
        grid=(n_seq, nq),
        in_specs=in_specs,
        out_specs=pl.BlockSpec((tq, C_W), lambda b, i: (b * nq + i, 0)),
        out_shape=jax.ShapeDtypeStruct((n_seq * seq_len, C_W), F32),
        compiler_params=_cparams(("arbitrary", "arbitrary")),
        name=f"diff_mixer_{seq_len}",
    )(*args)


def _outproj_kernel(*refs, routed, n_mix):
    x_ref = refs[0]
    mix_refs = refs[1:1 + 2 * n_mix]
    rest = refs[1 + 2 * n_mix:]
    if routed:
        w_ref, gm_ref, g_ref, sh_ref, sc_ref, r_ref, x1_ref, h_ref, lg_ref = rest
    else:
        w_ref, gm_ref, g_ref, sh_ref, sc_ref, x1_ref, h_ref = rest
    is_ctx = pl.program_id(0) < N_CTX // x_ref.shape[0]
    mixed = jnp.concatenate(
        [jnp.where(is_ctx, mix_refs[2 * m][...], mix_refs[2 * m + 1][...]) for m in range(n_mix)], axis=1)
    x1 = x_ref[...] + gm_ref[...] * _dot(mixed.astype(BF16), w_ref[...])
    x1_ref[...] = x1
    h = _rms(x1, g_ref[...]) * (1.0 + sc_ref[...]) + sh_ref[...]
    h_ref[...] = h.astype(h_ref.dtype)
    if routed:
        lg_ref[...] = _top2_route(_dot_hi(h, r_ref[...]))


def out_projection(x, mixes, w_out_b, mod, norm_g, l, router_p=None):
    tm = 512
    n_ctx_tiles = N_CTX // tm
    n_lat_tiles = N_LAT // tm
    routed = router_p is not None
    mod_spec = lambda k: pl.BlockSpec((None, None, 1, D_MODEL), lambda i: (l, _cond_row(i, tm), 0, k))
    in_specs = [pl.BlockSpec((tm, D_MODEL), lambda i: (i, 0))]
    args = [x]
    for ctx_arr, lat_arr in mixes:
        w = ctx_arr.shape[1]
        in_specs.append(pl.BlockSpec((tm, w), lambda i: (jnp.minimum(i, n_ctx_tiles - 1), 0)))
        in_specs.append(pl.BlockSpec((tm, w), lambda i: (jnp.clip(i - n_ctx_tiles, 0, n_lat_tiles - 1), 0)))
        args += [ctx_arr, lat_arr]
    in_specs += [pl.BlockSpec((None, D_MODEL, D_MODEL), lambda i: (l, 0, 0)),
                 mod_spec(2),
                 pl.BlockSpec((None, 1, D_MODEL), lambda i: (l, 0, 0)),
                 mod_spec(3), mod_spec(4)]
    args += [w_out_b, mod, norm_g, mod, mod]
    out_specs = [pl.BlockSpec((tm, D_MODEL), lambda i: (i, 0)),
                 pl.BlockSpec((tm, D_MODEL), lambda i: (i, 0))]
    out_shape = [jax.ShapeDtypeStruct((N_TOK, D_MODEL), F32),
                 jax.ShapeDtypeStruct((N_TOK, D_MODEL), F32 if routed else BF16)]
    if routed:
        in_specs.append(pl.BlockSpec((D_MODEL, LANES), lambda i: (0, 0)))
        args.append(router_p)
        out_specs.append(pl.BlockSpec((tm, LANES), lambda i: (i, 0)))
        out_shape.append(jax.ShapeDtypeStruct((N_TOK, LANES), F32))
    return pl.pallas_call(
        functools.partial(_outproj_kernel, routed=routed, n_mix=len(mixes)),
        grid=(N_TOK // tm,),
        in_specs=in_specs, out_specs=out_specs, out_shape=out_shape,
        compiler_params=_cparams(("arbitrary",)),
        name="out_projection",
    )(*args)


def _top2_gate(logits):
    lane = _lane(logits.shape)
    neg = jnp.float32(-jnp.inf)
    lg = jnp.where(lane < N_EXP, logits, neg)
    t1 = jnp.max(lg, axis=-1, keepdims=True)
    i1 = jnp.min(jnp.where(lg == t1, lane, LANES), axis=-1, keepdims=True)
    lg2 = jnp.where(lane == i1, neg, lg)
    t2 = jnp.max(lg2, axis=-1, keepdims=True)
    i2 = jnp.min(jnp.where(lg2 == t2, lane, LANES), axis=-1, keepdims=True)
    e2 = jnp.exp(t2 - t1)
    w1 = 1.0 / (1.0 + e2)
    w2 = e2 / (1.0 + e2)
    return jnp.where(lane == i1, w1, 0.0) + jnp.where(lane == i2, w2, 0.0)


def _ffn_kernel(*refs, routed):
    if routed:
        h_ref, wg_ref, wu_ref, wd_ref, x_ref, gf_ref, lg_ref, o_ref, acc_ref = refs
    else:
        h_ref, wg_ref, wu_ref, wd_ref, x_ref, gf_ref, o_ref, acc_ref = refs
    j = pl.program_id(1)

    @pl.when(j == 0)
    def _():
        acc_ref[...] = jnp.zeros_like(acc_ref)

    h = h_ref[...]
    act = _silu(_dot(h, wg_ref[...])) * _dot(h, wu_ref[...])
    if routed:
        gate = _top2_gate(lg_ref[...])
        act = act * jnp.sum(jnp.where(_lane(gate.shape) == j, gate, 0.0), axis=-1, keepdims=True)
    acc_ref[...] += _dot(act.astype(BF16), wd_ref[...])

    @pl.when(j == pl.num_programs(1) - 1)
    def _():
        o_ref[...] = x_ref[...] + gf_ref[...] * acc_ref[...]


def ffn(h, w_gu, w_d, x1, mod, l, logits=None):
    tm = 512
    lyr = l // 2
    routed = logits is not None
    if routed:
        n_chunk = N_EXP
        gu_spec = lambda half: pl.BlockSpec((None, None, D_MODEL, FF_CHUNK), lambda i, j: (lyr, j, 0, half))
        d_spec = pl.BlockSpec((None, None, FF_CHUNK, D_MODEL), lambda i, j: (lyr, j, 0, 0))
    else:
        n_chunk = D_FF // FF_CHUNK
        gu_spec = lambda half: pl.BlockSpec((None, D_MODEL, FF_CHUNK), lambda i, j: (lyr, 0, half * n_chunk + j))
        d_spec = pl.BlockSpec((None, FF_CHUNK, D_MODEL), lambda i, j: (lyr, j, 0))
    in_specs = [pl.BlockSpec((tm, D_MODEL), lambda i, j: (i, 0)),
                gu_spec(0), gu_spec(1), d_spec,
                pl.BlockSpec((tm, D_MODEL), lambda i, j: (i, 0)),
                pl.BlockSpec((None, None, 1, D_MODEL), lambda i, j: (l, _cond_row(i, tm), 0, 5))]
    args = [h, w_gu, w_gu, w_d, x1, mod]
    if routed:
        in_specs.append(pl.BlockSpec((tm, LANES), lambda i, j: (i, 0)))
        args.append(logits)
    return pl.pallas_call(
        functools.partial(_ffn_kernel, routed=routed),
        grid=(N_TOK // tm, n_chunk),
        in_specs=in_specs,
        out_specs=pl.BlockSpec((tm, D_MODEL), lambda i, j: (i, 0)),
        out_shape=jax.ShapeDtypeStruct((N_TOK, D_MODEL), F32),
        scratch_shapes=[pltpu.VMEM((tm, D_MODEL), F32)],
        compiler_params=_cparams(("arbitrary", "arbitrary")),
        name="ffn_routed" if routed else "ffn_dense",
    )(*args)


MOE_TILE = 256
MOE_SLOTS = TOP_K * N_TOK + N_EXP * MOE_TILE
MOE_TILES = MOE_SLOTS // MOE_TILE
ROUTE_I1, ROUTE_I2, ROUTE_W1, ROUTE_W2 = 0, 1, 2, 3


def _top2_route(logits):
    lane = _lane(logits.shape)
    neg = jnp.float32(-jnp.inf)
    lg = jnp.where(lane < N_EXP, logits, neg)
    t1 = jnp.max(lg, axis=-1, keepdims=True)
    i1 = jnp.min(jnp.where(lg == t1, lane, LANES), axis=-1, keepdims=True)
    lg2 = jnp.where(lane == i1, neg, lg)
    t2 = jnp.max(lg2, axis=-1, keepdims=True)
    i2 = jnp.min(jnp.where(lg2 == t2, lane, LANES), axis=-1, keepdims=True)
    e2 = jnp.exp(t2 - t1)
    w1 = 1.0 / (1.0 + e2)
    w2 = e2 / (1.0 + e2)
    rec = jnp.where(lane == ROUTE_I1, i1.astype(F32), 0.0) + jnp.where(lane == ROUTE_I2, i2.astype(F32), 0.0)
    return rec + jnp.where(lane == ROUTE_W1, w1, 0.0) + jnp.where(lane == ROUTE_W2, w2, 0.0)


def _routing_tables(route):
    expert = jnp.concatenate([route[:, ROUTE_I1], route[:, ROUTE_I2]]).astype(jnp.int32)
    weight = jnp.concatenate([route[:, ROUTE_W1], route[:, ROUTE_W2]])
    token = jnp.tile(jnp.arange(N_TOK, dtype=jnp.int32), TOP_K)
    onehot = (expert[:, None] == jnp.arange(N_EXP, dtype=jnp.int32)[None, :]).astype(jnp.int32)
    running = jnp.cumsum(onehot, axis=0)
    counts = running[-1]
    rank = jnp.sum(running * onehot, axis=1) - 1
    padded = (counts + MOE_TILE - 1) // MOE_TILE * MOE_TILE
    ends = jnp.cumsum(padded)
    slot = (ends - padded)[expert] + rank
    slot_token = jnp.zeros((MOE_SLOTS,), jnp.int32).at[slot].set(token)
    slot_weight = jnp.zeros((MOE_SLOTS,), F32).at[slot].set(weight)
    tile_start = jnp.arange(MOE_TILES, dtype=jnp.int32) * MOE_TILE
    tile_expert = jnp.minimum(jnp.sum(tile_start[:, None] >= ends[None, :], axis=1), N_EXP - 1).astype(jnp.int32)
    n_used = (ends[-1] // MOE_TILE).astype(jnp.int32).reshape(1)
    return slot_token, slot_weight.reshape(MOE_SLOTS, 1), slot, tile_expert, n_used


def _row_copy(src_hbm, row, dst, dst_row, sem):
    return pltpu.make_async_copy(src_hbm.at[pl.ds(row, 1)], dst.at[pl.ds(dst_row, 1)], sem)


def _experts_kernel(tok_ref, exp_ref, used_ref, h_hbm, wg_ref, wu_ref, wd_ref, sw_ref, y_ref, x_buf, sems):
    t = pl.program_id(0)
    n_used = used_ref[0]

    def start_gather(tile, half):
        for r in range(MOE_TILE):
            _row_copy(h_hbm, tok_ref[tile * MOE_TILE + r], x_buf.at[half], r, sems.at[half]).start()

    @pl.when(t == 0)
    def _():
        start_gather(0, 0)

    @pl.when(t + 1 < n_used)
    def _():
        start_gather(t + 1, (t + 1) % 2)

    @pl.when(t < n_used)
    def _():
        half = t % 2
        for r in range(MOE_TILE):
            _row_copy(h_hbm, 0, x_buf.at[half], r, sems.at[half]).wait()
        x = x_buf[half].astype(BF16)
        act = _silu(_dot(x, wg_ref[...])) * _dot(x, wu_ref[...]) * sw_ref[...]
        y_ref[...] = _dot(act.astype(BF16), wd_ref[...])

    @pl.when(t >= n_used)
    def _():
        y_ref[...] = jnp.zeros_like(y_ref)


def moe_experts(h, route, w_gu, w_d, l):
    lyr = l // 2
    slot_token, slot_weight, slot, tile_expert, n_used = _routing_tables(route)
    gu_spec = lambda half: pl.BlockSpec((None, None, D_MODEL, FF_CHUNK),
                                        lambda t, tok, exp, used: (lyr, exp[t], 0, half))
    y = pl.pallas_call(
        _experts_kernel,
        grid_spec=pltpu.PrefetchScalarGridSpec(
            num_scalar_prefetch=3,
            grid=(MOE_TILES,),
            in_specs=[pl.BlockSpec(memory_space=pl.ANY),
                      gu_spec(0), gu_spec(1),
                      pl.BlockSpec((None, None, FF_CHUNK, D_MODEL), lambda t, tok, exp, used: (lyr, exp[t], 0, 0)),
                      pl.BlockSpec((MOE_TILE, 1), lambda t, tok, exp, used: (t, 0))],
            out_specs=pl.BlockSpec((MOE_TILE, D_MODEL), lambda t, tok, exp, used: (t, 0)),
            scratch_shapes=[pltpu.VMEM((2, MOE_TILE, D_MODEL), F32), pltpu.SemaphoreType.DMA((2,))]),
        out_shape=jax.ShapeDtypeStruct((MOE_SLOTS, D_MODEL), F32),
        compiler_params=_cparams(("arbitrary",)),
        name="moe_experts",
    )(slot_token, tile_expert, n_used, h, w_gu, w_gu, w_d, slot_weight)
    return y, slot


def _moe_combine_kernel(slot_ref, y_hbm, x_ref, gf_ref, o_ref, y_buf, sems):
    i = pl.program_id(0)
    tm = x_ref.shape[0]

    def start_gather(tile, half):
        for k in range(TOP_K):
            for r in range(tm):
                _row_copy(y_hbm, slot_ref[k * N_TOK + tile * tm + r], y_buf.at[half, k], r, sems.at[half]).start()

    @pl.when(i == 0)
    def _():
        start_gather(0, 0)

    @pl.when(i + 1 < pl.num_programs(0))
    def _():
        start_gather(i + 1, (i + 1) % 2)

    half = i % 2
    for k in range(TOP_K):
        for r in range(tm):
            _row_copy(y_hbm, 0, y_buf.at[half, k], r, sems.at[half]).wait()
    o_ref[...] = x_ref[...] + gf_ref[...] * (y_buf[half, 0] + y_buf[half, 1])


def moe_combine(y, slot, x1, mod, l):
    tm = 256
    return pl.pallas_call(
        _moe_combine_kernel,
        grid_spec=pltpu.PrefetchScalarGridSpec(
            num_scalar_prefetch=1,
            grid=(N_TOK // tm,),
            in_specs=[pl.BlockSpec(memory_space=pl.ANY),
                      pl.BlockSpec((tm, D_MODEL), lambda i, s: (i, 0)),
                      pl.BlockSpec((None, None, 1, D_MODEL), lambda i, s: (l, _cond_row(i, tm), 0, 5))],
            out_specs=pl.BlockSpec((tm, D_MODEL), lambda i, s: (i, 0)),
            scratch_shapes=[pltpu.VMEM((2, TOP_K, tm, D_MODEL), F32), pltpu.SemaphoreType.DMA((2,))]),
        out_shape=jax.ShapeDtypeStruct((N_TOK, D_MODEL), F32),
        compiler_params=_cparams(("arbitrary",)),
        name="moe_combine",
    )(slot, y, x1, mod)


def _final_norm_kernel(x_ref, g_ref, o_ref):
    o_ref[...] = _rms(x_ref[...], g_ref[...])


def final_norm_call(x, g, tok0, n_tok):
    tm = 1024
    i0 = tok0 // tm
    return pl.pallas_call(
        _final_norm_kernel,
        grid=(n_tok // tm,),
        in_specs=[pl.BlockSpec((tm, D_MODEL), lambda i: (i0 + i, 0)),
                  pl.BlockSpec((1, D_MODEL), lambda i: (0, 0))],
        out_specs=pl.BlockSpec((tm, D_MODEL), lambda i: (i, 0)),
        out_shape=jax.ShapeDtypeStruct((n_tok, D_MODEL), F32),
        compiler_params=_cparams(("arbitrary",)),
        name="final_norm",
    )(x, g)


def _cache_store_kernel(*refs):
    n = len(refs) // 3
    for src, dst in zip(refs[:n], refs[2 * n:]):
        dst[...] = src[...].reshape(dst.shape)


def cache_store(sources, all_layers, l):
    n_sq = 2
    tm = n_sq * SEQ
    return pl.pallas_call(
        _cache_store_kernel,
        grid=(N_CTX // tm,),
        in_specs=([pl.BlockSpec((tm, s.shape[1]), lambda i: (i, 0)) for s in sources]
                  + [pl.BlockSpec(memory_space=pl.ANY)] * len(all_layers)),
        out_specs=[pl.BlockSpec((n_sq, None, SEQ, a.shape[-1]), lambda i: (i, l, 0, 0)) for a in all_layers],
        out_shape=[jax.ShapeDtypeStruct(a.shape, F32) for a in all_layers],
        input_output_aliases={len(sources) + k: k for k in range(len(all_layers))},
        compiler_params=_cparams(("arbitrary",)),
        name="cache_store",
    )(*sources, *all_layers)


def _gate_row(p):
    return jnp.zeros((1, LANES), F32).at[0, GATE_A:GATE_A + 2 * H_A].set(p.reshape(-1))


def _tile_lanes(g, width):
    return jnp.tile(g, width // g.shape[0])[None, :]


def kernel(x_prompt, x_sample, state_delta, cache_gqa_k, cache_gqa_v, cache_diff_k, cache_diff_v, c, c_ctx,
           ada_w, ada_b, norm_mix, norm_ffn, w_in, conv_w, dn_a_log, dn_dt_bias, dn_norm, gqa_q_norm,
           gqa_k_norm, diff_lambda, diff_norm, w_out, ffn_w_gu, ffn_w_down, moe_router, moe_w_gu, moe_w_down,
           final_norm):
    cond8 = jnp.concatenate([c_ctx[None, :], c, jnp.zeros((8 - N_COND, D_MODEL), F32)], axis=0)
    gate_lo = 4 * A_W
    gate_hi = gate_lo + 4 * H_A
    w_in_p = jnp.concatenate(
        [w_in[:, :, :gate_lo], w_in[:, :, gate_hi:], w_in[:, :, gate_lo:gate_hi],
         jnp.zeros((DEPTH, D_MODEL, LANES - 4 * H_A), F32)], axis=-1).astype(BF16)
    w_out_b = w_out.astype(BF16)
    norm_mix3 = norm_mix.reshape(DEPTH, 1, D_MODEL)
    norm_ffn3 = norm_ffn.reshape(DEPTH, 1, D_MODEL)
    ffn_gu_b, ffn_d_b = ffn_w_gu.astype(BF16), ffn_w_down.astype(BF16)
    moe_gu_b, moe_d_b = moe_w_gu.astype(BF16), moe_w_down.astype(BF16)
    router_p = jnp.pad(moe_router, ((0, 0), (0, 0), (0, LANES - N_EXP)))
    rope_b = _rope_tables(DEC_SEQ // GRID_W, HEAD_DIM)
    rope_c = _rope_tables(DEC_SEQ // GRID_W, DIFF_HD)
    s0_ctx = jnp.zeros((BATCH, 2, H_A, HEAD_DIM, HEAD_DIM), F32)
    cache_b = (cache_gqa_k.reshape(DEC_BATCH, DEPTH, PAST_LEN, KV_B * HEAD_DIM),
               cache_gqa_v.reshape(DEC_BATCH, DEPTH, PAST_LEN, KV_B * HEAD_DIM))
    cache_c = (cache_diff_k.reshape(DEC_BATCH, DEPTH, PAST_LEN, C_W),
               cache_diff_v.reshape(DEC_BATCH, DEPTH, PAST_LEN, C_W))

    mod = ada_modulation(cond8, ada_w, ada_b)[:, :N_COND].reshape(DEPTH, N_COND, 1, 6 * D_MODEL)

    x = jnp.concatenate([x_prompt.reshape(N_CTX, D_MODEL), x_sample.reshape(N_LAT, D_MODEL)], axis=0)
    all_states = jnp.zeros((BATCH, DEPTH, 2, H_A, HEAD_DIM, HEAD_DIM), F32)
    ctx_layers = tuple(jnp.zeros((BATCH, DEPTH, SEQ, w), F32)
                       for w in (KV_B * HEAD_DIM, KV_B * HEAD_DIM, C_W, C_W))
    ctx = dict(n_seq=BATCH, seq_len=SEQ, tok0=0)
    lat = dict(n_seq=DEC_BATCH, seq_len=DEC_SEQ, tok0=N_CTX)
    for l in range(DEPTH):
        qkv_a, z_a, q_b, k_b, v_b, q_c, k_c, v_c, gate = in_projection(x, mod, norm_mix3, w_in_p, l)

        dn_args = (qkv_a, z_a, gate, conv_w[l], _gate_row(dn_a_log[l]), _gate_row(dn_dt_bias[l]),
                   _tile_lanes(dn_norm[l], LANES))
        oa_ctx, all_states = delta_mixer(*dn_args, s0_ctx, n_sub=2, all_states=all_states, layer=l, **ctx)
        oa_lat, _ = delta_mixer(*dn_args, state_delta[:, l], n_sub=1, **lat)

        gq, gk = _tile_lanes(gqa_q_norm[l], LANES), _tile_lanes(gqa_k_norm[l], LANES)
        ob_ctx, kn_ctx = gqa_mixer(q_b, k_b, v_b, gq, gk, **ctx)
        (ob_lat,) = gqa_mixer(q_b, k_b, v_b, gq, gk, cache=cache_b, rope=rope_b, layer=l, **lat)

        lam_init = 0.8 - 0.6 * math.exp(-0.3 * l)
        gn = _tile_lanes(diff_norm[l], C_W)
        oc_ctx = diff_mixer(q_c, k_c, v_c, diff_lambda[l], gn, lam_init, **ctx)
        oc_lat = diff_mixer(q_c, k_c, v_c, diff_lambda[l], gn, lam_init, cache=cache_c, rope=rope_c, layer=l, **lat)

        mixes = ((oa_ctx, oa_lat), (ob_ctx, ob_lat), (oc_ctx, oc_lat))
        if l % 2 == 0:
            x1, h2 = out_projection(x, mixes, w_out_b, mod, norm_ffn3, l)
            x = ffn(h2, ffn_gu_b, ffn_d_b, x1, mod, l)
        else:
            x1, h2, route = out_projection(x, mixes, w_out_b, mod, norm_ffn3, l, router_p[l // 2])
            y_slots, slot = moe_experts(h2, route, moe_gu_b, moe_d_b, l)
            x = moe_combine(y_slots, slot, x1, mod, l)

        ctx_layers = cache_store((kn_ctx, v_b, k_c, v_c), ctx_layers, l)

    y_prompt = final_norm_call(x, final_norm[None, :], 0, N_CTX).reshape(BATCH, SEQ, D_MODEL)
    y_sample = final_norm_call(x, final_norm[None, :], N_CTX, N_LAT).reshape(DEC_BATCH, DEC_SEQ, D_MODEL)
    new_gk, new_gv, new_dk, new_dv = ctx_layers
    return (y_prompt, y_sample, all_states,
            new_gk.reshape(BATCH, DEPTH, SEQ, KV_B, HEAD_DIM), new_gv.reshape(BATCH, DEPTH, SEQ, KV_B, HEAD_DIM),
            new_dk.reshape(BATCH, DEPTH, SEQ, H_C, 2, DIFF_HD), new_dv.reshape(BATCH, DEPTH, SEQ, H_C, 2 * DIFF_HD))
```

```python
import functools
import math

import jax
import jax.numpy as jnp
from jax import lax
from jax.experimental import pallas as pl
from jax.experimental.pallas import tpu as pltpu

D_MODEL = 1024
BATCH = 16
SEQ = 256
DEPTH = 4
DEC_BATCH = 2
DEC_SEQ = 1024
PAST_LEN = 512
GRID_W = 64
HEAD_DIM = 64
H_A = 6
A_W = H_A * HEAD_DIM
H_B = 6
KV_B = 2
B_W = H_B * HEAD_DIM
H_C = 4
DIFF_HD = 32
C_W = H_C * 2 * DIFF_HD
CONV_W = 3
CHUNK = 64
ROPE_THETA = 10000.0
D_FF = 2816
N_EXP = 8
TOP_K = 2
D_FF_E = 1408
EPS = 1e-6

N_CTX = BATCH * SEQ
N_LAT = DEC_BATCH * DEC_SEQ
N_TOK = N_CTX + N_LAT
N_COND = 1 + DEC_BATCH
IN_PAD = 3072
FF_CHUNK = D_FF_E
LANES = 128
VMEM_LIMIT = 56 * 1024 * 1024

IN_SPLIT = (("qkv_a", 3 * A_W), ("z_a", A_W), ("q_b", B_W), ("k_b", KV_B * HEAD_DIM), ("v_b", KV_B * HEAD_DIM),
            ("q_c", C_W), ("k_c", C_W), ("v_c", C_W), ("gate", LANES))
GATE_BETA, GATE_A = 0, 2 * H_A

F32 = jnp.float32
BF16 = jnp.bfloat16
HI = lax.Precision.HIGHEST


def _cparams(sem):
    return pltpu.CompilerParams(dimension_semantics=sem, vmem_limit_bytes=VMEM_LIMIT)


def _dot(a, b):
    return jnp.dot(a, b, preferred_element_type=F32)


def _dot_hi(a, b):
    return jnp.dot(a, b, preferred_element_type=F32, precision=HI)


def _dot_nt(a, b):
    return lax.dot_general(a, b, (((1,), (1,)), ((), ())), preferred_element_type=F32)


def _silu(x):
    return x * jax.nn.sigmoid(x)


def _softplus(x):
    return jnp.maximum(x, 0.0) + jnp.log1p(jnp.exp(-jnp.abs(x)))


def _rms(x, g):
    return x * lax.rsqrt(jnp.mean(x * x, axis=-1, keepdims=True) + EPS) * g


def _lane(shape):
    return lax.broadcasted_iota(jnp.int32, shape, len(shape) - 1)


def _group_mean_matrix(width, group):
    i = lax.broadcasted_iota(jnp.int32, (width, width), 0)
    j = lax.broadcasted_iota(jnp.int32, (width, width), 1)
    return jnp.where(i // group == j // group, 1.0 / group, 0.0).astype(BF16)


def _group_rms(x, gmat, g):
    sq = x * x
    hi = sq.astype(BF16)
    lo = (sq - hi.astype(F32)).astype(BF16)
    ms = _dot(jnp.concatenate([hi, lo], axis=1), jnp.concatenate([gmat, gmat], axis=0))
    return x * lax.rsqrt(ms + EPS) * g


def _cond_row(i, tm):
    return jnp.maximum((i * tm) // DEC_SEQ - (N_CTX // DEC_SEQ - 1), 0)


def _ada_kernel(c_ref, w_ref, b_ref, o_ref):
    o_ref[...] = _dot_hi(_silu(c_ref[...]), w_ref[...]) + b_ref[...]


def ada_modulation(cond8, ada_w, ada_b):
    tn = 1024
    return pl.pallas_call(
        _ada_kernel,
        grid=(DEPTH, 6 * D_MODEL // tn),
        in_specs=[pl.BlockSpec((8, D_MODEL), lambda l, j: (0, 0)),
                  pl.BlockSpec((None, D_MODEL, tn), lambda l, j: (l, 0, j)),
                  pl.BlockSpec((None, 1, tn), lambda l, j: (l, 0, j))],
        out_specs=pl.BlockSpec((None, 8, tn), lambda l, j: (l, 0, j)),
        out_shape=jax.ShapeDtypeStruct((DEPTH, 8, 6 * D_MODEL), F32),
        compiler_params=_cparams(("arbitrary", "arbitrary")),
        name="ada_modulation",
    )(cond8, ada_w, ada_b.reshape(DEPTH, 1, 6 * D_MODEL))


def _inproj_kernel(x_ref, g_ref, sh_ref, sc_ref, w_ref, *o_refs):
    h = _rms(x_ref[...], g_ref[...]) * (1.0 + sc_ref[...]) + sh_ref[...]
    acc = _dot(h.astype(BF16), w_ref[...])
    off = 0
    for o_ref, (_, width) in zip(o_refs, IN_SPLIT):
        o_ref[...] = acc[:, off:off + width]
        off += width


def in_projection(x, mod, norm_g, w_in_p, l):
    tm = 512
    return pl.pallas_call(
        _inproj_kernel,
        grid=(N_TOK // tm,),
        in_specs=[pl.BlockSpec((tm, D_MODEL), lambda i: (i, 0)),
                  pl.BlockSpec((None, 1, D_MODEL), lambda i: (l, 0, 0)),
                  pl.BlockSpec((None, None, 1, D_MODEL), lambda i: (l, _cond_row(i, tm), 0, 0)),
                  pl.BlockSpec((None, None, 1, D_MODEL), lambda i: (l, _cond_row(i, tm), 0, 1)),
                  pl.BlockSpec((None, D_MODEL, IN_PAD), lambda i: (l, 0, 0))],
        out_specs=[pl.BlockSpec((tm, w), lambda i: (i, 0)) for _, w in IN_SPLIT],
        out_shape=[jax.ShapeDtypeStruct((N_TOK, w), F32) for _, w in IN_SPLIT],
        compiler_params=_cparams(("arbitrary",)),
        name="in_projection",
    )(x, norm_g, mod, mod, w_in_p)


SOLVE_BLOCK = 16


def _solve_unit_triangular(lmat, rhs, lo, diag_blk, eye):
    def both(a):
        return jnp.concatenate([jnp.where(lo, a, 0.0), jnp.where(lo, 0.0, a)], axis=0)

    neg_d = jnp.where(diag_blk, -lmat, 0.0)
    pd = both(neg_d)
    po = both(-lmat - neg_d)
    t = eye + pd
    p = _dot(pd.astype(BF16), pd.astype(BF16))
    n_fac = int(math.log2(SOLVE_BLOCK))
    for j in range(1, n_fac):
        if j + 1 < n_fac:
            tp = _dot(p.astype(BF16), jnp.concatenate([t, p], axis=1).astype(BF16))
            t = t + tp[:, :LANES]
            p = tp[:, LANES:]
        else:
            t = t + _dot(p.astype(BF16), t.astype(BF16))
    cm = _dot(t.astype(BF16), jnp.concatenate([rhs, po], axis=1).astype(BF16))
    c, m = cm[:, :LANES], cm[:, LANES:].astype(BF16)
    x = c
    for _ in range(CHUNK // SOLVE_BLOCK - 1):
        x = c + _dot(m, x.astype(BF16))
    return x


def _delta_kernel(qkv_ref, z_ref, gate_ref, cw_ref, alog_ref, dtb_ref, ng_ref, s0_ref, o_ref, sfin_ref,
                  pad_s, kk_s, qq_s, vk_s, gc_s, beta_s, u_s, w_s, qd_s, a_s, kdt_s, egt_s, sa_s, o_s,
                  *, seq_len):
    n_chunk = seq_len // CHUNK
    n_pair = H_A // 2
    half = HEAD_DIM
    lo = _lane((CHUNK, LANES)) < half
    lo_row = _lane((1, LANES)) < half
    lo16 = _lane((CHUNK, LANES)).astype(F32).astype(BF16) < half
    row = lax.broadcasted_iota(jnp.int32, (CHUNK, LANES), 0)
    col = _lane((CHUNK, LANES)) % half
    ahead = jnp.where(lo, row - col, col - row)
    incl = ahead >= 0
    strict = ahead > 0
    diag_blk = row // SOLVE_BLOCK == col // SOLVE_BLOCK
    big_row = lax.broadcasted_iota(jnp.int32, (LANES, LANES), 0)
    anti = big_row // half + _lane((LANES, LANES)) // half == 1
    eye = (big_row == _lane((LANES, LANES))).astype(F32)

    rb = 128
    zero8 = jnp.zeros((8, 3 * A_W), F32)
    pad_s[0:8, :] = zero8
    pad_s[8 + seq_len:16 + seq_len, :] = zero8
    pad_s[8:8 + seq_len, :] = qkv_ref[...]
    lo_rb = _lane((rb, LANES)) < half
    for r0 in range(0, seq_len, rb):
        y = _silu(pad_s[7 + r0:7 + r0 + rb, :] * cw_ref[0:1, :] + pad_s[8 + r0:8 + r0 + rb, :] * cw_ref[1:2, :]
                  + pad_s[9 + r0:9 + r0 + rb, :] * cw_ref[2:3, :])

        def dup(part, p):
            c = y[:, part * A_W + p * LANES:part * A_W + (p + 1) * LANES]
            r = pltpu.roll(c, half, axis=1)
            return jnp.where(lo_rb, c, r), jnp.where(lo_rb, r, c)

        def l2n(x):
            return x * lax.rsqrt(0.5 * jnp.sum(x * x, axis=-1, keepdims=True) + EPS)

        for p in range(n_pair):
            qs, ks, vs = dup(0, p), dup(1, p), dup(2, p)
            for e in range(2):
                h = 2 * p + e
                kk = l2n(ks[e])
                kk_s[h, r0:r0 + rb, :] = kk
                qq_s[h, r0:r0 + rb, :] = l2n(qs[e]) * (HEAD_DIM ** -0.5)
                vk_s[h, r0:r0 + rb, :] = jnp.where(lo_rb, vs[e], kk)

    gt = gate_ref[...]
    beta_s[...] = jax.nn.sigmoid(gt)
    g = -jnp.exp(alog_ref[...]) * _softplus(gt + dtb_ref[...])
    g1 = g.astype(BF16)
    r1 = g - g1.astype(F32)
    g2 = r1.astype(BF16)
    g3 = (r1 - g2.astype(F32)).astype(BF16)
    g_split = jnp.concatenate([g1, g2, g3], axis=1)
    cb = 256
    ci = lax.broadcasted_iota(jnp.int32, (cb, cb), 0)
    cj = lax.broadcasted_iota(jnp.int32, (cb, cb), 1)
    same = ci // CHUNK == cj // CHUNK
    m_pre = (same & (ci >= cj)).astype(BF16)
    m_suf = (same & (ci <= cj)).astype(BF16)
    fwd_cols = (_lane((cb, LANES)) >= GATE_A) & (_lane((cb, LANES)) < GATE_A + H_A)
    for r0 in range(0, seq_len, cb):
        blk = g_split[r0:r0 + cb, :]
        pre, suf = _dot(m_pre, blk), _dot(m_suf, blk)
        pre = pre[:, :LANES] + pre[:, LANES:2 * LANES] + pre[:, 2 * LANES:]
        suf = suf[:, :LANES] + suf[:, LANES:2 * LANES] + suf[:, 2 * LANES:]
        gc_s[r0:r0 + cb, :] = jnp.where(fwd_cols, pre, suf)

    for h in range(H_A):
        zero = jnp.zeros((half, half), F32)
        sa_s[h] = jnp.concatenate([jnp.concatenate([zero, s0_ref[1, h]], axis=1),
                                   jnp.concatenate([s0_ref[0, h], zero], axis=1)], axis=0)

    def prep(c, carry):
        rows = pl.ds(pl.multiple_of(c * CHUNK, CHUNK), CHUNK)
        gcb = gc_s[rows, :]
        gct = gcb.T
        bb = beta_s[rows, :]
        for h in range(H_A):
            cf = gcb[:, GATE_A + h:GATE_A + h + 1]
            cbk = gcb[:, GATE_A + H_A + h:GATE_A + H_A + h + 1]
            gc_fb = jnp.where(lo, cf, cbk)
            gr_fb = jnp.concatenate([gct[GATE_A + h:GATE_A + h + 1, :],
                                     gct[GATE_A + H_A + h:GATE_A + H_A + h + 1, :]], axis=1)
            decay = jnp.where(incl, jnp.exp(jnp.where(incl, gc_fb - gr_fb, 0.0)), 0.0)
            bf = bb[:, GATE_BETA + h:GATE_BETA + h + 1]
            bbk = bb[:, GATE_BETA + H_A + h:GATE_BETA + H_A + h + 1]
            egf, egb = jnp.exp(cf), jnp.exp(cbk)
            tot_f = gcb[CHUNK - 1:CHUNK, GATE_A + h:GATE_A + h + 1]
            tot_b = gcb[0:1, GATE_A + H_A + h:GATE_A + H_A + h + 1]
            kk, qq, vk = kk_s[h, rows, :], qq_s[h, rows, :], vk_s[h, rows, :]
            k16 = kk[:, :half].astype(BF16)
            kq = jnp.concatenate([k16, qq[:, :half].astype(BF16)], axis=0)
            gram = _dot_nt(kq, jnp.concatenate([k16, k16], axis=0))
            lmat = jnp.where(strict, gram[:CHUNK] * jnp.where(lo, bf, bbk) * decay, 0.0)
            a_s[h, rows, :] = jnp.where(incl, gram[CHUNK:] * decay, 0.0).astype(BF16)
            x_f = vk * jnp.where(lo, bf, bf * egf)
            x_b = vk * jnp.where(lo, bbk, bbk * egb)
            x = _solve_unit_triangular(lmat, jnp.concatenate([x_f, x_b], axis=0), lo, diag_blk, eye)
            x_f, x_b = x[:CHUNK], pltpu.roll(x[CHUNK:], half, axis=1)
            u_s[h, rows, :] = jnp.where(lo, x_f, x_b)
            w_s[h, rows, :] = jnp.where(lo, x_b, x_f).astype(BF16)
            qd_s[h, rows, :] = (qq * jnp.where(lo, egb, egf)).astype(BF16)
            kd = kk * jnp.where(lo, jnp.exp(tot_b - cbk), jnp.exp(tot_f - cf))
            kdt_s[h, c] = kd.T.astype(BF16)
            egt_s[h, c] = jnp.exp(jnp.where(lo_row, tot_f, tot_b))
        return carry

    lax.fori_loop(0, n_chunk, prep, 0)

    def scan(s, carry):
        rf = pl.ds(pl.multiple_of(s * CHUNK, CHUNK), CHUNK)
        sb = n_chunk - 1 - s
        rbk = pl.ds(pl.multiple_of(sb * CHUNK, CHUNK), CHUNK)
        for h in range(H_A):
            u = jnp.where(lo, u_s[h, rf, :], u_s[h, rbk, :])
            w = jnp.where(lo16, w_s[h, rbk, :], w_s[h, rf, :])
            qd = jnp.where(lo16, qd_s[h, rbk, :], qd_s[h, rf, :])
            a = jnp.where(lo16, a_s[h, rf, :], a_s[h, rbk, :])
            kdt = jnp.concatenate([kdt_s[h, sb][:half], kdt_s[h, s][half:]], axis=0)
            egt = jnp.where(lo_row, egt_s[h, s], egt_s[h, sb])
            st = sa_s[h]
            wq = _dot(jnp.concatenate([w, qd], axis=0), st.astype(BF16))
            v = u - wq[:CHUNK]
            vbd = jnp.concatenate([jnp.where(lo, v, 0.0), jnp.where(lo, 0.0, v)], axis=0).astype(BF16)
            o = wq[CHUNK:] + _dot(a, vbd)
            sa_s[h] = st * egt + jnp.where(anti, _dot(kdt, v.astype(BF16)), 0.0)
            o_s[h, rf, 0:half] = o[:, :half]
            o_s[h, rbk, half:LANES] = o[:, half:]
        return carry

    lax.fori_loop(0, n_chunk, scan, 0)

    for r0 in range(0, seq_len, rb):
        for p in range(n_pair):
            nrm = []
            for e in range(2):
                ofb = o_s[2 * p + e, r0:r0 + rb, :]
                oo = ofb + pltpu.roll(ofb, half, axis=1)
                nrm.append(oo * lax.rsqrt(jnp.sum(oo * oo, axis=-1, keepdims=True) * (0.5 / HEAD_DIM) + EPS))
            o_ref[r0:r0 + rb, p * LANES:(p + 1) * LANES] = (
                jnp.where(lo_rb, nrm[0], nrm[1]) * ng_ref[...] * _silu(z_ref[r0:r0 + rb, p * LANES:(p + 1) * LANES]))

    for h in range(H_A):
        st = sa_s[h]
        sfin_ref[0, h] = st[half:, :half]
        sfin_ref[1, h] = st[:half, half:]


def delta_mixer(qkv, z, gate, cw, alog_row, dtb_row, ng2, s0, *, n_seq, seq_len, tok0):
    n_chunk = seq_len // CHUNK
    b0 = tok0 // seq_len
    tok = lambda w: pl.BlockSpec((seq_len, w), lambda b: (b0 + b, 0))
    full = lambda a: pl.BlockSpec(a.shape, lambda b: (0,) * a.ndim)
    state = pl.BlockSpec((None, 2, H_A, HEAD_DIM, HEAD_DIM), lambda b: (b, 0, 0, 0, 0))
    per_head = lambda dt: pltpu.VMEM((H_A, seq_len, LANES), dt)
    return pl.pallas_call(
        functools.partial(_delta_kernel, seq_len=seq_len),
        grid=(n_seq,),
        in_specs=[tok(3 * A_W), tok(A_W), tok(LANES), full(cw), full(alog_row), full(dtb_row), full(ng2), state],
        out_specs=[pl.BlockSpec((seq_len, A_W), lambda b: (b, 0)), state],
        out_shape=[jax.ShapeDtypeStruct((n_seq * seq_len, A_W), F32),
                   jax.ShapeDtypeStruct((n_seq, 2, H_A, HEAD_DIM, HEAD_DIM), F32)],
        scratch_shapes=[pltpu.VMEM((seq_len + 16, 3 * A_W), F32),
                        per_head(F32), per_head(F32), per_head(F32),
                        pltpu.VMEM((seq_len, LANES), F32),
                        pltpu.VMEM((seq_len, LANES), F32),
                        per_head(F32), per_head(BF16), per_head(BF16), per_head(BF16),
                        pltpu.VMEM((H_A, n_chunk, LANES, HEAD_DIM), BF16),
                        pltpu.VMEM((H_A, n_chunk, 1, LANES), F32),
                        pltpu.VMEM((H_A, LANES, LANES), F32),
                        per_head(F32)],
        compiler_params=_cparams(("arbitrary",)),
        name=f"delta_mixer_{seq_len}",
    )(qkv, z, gate, cw, alog_row, dtb_row, ng2, s0)


def _delta_kernel(*refs, seq_len, n_sub, aliased):
    qkv_ref, z_ref, gate_ref, cw_ref, alog_ref, dtb_ref, ng_ref, s0_ref = refs[:8]
    (o_ref, sfin_ref, pad_s, kk_s, qq_s, vk_s, gc_s, beta_s, u_s, w_s, qd_s, a_s, kdt_s, egt_s, sa_s, o_s,
     tp_s, t32_s, rp_s, c32_s, m_s, x_s, vbd_s, v_s, op_s) = refs[9 if aliased else 8:]
    n_chunk = seq_len // CHUNK
    n_pair = H_A // 2
    n_unit = n_sub * H_A
    half = HEAD_DIM
    lo = _lane((CHUNK, LANES)) < half
    lo_row = _lane((1, LANES)) < half
    lo16 = _lane((CHUNK, LANES)).astype(F32).astype(BF16) < half
    row = lax.broadcasted_iota(jnp.int32, (CHUNK, LANES), 0)
    col = _lane((CHUNK, LANES)) % half
    ahead = jnp.where(lo, row - col, col - row)
    incl = ahead >= 0
    strict = ahead > 0
    diag_blk = row // SOLVE_BLOCK == col // SOLVE_BLOCK
    big_row = lax.broadcasted_iota(jnp.int32, (LANES, LANES), 0)
    anti = big_row // half + _lane((LANES, LANES)) // half == 1
    eye = (big_row == _lane((LANES, LANES))).astype(F32)

    def both(a):
        return jnp.concatenate([jnp.where(lo, a, 0.0), jnp.where(lo, 0.0, a)], axis=0)

    rb = 128
    lo_rb = _lane((rb, LANES)) < half
    zero8 = jnp.zeros((8, 3 * A_W), F32)
    pad_s[0:8, :] = zero8
    pad_s[8 + seq_len:16 + seq_len, :] = zero8

    def l2n(x):
        return x * lax.rsqrt(0.5 * jnp.sum(x * x, axis=-1, keepdims=True) + EPS)

    for sq in range(n_sub):
        pad_s[8:8 + seq_len, :] = qkv_ref[sq * seq_len:(sq + 1) * seq_len, :]
        for r0 in range(0, seq_len, rb):
            y = _silu(pad_s[7 + r0:7 + r0 + rb, :] * cw_ref[0:1, :] + pad_s[8 + r0:8 + r0 + rb, :] * cw_ref[1:2, :]
                      + pad_s[9 + r0:9 + r0 + rb, :] * cw_ref[2:3, :])

            def dup(part, p):
                c = y[:, part * A_W + p * LANES:part * A_W + (p + 1) * LANES]
                r = pltpu.roll(c, half, axis=1)
                return jnp.where(lo_rb, c, r), jnp.where(lo_rb, r, c)

            for p in range(n_pair):
                qs, ks, vs = dup(0, p), dup(1, p), dup(2, p)
                for e in range(2):
                    u = sq * H_A + 2 * p + e
                    kk = l2n(ks[e])
                    kk_s[u, r0:r0 + rb, :] = kk
                    qq_s[u, r0:r0 + rb, :] = l2n(qs[e]) * (HEAD_DIM ** -0.5)
                    vk_s[u, r0:r0 + rb, :] = jnp.where(lo_rb, vs[e], kk)

    cb = 256
    ci = lax.broadcasted_iota(jnp.int32, (cb, cb), 0)
    cj = lax.broadcasted_iota(jnp.int32, (cb, cb), 1)
    same = ci // CHUNK == cj // CHUNK
    m_pre = jnp.where(same & (ci >= cj), 1.0, 0.0).astype(BF16)
    m_suf = jnp.where(same & (ci <= cj), 1.0, 0.0).astype(BF16)
    fwd_cols = (_lane((cb, LANES)) >= GATE_A) & (_lane((cb, LANES)) < GATE_A + H_A)
    for r0 in range(0, n_sub * seq_len, cb):
        gt = gate_ref[r0:r0 + cb, :]
        beta_s[r0:r0 + cb, :] = jax.nn.sigmoid(gt)
        g = -jnp.exp(alog_ref[...]) * _softplus(gt + dtb_ref[...])
        g1 = g.astype(BF16)
        r1 = g - g1.astype(F32)
        g2 = r1.astype(BF16)
        g3 = (r1 - g2.astype(F32)).astype(BF16)
        blk = jnp.concatenate([g1, g2, g3], axis=1)
        pre, suf = _dot(m_pre, blk), _dot(m_suf, blk)
        pre = pre[:, :LANES] + pre[:, LANES:2 * LANES] + pre[:, 2 * LANES:]
        suf = suf[:, :LANES] + suf[:, LANES:2 * LANES] + suf[:, 2 * LANES:]
        gc_s[r0:r0 + cb, :] = jnp.where(fwd_cols, pre, suf)

    zero = jnp.zeros((half, half), F32)
    for sq in range(n_sub):
        for h in range(H_A):
            sa_s[sq * H_A + h] = jnp.concatenate([jnp.concatenate([zero, s0_ref[sq, 1, h]], axis=1),
                                                  jnp.concatenate([s0_ref[sq, 0, h], zero], axis=1)], axis=0)

    n_fac = int(math.log2(SOLVE_BLOCK))

    def prep(c, carry):
        rows = pl.ds(pl.multiple_of(c * CHUNK, CHUNK), CHUNK)
        for sq in range(n_sub):
            grow = pl.ds(pl.multiple_of(sq * seq_len + c * CHUNK, CHUNK), CHUNK)
            gcb = gc_s[grow, :]
            gct = gcb.T
            bb = beta_s[grow, :]
            for h in range(H_A):
                u = sq * H_A + h
                full = (CHUNK, LANES)
                cff = jnp.broadcast_to(gcb[:, GATE_A + h:GATE_A + h + 1], full)
                cbb = jnp.broadcast_to(gcb[:, GATE_A + H_A + h:GATE_A + H_A + h + 1], full)
                bff = jnp.broadcast_to(bb[:, GATE_BETA + h:GATE_BETA + h + 1], full)
                bbb = jnp.broadcast_to(bb[:, GATE_BETA + H_A + h:GATE_BETA + H_A + h + 1], full)
                gr_fb = jnp.concatenate([gct[GATE_A + h:GATE_A + h + 1, :],
                                         gct[GATE_A + H_A + h:GATE_A + H_A + h + 1, :]], axis=1)
                decay = jnp.where(incl, jnp.exp(jnp.where(incl, jnp.where(lo, cff, cbb) - gr_fb, 0.0)), 0.0)
                egf, egb = jnp.exp(cff), jnp.exp(cbb)
                tot_f = gcb[CHUNK - 1:CHUNK, GATE_A + h:GATE_A + h + 1]
                tot_b = gcb[0:1, GATE_A + H_A + h:GATE_A + H_A + h + 1]
                kk, qq, vk = kk_s[u, rows, :], qq_s[u, rows, :], vk_s[u, rows, :]
                k16 = kk[:, :half].astype(BF16)
                kq = jnp.concatenate([k16, qq[:, :half].astype(BF16)], axis=0)
                gram = _dot_nt(kq, jnp.concatenate([k16, k16], axis=0))
                neg_l = jnp.where(strict, gram[:CHUNK] * jnp.where(lo, -bff, -bbb) * decay, 0.0)
                a_s[u, rows, :] = jnp.where(incl, gram[CHUNK:] * decay, 0.0).astype(BF16)
                neg_d = jnp.where(diag_blk, neg_l, 0.0)
                pd = both(neg_d)
                t32_s[u] = eye + pd
                tp_s[u, :, :LANES] = (eye + pd).astype(BF16)
                tp_s[u, :, LANES:] = pd.astype(BF16)
                x_f = vk * jnp.where(lo, bff, bff * egf)
                x_b = vk * jnp.where(lo, bbb, bbb * egb)
                rp_s[u, :, :LANES] = jnp.concatenate([x_f, x_b], axis=0).astype(BF16)
                rp_s[u, :, LANES:] = both(neg_l - neg_d).astype(BF16)
                qd_s[u, rows, :] = (qq * jnp.where(lo, egb, egf)).astype(BF16)
                kd = kk * jnp.exp(jnp.where(lo, tot_b - cbb, tot_f - cff))
                kdt_s[u, c] = kd.T.astype(BF16)
                egt_s[u, c] = jnp.exp(jnp.where(lo_row, tot_f, tot_b))
        for u in range(n_unit):
            tp_s[u, :, LANES:] = _dot(tp_s[u, :, LANES:], tp_s[u, :, LANES:]).astype(BF16)
        for j in range(1, n_fac):
            for u in range(n_unit):
                if j + 1 < n_fac:
                    tp = _dot(tp_s[u, :, LANES:], tp_s[u])
                    t = t32_s[u] + tp[:, :LANES]
                    t32_s[u] = t
                    tp_s[u, :, :LANES] = t.astype(BF16)
                    tp_s[u, :, LANES:] = tp[:, LANES:].astype(BF16)
                else:
                    t = t32_s[u] + _dot(tp_s[u, :, LANES:], tp_s[u, :, :LANES])
                    tp_s[u, :, :LANES] = t.astype(BF16)
        for u in range(n_unit):
            cm = _dot(tp_s[u, :, :LANES], rp_s[u])
            c32_s[u] = cm[:, :LANES]
            x_s[u] = cm[:, :LANES].astype(BF16)
            m_s[u] = cm[:, LANES:].astype(BF16)
        n_sweep = CHUNK // SOLVE_BLOCK - 1
        for it in range(n_sweep):
            for u in range(n_unit):
                x = c32_s[u] + _dot(m_s[u], x_s[u])
                if it + 1 < n_sweep:
                    x_s[u] = x.astype(BF16)
                else:
                    x_f, x_b = x[:CHUNK], pltpu.roll(x[CHUNK:], half, axis=1)
                    u_s[u, rows, :] = jnp.where(lo, x_f, x_b)
                    w_s[u, rows, :] = jnp.where(lo, x_b, x_f).astype(BF16)
        return carry

    lax.fori_loop(0, n_chunk, prep, 0)

    def scan(s, carry):
        rf = pl.ds(pl.multiple_of(s * CHUNK, CHUNK), CHUNK)
        sb = n_chunk - 1 - s
        rbk = pl.ds(pl.multiple_of(sb * CHUNK, CHUNK), CHUNK)
        for u in range(n_unit):
            w = jnp.where(lo16, w_s[u, rbk, :], w_s[u, rf, :])
            qd = jnp.where(lo16, qd_s[u, rbk, :], qd_s[u, rf, :])
            wq = _dot(jnp.concatenate([w, qd], axis=0), sa_s[u].astype(BF16))
            v = jnp.where(lo, u_s[u, rf, :], u_s[u, rbk, :]) - wq[:CHUNK]
            v_s[u] = v.astype(BF16)
            vbd_s[u] = both(v).astype(BF16)
            op_s[u] = wq[CHUNK:]
        for u in range(n_unit):
            a = jnp.where(lo16, a_s[u, rf, :], a_s[u, rbk, :])
            o = op_s[u] + _dot(a, vbd_s[u])
            o_s[u, rf, 0:half] = o[:, :half]
            o_s[u, rbk, half:LANES] = o[:, half:]
            kdt = jnp.concatenate([kdt_s[u, sb][:half], kdt_s[u, s][half:]], axis=0)
            egt = jnp.where(lo_row, egt_s[u, s], egt_s[u, sb])
            sa_s[u] = sa_s[u] * egt + jnp.where(anti, _dot(kdt, v_s[u]), 0.0)
        return carry

    lax.fori_loop(0, n_chunk, scan, 0)

    for sq in range(n_sub):
        for r0 in range(0, seq_len, rb):
            tr = sq * seq_len + r0
            for p in range(n_pair):
                nrm = []
                for e in range(2):
                    ofb = o_s[sq * H_A + 2 * p + e, r0:r0 + rb, :]
                    oo = ofb + pltpu.roll(ofb, half, axis=1)
                    nrm.append(oo * lax.rsqrt(jnp.sum(oo * oo, axis=-1, keepdims=True) * (0.5 / HEAD_DIM) + EPS))
                o_ref[tr:tr + rb, p * LANES:(p + 1) * LANES] = (
                    jnp.where(lo_rb, nrm[0], nrm[1]) * ng_ref[...]
                    * _silu(z_ref[tr:tr + rb, p * LANES:(p + 1) * LANES]))
        for h in range(H_A):
            st = sa_s[sq * H_A + h]
            sfin_ref[sq, 0, h] = st[half:, :half]
            sfin_ref[sq, 1, h] = st[:half, half:]


def delta_mixer(qkv, z, gate, cw, alog_row, dtb_row, ng2, s0, *, n_seq, seq_len, tok0, n_sub,
                all_states=None, layer=0):
    n_chunk = seq_len // CHUNK
    n_unit = n_sub * H_A
    blk = n_sub * seq_len
    b0 = tok0 // blk
    tok = lambda w: pl.BlockSpec((blk, w), lambda b: (b0 + b, 0))
    full = lambda a: pl.BlockSpec(a.shape, lambda b: (0,) * a.ndim)
    state = pl.BlockSpec((n_sub, 2, H_A, HEAD_DIM, HEAD_DIM), lambda b: (b, 0, 0, 0, 0))
    per_unit = lambda dt: pltpu.VMEM((n_unit, seq_len, LANES), dt)
    stage = lambda rows, cols, dt: pltpu.VMEM((n_unit, rows, cols), dt)
    aliased = all_states is not None
    in_specs = [tok(3 * A_W), tok(A_W), tok(LANES), full(cw), full(alog_row), full(dtb_row), full(ng2), state]
    args = [qkv, z, gate, cw, alog_row, dtb_row, ng2, s0]
    if aliased:
        in_specs.append(pl.BlockSpec(memory_space=pl.ANY))
        args.append(all_states)
        state_out = pl.BlockSpec((n_sub, None, 2, H_A, HEAD_DIM, HEAD_DIM), lambda b: (b, layer, 0, 0, 0, 0))
        state_shape = jax.ShapeDtypeStruct(all_states.shape, F32)
    else:
        state_out = state
        state_shape = jax.ShapeDtypeStruct((n_seq, 2, H_A, HEAD_DIM, HEAD_DIM), F32)
    return pl.pallas_call(
        functools.partial(_delta_kernel, seq_len=seq_len, n_sub=n_sub, aliased=aliased),
        grid=(n_seq // n_sub,),
        in_specs=in_specs,
        out_specs=[pl.BlockSpec((blk, A_W), lambda b: (b, 0)), state_out],
        out_shape=[jax.ShapeDtypeStruct((n_seq * seq_len, A_W), F32), state_shape],
        input_output_aliases={len(args) - 1: 1} if aliased else {},
        scratch_shapes=[pltpu.VMEM((seq_len + 16, 3 * A_W), F32),
                        per_unit(F32), per_unit(F32), per_unit(F32),
                        pltpu.VMEM((blk, LANES), F32),
                        pltpu.VMEM((blk, LANES), F32),
                        per_unit(F32), per_unit(BF16), per_unit(BF16), per_unit(BF16),
                        pltpu.VMEM((n_unit, n_chunk, LANES, HEAD_DIM), BF16),
                        pltpu.VMEM((n_unit, n_chunk, 1, LANES), F32),
                        stage(LANES, LANES, F32),
                        per_unit(F32),
                        stage(LANES, 2 * LANES, BF16),
                        stage(LANES, LANES, F32),
                        stage(LANES, 2 * LANES, BF16),
                        stage(LANES, LANES, F32),
                        stage(LANES, LANES, BF16),
                        stage(LANES, LANES, BF16),
                        stage(LANES, LANES, BF16),
                        stage(CHUNK, LANES, BF16),
                        stage(CHUNK, LANES, F32)],
        compiler_params=_cparams(("arbitrary",)),
        name=f"delta_mixer_{seq_len}",
    )(*args)


def _rope(x, cos, sin_signed, quarter):
    width = x.shape[-1]
    swapped = jnp.where((_lane(x.shape) % (2 * quarter)) < quarter,
                        pltpu.roll(x, width - quarter, axis=1), pltpu.roll(x, quarter, axis=1))
    return x * cos + swapped * sin_signed


def _rope_tables(rows, dim):
    nf = dim // 4
    inv = ROPE_THETA ** (-jnp.arange(nf, dtype=F32) / nf)
    r = jnp.repeat(jnp.arange(rows, dtype=F32), GRID_W)
    c = jnp.tile(jnp.arange(GRID_W, dtype=F32), rows)
    ar, ac = r[:, None] * inv, c[:, None] * inv
    cos = jnp.concatenate([jnp.cos(ar), jnp.cos(ar), jnp.cos(ac), jnp.cos(ac)], axis=1)
    sin = jnp.concatenate([-jnp.sin(ar), jnp.sin(ar), -jnp.sin(ac), jnp.sin(ac)], axis=1)
    return jnp.tile(cos, (1, LANES // dim)), jnp.tile(sin, (1, LANES // dim))


def _gqa_kernel(*refs, cached):
    if cached:
        (q_ref, k_ref, v_ref, gq_ref, gk_ref, ck_ref, cv_ref, cosq_ref, sinq_ref, cosk_ref, sink_ref,
         o_ref, k_s, v_s) = refs
    else:
        q_ref, k_ref, v_ref, gq_ref, gk_ref, o_ref, kn_ref, k_s, v_s = refs
    gmat = _group_mean_matrix(LANES, HEAD_DIM)
    quarter = HEAD_DIM // 4

    @pl.when(pl.program_id(1) == 0)
    def _():
        k = _group_rms(k_ref[...], gmat, gk_ref[...])
        v = v_ref[...]
        if cached:
            k = _rope(k, cosk_ref[...], sink_ref[...], quarter)
            k = jnp.concatenate([ck_ref[...], k], axis=0)
            v = jnp.concatenate([cv_ref[...], v], axis=0)
        else:
            kn_ref[...] = k
        k_s[...] = k.astype(BF16)
        v_s[...] = v.astype(BF16)

    k, v = k_s[...], v_s[...]
    tq = q_ref.shape[0]
    lo = _lane((tq, LANES)) < HEAD_DIM
    group = H_B // KV_B
    q_scale = (HEAD_DIM ** -0.5) * math.log2(math.e)
    qs = []
    for j in range(H_B // 2):
        q = _group_rms(q_ref[:, j * LANES:(j + 1) * LANES], gmat, gq_ref[...])
        if cached:
            q = _rope(q, cosq_ref[...], sinq_ref[...], quarter)
        q = q * q_scale
        qs.append((q, pltpu.roll(q, HEAD_DIM, axis=1)))
    outs = [None] * H_B
    for kv in range(KV_B):
        keep = lo if kv == 0 else ~lo
        heads = range(kv * group, (kv + 1) * group)
        q3 = jnp.concatenate([jnp.where(keep, qs[h // 2][0 if h % 2 == kv else 1], 0.0).astype(BF16)
                              for h in heads], axis=0)
        s = _dot_nt(q3, k)
        p = jnp.exp2(s - jnp.max(s, axis=-1, keepdims=True))
        r = _dot(p.astype(BF16), v) / jnp.sum(p, axis=-1, keepdims=True)
        for i, h in enumerate(heads):
            rh = r[i * tq:(i + 1) * tq]
            outs[h] = rh if h % 2 == kv else pltpu.roll(rh, HEAD_DIM, axis=1)
    for j in range(H_B // 2):
        o_ref[:, j * LANES:(j + 1) * LANES] = jnp.where(lo, outs[2 * j], outs[2 * j + 1])


def gqa_mixer(q, k, v, gq, gk, *, n_seq, seq_len, tok0, cache=None, rope=None, layer=0):
    tq = 256
    nq = seq_len // tq
    b0 = tok0 // seq_len
    q0 = tok0 // tq
    cached = cache is not None
    kvw = KV_B * HEAD_DIM
    in_specs = [pl.BlockSpec((tq, B_W), lambda b, i: (q0 + b * nq + i, 0)),
                pl.BlockSpec((seq_len, kvw), lambda b, i: (b0 + b, 0)),
                pl.BlockSpec((seq_len, kvw), lambda b, i: (b0 + b, 0)),
                pl.BlockSpec((1, LANES), lambda b, i: (0, 0)),
                pl.BlockSpec((1, LANES), lambda b, i: (0, 0))]
    args = [q, k, v, gq, gk]
    out_specs = [pl.BlockSpec((tq, B_W), lambda b, i: (b * nq + i, 0))]
    out_shape = [jax.ShapeDtypeStruct((n_seq * seq_len, B_W), F32)]
    if cached:
        cos, sin = rope
        cache_spec = pl.BlockSpec((None, None, PAST_LEN, kvw), lambda b, i: (b, layer, 0, 0))
        in_specs += [cache_spec, cache_spec,
                     pl.BlockSpec((tq, LANES), lambda b, i: (i, 0)),
                     pl.BlockSpec((tq, LANES), lambda b, i: (i, 0)),
                     pl.BlockSpec((seq_len, LANES), lambda b, i: (0, 0)),
                     pl.BlockSpec((seq_len, LANES), lambda b, i: (0, 0))]
        args += [cache[0], cache[1], cos, sin, cos, sin]
    else:
        out_specs.append(pl.BlockSpec((seq_len, kvw), lambda b, i: (b, 0)))
        out_shape.append(jax.ShapeDtypeStruct((n_seq * seq_len, kvw), F32))
    n_keys = seq_len + (PAST_LEN if cached else 0)
    return pl.pallas_call(
        functools.partial(_gqa_kernel, cached=cached),
        grid=(n_seq, nq),
        in_specs=in_specs, out_specs=out_specs, out_shape=out_shape,
        scratch_shapes=[pltpu.VMEM((n_keys, kvw), BF16), pltpu.VMEM((n_keys, kvw), BF16)],
        compiler_params=_cparams(("arbitrary", "arbitrary")),
        name=f"gqa_mixer_{seq_len}",
    )(*args)


def _diff_kernel(*refs, cached, lam_init):
    if cached:
        (q_ref, k_ref, v_ref, lam_ref, gn_ref, ck_ref, cv_ref, cosq_ref, sinq_ref, cosk_ref, sink_ref,
         o_ref) = refs
    else:
        q_ref, k_ref, v_ref, lam_ref, gn_ref, o_ref = refs
    quarter = DIFF_HD // 4
    lv = lam_ref[...]
    lam = (jnp.exp(jnp.sum(lv[0:1] * lv[1:2], axis=-1, keepdims=True))
           - jnp.exp(jnp.sum(lv[2:3] * lv[3:4], axis=-1, keepdims=True)) + lam_init)
    q, k, v = q_ref[...], k_ref[...], v_ref[...]
    if cached:
        cq = jnp.concatenate([cosq_ref[...]] * 2, axis=1)
        sq = jnp.concatenate([sinq_ref[...]] * 2, axis=1)
        ck = jnp.concatenate([cosk_ref[...]] * 2, axis=1)
        sk = jnp.concatenate([sink_ref[...]] * 2, axis=1)
        q = _rope(q, cq, sq, quarter)
        k = _rope(k, ck, sk, quarter)
        k = jnp.concatenate([ck_ref[...], k], axis=0)
        v = jnp.concatenate([cv_ref[...], v], axis=0)
    k = k.astype(BF16)
    v = v.astype(BF16)
    q = q * ((DIFF_HD ** -0.5) * math.log2(math.e))
    tq = q.shape[0]
    lane = _lane(q.shape)
    o = jnp.zeros(q.shape, F32)
    for h in range(H_C):
        ps, cs = [], []
        for m in range(2):
            qm = jnp.where(lane // DIFF_HD == 2 * h + m, q, 0.0).astype(BF16)
            s = _dot_nt(qm, k)
            p = jnp.exp2(s - jnp.max(s, axis=-1, keepdims=True))
            ps.append(p)
            cs.append((1.0 if m == 0 else lam) / jnp.sum(p, axis=-1, keepdims=True))
        a = ps[0] * cs[0] - ps[1] * cs[1]
        o = jnp.where(lane // (2 * DIFF_HD) == h, _dot(a.astype(BF16), v), o)
    o_ref[...] = _group_rms(o, _group_mean_matrix(C_W, 2 * DIFF_HD), gn_ref[...]) * (1.0 - lam_init)


def diff_mixer(q, k, v, lam_p, gn, lam_init, *, n_seq, seq_len, tok0, cache=None, rope=None, layer=0):
    tq = 256
    nq = seq_len // tq
    b0 = tok0 // seq_len
    q0 = tok0 // tq
    cached = cache is not None
    in_specs = [pl.BlockSpec((tq, C_W), lambda b, i: (q0 + b * nq + i, 0)),
                pl.BlockSpec((seq_len, C_W), lambda b, i: (b0 + b, 0)),
                pl.BlockSpec((seq_len, C_W), lambda b, i: (b0 + b, 0)),
                pl.BlockSpec((4, DIFF_HD), lambda b, i: (0, 0)),
                pl.BlockSpec((1, C_W), lambda b, i: (0, 0))]
    args = [q, k, v, lam_p, gn]
    if cached:
        cos, sin = rope
        cache_spec = pl.BlockSpec((None, None, PAST_LEN, C_W), lambda b, i: (b, layer, 0, 0))
        in_specs += [cache_spec, cache_spec,
                     pl.BlockSpec((tq, LANES), lambda b, i: (i, 0)),
                     pl.BlockSpec((tq, LANES), lambda b, i: (i, 0)),
                     pl.BlockSpec((seq_len, LANES), lambda b, i: (0, 0)),
                     pl.BlockSpec((seq_len, LANES), lambda b, i: (0, 0))]
        args += [cache[0], cache[1], cos, sin, cos, sin]
    return pl.pallas_call(
        functools.partial(_diff_kernel, cached=cached, lam_init=lam_init),
        grid=(n_seq, nq),
        in_specs=in_specs,
        out_specs=pl.BlockSpec((tq, C_W), lambda b, i: (b * nq + i, 0)),
        out_shape=jax.ShapeDtypeStruct((n_seq * seq_len, C_W), F32),
        compiler_params=_cparams(("arbitrary", "arbitrary")),
        name=f"diff_mixer_{seq_len}",
    )(*args)


def _outproj_kernel(*refs, routed, n_mix):
    x_ref = refs[0]
    mix_refs = refs[1:1 + 2 * n_mix]
    rest = refs[1 + 2 * n_mix:]
    if routed:
        w_ref, gm_ref, g_ref, sh_ref, sc_ref, r_ref, x1_ref, h_ref, lg_ref = rest
    else:
        w_ref, gm_ref, g_ref, sh_ref, sc_ref, x1_ref, h_ref = rest
    is_ctx = pl.program_id(0) < N_CTX // x_ref.shape[0]
    mixed = jnp.concatenate(
        [jnp.where(is_ctx, mix_refs[2 * m][...], mix_refs[2 * m + 1][...]) for m in range(n_mix)], axis=1)
    x1 = x_ref[...] + gm_ref[...] * _dot(mixed.astype(BF16), w_ref[...])
    x1_ref[...] = x1
    h = _rms(x1, g_ref[...]) * (1.0 + sc_ref[...]) + sh_ref[...]
    h_ref[...] = h.astype(h_ref.dtype)
    if routed:
        lg_ref[...] = _top2_route(_dot_hi(h, r_ref[...]))


def out_projection(x, mixes, w_out_b, mod, norm_g, l, router_p=None):
    tm = 512
    n_ctx_tiles = N_CTX // tm
    n_lat_tiles = N_LAT // tm
    routed = router_p is not None
    mod_spec = lambda k: pl.BlockSpec((None, None, 1, D_MODEL), lambda i: (l, _cond_row(i, tm), 0, k))
    in_specs = [pl.BlockSpec((tm, D_MODEL), lambda i: (i, 0))]
    args = [x]
    for ctx_arr, lat_arr in mixes:
        w = ctx_arr.shape[1]
        in_specs.append(pl.BlockSpec((tm, w), lambda i: (jnp.minimum(i, n_ctx_tiles - 1), 0)))
        in_specs.append(pl.BlockSpec((tm, w), lambda i: (jnp.clip(i - n_ctx_tiles, 0, n_lat_tiles - 1), 0)))
        args += [ctx_arr, lat_arr]
    in_specs += [pl.BlockSpec((None, D_MODEL, D_MODEL), lambda i: (l, 0, 0)),
                 mod_spec(2),
                 pl.BlockSpec((None, 1, D_MODEL), lambda i: (l, 0, 0)),
                 mod_spec(3), mod_spec(4)]
    args += [w_out_b, mod, norm_g, mod, mod]
    out_specs = [pl.BlockSpec((tm, D_MODEL), lambda i: (i, 0)),
                 pl.BlockSpec((tm, D_MODEL), lambda i: (i, 0))]
    out_shape = [jax.ShapeDtypeStruct((N_TOK, D_MODEL), F32),
                 jax.ShapeDtypeStruct((N_TOK, D_MODEL), F32 if routed else BF16)]
    if routed:
        in_specs.append(pl.BlockSpec((D_MODEL, LANES), lambda i: (0, 0)))
        args.append(router_p)
        out_specs.append(pl.BlockSpec((tm, LANES), lambda i: (i, 0)))
        out_shape.append(jax.ShapeDtypeStruct((N_TOK, LANES), F32))
    return pl.pallas_call(
        functools.partial(_outproj_kernel, routed=routed, n_mix=len(mixes)),
        grid=(N_TOK // tm,),
        in_specs=in_specs, out_specs=out_specs, out_shape=out_shape,
        compiler_params=_cparams(("arbitrary",)),
        name="out_projection",
    )(*args)


def _top2_gate(logits):
    lane = _lane(logits.shape)
    neg = jnp.float32(-jnp.inf)
    lg = jnp.where(lane < N_EXP, logits, neg)
    t1 = jnp.max(lg, axis=-1, keepdims=True)
    i1 = jnp.min(jnp.where(lg == t1, lane, LANES), axis=-1, keepdims=True)
    lg2 = jnp.where(lane == i1, neg, lg)
    t2 = jnp.max(lg2, axis=-1, keepdims=True)
    i2 = jnp.min(jnp.where(lg2 == t2, lane, LANES), axis=-1, keepdims=True)
    e2 = jnp.exp(t2 - t1)
    w1 = 1.0 / (1.0 + e2)
    w2 = e2 / (1.0 + e2)
    return jnp.where(lane == i1, w1, 0.0) + jnp.where(lane == i2, w2, 0.0)


def _ffn_kernel(*refs, routed):
    if routed:
        h_ref, wg_ref, wu_ref, wd_ref, x_ref, gf_ref, lg_ref, o_ref, acc_ref = refs
    else:
        h_ref, wg_ref, wu_ref, wd_ref, x_ref, gf_ref, o_ref, acc_ref = refs
    j = pl.program_id(1)

    @pl.when(j == 0)
    def _():
        acc_ref[...] = jnp.zeros_like(acc_ref)

    h = h_ref[...]
    act = _silu(_dot(h, wg_ref[...])) * _dot(h, wu_ref[...])
    if routed:
        gate = _top2_gate(lg_ref[...])
        act = act * jnp.sum(jnp.where(_lane(gate.shape) == j, gate, 0.0), axis=-1, keepdims=True)
    acc_ref[...] += _dot(act.astype(BF16), wd_ref[...])

    @pl.when(j == pl.num_programs(1) - 1)
    def _():
        o_ref[...] = x_ref[...] + gf_ref[...] * acc_ref[...]


def ffn(h, w_gu, w_d, x1, mod, l, logits=None):
    tm = 512
    lyr = l // 2
    routed = logits is not None
    if routed:
        n_chunk = N_EXP
        gu_spec = lambda half: pl.BlockSpec((None, None, D_MODEL, FF_CHUNK), lambda i, j: (lyr, j, 0, half))
        d_spec = pl.BlockSpec((None, None, FF_CHUNK, D_MODEL), lambda i, j: (lyr, j, 0, 0))
    else:
        n_chunk = D_FF // FF_CHUNK
        gu_spec = lambda half: pl.BlockSpec((None, D_MODEL, FF_CHUNK), lambda i, j: (lyr, 0, half * n_chunk + j))
        d_spec = pl.BlockSpec((None, FF_CHUNK, D_MODEL), lambda i, j: (lyr, j, 0))
    in_specs = [pl.BlockSpec((tm, D_MODEL), lambda i, j: (i, 0)),
                gu_spec(0), gu_spec(1), d_spec,
                pl.BlockSpec((tm, D_MODEL), lambda i, j: (i, 0)),
                pl.BlockSpec((None, None, 1, D_MODEL), lambda i, j: (l, _cond_row(i, tm), 0, 5))]
    args = [h, w_gu, w_gu, w_d, x1, mod]
    if routed:
        in_specs.append(pl.BlockSpec((tm, LANES), lambda i, j: (i, 0)))
        args.append(logits)
    return pl.pallas_call(
        functools.partial(_ffn_kernel, routed=routed),
        grid=(N_TOK // tm, n_chunk),
        in_specs=in_specs,
        out_specs=pl.BlockSpec((tm, D_MODEL), lambda i, j: (i, 0)),
        out_shape=jax.ShapeDtypeStruct((N_TOK, D_MODEL), F32),
        scratch_shapes=[pltpu.VMEM((tm, D_MODEL), F32)],
        compiler_params=_cparams(("arbitrary", "arbitrary")),
        name="ffn_routed" if routed else "ffn_dense",
    )(*args)


MOE_TILE = 256
MOE_SLOTS = TOP_K * N_TOK + N_EXP * MOE_TILE
MOE_TILES = MOE_SLOTS // MOE_TILE
ROUTE_I1, ROUTE_I2, ROUTE_W1, ROUTE_W2 = 0, 1, 2, 3


def _top2_route(logits):
    lane = _lane(logits.shape)
    neg = jnp.float32(-jnp.inf)
    lg = jnp.where(lane < N_EXP, logits, neg)
    t1 = jnp.max(lg, axis=-1, keepdims=True)
    i1 = jnp.min(jnp.where(lg == t1, lane, LANES), axis=-1, keepdims=True)
    lg2 = jnp.where(lane == i1, neg, lg)
    t2 = jnp.max(lg2, axis=-1, keepdims=True)
    i2 = jnp.min(jnp.where(lg2 == t2, lane, LANES), axis=-1, keepdims=True)
    e2 = jnp.exp(t2 - t1)
    w1 = 1.0 / (1.0 + e2)
    w2 = e2 / (1.0 + e2)
    rec = jnp.where(lane == ROUTE_I1, i1.astype(F32), 0.0) + jnp.where(lane == ROUTE_I2, i2.astype(F32), 0.0)
    return rec + jnp.where(lane == ROUTE_W1, w1, 0.0) + jnp.where(lane == ROUTE_W2, w2, 0.0)


def _routing_tables(route):
    n_assign = TOP_K * N_TOK
    expert = jnp.concatenate([route[:, ROUTE_I1], route[:, ROUTE_I2]]).astype(jnp.int32)
    index = jnp.arange(n_assign, dtype=jnp.int32)
    onehot = (expert[:, None] == jnp.arange(N_EXP, dtype=jnp.int32)[None, :]).astype(jnp.int32)
    running = jnp.cumsum(onehot, axis=0)
    counts = running[-1]
    rank = jnp.sum(running * onehot, axis=1) - 1
    padded = (counts + MOE_TILE - 1) // MOE_TILE * MOE_TILE
    ends = jnp.cumsum(padded)
    starts = ends - padded
    slot = jnp.sum(onehot * starts[None, :], axis=1) + rank
    sorted_token = (jnp.sort(expert * n_assign + index) % n_assign) % N_TOK
    tile_start = jnp.arange(MOE_TILES, dtype=jnp.int32) * MOE_TILE
    tile_expert = jnp.minimum(jnp.sum(tile_start[:, None] >= ends[None, :], axis=1), N_EXP - 1).astype(jnp.int32)
    first_sorted = (jnp.cumsum(counts) - counts)[tile_expert] + tile_start - starts[tile_expert]
    n_used = (ends[-1] // MOE_TILE).astype(jnp.int32).reshape(1)
    return sorted_token, slot, tile_expert, first_sorted.astype(jnp.int32), n_used


def _row_copy(src_hbm, row, dst, dst_row, sem):
    return pltpu.make_async_copy(src_hbm.at[pl.ds(row, 1)], dst.at[pl.ds(dst_row, 1)], sem)


def _experts_kernel(tok_ref, exp_ref, first_ref, used_ref, h_hbm, wg_ref, wu_ref, wd_ref, y_ref, x_buf, sems):
    t = pl.program_id(0)
    n_used = used_ref[0]

    def start_gather(tile, half):
        first = first_ref[tile]
        for r in range(MOE_TILE):
            tok = tok_ref[jnp.minimum(first + r, TOP_K * N_TOK - 1)]
            _row_copy(h_hbm, tok, x_buf.at[half], r, sems.at[half]).start(priority=r % 2)

    @pl.when(t == 0)
    def _():
        start_gather(0, 0)

    @pl.when(t + 1 < n_used)
    def _():
        start_gather(t + 1, (t + 1) % 2)

    @pl.when(t < n_used)
    def _():
        half = t % 2
        for r in range(MOE_TILE):
            _row_copy(h_hbm, 0, x_buf.at[half], r, sems.at[half]).wait()
        x = x_buf[half].astype(BF16)
        act = _silu(_dot(x, wg_ref[...])) * _dot(x, wu_ref[...])
        y_ref[...] = _dot(act.astype(BF16), wd_ref[...])

    @pl.when(t >= n_used)
    def _():
        y_ref[...] = jnp.zeros_like(y_ref)


def moe_experts(h, route, w_gu, w_d, l):
    lyr = l // 2
    sorted_token, slot, tile_expert, first_sorted, n_used = _routing_tables(route)
    gu_spec = lambda half: pl.BlockSpec((None, None, D_MODEL, FF_CHUNK),
                                        lambda t, tok, exp, first, used: (lyr, exp[t], 0, half))
    y = pl.pallas_call(
        _experts_kernel,
        grid_spec=pltpu.PrefetchScalarGridSpec(
            num_scalar_prefetch=4,
            grid=(MOE_TILES,),
            in_specs=[pl.BlockSpec(memory_space=pl.ANY),
                      gu_spec(0), gu_spec(1),
                      pl.BlockSpec((None, None, FF_CHUNK, D_MODEL),
                                   lambda t, tok, exp, first, used: (lyr, exp[t], 0, 0))],
            out_specs=pl.BlockSpec((MOE_TILE, D_MODEL), lambda t, tok, exp, first, used: (t, 0)),
            scratch_shapes=[pltpu.VMEM((2, MOE_TILE, D_MODEL), F32), pltpu.SemaphoreType.DMA((2,))]),
        out_shape=jax.ShapeDtypeStruct((MOE_SLOTS, D_MODEL), F32),
        compiler_params=_cparams(("arbitrary",)),
        name="moe_experts",
    )(sorted_token, tile_expert, first_sorted, n_used, h, w_gu, w_gu, w_d)
    return y, slot


def _moe_combine_kernel(slot_ref, y_hbm, x_ref, gf_ref, route_ref, o_ref, y_buf, sems):
    i = pl.program_id(0)
    tm = x_ref.shape[0]

    def start_gather(tile, half):
        for k in range(TOP_K):
            for r in range(tm):
                _row_copy(y_hbm, slot_ref[k * N_TOK + tile * tm + r], y_buf.at[half, k], r,
                          sems.at[half]).start(priority=r % 2)

    @pl.when(i == 0)
    def _():
        start_gather(0, 0)

    @pl.when(i + 1 < pl.num_programs(0))
    def _():
        start_gather(i + 1, (i + 1) % 2)

    half = i % 2
    for k in range(TOP_K):
        for r in range(tm):
            _row_copy(y_hbm, 0, y_buf.at[half, k], r, sems.at[half]).wait()
    route = route_ref[...]
    f = (route[:, ROUTE_W1:ROUTE_W1 + 1] * y_buf[half, 0] + route[:, ROUTE_W2:ROUTE_W2 + 1] * y_buf[half, 1])
    o_ref[...] = x_ref[...] + gf_ref[...] * f


def moe_combine(y, slot, x1, mod, route, l):
    tm = 256
    return pl.pallas_call(
        _moe_combine_kernel,
        grid_spec=pltpu.PrefetchScalarGridSpec(
            num_scalar_prefetch=1,
            grid=(N_TOK // tm,),
            in_specs=[pl.BlockSpec(memory_space=pl.ANY),
                      pl.BlockSpec((tm, D_MODEL), lambda i, s: (i, 0)),
                      pl.BlockSpec((None, None, 1, D_MODEL), lambda i, s: (l, _cond_row(i, tm), 0, 5)),
                      pl.BlockSpec((tm, LANES), lambda i, s: (i, 0))],
            out_specs=pl.BlockSpec((tm, D_MODEL), lambda i, s: (i, 0)),
            scratch_shapes=[pltpu.VMEM((2, TOP_K, tm, D_MODEL), F32), pltpu.SemaphoreType.DMA((2,))]),
        out_shape=jax.ShapeDtypeStruct((N_TOK, D_MODEL), F32),
        compiler_params=_cparams(("arbitrary",)),
        name="moe_combine",
    )(slot, y, x1, mod, route)


def _final_norm_kernel(x_ref, g_ref, o_ref):
    o_ref[...] = _rms(x_ref[...], g_ref[...])


def final_norm_call(x, g, tok0, n_tok):
    tm = 1024
    i0 = tok0 // tm
    return pl.pallas_call(
        _final_norm_kernel,
        grid=(n_tok // tm,),
        in_specs=[pl.BlockSpec((tm, D_MODEL), lambda i: (i0 + i, 0)),
                  pl.BlockSpec((1, D_MODEL), lambda i: (0, 0))],
        out_specs=pl.BlockSpec((tm, D_MODEL), lambda i: (i, 0)),
        out_shape=jax.ShapeDtypeStruct((n_tok, D_MODEL), F32),
        compiler_params=_cparams(("arbitrary",)),
        name="final_norm",
    )(x, g)


def _cache_store_kernel(*refs):
    n = len(refs) // 3
    for src, dst in zip(refs[:n], refs[2 * n:]):
        dst[...] = src[...].reshape(dst.shape)


def cache_store(sources, all_layers, l):
    n_sq = 2
    tm = n_sq * SEQ
    return pl.pallas_call(
        _cache_store_kernel,
        grid=(N_CTX // tm,),
        in_specs=([pl.BlockSpec((tm, s.shape[1]), lambda i: (i, 0)) for s in sources]
                  + [pl.BlockSpec(memory_space=pl.ANY)] * len(all_layers)),
        out_specs=[pl.BlockSpec((n_sq, None, SEQ, a.shape[-1]), lambda i: (i, l, 0, 0)) for a in all_layers],
        out_shape=[jax.ShapeDtypeStruct(a.shape, F32) for a in all_layers],
        input_output_aliases={len(sources) + k: k for k in range(len(all_layers))},
        compiler_params=_cparams(("arbitrary",)),
        name="cache_store",
    )(*sources, *all_layers)


def _gate_row(p):
    return jnp.zeros((1, LANES), F32).at[0, GATE_A:GATE_A + 2 * H_A].set(p.reshape(-1))


def _tile_lanes(g, width):
    return jnp.tile(g, width // g.shape[0])[None, :]


def kernel(x_prompt, x_sample, state_delta, cache_gqa_k, cache_gqa_v, cache_diff_k, cache_diff_v, c, c_ctx,
           ada_w, ada_b, norm_mix, norm_ffn, w_in, conv_w, dn_a_log, dn_dt_bias, dn_norm, gqa_q_norm,
           gqa_k_norm, diff_lambda, diff_norm, w_out, ffn_w_gu, ffn_w_down, moe_router, moe_w_gu, moe_w_down,
           final_norm):
    cond8 = jnp.concatenate([c_ctx[None, :], c, jnp.zeros((8 - N_COND, D_MODEL), F32)], axis=0)
    gate_lo = 4 * A_W
    gate_hi = gate_lo + 4 * H_A
    w_in_p = jnp.concatenate(
        [w_in[:, :, :gate_lo], w_in[:, :, gate_hi:], w_in[:, :, gate_lo:gate_hi],
         jnp.zeros((DEPTH, D_MODEL, LANES - 4 * H_A), F32)], axis=-1).astype(BF16)
    w_out_b = w_out.astype(BF16)
    norm_mix3 = norm_mix.reshape(DEPTH, 1, D_MODEL)
    norm_ffn3 = norm_ffn.reshape(DEPTH, 1, D_MODEL)
    ffn_gu_b, ffn_d_b = ffn_w_gu.astype(BF16), ffn_w_down.astype(BF16)
    moe_gu_b, moe_d_b = moe_w_gu.astype(BF16), moe_w_down.astype(BF16)
    router_p = jnp.pad(moe_router, ((0, 0), (0, 0), (0, LANES - N_EXP)))
    rope_b = _rope_tables(DEC_SEQ // GRID_W, HEAD_DIM)
    rope_c = _rope_tables(DEC_SEQ // GRID_W, DIFF_HD)
    s0_ctx = jnp.zeros((BATCH, 2, H_A, HEAD_DIM, HEAD_DIM), F32)
    cache_b = (cache_gqa_k.reshape(DEC_BATCH, DEPTH, PAST_LEN, KV_B * HEAD_DIM),
               cache_gqa_v.reshape(DEC_BATCH, DEPTH, PAST_LEN, KV_B * HEAD_DIM))
    cache_c = (cache_diff_k.reshape(DEC_BATCH, DEPTH, PAST_LEN, C_W),
               cache_diff_v.reshape(DEC_BATCH, DEPTH, PAST_LEN, C_W))

    mod = ada_modulation(cond8, ada_w, ada_b)[:, :N_COND].reshape(DEPTH, N_COND, 1, 6 * D_MODEL)

    x = jnp.concatenate([x_prompt.reshape(N_CTX, D_MODEL), x_sample.reshape(N_LAT, D_MODEL)], axis=0)
    all_states = jnp.zeros((BATCH, DEPTH, 2, H_A, HEAD_DIM, HEAD_DIM), F32)
    ctx_layers = tuple(jnp.zeros((BATCH, DEPTH, SEQ, w), F32)
                       for w in (KV_B * HEAD_DIM, KV_B * HEAD_DIM, C_W, C_W))
    ctx = dict(n_seq=BATCH, seq_len=SEQ, tok0=0)
    lat = dict(n_seq=DEC_BATCH, seq_len=DEC_SEQ, tok0=N_CTX)
    for l in range(DEPTH):
        qkv_a, z_a, q_b, k_b, v_b, q_c, k_c, v_c, gate = in_projection(x, mod, norm_mix3, w_in_p, l)

        dn_args = (qkv_a, z_a, gate, conv_w[l], _gate_row(dn_a_log[l]), _gate_row(dn_dt_bias[l]),
                   _tile_lanes(dn_norm[l], LANES))
        oa_ctx, all_states = delta_mixer(*dn_args, s0_ctx, n_sub=2, all_states=all_states, layer=l, **ctx)
        oa_lat, _ = delta_mixer(*dn_args, state_delta[:, l], n_sub=1, **lat)

        gq, gk = _tile_lanes(gqa_q_norm[l], LANES), _tile_lanes(gqa_k_norm[l], LANES)
        ob_ctx, kn_ctx = gqa_mixer(q_b, k_b, v_b, gq, gk, **ctx)
        (ob_lat,) = gqa_mixer(q_b, k_b, v_b, gq, gk, cache=cache_b, rope=rope_b, layer=l, **lat)

        lam_init = 0.8 - 0.6 * math.exp(-0.3 * l)
        gn = _tile_lanes(diff_norm[l], C_W)
        oc_ctx = diff_mixer(q_c, k_c, v_c, diff_lambda[l], gn, lam_init, **ctx)
        oc_lat = diff_mixer(q_c, k_c, v_c, diff_lambda[l], gn, lam_init, cache=cache_c, rope=rope_c, layer=l, **lat)

        mixes = ((oa_ctx, oa_lat), (ob_ctx, ob_lat), (oc_ctx, oc_lat))
        if l % 2 == 0:
            x1, h2 = out_projection(x, mixes, w_out_b, mod, norm_ffn3, l)
            x = ffn(h2, ffn_gu_b, ffn_d_b, x1, mod, l)
        else:
            x1, h2, route = out_projection(x, mixes, w_out_b, mod, norm_ffn3, l, router_p[l // 2])
            y_slots, slot = moe_experts(h2, route, moe_gu_b, moe_d_b, l)
            x = moe_combine(y_slots, slot, x1, mod, route, l)

        ctx_layers = cache_store((kn_ctx, v_b, k_c, v_c), ctx_layers, l)

    y_prompt = final_norm_call(x, final_norm[None, :], 0, N_CTX).reshape(BATCH, SEQ, D_MODEL)
    y_sample = final_norm_call(x, final_norm[None, :], N_CTX, N_LAT).reshape(DEC_BATCH, DEC_SEQ, D_MODEL)
    new_gk, new_gv, new_dk, new_dv = ctx_layers
    return (y_prompt, y_sample, all_states,
            new_gk.reshape(BATCH, DEPTH, SEQ, KV_B, HEAD_DIM), new_gv.reshape(BATCH, DEPTH, SEQ, KV_B, HEAD_DIM),
            new_dk.reshape(BATCH, DEPTH, SEQ, H_C, 2, DIFF_HD), new_dv.reshape(BATCH, DEPTH, SEQ, H_C, 2 * DIFF_HD))
```

```python
import functools
import math

import jax
import jax.numpy as jnp
from jax import lax
from jax.experimental import pallas as pl
from jax.experimental.pallas import tpu as pltpu

D_MODEL = 1024
BATCH = 16
SEQ = 256
DEPTH = 4
DEC_BATCH = 2
DEC_SEQ = 1024
PAST_LEN = 512
GRID_W = 64
HEAD_DIM = 64
H_A = 6
A_W = H_A * HEAD_DIM
H_B = 6
KV_B = 2
B_W = H_B * HEAD_DIM
H_C = 4
DIFF_HD = 32
C_W = H_C * 2 * DIFF_HD
CONV_W = 3
CHUNK = 64
ROPE_THETA = 10000.0
D_FF = 2816
N_EXP = 8
TOP_K = 2
D_FF_E = 1408
EPS = 1e-6

N_CTX = BATCH * SEQ
N_LAT = DEC_BATCH * DEC_SEQ
N_TOK = N_CTX + N_LAT
N_COND = 1 + DEC_BATCH
IN_PAD = 3072
FF_CHUNK = D_FF_E
LANES = 128
VMEM_LIMIT = 56 * 1024 * 1024

IN_SPLIT = (("qkv_a", 3 * A_W), ("z_a", A_W), ("q_b", B_W), ("k_b", KV_B * HEAD_DIM), ("v_b", KV_B * HEAD_DIM),
            ("q_c", C_W), ("k_c", C_W), ("v_c", C_W), ("gate", LANES))
GATE_BETA, GATE_A = 0, 2 * H_A

F32 = jnp.float32
BF16 = jnp.bfloat16
HI = lax.Precision.HIGHEST


def _cparams(sem):
    return pltpu.CompilerParams(dimension_semantics=sem, vmem_limit_bytes=VMEM_LIMIT)


def _dot(a, b):
    return jnp.dot(a, b, preferred_element_type=F32)


def _dot_hi(a, b):
    return jnp.dot(a, b, preferred_element_type=F32, precision=HI)


def _dot_split(a, b):
    a_hi, b_hi = a.astype(BF16), b.astype(BF16)
    a_lo = (a - a_hi.astype(F32)).astype(BF16)
    b_lo = (b - b_hi.astype(F32)).astype(BF16)
    return _dot(jnp.concatenate([a_hi, a_lo, a_hi], axis=1), jnp.concatenate([b_hi, b_hi, b_lo], axis=0))


def _dot_nt(a, b):
    return lax.dot_general(a, b, (((1,), (1,)), ((), ())), preferred_element_type=F32)


def _silu(x):
    return x * jax.nn.sigmoid(x)


def _softplus(x):
    return jnp.maximum(x, 0.0) + jnp.log1p(jnp.exp(-jnp.abs(x)))


def _rms(x, g):
    return x * lax.rsqrt(jnp.mean(x * x, axis=-1, keepdims=True) + EPS) * g


def _lane(shape):
    return lax.broadcasted_iota(jnp.int32, shape, len(shape) - 1)


def _group_mean_matrix(width, group):
    i = lax.broadcasted_iota(jnp.int32, (width, width), 0)
    j = lax.broadcasted_iota(jnp.int32, (width, width), 1)
    return jnp.where(i // group == j // group, 1.0 / group, 0.0).astype(BF16)


def _group_rms(x, gmat, g):
    sq = x * x
    hi = sq.astype(BF16)
    lo = (sq - hi.astype(F32)).astype(BF16)
    ms = _dot(jnp.concatenate([hi, lo], axis=1), jnp.concatenate([gmat, gmat], axis=0))
    return x * lax.rsqrt(ms + EPS) * g


def _cond_row(i, tm):
    return jnp.maximum((i * tm) // DEC_SEQ - (N_CTX // DEC_SEQ - 1), 0)


def _ada_kernel(c_ref, w_ref, b_ref, o_ref):
    o_ref[...] = _dot_hi(_silu(c_ref[...]), w_ref[...]) + b_ref[...]


def ada_modulation(cond8, ada_w, ada_b):
    tn = 1024
    return pl.pallas_call(
        _ada_kernel,
        grid=(DEPTH, 6 * D_MODEL // tn),
        in_specs=[pl.BlockSpec((8, D_MODEL), lambda l, j: (0, 0)),
                  pl.BlockSpec((None, D_MODEL, tn), lambda l, j: (l, 0, j)),
                  pl.BlockSpec((None, 1, tn), lambda l, j: (l, 0, j))],
        out_specs=pl.BlockSpec((None, 8, tn), lambda l, j: (l, 0, j)),
        out_shape=jax.ShapeDtypeStruct((DEPTH, 8, 6 * D_MODEL), F32),
        compiler_params=_cparams(("arbitrary", "arbitrary")),
        name="ada_modulation",
    )(cond8, ada_w, ada_b.reshape(DEPTH, 1, 6 * D_MODEL))


def _inproj_kernel(x_ref, g_ref, sh_ref, sc_ref, w_ref, *o_refs):
    h = _rms(x_ref[...], g_ref[...]) * (1.0 + sc_ref[...]) + sh_ref[...]
    acc = _dot(h.astype(BF16), w_ref[...])
    off = 0
    for o_ref, (_, width) in zip(o_refs, IN_SPLIT):
        o_ref[...] = acc[:, off:off + width]
        off += width


def in_projection(x, mod, norm_g, w_in_p, l):
    tm = 512
    return pl.pallas_call(
        _inproj_kernel,
        grid=(N_TOK // tm,),
        in_specs=[pl.BlockSpec((tm, D_MODEL), lambda i: (i, 0)),
                  pl.BlockSpec((None, 1, D_MODEL), lambda i: (l, 0, 0)),
                  pl.BlockSpec((None, None, 1, D_MODEL), lambda i: (l, _cond_row(i, tm), 0, 0)),
                  pl.BlockSpec((None, None, 1, D_MODEL), lambda i: (l, _cond_row(i, tm), 0, 1)),
                  pl.BlockSpec((None, D_MODEL, IN_PAD), lambda i: (l, 0, 0))],
        out_specs=[pl.BlockSpec((tm, w), lambda i: (i, 0)) for _, w in IN_SPLIT],
        out_shape=[jax.ShapeDtypeStruct((N_TOK, w), F32) for _, w in IN_SPLIT],
        compiler_params=_cparams(("arbitrary",)),
        name="in_projection",
    )(x, norm_g, mod, mod, w_in_p)


SOLVE_BLOCK = 16


def _solve_unit_triangular(lmat, rhs, lo, diag_blk, eye):
    def both(a):
        return jnp.concatenate([jnp.where(lo, a, 0.0), jnp.where(lo, 0.0, a)], axis=0)

    neg_d = jnp.where(diag_blk, -lmat, 0.0)
    pd = both(neg_d)
    po = both(-lmat - neg_d)
    t = eye + pd
    p = _dot(pd.astype(BF16), pd.astype(BF16))
    n_fac = int(math.log2(SOLVE_BLOCK))
    for j in range(1, n_fac):
        if j + 1 < n_fac:
            tp = _dot(p.astype(BF16), jnp.concatenate([t, p], axis=1).astype(BF16))
            t = t + tp[:, :LANES]
            p = tp[:, LANES:]
        else:
            t = t + _dot(p.astype(BF16), t.astype(BF16))
    cm = _dot(t.astype(BF16), jnp.concatenate([rhs, po], axis=1).astype(BF16))
    c, m = cm[:, :LANES], cm[:, LANES:].astype(BF16)
    x = c
    for _ in range(CHUNK // SOLVE_BLOCK - 1):
        x = c + _dot(m, x.astype(BF16))
    return x


def _delta_kernel(qkv_ref, z_ref, gate_ref, cw_ref, alog_ref, dtb_ref, ng_ref, s0_ref, o_ref, sfin_ref,
                  pad_s, kk_s, qq_s, vk_s, gc_s, beta_s, u_s, w_s, qd_s, a_s, kdt_s, egt_s, sa_s, o_s,
                  *, seq_len):
    n_chunk = seq_len // CHUNK
    n_pair = H_A // 2
    half = HEAD_DIM
    lo = _lane((CHUNK, LANES)) < half
    lo_row = _lane((1, LANES)) < half
    lo16 = _lane((CHUNK, LANES)).astype(F32).astype(BF16) < half
    row = lax.broadcasted_iota(jnp.int32, (CHUNK, LANES), 0)
    col = _lane((CHUNK, LANES)) % half
    ahead = jnp.where(lo, row - col, col - row)
    incl = ahead >= 0
    strict = ahead > 0
    diag_blk = row // SOLVE_BLOCK == col // SOLVE_BLOCK
    big_row = lax.broadcasted_iota(jnp.int32, (LANES, LANES), 0)
    anti = big_row // half + _lane((LANES, LANES)) // half == 1
    eye = (big_row == _lane((LANES, LANES))).astype(F32)

    rb = 128
    zero8 = jnp.zeros((8, 3 * A_W), F32)
    pad_s[0:8, :] = zero8
    pad_s[8 + seq_len:16 + seq_len, :] = zero8
    pad_s[8:8 + seq_len, :] = qkv_ref[...]
    lo_rb = _lane((rb, LANES)) < half
    for r0 in range(0, seq_len, rb):
        y = _silu(pad_s[7 + r0:7 + r0 + rb, :] * cw_ref[0:1, :] + pad_s[8 + r0:8 + r0 + rb, :] * cw_ref[1:2, :]
                  + pad_s[9 + r0:9 + r0 + rb, :] * cw_ref[2:3, :])

        def dup(part, p):
            c = y[:, part * A_W + p * LANES:part * A_W + (p + 1) * LANES]
            r = pltpu.roll(c, half, axis=1)
            return jnp.where(lo_rb, c, r), jnp.where(lo_rb, r, c)

        def l2n(x):
            return x * lax.rsqrt(0.5 * jnp.sum(x * x, axis=-1, keepdims=True) + EPS)

        for p in range(n_pair):
            qs, ks, vs = dup(0, p), dup(1, p), dup(2, p)
            for e in range(2):
                h = 2 * p + e
                kk = l2n(ks[e])
                kk_s[h, r0:r0 + rb, :] = kk
                qq_s[h, r0:r0 + rb, :] = l2n(qs[e]) * (HEAD_DIM ** -0.5)
                vk_s[h, r0:r0 + rb, :] = jnp.where(lo_rb, vs[e], kk)

    gt = gate_ref[...]
    beta_s[...] = jax.nn.sigmoid(gt)
    g = -jnp.exp(alog_ref[...]) * _softplus(gt + dtb_ref[...])
    g1 = g.astype(BF16)
    r1 = g - g1.astype(F32)
    g2 = r1.astype(BF16)
    g3 = (r1 - g2.astype(F32)).astype(BF16)
    g_split = jnp.concatenate([g1, g2, g3], axis=1)
    cb = 256
    ci = lax.broadcasted_iota(jnp.int32, (cb, cb), 0)
    cj = lax.broadcasted_iota(jnp.int32, (cb, cb), 1)
    same = ci // CHUNK == cj // CHUNK
    m_pre = (same & (ci >= cj)).astype(BF16)
    m_suf = (same & (ci <= cj)).astype(BF16)
    fwd_cols = (_lane((cb, LANES)) >= GATE_A) & (_lane((cb, LANES)) < GATE_A + H_A)
    for r0 in range(0, seq_len, cb):
        blk = g_split[r0:r0 + cb, :]
        pre, suf = _dot(m_pre, blk), _dot(m_suf, blk)
        pre = pre[:, :LANES] + pre[:, LANES:2 * LANES] + pre[:, 2 * LANES:]
        suf = suf[:, :LANES] + suf[:, LANES:2 * LANES] + suf[:, 2 * LANES:]
        gc_s[r0:r0 + cb, :] = jnp.where(fwd_cols, pre, suf)

    for h in range(H_A):
        zero = jnp.zeros((half, half), F32)
        sa_s[h] = jnp.concatenate([jnp.concatenate([zero, s0_ref[1, h]], axis=1),
                                   jnp.concatenate([s0_ref[0, h], zero], axis=1)], axis=0)

    def prep(c, carry):
        rows = pl.ds(pl.multiple_of(c * CHUNK, CHUNK), CHUNK)
        gcb = gc_s[rows, :]
        gct = gcb.T
        bb = beta_s[rows, :]
        for h in range(H_A):
            cf = gcb[:, GATE_A + h:GATE_A + h + 1]
            cbk = gcb[:, GATE_A + H_A + h:GATE_A + H_A + h + 1]
            gc_fb = jnp.where(lo, cf, cbk)
            gr_fb = jnp.concatenate([gct[GATE_A + h:GATE_A + h + 1, :],
                                     gct[GATE_A + H_A + h:GATE_A + H_A + h + 1, :]], axis=1)
            decay = jnp.where(incl, jnp.exp(jnp.where(incl, gc_fb - gr_fb, 0.0)), 0.0)
            bf = bb[:, GATE_BETA + h:GATE_BETA + h + 1]
            bbk = bb[:, GATE_BETA + H_A + h:GATE_BETA + H_A + h + 1]
            egf, egb = jnp.exp(cf), jnp.exp(cbk)
            tot_f = gcb[CHUNK - 1:CHUNK, GATE_A + h:GATE_A + h + 1]
            tot_b = gcb[0:1, GATE_A + H_A + h:GATE_A + H_A + h + 1]
            kk, qq, vk = kk_s[h, rows, :], qq_s[h, rows, :], vk_s[h, rows, :]
            k16 = kk[:, :half].astype(BF16)
            kq = jnp.concatenate([k16, qq[:, :half].astype(BF16)], axis=0)
            gram = _dot_nt(kq, jnp.concatenate([k16, k16], axis=0))
            lmat = jnp.where(strict, gram[:CHUNK] * jnp.where(lo, bf, bbk) * decay, 0.0)
            a_s[h, rows, :] = jnp.where(incl, gram[CHUNK:] * decay, 0.0).astype(BF16)
            x_f = vk * jnp.where(lo, bf, bf * egf)
            x_b = vk * jnp.where(lo, bbk, bbk * egb)
            x = _solve_unit_triangular(lmat, jnp.concatenate([x_f, x_b], axis=0), lo, diag_blk, eye)
            x_f, x_b = x[:CHUNK], pltpu.roll(x[CHUNK:], half, axis=1)
            u_s[h, rows, :] = jnp.where(lo, x_f, x_b)
            w_s[h, rows, :] = jnp.where(lo, x_b, x_f).astype(BF16)
            qd_s[h, rows, :] = (qq * jnp.where(lo, egb, egf)).astype(BF16)
            kd = kk * jnp.where(lo, jnp.exp(tot_b - cbk), jnp.exp(tot_f - cf))
            kdt_s[h, c] = kd.T.astype(BF16)
            egt_s[h, c] = jnp.exp(jnp.where(lo_row, tot_f, tot_b))
        return carry

    lax.fori_loop(0, n_chunk, prep, 0)

    def scan(s, carry):
        rf = pl.ds(pl.multiple_of(s * CHUNK, CHUNK), CHUNK)
        sb = n_chunk - 1 - s
        rbk = pl.ds(pl.multiple_of(sb * CHUNK, CHUNK), CHUNK)
        for h in range(H_A):
            u = jnp.where(lo, u_s[h, rf, :], u_s[h, rbk, :])
            w = jnp.where(lo16, w_s[h, rbk, :], w_s[h, rf, :])
            qd = jnp.where(lo16, qd_s[h, rbk, :], qd_s[h, rf, :])
            a = jnp.where(lo16, a_s[h, rf, :], a_s[h, rbk, :])
            kdt = jnp.concatenate([kdt_s[h, sb][:half], kdt_s[h, s][half:]], axis=0)
            egt = jnp.where(lo_row, egt_s[h, s], egt_s[h, sb])
            st = sa_s[h]
            wq = _dot(jnp.concatenate([w, qd], axis=0), st.astype(BF16))
            v = u - wq[:CHUNK]
            vbd = jnp.concatenate([jnp.where(lo, v, 0.0), jnp.where(lo, 0.0, v)], axis=0).astype(BF16)
            o = wq[CHUNK:] + _dot(a, vbd)
            sa_s[h] = st * egt + jnp.where(anti, _dot(kdt, v.astype(BF16)), 0.0)
            o_s[h, rf, 0:half] = o[:, :half]
            o_s[h, rbk, half:LANES] = o[:, half:]
        return carry

    lax.fori_loop(0, n_chunk, scan, 0)

    for r0 in range(0, seq_len, rb):
        for p in range(n_pair):
            nrm = []
            for e in range(2):
                ofb = o_s[2 * p + e, r0:r0 + rb, :]
                oo = ofb + pltpu.roll(ofb, half, axis=1)
                nrm.append(oo * lax.rsqrt(jnp.sum(oo * oo, axis=-1, keepdims=True) * (0.5 / HEAD_DIM) + EPS))
            o_ref[r0:r0 + rb, p * LANES:(p + 1) * LANES] = (
                jnp.where(lo_rb, nrm[0], nrm[1]) * ng_ref[...] * _silu(z_ref[r0:r0 + rb, p * LANES:(p + 1) * LANES]))

    for h in range(H_A):
        st = sa_s[h]
        sfin_ref[0, h] = st[half:, :half]
        sfin_ref[1, h] = st[:half, half:]


def delta_mixer(qkv, z, gate, cw, alog_row, dtb_row, ng2, s0, *, n_seq, seq_len, tok0):
    n_chunk = seq_len // CHUNK
    b0 = tok0 // seq_len
    tok = lambda w: pl.BlockSpec((seq_len, w), lambda b: (b0 + b, 0))
    full = lambda a: pl.BlockSpec(a.shape, lambda b: (0,) * a.ndim)
    state = pl.BlockSpec((None, 2, H_A, HEAD_DIM, HEAD_DIM), lambda b: (b, 0, 0, 0, 0))
    per_head = lambda dt: pltpu.VMEM((H_A, seq_len, LANES), dt)
    return pl.pallas_call(
        functools.partial(_delta_kernel, seq_len=seq_len),
        grid=(n_seq,),
        in_specs=[tok(3 * A_W), tok(A_W), tok(LANES), full(cw), full(alog_row), full(dtb_row), full(ng2), state],
        out_specs=[pl.BlockSpec((seq_len, A_W), lambda b: (b, 0)), state],
        out_shape=[jax.ShapeDtypeStruct((n_seq * seq_len, A_W), F32),
                   jax.ShapeDtypeStruct((n_seq, 2, H_A, HEAD_DIM, HEAD_DIM), F32)],
        scratch_shapes=[pltpu.VMEM((seq_len + 16, 3 * A_W), F32),
                        per_head(F32), per_head(F32), per_head(F32),
                        pltpu.VMEM((seq_len, LANES), F32),
                        pltpu.VMEM((seq_len, LANES), F32),
                        per_head(F32), per_head(BF16), per_head(BF16), per_head(BF16),
                        pltpu.VMEM((H_A, n_chunk, LANES, HEAD_DIM), BF16),
                        pltpu.VMEM((H_A, n_chunk, 1, LANES), F32),
                        pltpu.VMEM((H_A, LANES, LANES), F32),
                        per_head(F32)],
        compiler_params=_cparams(("arbitrary",)),
        name=f"delta_mixer_{seq_len}",
    )(qkv, z, gate, cw, alog_row, dtb_row, ng2, s0)


def _delta_kernel(*refs, seq_len, n_sub, aliased):
    qkv_ref, z_ref, gate_ref, cw_ref, alog_ref, dtb_ref, ng_ref, s0_ref = refs[:8]
    (o_ref, sfin_ref, pad_s, kk_s, qq_s, vk_s, gc_s, beta_s, u_s, w_s, qd_s, a_s, kdt_s, egt_s, sa_s, o_s,
     tp_s, t32_s, rp_s, c32_s, m_s, x_s, vbd_s, v_s, op_s) = refs[9 if aliased else 8:]
    n_chunk = seq_len // CHUNK
    n_pair = H_A // 2
    n_unit = n_sub * H_A
    half = HEAD_DIM
    lo = _lane((CHUNK, LANES)) < half
    lo_row = _lane((1, LANES)) < half
    lo16 = _lane((CHUNK, LANES)).astype(F32).astype(BF16) < half
    row = lax.broadcasted_iota(jnp.int32, (CHUNK, LANES), 0)
    col = _lane((CHUNK, LANES)) % half
    ahead = jnp.where(lo, row - col, col - row)
    incl = ahead >= 0
    strict = ahead > 0
    diag_blk = row // SOLVE_BLOCK == col // SOLVE_BLOCK
    big_row = lax.broadcasted_iota(jnp.int32, (LANES, LANES), 0)
    anti = big_row // half + _lane((LANES, LANES)) // half == 1
    eye = (big_row == _lane((LANES, LANES))).astype(F32)

    def both(a):
        return jnp.concatenate([jnp.where(lo, a, 0.0), jnp.where(lo, 0.0, a)], axis=0)

    rb = 128
    lo_rb = _lane((rb, LANES)) < half
    zero8 = jnp.zeros((8, 3 * A_W), F32)
    pad_s[0:8, :] = zero8
    pad_s[8 + seq_len:16 + seq_len, :] = zero8

    def l2n(x):
        return x * lax.rsqrt(0.5 * jnp.sum(x * x, axis=-1, keepdims=True) + EPS)

    for sq in range(n_sub):
        pad_s[8:8 + seq_len, :] = qkv_ref[sq * seq_len:(sq + 1) * seq_len, :]
        for r0 in range(0, seq_len, rb):
            y = _silu(pad_s[7 + r0:7 + r0 + rb, :] * cw_ref[0:1, :] + pad_s[8 + r0:8 + r0 + rb, :] * cw_ref[1:2, :]
                      + pad_s[9 + r0:9 + r0 + rb, :] * cw_ref[2:3, :])

            def dup(part, p):
                c = y[:, part * A_W + p * LANES:part * A_W + (p + 1) * LANES]
                r = pltpu.roll(c, half, axis=1)
                return jnp.where(lo_rb, c, r), jnp.where(lo_rb, r, c)

            for p in range(n_pair):
                qs, ks, vs = dup(0, p), dup(1, p), dup(2, p)
                for e in range(2):
                    u = sq * H_A + 2 * p + e
                    kk = l2n(ks[e])
                    kk_s[u, r0:r0 + rb, :] = kk
                    qq_s[u, r0:r0 + rb, :] = l2n(qs[e]) * (HEAD_DIM ** -0.5)
                    vk_s[u, r0:r0 + rb, :] = jnp.where(lo_rb, vs[e], kk)

    cb = 256
    ci = lax.broadcasted_iota(jnp.int32, (cb, cb), 0)
    cj = lax.broadcasted_iota(jnp.int32, (cb, cb), 1)
    same = ci // CHUNK == cj // CHUNK
    m_pre = jnp.where(same & (ci >= cj), 1.0, 0.0).astype(BF16)
    m_suf = jnp.where(same & (ci <= cj), 1.0, 0.0).astype(BF16)
    fwd_cols = (_lane((cb, LANES)) >= GATE_A) & (_lane((cb, LANES)) < GATE_A + H_A)
    for r0 in range(0, n_sub * seq_len, cb):
        gt = gate_ref[r0:r0 + cb, :]
        beta_s[r0:r0 + cb, :] = jax.nn.sigmoid(gt)
        g = -jnp.exp(alog_ref[...]) * _softplus(gt + dtb_ref[...])
        g1 = g.astype(BF16)
        r1 = g - g1.astype(F32)
        g2 = r1.astype(BF16)
        g3 = (r1 - g2.astype(F32)).astype(BF16)
        blk = jnp.concatenate([g1, g2, g3], axis=1)
        pre, suf = _dot(m_pre, blk), _dot(m_suf, blk)
        pre = pre[:, :LANES] + pre[:, LANES:2 * LANES] + pre[:, 2 * LANES:]
        suf = suf[:, :LANES] + suf[:, LANES:2 * LANES] + suf[:, 2 * LANES:]
        gc_s[r0:r0 + cb, :] = jnp.where(fwd_cols, pre, suf)

    zero = jnp.zeros((half, half), F32)
    for sq in range(n_sub):
        for h in range(H_A):
            sa_s[sq * H_A + h] = jnp.concatenate([jnp.concatenate([zero, s0_ref[sq, 1, h]], axis=1),
                                                  jnp.concatenate([s0_ref[sq, 0, h], zero], axis=1)], axis=0)

    n_fac = int(math.log2(SOLVE_BLOCK))

    def prep(c, carry):
        rows = pl.ds(pl.multiple_of(c * CHUNK, CHUNK), CHUNK)
        for sq in range(n_sub):
            grow = pl.ds(pl.multiple_of(sq * seq_len + c * CHUNK, CHUNK), CHUNK)
            gcb = gc_s[grow, :]
            gct = gcb.T
            bb = beta_s[grow, :]
            for h in range(H_A):
                u = sq * H_A + h
                full = (CHUNK, LANES)
                cff = jnp.broadcast_to(gcb[:, GATE_A + h:GATE_A + h + 1], full)
                cbb = jnp.broadcast_to(gcb[:, GATE_A + H_A + h:GATE_A + H_A + h + 1], full)
                bff = jnp.broadcast_to(bb[:, GATE_BETA + h:GATE_BETA + h + 1], full)
                bbb = jnp.broadcast_to(bb[:, GATE_BETA + H_A + h:GATE_BETA + H_A + h + 1], full)
                gr_fb = jnp.concatenate([gct[GATE_A + h:GATE_A + h + 1, :],
                                         gct[GATE_A + H_A + h:GATE_A + H_A + h + 1, :]], axis=1)
                decay = jnp.where(incl, jnp.exp(jnp.where(incl, jnp.where(lo, cff, cbb) - gr_fb, 0.0)), 0.0)
                egf, egb = jnp.exp(cff), jnp.exp(cbb)
                tot_f = gcb[CHUNK - 1:CHUNK, GATE_A + h:GATE_A + h + 1]
                tot_b = gcb[0:1, GATE_A + H_A + h:GATE_A + H_A + h + 1]
                kk, qq, vk = kk_s[u, rows, :], qq_s[u, rows, :], vk_s[u, rows, :]
                k16 = kk[:, :half].astype(BF16)
                kq = jnp.concatenate([k16, qq[:, :half].astype(BF16)], axis=0)
                gram = _dot_nt(kq, jnp.concatenate([k16, k16], axis=0))
                neg_l = jnp.where(strict, gram[:CHUNK] * jnp.where(lo, -bff, -bbb) * decay, 0.0)
                a_s[u, rows, :] = jnp.where(incl, gram[CHUNK:] * decay, 0.0).astype(BF16)
                neg_d = jnp.where(diag_blk, neg_l, 0.0)
                pd = both(neg_d)
                t32_s[u] = eye + pd
                tp_s[u, :, :LANES] = (eye + pd).astype(BF16)
                tp_s[u, :, LANES:] = pd.astype(BF16)
                x_f = vk * jnp.where(lo, bff, bff * egf)
                x_b = vk * jnp.where(lo, bbb, bbb * egb)
                rp_s[u, :, :LANES] = jnp.concatenate([x_f, x_b], axis=0).astype(BF16)
                rp_s[u, :, LANES:] = both(neg_l - neg_d).astype(BF16)
                qd_s[u, rows, :] = (qq * jnp.where(lo, egb, egf)).astype(BF16)
                kd = kk * jnp.exp(jnp.where(lo, tot_b - cbb, tot_f - cff))
                kdt_s[u, c] = kd.T.astype(BF16)
                egt_s[u, c] = jnp.exp(jnp.where(lo_row, tot_f, tot_b))
        for u in range(n_unit):
            tp_s[u, :, LANES:] = _dot(tp_s[u, :, LANES:], tp_s[u, :, LANES:]).astype(BF16)
        for j in range(1, n_fac):
            for u in range(n_unit):
                if j + 1 < n_fac:
                    tp = _dot(tp_s[u, :, LANES:], tp_s[u])
                    t = t32_s[u] + tp[:, :LANES]
                    t32_s[u] = t
                    tp_s[u, :, :LANES] = t.astype(BF16)
                    tp_s[u, :, LANES:] = tp[:, LANES:].astype(BF16)
                else:
                    t = t32_s[u] + _dot(tp_s[u, :, LANES:], tp_s[u, :, :LANES])
                    tp_s[u, :, :LANES] = t.astype(BF16)
        for u in range(n_unit):
            cm = _dot(tp_s[u, :, :LANES], rp_s[u])
            c32_s[u] = cm[:, :LANES]
            rp_s[u] = cm.astype(BF16)
        assert CHUNK // SOLVE_BLOCK == 4
        for u in range(n_unit):
            mm = _dot(rp_s[u, :, LANES:], rp_s[u])
            y1 = c32_s[u] + mm[:, :LANES]
            c32_s[u] = y1
            x_s[u] = y1.astype(BF16)
            m_s[u] = mm[:, LANES:].astype(BF16)
        for u in range(n_unit):
            x = c32_s[u] + _dot(m_s[u], x_s[u])
            x_f, x_b = x[:CHUNK], pltpu.roll(x[CHUNK:], half, axis=1)
            u_s[u, rows, :] = jnp.where(lo, x_f, x_b)
            w_s[u, rows, :] = jnp.where(lo, x_b, x_f).astype(BF16)
        return carry

    lax.fori_loop(0, n_chunk, prep, 0)

    def scan(s, carry):
        rf = pl.ds(pl.multiple_of(s * CHUNK, CHUNK), CHUNK)
        sb = n_chunk - 1 - s
        rbk = pl.ds(pl.multiple_of(sb * CHUNK, CHUNK), CHUNK)
        for u in range(n_unit):
            w = jnp.where(lo16, w_s[u, rbk, :], w_s[u, rf, :])
            qd = jnp.where(lo16, qd_s[u, rbk, :], qd_s[u, rf, :])
            wq = _dot(jnp.concatenate([w, qd], axis=0), sa_s[u].astype(BF16))
            v = jnp.where(lo, u_s[u, rf, :], u_s[u, rbk, :]) - wq[:CHUNK]
            v_s[u] = v.astype(BF16)
            vbd_s[u] = both(v).astype(BF16)
            op_s[u] = wq[CHUNK:]
        for u in range(n_unit):
            a = jnp.where(lo16, a_s[u, rf, :], a_s[u, rbk, :])
            o = op_s[u] + _dot(a, vbd_s[u])
            o_s[u, rf, 0:half] = o[:, :half]
            o_s[u, rbk, half:LANES] = o[:, half:]
            kdt = jnp.concatenate([kdt_s[u, sb][:half], kdt_s[u, s][half:]], axis=0)
            egt = jnp.where(lo_row, egt_s[u, s], egt_s[u, sb])
            sa_s[u] = sa_s[u] * egt + jnp.where(anti, _dot(kdt, v_s[u]), 0.0)
        return carry

    lax.fori_loop(0, n_chunk, scan, 0)

    for sq in range(n_sub):
        for r0 in range(0, seq_len, rb):
            tr = sq * seq_len + r0
            for p in range(n_pair):
                nrm = []
                for e in range(2):
                    ofb = o_s[sq * H_A + 2 * p + e, r0:r0 + rb, :]
                    oo = ofb + pltpu.roll(ofb, half, axis=1)
                    nrm.append(oo * lax.rsqrt(jnp.sum(oo * oo, axis=-1, keepdims=True) * (0.5 / HEAD_DIM) + EPS))
                o_ref[tr:tr + rb, p * LANES:(p + 1) * LANES] = (
                    jnp.where(lo_rb, nrm[0], nrm[1]) * ng_ref[...]
                    * _silu(z_ref[tr:tr + rb, p * LANES:(p + 1) * LANES]))
        for h in range(H_A):
            st = sa_s[sq * H_A + h]
            sfin_ref[sq, 0, h] = st[half:, :half]
            sfin_ref[sq, 1, h] = st[:half, half:]


def delta_mixer(qkv, z, gate, cw, alog_row, dtb_row, ng2, s0, *, n_seq, seq_len, tok0, n_sub,
                all_states=None, layer=0):
    n_chunk = seq_len // CHUNK
    n_unit = n_sub * H_A
    blk = n_sub * seq_len
    b0 = tok0 // blk
    tok = lambda w: pl.BlockSpec((blk, w), lambda b: (b0 + b, 0))
    full = lambda a: pl.BlockSpec(a.shape, lambda b: (0,) * a.ndim)
    state = pl.BlockSpec((n_sub, 2, H_A, HEAD_DIM, HEAD_DIM), lambda b: (b, 0, 0, 0, 0))
    per_unit = lambda dt: pltpu.VMEM((n_unit, seq_len, LANES), dt)
    stage = lambda rows, cols, dt: pltpu.VMEM((n_unit, rows, cols), dt)
    aliased = all_states is not None
    in_specs = [tok(3 * A_W), tok(A_W), tok(LANES), full(cw), full(alog_row), full(dtb_row), full(ng2), state]
    args = [qkv, z, gate, cw, alog_row, dtb_row, ng2, s0]
    if aliased:
        in_specs.append(pl.BlockSpec(memory_space=pl.ANY))
        args.append(all_states)
        state_out = pl.BlockSpec((n_sub, None, 2, H_A, HEAD_DIM, HEAD_DIM), lambda b: (b, layer, 0, 0, 0, 0))
        state_shape = jax.ShapeDtypeStruct(all_states.shape, F32)
    else:
        state_out = state
        state_shape = jax.ShapeDtypeStruct((n_seq, 2, H_A, HEAD_DIM, HEAD_DIM), F32)
    return pl.pallas_call(
        functools.partial(_delta_kernel, seq_len=seq_len, n_sub=n_sub, aliased=aliased),
        grid=(n_seq // n_sub,),
        in_specs=in_specs,
        out_specs=[pl.BlockSpec((blk, A_W), lambda b: (b, 0)), state_out],
        out_shape=[jax.ShapeDtypeStruct((n_seq * seq_len, A_W), F32), state_shape],
        input_output_aliases={len(args) - 1: 1} if aliased else {},
        scratch_shapes=[pltpu.VMEM((seq_len + 16, 3 * A_W), F32),
                        per_unit(F32), per_unit(F32), per_unit(F32),
                        pltpu.VMEM((blk, LANES), F32),
                        pltpu.VMEM((blk, LANES), F32),
                        per_unit(F32), per_unit(BF16), per_unit(BF16), per_unit(BF16),
                        pltpu.VMEM((n_unit, n_chunk, LANES, HEAD_DIM), BF16),
                        pltpu.VMEM((n_unit, n_chunk, 1, LANES), F32),
                        stage(LANES, LANES, F32),
                        per_unit(F32),
                        stage(LANES, 2 * LANES, BF16),
                        stage(LANES, LANES, F32),
                        stage(LANES, 2 * LANES, BF16),
                        stage(LANES, LANES, F32),
                        stage(LANES, LANES, BF16),
                        stage(LANES, LANES, BF16),
                        stage(LANES, LANES, BF16),
                        stage(CHUNK, LANES, BF16),
                        stage(CHUNK, LANES, F32)],
        compiler_params=_cparams(("arbitrary",)),
        name=f"delta_mixer_{seq_len}",
    )(*args)


def _rope(x, cos, sin_signed, quarter):
    width = x.shape[-1]
    swapped = jnp.where((_lane(x.shape) % (2 * quarter)) < quarter,
                        pltpu.roll(x, width - quarter, axis=1), pltpu.roll(x, quarter, axis=1))
    return x * cos + swapped * sin_signed


def _rope_tables(rows, dim):
    nf = dim // 4
    inv = ROPE_THETA ** (-jnp.arange(nf, dtype=F32) / nf)
    r = jnp.repeat(jnp.arange(rows, dtype=F32), GRID_W)
    c = jnp.tile(jnp.arange(GRID_W, dtype=F32), rows)
    ar, ac = r[:, None] * inv, c[:, None] * inv
    cos = jnp.concatenate([jnp.cos(ar), jnp.cos(ar), jnp.cos(ac), jnp.cos(ac)], axis=1)
    sin = jnp.concatenate([-jnp.sin(ar), jnp.sin(ar), -jnp.sin(ac), jnp.sin(ac)], axis=1)
    return jnp.tile(cos, (1, LANES // dim)), jnp.tile(sin, (1, LANES // dim))


def _gqa_kernel(*refs, cached):
    if cached:
        (q_ref, k_ref, v_ref, gq_ref, gk_ref, ck_ref, cv_ref, cosq_ref, sinq_ref, cosk_ref, sink_ref,
         o_ref, k_s, v_s) = refs
    else:
        q_ref, k_ref, v_ref, gq_ref, gk_ref, o_ref, kn_ref, k_s, v_s = refs
    gmat = _group_mean_matrix(LANES, HEAD_DIM)
    quarter = HEAD_DIM // 4

    @pl.when(pl.program_id(1) == 0)
    def _():
        k = _group_rms(k_ref[...], gmat, gk_ref[...])
        v = v_ref[...]
        if cached:
            k = _rope(k, cosk_ref[...], sink_ref[...], quarter)
            k = jnp.concatenate([ck_ref[...], k], axis=0)
            v = jnp.concatenate([cv_ref[...], v], axis=0)
        else:
            kn_ref[...] = k
        k_s[...] = k.astype(BF16)
        v_s[...] = v.astype(BF16)

    k, v = k_s[...], v_s[...]
    tq = q_ref.shape[0]
    lo = _lane((tq, LANES)) < HEAD_DIM
    group = H_B // KV_B
    q_scale = (HEAD_DIM ** -0.5) * math.log2(math.e)
    qs = []
    for j in range(H_B // 2):
        q = _group_rms(q_ref[:, j * LANES:(j + 1) * LANES], gmat, gq_ref[...])
        if cached:
            q = _rope(q, cosq_ref[...], sinq_ref[...], quarter)
        q = q * q_scale
        qs.append((q, pltpu.roll(q, HEAD_DIM, axis=1)))
    outs = [None] * H_B
    for kv in range(KV_B):
        keep = lo if kv == 0 else ~lo
        heads = range(kv * group, (kv + 1) * group)
        q3 = jnp.concatenate([jnp.where(keep, qs[h // 2][0 if h % 2 == kv else 1], 0.0).astype(BF16)
                              for h in heads], axis=0)
        s = _dot_nt(q3, k)
        p = jnp.exp2(s - jnp.max(s, axis=-1, keepdims=True))
        r = _dot(p.astype(BF16), v) / jnp.sum(p, axis=-1, keepdims=True)
        for i, h in enumerate(heads):
            rh = r[i * tq:(i + 1) * tq]
            outs[h] = rh if h % 2 == kv else pltpu.roll(rh, HEAD_DIM, axis=1)
    for j in range(H_B // 2):
        o_ref[:, j * LANES:(j + 1) * LANES] = jnp.where(lo, outs[2 * j], outs[2 * j + 1])


def gqa_mixer(q, k, v, gq, gk, *, n_seq, seq_len, tok0, cache=None, rope=None, layer=0):
    tq = 256
    nq = seq_len // tq
    b0 = tok0 // seq_len
    q0 = tok0 // tq
    cached = cache is not None
    kvw = KV_B * HEAD_DIM
    in_specs = [pl.BlockSpec((tq, B_W), lambda b, i: (q0 + b * nq + i, 0)),
                pl.BlockSpec((seq_len, kvw), lambda b, i: (b0 + b, 0)),
                pl.BlockSpec((seq_len, kvw), lambda b, i: (b0 + b, 0)),
                pl.BlockSpec((1, LANES), lambda b, i: (0, 0)),
                pl.BlockSpec((1, LANES), lambda b, i: (0, 0))]
    args = [q, k, v, gq, gk]
    out_specs = [pl.BlockSpec((tq, B_W), lambda b, i: (b * nq + i, 0))]
    out_shape = [jax.ShapeDtypeStruct((n_seq * seq_len, B_W), F32)]
    if cached:
        cos, sin = rope
        cache_spec = pl.BlockSpec((None, None, PAST_LEN, kvw), lambda b, i: (b, layer, 0, 0))
        in_specs += [cache_spec, cache_spec,
                     pl.BlockSpec((tq, LANES), lambda b, i: (i, 0)),
                     pl.BlockSpec((tq, LANES), lambda b, i: (i, 0)),
                     pl.BlockSpec((seq_len, LANES), lambda b, i: (0, 0)),
                     pl.BlockSpec((seq_len, LANES), lambda b, i: (0, 0))]
        args += [cache[0], cache[1], cos, sin, cos, sin]
    else:
        out_specs.append(pl.BlockSpec((seq_len, kvw), lambda b, i: (b, 0)))
        out_shape.append(jax.ShapeDtypeStruct((n_seq * seq_len, kvw), F32))
    n_keys = seq_len + (PAST_LEN if cached else 0)
    return pl.pallas_call(
        functools.partial(_gqa_kernel, cached=cached),
        grid=(n_seq, nq),
        in_specs=in_specs, out_specs=out_specs, out_shape=out_shape,
        scratch_shapes=[pltpu.VMEM((n_keys, kvw), BF16), pltpu.VMEM((n_keys, kvw), BF16)],
        compiler_params=_cparams(("arbitrary", "arbitrary")),
        name=f"gqa_mixer_{seq_len}",
    )(*args)


def _diff_kernel(*refs, cached, lam_init):
    if cached:
        (q_ref, k_ref, v_ref, lam_ref, gn_ref, ck_ref, cv_ref, cosq_ref, sinq_ref, cosk_ref, sink_ref,
         o_ref) = refs
    else:
        q_ref, k_ref, v_ref, lam_ref, gn_ref, o_ref = refs
    quarter = DIFF_HD // 4
    lv = lam_ref[...]
    lam = (jnp.exp(jnp.sum(lv[0:1] * lv[1:2], axis=-1, keepdims=True))
           - jnp.exp(jnp.sum(lv[2:3] * lv[3:4], axis=-1, keepdims=True)) + lam_init)
    q, k, v = q_ref[...], k_ref[...], v_ref[...]
    if cached:
        cq = jnp.concatenate([cosq_ref[...]] * 2, axis=1)
        sq = jnp.concatenate([sinq_ref[...]] * 2, axis=1)
        ck = jnp.concatenate([cosk_ref[...]] * 2, axis=1)
        sk = jnp.concatenate([sink_ref[...]] * 2, axis=1)
        q = _rope(q, cq, sq, quarter)
        k = _rope(k, ck, sk, quarter)
        k = jnp.concatenate([ck_ref[...], k], axis=0)
        v = jnp.concatenate([cv_ref[...], v], axis=0)
    k = k.astype(BF16)
    v = v.astype(BF16)
    q = q * ((DIFF_HD ** -0.5) * math.log2(math.e))
    tq = q.shape[0]
    lane = _lane(q.shape)
    o = jnp.zeros(q.shape, F32)
    for h in range(H_C):
        ps, cs = [], []
        for m in range(2):
            qm = jnp.where(lane // DIFF_HD == 2 * h + m, q, 0.0).astype(BF16)
            s = _dot_nt(qm, k)
            p = jnp.exp2(s - jnp.max(s, axis=-1, keepdims=True))
            ps.append(p)
            cs.append((1.0 if m == 0 else lam) / jnp.sum(p, axis=-1, keepdims=True))
        a = ps[0] * cs[0] - ps[1] * cs[1]
        o = jnp.where(lane // (2 * DIFF_HD) == h, _dot(a.astype(BF16), v), o)
    o_ref[...] = _group_rms(o, _group_mean_matrix(C_W, 2 * DIFF_HD), gn_ref[...]) * (1.0 - lam_init)


def diff_mixer(q, k, v, lam_p, gn, lam_init, *, n_seq, seq_len, tok0, cache=None, rope=None, layer=0):
    tq = 256
    nq = seq_len // tq
    b0 = tok0 // seq_len
    q0 = tok0 // tq
    cached = cache is not None
    in_specs = [pl.BlockSpec((tq, C_W), lambda b, i: (q0 + b * nq + i, 0)),
                pl.BlockSpec((seq_len, C_W), lambda b, i: (b0 + b, 0)),
                pl.BlockSpec((seq_len, C_W), lambda b, i: (b0 + b, 0)),
                pl.BlockSpec((4, DIFF_HD), lambda b, i: (0, 0)),
                pl.BlockSpec((1, C_W), lambda b, i: (0, 0))]
    args = [q, k, v, lam_p, gn]
    if cached:
        cos, sin = rope
        cache_spec = pl.BlockSpec((None, None, PAST_LEN, C_W), lambda b, i: (b, layer, 0, 0))
        in_specs += [cache_spec, cache_spec,
                     pl.BlockSpec((tq, LANES), lambda b, i: (i, 0)),
                     pl.BlockSpec((tq, LANES), lambda b, i: (i, 0)),
                     pl.BlockSpec((seq_len, LANES), lambda b, i: (0, 0)),
                     pl.BlockSpec((seq_len, LANES), lambda b, i: (0, 0))]
        args += [cache[0], cache[1], cos, sin, cos, sin]
    return pl.pallas_call(
        functools.partial(_diff_kernel, cached=cached, lam_init=lam_init),
        grid=(n_seq, nq),
        in_specs=in_specs,
        out_specs=pl.BlockSpec((tq, C_W), lambda b, i: (b * nq + i, 0)),
        out_shape=jax.ShapeDtypeStruct((n_seq * seq_len, C_W), F32),
        compiler_params=_cparams(("arbitrary", "arbitrary")),
        name=f"diff_mixer_{seq_len}",
    )(*args)


def _outproj_kernel(*refs, routed, n_mix):
    x_ref = refs[0]
    mix_refs = refs[1:1 + 2 * n_mix]
    rest = refs[1 + 2 * n_mix:]
    if routed:
        w_ref, gm_ref, g_ref, sh_ref, sc_ref, r_ref, x1_ref, h_ref, lg_ref = rest
    else:
        w_ref, gm_ref, g_ref, sh_ref, sc_ref, x1_ref, h_ref = rest
    is_ctx = pl.program_id(0) < N_CTX // x_ref.shape[0]
    mixed = jnp.concatenate(
        [jnp.where(is_ctx, mix_refs[2 * m][...], mix_refs[2 * m + 1][...]) for m in range(n_mix)], axis=1)
    x1 = x_ref[...] + gm_ref[...] * _dot(mixed.astype(BF16), w_ref[...])
    x1_ref[...] = x1
    h = _rms(x1, g_ref[...]) * (1.0 + sc_ref[...]) + sh_ref[...]
    h_ref[...] = h.astype(h_ref.dtype)
    if routed:
        lg_ref[...] = _top2_route(_dot_split(h, r_ref[...]))


def out_projection(x, mixes, w_out_b, mod, norm_g, l, router_p=None):
    tm = 512
    n_ctx_tiles = N_CTX // tm
    n_lat_tiles = N_LAT // tm
    routed = router_p is not None
    mod_spec = lambda k: pl.BlockSpec((None, None, 1, D_MODEL), lambda i: (l, _cond_row(i, tm), 0, k))
    in_specs = [pl.BlockSpec((tm, D_MODEL), lambda i: (i, 0))]
    args = [x]
    for ctx_arr, lat_arr in mixes:
        w = ctx_arr.shape[1]
        in_specs.append(pl.BlockSpec((tm, w), lambda i: (jnp.minimum(i, n_ctx_tiles - 1), 0)))
        in_specs.append(pl.BlockSpec((tm, w), lambda i: (jnp.clip(i - n_ctx_tiles, 0, n_lat_tiles - 1), 0)))
        args += [ctx_arr, lat_arr]
    in_specs += [pl.BlockSpec((None, D_MODEL, D_MODEL), lambda i: (l, 0, 0)),
                 mod_spec(2),
                 pl.BlockSpec((None, 1, D_MODEL), lambda i: (l, 0, 0)),
                 mod_spec(3), mod_spec(4)]
    args += [w_out_b, mod, norm_g, mod, mod]
    out_specs = [pl.BlockSpec((tm, D_MODEL), lambda i: (i, 0)),
                 pl.BlockSpec((tm, D_MODEL), lambda i: (i, 0))]
    out_shape = [jax.ShapeDtypeStruct((N_TOK, D_MODEL), F32),
                 jax.ShapeDtypeStruct((N_TOK, D_MODEL), F32 if routed else BF16)]
    if routed:
        in_specs.append(pl.BlockSpec((D_MODEL, LANES), lambda i: (0, 0)))
        args.append(router_p)
        out_specs.append(pl.BlockSpec((tm, LANES), lambda i: (i, 0)))
        out_shape.append(jax.ShapeDtypeStruct((N_TOK, LANES), F32))
    return pl.pallas_call(
        functools.partial(_outproj_kernel, routed=routed, n_mix=len(mixes)),
        grid=(N_TOK // tm,),
        in_specs=in_specs, out_specs=out_specs, out_shape=out_shape,
        compiler_params=_cparams(("arbitrary",)),
        name="out_projection",
    )(*args)


def _top2_gate(logits):
    lane = _lane(logits.shape)
    neg = jnp.float32(-jnp.inf)
    lg = jnp.where(lane < N_EXP, logits, neg)
    t1 = jnp.max(lg, axis=-1, keepdims=True)
    i1 = jnp.min(jnp.where(lg == t1, lane, LANES), axis=-1, keepdims=True)
    lg2 = jnp.where(lane == i1, neg, lg)
    t2 = jnp.max(lg2, axis=-1, keepdims=True)
    i2 = jnp.min(jnp.where(lg2 == t2, lane, LANES), axis=-1, keepdims=True)
    e2 = jnp.exp(t2 - t1)
    w1 = 1.0 / (1.0 + e2)
    w2 = e2 / (1.0 + e2)
    return jnp.where(lane == i1, w1, 0.0) + jnp.where(lane == i2, w2, 0.0)


def _ffn_kernel(*refs, routed):
    if routed:
        h_ref, wg_ref, wu_ref, wd_ref, x_ref, gf_ref, lg_ref, o_ref, acc_ref = refs
    else:
        h_ref, wg_ref, wu_ref, wd_ref, x_ref, gf_ref, o_ref, acc_ref = refs
    j = pl.program_id(1)

    @pl.when(j == 0)
    def _():
        acc_ref[...] = jnp.zeros_like(acc_ref)

    h = h_ref[...]
    act = _silu(_dot(h, wg_ref[...])) * _dot(h, wu_ref[...])
    if routed:
        gate = _top2_gate(lg_ref[...])
        act = act * jnp.sum(jnp.where(_lane(gate.shape) == j, gate, 0.0), axis=-1, keepdims=True)
    acc_ref[...] += _dot(act.astype(BF16), wd_ref[...])

    @pl.when(j == pl.num_programs(1) - 1)
    def _():
        o_ref[...] = x_ref[...] + gf_ref[...] * acc_ref[...]


def ffn(h, w_gu, w_d, x1, mod, l, logits=None):
    tm = 512
    lyr = l // 2
    routed = logits is not None
    if routed:
        n_chunk = N_EXP
        gu_spec = lambda half: pl.BlockSpec((None, None, D_MODEL, FF_CHUNK), lambda i, j: (lyr, j, 0, half))
        d_spec = pl.BlockSpec((None, None, FF_CHUNK, D_MODEL), lambda i, j: (lyr, j, 0, 0))
    else:
        n_chunk = D_FF // FF_CHUNK
        gu_spec = lambda half: pl.BlockSpec((None, D_MODEL, FF_CHUNK), lambda i, j: (lyr, 0, half * n_chunk + j))
        d_spec = pl.BlockSpec((None, FF_CHUNK, D_MODEL), lambda i, j: (lyr, j, 0))
    in_specs = [pl.BlockSpec((tm, D_MODEL), lambda i, j: (i, 0)),
                gu_spec(0), gu_spec(1), d_spec,
                pl.BlockSpec((tm, D_MODEL), lambda i, j: (i, 0)),
                pl.BlockSpec((None, None, 1, D_MODEL), lambda i, j: (l, _cond_row(i, tm), 0, 5))]
    args = [h, w_gu, w_gu, w_d, x1, mod]
    if routed:
        in_specs.append(pl.BlockSpec((tm, LANES), lambda i, j: (i, 0)))
        args.append(logits)
    return pl.pallas_call(
        functools.partial(_ffn_kernel, routed=routed),
        grid=(N_TOK // tm, n_chunk),
        in_specs=in_specs,
        out_specs=pl.BlockSpec((tm, D_MODEL), lambda i, j: (i, 0)),
        out_shape=jax.ShapeDtypeStruct((N_TOK, D_MODEL), F32),
        scratch_shapes=[pltpu.VMEM((tm, D_MODEL), F32)],
        compiler_params=_cparams(("arbitrary", "arbitrary")),
        name="ffn_routed" if routed else "ffn_dense",
    )(*args)


MOE_TILE = 256
MOE_SLOTS = TOP_K * N_TOK + N_EXP * MOE_TILE
MOE_TILES = MOE_SLOTS // MOE_TILE
ROUTE_I1, ROUTE_I2, ROUTE_W1, ROUTE_W2 = 0, 1, 2, 3


def _top2_route(logits):
    lane = _lane(logits.shape)
    neg = jnp.float32(-jnp.inf)
    lg = jnp.where(lane < N_EXP, logits, neg)
    t1 = jnp.max(lg, axis=-1, keepdims=True)
    i1 = jnp.min(jnp.where(lg == t1, lane, LANES), axis=-1, keepdims=True)
    lg2 = jnp.where(lane == i1, neg, lg)
    t2 = jnp.max(lg2, axis=-1, keepdims=True)
    i2 = jnp.min(jnp.where(lg2 == t2, lane, LANES), axis=-1, keepdims=True)
    e2 = jnp.exp(t2 - t1)
    w1 = 1.0 / (1.0 + e2)
    w2 = e2 / (1.0 + e2)
    rec = jnp.where(lane == ROUTE_I1, i1.astype(F32), 0.0) + jnp.where(lane == ROUTE_I2, i2.astype(F32), 0.0)
    return rec + jnp.where(lane == ROUTE_W1, w1, 0.0) + jnp.where(lane == ROUTE_W2, w2, 0.0)


def _routing_tables(route):
    n_assign = TOP_K * N_TOK
    expert = jnp.concatenate([route[:, ROUTE_I1], route[:, ROUTE_I2]]).astype(jnp.int32)
    index = jnp.arange(n_assign, dtype=jnp.int32)
    onehot = (expert[:, None] == jnp.arange(N_EXP, dtype=jnp.int32)[None, :]).astype(jnp.int32)
    running = jnp.cumsum(onehot, axis=0)
    counts = running[-1]
    rank = jnp.sum(running * onehot, axis=1) - 1
    padded = (counts + MOE_TILE - 1) // MOE_TILE * MOE_TILE
    ends = jnp.cumsum(padded)
    starts = ends - padded
    slot = jnp.sum(onehot * starts[None, :], axis=1) + rank
    sorted_token = (jnp.sort(expert * n_assign + index) % n_assign) % N_TOK
    tile_start = jnp.arange(MOE_TILES, dtype=jnp.int32) * MOE_TILE
    tile_expert = jnp.minimum(jnp.sum(tile_start[:, None] >= ends[None, :], axis=1), N_EXP - 1).astype(jnp.int32)
    first_sorted = (jnp.cumsum(counts) - counts)[tile_expert] + tile_start - starts[tile_expert]
    n_used = (ends[-1] // MOE_TILE).astype(jnp.int32).reshape(1)
    return sorted_token, slot, tile_expert, first_sorted.astype(jnp.int32), n_used


def _row_copy(src_hbm, row, dst, dst_row, sem):
    return pltpu.make_async_copy(src_hbm.at[pl.ds(row, 1)], dst.at[pl.ds(dst_row, 1)], sem)


def _experts_kernel(tok_ref, exp_ref, first_ref, used_ref, h_hbm, wg_ref, wu_ref, wd_ref, y_ref, x_buf, sems,
                    wg_s, wu_s, wd_s):
    t = pl.program_id(0)
    n_used = used_ref[0]

    def start_gather(tile, half):
        first = first_ref[tile]
        for r in range(MOE_TILE):
            tok = tok_ref[jnp.minimum(first + r, TOP_K * N_TOK - 1)]
            _row_copy(h_hbm, tok, x_buf.at[half], r, sems.at[half]).start(priority=r % 2)

    @pl.when(t == 0)
    def _():
        start_gather(0, 0)

    @pl.when(t + 1 < n_used)
    def _():
        start_gather(t + 1, (t + 1) % 2)

    @pl.when(t < n_used)
    def _():
        half = t % 2
        for r in range(MOE_TILE):
            _row_copy(h_hbm, 0, x_buf.at[half], r, sems.at[half]).wait()
        @pl.when((t == 0) | (exp_ref[t] != exp_ref[jnp.maximum(t - 1, 0)]))
        def _():
            wg_s[...] = wg_ref[...].astype(BF16)
            wu_s[...] = wu_ref[...].astype(BF16)
            wd_s[...] = wd_ref[...].astype(BF16)

        x = x_buf[half].astype(BF16)
        act = _silu(_dot(x, wg_s[...])) * _dot(x, wu_s[...])
        y_ref[...] = _dot(act.astype(BF16), wd_s[...])

    @pl.when(t >= n_used)
    def _():
        y_ref[...] = jnp.zeros_like(y_ref)


def moe_experts(h, route, w_gu, w_d, l):
    lyr = l // 2
    sorted_token, slot, tile_expert, first_sorted, n_used = _routing_tables(route)
    gu_spec = lambda half: pl.BlockSpec((None, None, D_MODEL, FF_CHUNK),
                                        lambda t, tok, exp, first, used: (lyr, exp[t], 0, half))
    y = pl.pallas_call(
        _experts_kernel,
        grid_spec=pltpu.PrefetchScalarGridSpec(
            num_scalar_prefetch=4,
            grid=(MOE_TILES,),
            in_specs=[pl.BlockSpec(memory_space=pl.ANY),
                      gu_spec(0), gu_spec(1),
                      pl.BlockSpec((None, None, FF_CHUNK, D_MODEL),
                                   lambda t, tok, exp, first, used: (lyr, exp[t], 0, 0))],
            out_specs=pl.BlockSpec((MOE_TILE, D_MODEL), lambda t, tok, exp, first, used: (t, 0)),
            scratch_shapes=[pltpu.VMEM((2, MOE_TILE, D_MODEL), F32), pltpu.SemaphoreType.DMA((2,)),
                            pltpu.VMEM((D_MODEL, FF_CHUNK), BF16), pltpu.VMEM((D_MODEL, FF_CHUNK), BF16),
                            pltpu.VMEM((FF_CHUNK, D_MODEL), BF16)]),
        out_shape=jax.ShapeDtypeStruct((MOE_SLOTS, D_MODEL), F32),
        compiler_params=_cparams(("arbitrary",)),
        name="moe_experts",
    )(sorted_token, tile_expert, first_sorted, n_used, h, w_gu, w_gu, w_d)
    return y, slot


def _moe_combine_kernel(slot_ref, y_hbm, x_ref, gf_ref, route_ref, o_ref, y_buf, sems):
    i = pl.program_id(0)
    tm = x_ref.shape[0]

    def start_gather(tile, half):
        for k in range(TOP_K):
            for r in range(tm):
                _row_copy(y_hbm, slot_ref[k * N_TOK + tile * tm + r], y_buf.at[half, k], r,
                          sems.at[half]).start(priority=r % 2)

    @pl.when(i == 0)
    def _():
        start_gather(0, 0)

    @pl.when(i + 1 < pl.num_programs(0))
    def _():
        start_gather(i + 1, (i + 1) % 2)

    half = i % 2
    for k in range(TOP_K):
        for r in range(tm):
            _row_copy(y_hbm, 0, y_buf.at[half, k], r, sems.at[half]).wait()
    route = route_ref[...]
    f = (route[:, ROUTE_W1:ROUTE_W1 + 1] * y_buf[half, 0] + route[:, ROUTE_W2:ROUTE_W2 + 1] * y_buf[half, 1])
    o_ref[...] = x_ref[...] + gf_ref[...] * f


def moe_combine(y, slot, x1, mod, route, l):
    tm = 256
    return pl.pallas_call(
        _moe_combine_kernel,
        grid_spec=pltpu.PrefetchScalarGridSpec(
            num_scalar_prefetch=1,
            grid=(N_TOK // tm,),
            in_specs=[pl.BlockSpec(memory_space=pl.ANY),
                      pl.BlockSpec((tm, D_MODEL), lambda i, s: (i, 0)),
                      pl.BlockSpec((None, None, 1, D_MODEL), lambda i, s: (l, _cond_row(i, tm), 0, 5)),
                      pl.BlockSpec((tm, LANES), lambda i, s: (i, 0))],
            out_specs=pl.BlockSpec((tm, D_MODEL), lambda i, s: (i, 0)),
            scratch_shapes=[pltpu.VMEM((2, TOP_K, tm, D_MODEL), F32), pltpu.SemaphoreType.DMA((2,))]),
        out_shape=jax.ShapeDtypeStruct((N_TOK, D_MODEL), F32),
        compiler_params=_cparams(("arbitrary",)),
        name="moe_combine",
    )(slot, y, x1, mod, route)


def _final_norm_kernel(x_ref, g_ref, o_ref):
    o_ref[...] = _rms(x_ref[...], g_ref[...])


def final_norm_call(x, g, tok0, n_tok):
    tm = 1024
    i0 = tok0 // tm
    return pl.pallas_call(
        _final_norm_kernel,
        grid=(n_tok // tm,),
        in_specs=[pl.BlockSpec((tm, D_MODEL), lambda i: (i0 + i, 0)),
                  pl.BlockSpec((1, D_MODEL), lambda i: (0, 0))],
        out_specs=pl.BlockSpec((tm, D_MODEL), lambda i: (i, 0)),
        out_shape=jax.ShapeDtypeStruct((n_tok, D_MODEL), F32),
        compiler_params=_cparams(("arbitrary",)),
        name="final_norm",
    )(x, g)


def _cache_store_kernel(*refs):
    n = len(refs) // 3
    for src, dst in zip(refs[:n], refs[2 * n:]):
        dst[...] = src[...].reshape(dst.shape)


def cache_store(sources, all_layers, l):
    n_sq = 2
    tm = n_sq * SEQ
    return pl.pallas_call(
        _cache_store_kernel,
        grid=(N_CTX // tm,),
        in_specs=([pl.BlockSpec((tm, s.shape[1]), lambda i: (i, 0)) for s in sources]
                  + [pl.BlockSpec(memory_space=pl.ANY)] * len(all_layers)),
        out_specs=[pl.BlockSpec((n_sq, None, SEQ, a.shape[-1]), lambda i: (i, l, 0, 0)) for a in all_layers],
        out_shape=[jax.ShapeDtypeStruct(a.shape, F32) for a in all_layers],
        input_output_aliases={len(sources) + k: k for k in range(len(all_layers))},
        compiler_params=_cparams(("arbitrary",)),
        name="cache_store",
    )(*sources, *all_layers)


def _gate_row(p):
    return jnp.zeros((1, LANES), F32).at[0, GATE_A:GATE_A + 2 * H_A].set(p.reshape(-1))


def _tile_lanes(g, width):
    return jnp.tile(g, width // g.shape[0])[None, :]


def kernel(x_prompt, x_sample, state_delta, cache_gqa_k, cache_gqa_v, cache_diff_k, cache_diff_v, c, c_ctx,
           ada_w, ada_b, norm_mix, norm_ffn, w_in, conv_w, dn_a_log, dn_dt_bias, dn_norm, gqa_q_norm,
           gqa_k_norm, diff_lambda, diff_norm, w_out, ffn_w_gu, ffn_w_down, moe_router, moe_w_gu, moe_w_down,
           final_norm):
    cond8 = jnp.concatenate([c_ctx[None, :], c, jnp.zeros((8 - N_COND, D_MODEL), F32)], axis=0)
    gate_lo = 4 * A_W
    gate_hi = gate_lo + 4 * H_A
    w_in_p = jnp.concatenate(
        [w_in[:, :, :gate_lo], w_in[:, :, gate_hi:], w_in[:, :, gate_lo:gate_hi],
         jnp.zeros((DEPTH, D_MODEL, LANES - 4 * H_A), F32)], axis=-1).astype(BF16)
    w_out_b = w_out.astype(BF16)
    norm_mix3 = norm_mix.reshape(DEPTH, 1, D_MODEL)
    norm_ffn3 = norm_ffn.reshape(DEPTH, 1, D_MODEL)
    ffn_gu_b, ffn_d_b = ffn_w_gu.astype(BF16), ffn_w_down.astype(BF16)
    router_p = jnp.pad(moe_router, ((0, 0), (0, 0), (0, LANES - N_EXP)))
    rope_b = _rope_tables(DEC_SEQ // GRID_W, HEAD_DIM)
    rope_c = _rope_tables(DEC_SEQ // GRID_W, DIFF_HD)
    s0_ctx = jnp.zeros((BATCH, 2, H_A, HEAD_DIM, HEAD_DIM), F32)
    cache_b = (cache_gqa_k.reshape(DEC_BATCH, DEPTH, PAST_LEN, KV_B * HEAD_DIM),
               cache_gqa_v.reshape(DEC_BATCH, DEPTH, PAST_LEN, KV_B * HEAD_DIM))
    cache_c = (cache_diff_k.reshape(DEC_BATCH, DEPTH, PAST_LEN, C_W),
               cache_diff_v.reshape(DEC_BATCH, DEPTH, PAST_LEN, C_W))

    mod = ada_modulation(cond8, ada_w, ada_b)[:, :N_COND].reshape(DEPTH, N_COND, 1, 6 * D_MODEL)

    x = jnp.concatenate([x_prompt.reshape(N_CTX, D_MODEL), x_sample.reshape(N_LAT, D_MODEL)], axis=0)
    all_states = jnp.zeros((BATCH, DEPTH, 2, H_A, HEAD_DIM, HEAD_DIM), F32)
    ctx_layers = tuple(jnp.zeros((BATCH, DEPTH, SEQ, w), F32)
                       for w in (KV_B * HEAD_DIM, KV_B * HEAD_DIM, C_W, C_W))
    ctx = dict(n_seq=BATCH, seq_len=SEQ, tok0=0)
    lat = dict(n_seq=DEC_BATCH, seq_len=DEC_SEQ, tok0=N_CTX)
    for l in range(DEPTH):
        qkv_a, z_a, q_b, k_b, v_b, q_c, k_c, v_c, gate = in_projection(x, mod, norm_mix3, w_in_p, l)

        dn_args = (qkv_a, z_a, gate, conv_w[l], _gate_row(dn_a_log[l]), _gate_row(dn_dt_bias[l]),
                   _tile_lanes(dn_norm[l], LANES))
        oa_ctx, all_states = delta_mixer(*dn_args, s0_ctx, n_sub=2, all_states=all_states, layer=l, **ctx)
        oa_lat, _ = delta_mixer(*dn_args, state_delta[:, l], n_sub=1, **lat)

        gq, gk = _tile_lanes(gqa_q_norm[l], LANES), _tile_lanes(gqa_k_norm[l], LANES)
        ob_ctx, kn_ctx = gqa_mixer(q_b, k_b, v_b, gq, gk, **ctx)
        (ob_lat,) = gqa_mixer(q_b, k_b, v_b, gq, gk, cache=cache_b, rope=rope_b, layer=l, **lat)

        lam_init = 0.8 - 0.6 * math.exp(-0.3 * l)
        gn = _tile_lanes(diff_norm[l], C_W)
        oc_ctx = diff_mixer(q_c, k_c, v_c, diff_lambda[l], gn, lam_init, **ctx)
        oc_lat = diff_mixer(q_c, k_c, v_c, diff_lambda[l], gn, lam_init, cache=cache_c, rope=rope_c, layer=l, **lat)

        mixes = ((oa_ctx, oa_lat), (ob_ctx, ob_lat), (oc_ctx, oc_lat))
        if l % 2 == 0:
            x1, h2 = out_projection(x, mixes, w_out_b, mod, norm_ffn3, l)
            x = ffn(h2, ffn_gu_b, ffn_d_b, x1, mod, l)
        else:
            x1, h2, route = out_projection(x, mixes, w_out_b, mod, norm_ffn3, l, router_p[l // 2])
            y_slots, slot = moe_experts(h2, route, moe_w_gu, moe_w_down, l)
            x = moe_combine(y_slots, slot, x1, mod, route, l)

        ctx_layers = cache_store((kn_ctx, v_b, k_c, v_c), ctx_layers, l)

    y_prompt = final_norm_call(x, final_norm[None, :], 0, N_CTX).reshape(BATCH, SEQ, D_MODEL)
    y_sample = final_norm_call(x, final_norm[None, :], N_CTX, N_LAT).reshape(DEC_BATCH, DEC_SEQ, D_MODEL)
    new_gk, new_gv, new_dk, new_dv = ctx_layers
    return (y_prompt, y_sample, all_states,
            new_gk.reshape(BATCH, DEPTH, SEQ, KV_B, HEAD_DIM), new_gv.reshape(BATCH, DEPTH, SEQ, KV_B, HEAD_DIM),
            new_dk.reshape(BATCH, DEPTH, SEQ, H_C, 2, DIFF_HD), new_dv.reshape(BATCH, DEPTH, SEQ, H_C, 2 * DIFF_HD))
```

```python
import functools
import math

import jax
import jax.numpy as jnp
from jax import lax
from jax.experimental import pallas as pl
from jax.experimental.pallas import tpu as pltpu

D_MODEL = 1024
BATCH = 16
SEQ = 256
DEPTH = 4
DEC_BATCH = 2
DEC_SEQ = 1024
PAST_LEN = 512
GRID_W = 64
HEAD_DIM = 64
H_A = 6
A_W = H_A * HEAD_DIM
H_B = 6
KV_B = 2
B_W = H_B * HEAD_DIM
H_C = 4
DIFF_HD = 32
C_W = H_C * 2 * DIFF_HD
CONV_W = 3
CHUNK = 64
ROPE_THETA = 10000.0
D_FF = 2816
N_EXP = 8
TOP_K = 2
D_FF_E = 1408
EPS = 1e-6

N_CTX = BATCH * SEQ
N_LAT = DEC_BATCH * DEC_SEQ
N_TOK = N_CTX + N_LAT
N_COND = 1 + DEC_BATCH
IN_PAD = 3072
FF_CHUNK = D_FF_E
LANES = 128
VMEM_LIMIT = 56 * 1024 * 1024

IN_SPLIT = (("qkv_a", 3 * A_W), ("z_a", A_W), ("q_b", B_W), ("k_b", KV_B * HEAD_DIM), ("v_b", KV_B * HEAD_DIM),
            ("q_c", C_W), ("k_c", C_W), ("v_c", C_W), ("gate", LANES))
GATE_BETA, GATE_A = 0, 2 * H_A

F32 = jnp.float32
BF16 = jnp.bfloat16
HI = lax.Precision.HIGHEST


def _cparams(sem):
    return pltpu.CompilerParams(dimension_semantics=sem, vmem_limit_bytes=VMEM_LIMIT)


def _dot(a, b):
    return jnp.dot(a, b, preferred_element_type=F32)


def _dot_hi(a, b):
    return jnp.dot(a, b, preferred_element_type=F32, precision=HI)


def _dot_split(a, b):
    a_hi, b_hi = a.astype(BF16), b.astype(BF16)
    a_lo = (a - a_hi.astype(F32)).astype(BF16)
    b_lo = (b - b_hi.astype(F32)).astype(BF16)
    return _dot(jnp.concatenate([a_hi, a_lo, a_hi], axis=1), jnp.concatenate([b_hi, b_hi, b_lo], axis=0))


def _dot_nt(a, b):
    return lax.dot_general(a, b, (((1,), (1,)), ((), ())), preferred_element_type=F32)


def _silu(x):
    return x * jax.nn.sigmoid(x)


def _softplus(x):
    return jnp.maximum(x, 0.0) + jnp.log1p(jnp.exp(-jnp.abs(x)))


def _rms(x, g):
    return x * lax.rsqrt(jnp.mean(x * x, axis=-1, keepdims=True) + EPS) * g


def _lane(shape):
    return lax.broadcasted_iota(jnp.int32, shape, len(shape) - 1)


def _group_mean_matrix(width, group):
    i = lax.broadcasted_iota(jnp.int32, (width, width), 0)
    j = lax.broadcasted_iota(jnp.int32, (width, width), 1)
    return jnp.where(i // group == j // group, 1.0 / group, 0.0).astype(BF16)


def _group_rms(x, gmat, g):
    sq = x * x
    hi = sq.astype(BF16)
    lo = (sq - hi.astype(F32)).astype(BF16)
    ms = _dot(jnp.concatenate([hi, lo], axis=1), jnp.concatenate([gmat, gmat], axis=0))
    return x * lax.rsqrt(ms + EPS) * g


def _cond_row(i, tm):
    return jnp.maximum((i * tm) // DEC_SEQ - (N_CTX // DEC_SEQ - 1), 0)


def _ada_kernel(c_ref, w_ref, b_ref, o_ref):
    o_ref[...] = _dot_split(_silu(c_ref[...]), w_ref[...]) + b_ref[...]


def ada_modulation(cond8, ada_w, ada_b):
    tn = 1024
    return pl.pallas_call(
        _ada_kernel,
        grid=(DEPTH, 6 * D_MODEL // tn),
        in_specs=[pl.BlockSpec((8, D_MODEL), lambda l, j: (0, 0)),
                  pl.BlockSpec((None, D_MODEL, tn), lambda l, j: (l, 0, j)),
                  pl.BlockSpec((None, 1, tn), lambda l, j: (l, 0, j))],
        out_specs=pl.BlockSpec((None, 8, tn), lambda l, j: (l, 0, j)),
        out_shape=jax.ShapeDtypeStruct((DEPTH, 8, 6 * D_MODEL), F32),
        compiler_params=_cparams(("arbitrary", "arbitrary")),
        name="ada_modulation",
    )(cond8, ada_w, ada_b.reshape(DEPTH, 1, 6 * D_MODEL))


def _inproj_kernel(x_ref, g_ref, sh_ref, sc_ref, w_ref, *o_refs):
    h = _rms(x_ref[...], g_ref[...]) * (1.0 + sc_ref[...]) + sh_ref[...]
    acc = _dot(h.astype(BF16), w_ref[...])
    off = 0
    for o_ref, (_, width) in zip(o_refs, IN_SPLIT):
        o_ref[...] = acc[:, off:off + width]
        off += width


def in_projection(x, mod, norm_g, w_in_p, l):
    tm = 512
    return pl.pallas_call(
        _inproj_kernel,
        grid=(N_TOK // tm,),
        in_specs=[pl.BlockSpec((tm, D_MODEL), lambda i: (i, 0)),
                  pl.BlockSpec((None, 1, D_MODEL), lambda i: (l, 0, 0)),
                  pl.BlockSpec((None, None, 1, D_MODEL), lambda i: (l, _cond_row(i, tm), 0, 0)),
                  pl.BlockSpec((None, None, 1, D_MODEL), lambda i: (l, _cond_row(i, tm), 0, 1)),
                  pl.BlockSpec((None, D_MODEL, IN_PAD), lambda i: (l, 0, 0))],
        out_specs=[pl.BlockSpec((tm, w), lambda i: (i, 0)) for _, w in IN_SPLIT],
        out_shape=[jax.ShapeDtypeStruct((N_TOK, w), F32) for _, w in IN_SPLIT],
        compiler_params=_cparams(("arbitrary",)),
        name="in_projection",
    )(x, norm_g, mod, mod, w_in_p)


SOLVE_BLOCK = 16


def _solve_unit_triangular(lmat, rhs, lo, diag_blk, eye):
    def both(a):
        return jnp.concatenate([jnp.where(lo, a, 0.0), jnp.where(lo, 0.0, a)], axis=0)

    neg_d = jnp.where(diag_blk, -lmat, 0.0)
    pd = both(neg_d)
    po = both(-lmat - neg_d)
    t = eye + pd
    p = _dot(pd.astype(BF16), pd.astype(BF16))
    n_fac = int(math.log2(SOLVE_BLOCK))
    for j in range(1, n_fac):
        if j + 1 < n_fac:
            tp = _dot(p.astype(BF16), jnp.concatenate([t, p], axis=1).astype(BF16))
            t = t + tp[:, :LANES]
            p = tp[:, LANES:]
        else:
            t = t + _dot(p.astype(BF16), t.astype(BF16))
    cm = _dot(t.astype(BF16), jnp.concatenate([rhs, po], axis=1).astype(BF16))
    c, m = cm[:, :LANES], cm[:, LANES:].astype(BF16)
    x = c
    for _ in range(CHUNK // SOLVE_BLOCK - 1):
        x = c + _dot(m, x.astype(BF16))
    return x


def _delta_kernel(qkv_ref, z_ref, gate_ref, cw_ref, alog_ref, dtb_ref, ng_ref, s0_ref, o_ref, sfin_ref,
                  pad_s, kk_s, qq_s, vk_s, gc_s, beta_s, u_s, w_s, qd_s, a_s, kdt_s, egt_s, sa_s, o_s,
                  *, seq_len):
    n_chunk = seq_len // CHUNK
    n_pair = H_A // 2
    half = HEAD_DIM
    lo = _lane((CHUNK, LANES)) < half
    lo_row = _lane((1, LANES)) < half
    lo16 = _lane((CHUNK, LANES)).astype(F32).astype(BF16) < half
    row = lax.broadcasted_iota(jnp.int32, (CHUNK, LANES), 0)
    col = _lane((CHUNK, LANES)) % half
    ahead = jnp.where(lo, row - col, col - row)
    incl = ahead >= 0
    strict = ahead > 0
    diag_blk = row // SOLVE_BLOCK == col // SOLVE_BLOCK
    big_row = lax.broadcasted_iota(jnp.int32, (LANES, LANES), 0)
    anti = big_row // half + _lane((LANES, LANES)) // half == 1
    eye = (big_row == _lane((LANES, LANES))).astype(F32)

    rb = 128
    zero8 = jnp.zeros((8, 3 * A_W), F32)
    pad_s[0:8, :] = zero8
    pad_s[8 + seq_len:16 + seq_len, :] = zero8
    pad_s[8:8 + seq_len, :] = qkv_ref[...]
    lo_rb = _lane((rb, LANES)) < half
    for r0 in range(0, seq_len, rb):
        y = _silu(pad_s[7 + r0:7 + r0 + rb, :] * cw_ref[0:1, :] + pad_s[8 + r0:8 + r0 + rb, :] * cw_ref[1:2, :]
                  + pad_s[9 + r0:9 + r0 + rb, :] * cw_ref[2:3, :])

        def dup(part, p):
            c = y[:, part * A_W + p * LANES:part * A_W + (p + 1) * LANES]
            r = pltpu.roll(c, half, axis=1)
            return jnp.where(lo_rb, c, r), jnp.where(lo_rb, r, c)

        def l2n(x):
            return x * lax.rsqrt(0.5 * jnp.sum(x * x, axis=-1, keepdims=True) + EPS)

        for p in range(n_pair):
            qs, ks, vs = dup(0, p), dup(1, p), dup(2, p)
            for e in range(2):
                h = 2 * p + e
                kk = l2n(ks[e])
                kk_s[h, r0:r0 + rb, :] = kk
                qq_s[h, r0:r0 + rb, :] = l2n(qs[e]) * (HEAD_DIM ** -0.5)
                vk_s[h, r0:r0 + rb, :] = jnp.where(lo_rb, vs[e], kk)

    gt = gate_ref[...]
    beta_s[...] = jax.nn.sigmoid(gt)
    g = -jnp.exp(alog_ref[...]) * _softplus(gt + dtb_ref[...])
    g1 = g.astype(BF16)
    r1 = g - g1.astype(F32)
    g2 = r1.astype(BF16)
    g3 = (r1 - g2.astype(F32)).astype(BF16)
    g_split = jnp.concatenate([g1, g2, g3], axis=1)
    cb = 256
    ci = lax.broadcasted_iota(jnp.int32, (cb, cb), 0)
    cj = lax.broadcasted_iota(jnp.int32, (cb, cb), 1)
    same = ci // CHUNK == cj // CHUNK
    m_pre = (same & (ci >= cj)).astype(BF16)
    m_suf = (same & (ci <= cj)).astype(BF16)
    fwd_cols = (_lane((cb, LANES)) >= GATE_A) & (_lane((cb, LANES)) < GATE_A + H_A)
    for r0 in range(0, seq_len, cb):
        blk = g_split[r0:r0 + cb, :]
        pre, suf = _dot(m_pre, blk), _dot(m_suf, blk)
        pre = pre[:, :LANES] + pre[:, LANES:2 * LANES] + pre[:, 2 * LANES:]
        suf = suf[:, :LANES] + suf[:, LANES:2 * LANES] + suf[:, 2 * LANES:]
        gc_s[r0:r0 + cb, :] = jnp.where(fwd_cols, pre, suf)

    for h in range(H_A):
        zero = jnp.zeros((half, half), F32)
        sa_s[h] = jnp.concatenate([jnp.concatenate([zero, s0_ref[1, h]], axis=1),
                                   jnp.concatenate([s0_ref[0, h], zero], axis=1)], axis=0)

    def prep(c, carry):
        rows = pl.ds(pl.multiple_of(c * CHUNK, CHUNK), CHUNK)
        gcb = gc_s[rows, :]
        gct = gcb.T
        bb = beta_s[rows, :]
        for h in range(H_A):
            cf = gcb[:, GATE_A + h:GATE_A + h + 1]
            cbk = gcb[:, GATE_A + H_A + h:GATE_A + H_A + h + 1]
            gc_fb = jnp.where(lo, cf, cbk)
            gr_fb = jnp.concatenate([gct[GATE_A + h:GATE_A + h + 1, :],
                                     gct[GATE_A + H_A + h:GATE_A + H_A + h + 1, :]], axis=1)
            decay = jnp.where(incl, jnp.exp(jnp.where(incl, gc_fb - gr_fb, 0.0)), 0.0)
            bf = bb[:, GATE_BETA + h:GATE_BETA + h + 1]
            bbk = bb[:, GATE_BETA + H_A + h:GATE_BETA + H_A + h + 1]
            egf, egb = jnp.exp(cf), jnp.exp(cbk)
            tot_f = gcb[CHUNK - 1:CHUNK, GATE_A + h:GATE_A + h + 1]
            tot_b = gcb[0:1, GATE_A + H_A + h:GATE_A + H_A + h + 1]
            kk, qq, vk = kk_s[h, rows, :], qq_s[h, rows, :], vk_s[h, rows, :]
            k16 = kk[:, :half].astype(BF16)
            kq = jnp.concatenate([k16, qq[:, :half].astype(BF16)], axis=0)
            gram = _dot_nt(kq, jnp.concatenate([k16, k16], axis=0))
            lmat = jnp.where(strict, gram[:CHUNK] * jnp.where(lo, bf, bbk) * decay, 0.0)
            a_s[h, rows, :] = jnp.where(incl, gram[CHUNK:] * decay, 0.0).astype(BF16)
            x_f = vk * jnp.where(lo, bf, bf * egf)
            x_b = vk * jnp.where(lo, bbk, bbk * egb)
            x = _solve_unit_triangular(lmat, jnp.concatenate([x_f, x_b], axis=0), lo, diag_blk, eye)
            x_f, x_b = x[:CHUNK], pltpu.roll(x[CHUNK:], half, axis=1)
            u_s[h, rows, :] = jnp.where(lo, x_f, x_b)
            w_s[h, rows, :] = jnp.where(lo, x_b, x_f).astype(BF16)
            qd_s[h, rows, :] = (qq * jnp.where(lo, egb, egf)).astype(BF16)
            kd = kk * jnp.where(lo, jnp.exp(tot_b - cbk), jnp.exp(tot_f - cf))
            kdt_s[h, c] = kd.T.astype(BF16)
            egt_s[h, c] = jnp.exp(jnp.where(lo_row, tot_f, tot_b))
        return carry

    lax.fori_loop(0, n_chunk, prep, 0)

    def scan(s, carry):
        rf = pl.ds(pl.multiple_of(s * CHUNK, CHUNK), CHUNK)
        sb = n_chunk - 1 - s
        rbk = pl.ds(pl.multiple_of(sb * CHUNK, CHUNK), CHUNK)
        for h in range(H_A):
            u = jnp.where(lo, u_s[h, rf, :], u_s[h, rbk, :])
            w = jnp.where(lo16, w_s[h, rbk, :], w_s[h, rf, :])
            qd = jnp.where(lo16, qd_s[h, rbk, :], qd_s[h, rf, :])
            a = jnp.where(lo16, a_s[h, rf, :], a_s[h, rbk, :])
            kdt = jnp.concatenate([kdt_s[h, sb][:half], kdt_s[h, s][half:]], axis=0)
            egt = jnp.where(lo_row, egt_s[h, s], egt_s[h, sb])
            st = sa_s[h]
            wq = _dot(jnp.concatenate([w, qd], axis=0), st.astype(BF16))
            v = u - wq[:CHUNK]
            vbd = jnp.concatenate([jnp.where(lo, v, 0.0), jnp.where(lo, 0.0, v)], axis=0).astype(BF16)
            o = wq[CHUNK:] + _dot(a, vbd)
            sa_s[h] = st * egt + jnp.where(anti, _dot(kdt, v.astype(BF16)), 0.0)
            o_s[h, rf, 0:half] = o[:, :half]
            o_s[h, rbk, half:LANES] = o[:, half:]
        return carry

    lax.fori_loop(0, n_chunk, scan, 0)

    for r0 in range(0, seq_len, rb):
        for p in range(n_pair):
            nrm = []
            for e in range(2):
                ofb = o_s[2 * p + e, r0:r0 + rb, :]
                oo = ofb + pltpu.roll(ofb, half, axis=1)
                nrm.append(oo * lax.rsqrt(jnp.sum(oo * oo, axis=-1, keepdims=True) * (0.5 / HEAD_DIM) + EPS))
            o_ref[r0:r0 + rb, p * LANES:(p + 1) * LANES] = (
                jnp.where(lo_rb, nrm[0], nrm[1]) * ng_ref[...] * _silu(z_ref[r0:r0 + rb, p * LANES:(p + 1) * LANES]))

    for h in range(H_A):
        st = sa_s[h]
        sfin_ref[0, h] = st[half:, :half]
        sfin_ref[1, h] = st[:half, half:]


def delta_mixer(qkv, z, gate, cw, alog_row, dtb_row, ng2, s0, *, n_seq, seq_len, tok0):
    n_chunk = seq_len // CHUNK
    b0 = tok0 // seq_len
    tok = lambda w: pl.BlockSpec((seq_len, w), lambda b: (b0 + b, 0))
    full = lambda a: pl.BlockSpec(a.shape, lambda b: (0,) * a.ndim)
    state = pl.BlockSpec((None, 2, H_A, HEAD_DIM, HEAD_DIM), lambda b: (b, 0, 0, 0, 0))
    per_head = lambda dt: pltpu.VMEM((H_A, seq_len, LANES), dt)
    return pl.pallas_call(
        functools.partial(_delta_kernel, seq_len=seq_len),
        grid=(n_seq,),
        in_specs=[tok(3 * A_W), tok(A_W), tok(LANES), full(cw), full(alog_row), full(dtb_row), full(ng2), state],
        out_specs=[pl.BlockSpec((seq_len, A_W), lambda b: (b, 0)), state],
        out_shape=[jax.ShapeDtypeStruct((n_seq * seq_len, A_W), F32),
                   jax.ShapeDtypeStruct((n_seq, 2, H_A, HEAD_DIM, HEAD_DIM), F32)],
        scratch_shapes=[pltpu.VMEM((seq_len + 16, 3 * A_W), F32),
                        per_head(F32), per_head(F32), per_head(F32),
                        pltpu.VMEM((seq_len, LANES), F32),
                        pltpu.VMEM((seq_len, LANES), F32),
                        per_head(F32), per_head(BF16), per_head(BF16), per_head(BF16),
                        pltpu.VMEM((H_A, n_chunk, LANES, HEAD_DIM), BF16),
                        pltpu.VMEM((H_A, n_chunk, 1, LANES), F32),
                        pltpu.VMEM((H_A, LANES, LANES), F32),
                        per_head(F32)],
        compiler_params=_cparams(("arbitrary",)),
        name=f"delta_mixer_{seq_len}",
    )(qkv, z, gate, cw, alog_row, dtb_row, ng2, s0)


def _delta_kernel(*refs, seq_len, n_sub, aliased):
    qkv_ref, z_ref, gate_ref, cw_ref, alog_ref, dtb_ref, ng_ref, s0_ref = refs[:8]
    (o_ref, sfin_ref, pad_s, kk_s, qq_s, vk_s, gc_s, beta_s, u_s, w_s, qd_s, a_s, kdt_s, egt_s, sa_s, o_s,
     tp_s, t32_s, rp_s, c32_s, m_s, x_s, vbd_s, v_s, op_s) = refs[9 if aliased else 8:]
    n_chunk = seq_len // CHUNK
    n_pair = H_A // 2
    n_unit = n_sub * H_A
    half = HEAD_DIM
    lo = _lane((CHUNK, LANES)) < half
    lo_row = _lane((1, LANES)) < half
    lo16 = _lane((CHUNK, LANES)).astype(F32).astype(BF16) < half
    row = lax.broadcasted_iota(jnp.int32, (CHUNK, LANES), 0)
    col = _lane((CHUNK, LANES)) % half
    ahead = jnp.where(lo, row - col, col - row)
    incl = ahead >= 0
    strict = ahead > 0
    diag_blk = row // SOLVE_BLOCK == col // SOLVE_BLOCK
    big_row = lax.broadcasted_iota(jnp.int32, (LANES, LANES), 0)
    anti = big_row // half + _lane((LANES, LANES)) // half == 1
    eye = (big_row == _lane((LANES, LANES))).astype(F32)

    def both(a):
        return jnp.concatenate([jnp.where(lo, a, 0.0), jnp.where(lo, 0.0, a)], axis=0)

    rb = 128
    lo_rb = _lane((rb, LANES)) < half
    zero8 = jnp.zeros((8, 3 * A_W), F32)
    pad_s[0:8, :] = zero8
    pad_s[8 + seq_len:16 + seq_len, :] = zero8

    def l2n(x):
        return x * lax.rsqrt(0.5 * jnp.sum(x * x, axis=-1, keepdims=True) + EPS)

    for sq in range(n_sub):
        pad_s[8:8 + seq_len, :] = qkv_ref[sq * seq_len:(sq + 1) * seq_len, :]
        for r0 in range(0, seq_len, rb):
            y = _silu(pad_s[7 + r0:7 + r0 + rb, :] * cw_ref[0:1, :] + pad_s[8 + r0:8 + r0 + rb, :] * cw_ref[1:2, :]
                      + pad_s[9 + r0:9 + r0 + rb, :] * cw_ref[2:3, :])

            def dup(part, p):
                c = y[:, part * A_W + p * LANES:part * A_W + (p + 1) * LANES]
                r = pltpu.roll(c, half, axis=1)
                return jnp.where(lo_rb, c, r), jnp.where(lo_rb, r, c)

            for p in range(n_pair):
                qs, ks, vs = dup(0, p), dup(1, p), dup(2, p)
                for e in range(2):
                    u = sq * H_A + 2 * p + e
                    kk = l2n(ks[e])
                    kk_s[u, r0:r0 + rb, :] = kk
                    qq_s[u, r0:r0 + rb, :] = l2n(qs[e]) * (HEAD_DIM ** -0.5)
                    vk_s[u, r0:r0 + rb, :] = jnp.where(lo_rb, vs[e], kk)

    cb = 256
    ci = lax.broadcasted_iota(jnp.int32, (cb, cb), 0)
    cj = lax.broadcasted_iota(jnp.int32, (cb, cb), 1)
    same = ci // CHUNK == cj // CHUNK
    m_pre = jnp.where(same & (ci >= cj), 1.0, 0.0).astype(BF16)
    m_suf = jnp.where(same & (ci <= cj), 1.0, 0.0).astype(BF16)
    fwd_cols = (_lane((cb, LANES)) >= GATE_A) & (_lane((cb, LANES)) < GATE_A + H_A)
    for r0 in range(0, n_sub * seq_len, cb):
        gt = gate_ref[r0:r0 + cb, :]
        beta_s[r0:r0 + cb, :] = jax.nn.sigmoid(gt)
        g = -jnp.exp(alog_ref[...]) * _softplus(gt + dtb_ref[...])
        g1 = g.astype(BF16)
        r1 = g - g1.astype(F32)
        g2 = r1.astype(BF16)
        g3 = (r1 - g2.astype(F32)).astype(BF16)
        blk = jnp.concatenate([g1, g2, g3], axis=1)
        pre, suf = _dot(m_pre, blk), _dot(m_suf, blk)
        pre = pre[:, :LANES] + pre[:, LANES:2 * LANES] + pre[:, 2 * LANES:]
        suf = suf[:, :LANES] + suf[:, LANES:2 * LANES] + suf[:, 2 * LANES:]
        gc_s[r0:r0 + cb, :] = jnp.where(fwd_cols, pre, suf)

    zero = jnp.zeros((half, half), F32)
    for sq in range(n_sub):
        for h in range(H_A):
            sa_s[sq * H_A + h] = jnp.concatenate([jnp.concatenate([zero, s0_ref[sq, 1, h]], axis=1),
                                                  jnp.concatenate([s0_ref[sq, 0, h], zero], axis=1)], axis=0)

    n_fac = int(math.log2(SOLVE_BLOCK))

    def prep(c, carry):
        rows = pl.ds(pl.multiple_of(c * CHUNK, CHUNK), CHUNK)
        for sq in range(n_sub):
            grow = pl.ds(pl.multiple_of(sq * seq_len + c * CHUNK, CHUNK), CHUNK)
            gcb = gc_s[grow, :]
            gct = gcb.T
            bb = beta_s[grow, :]
            for h in range(H_A):
                u = sq * H_A + h
                full = (CHUNK, LANES)
                cff = jnp.broadcast_to(gcb[:, GATE_A + h:GATE_A + h + 1], full)
                cbb = jnp.broadcast_to(gcb[:, GATE_A + H_A + h:GATE_A + H_A + h + 1], full)
                bff = jnp.broadcast_to(bb[:, GATE_BETA + h:GATE_BETA + h + 1], full)
                bbb = jnp.broadcast_to(bb[:, GATE_BETA + H_A + h:GATE_BETA + H_A + h + 1], full)
                gr_fb = jnp.concatenate([gct[GATE_A + h:GATE_A + h + 1, :],
                                         gct[GATE_A + H_A + h:GATE_A + H_A + h + 1, :]], axis=1)
                decay = jnp.where(incl, jnp.exp(jnp.where(incl, jnp.where(lo, cff, cbb) - gr_fb, 0.0)), 0.0)
                egf, egb = jnp.exp(cff), jnp.exp(cbb)
                tot_f = gcb[CHUNK - 1:CHUNK, GATE_A + h:GATE_A + h + 1]
                tot_b = gcb[0:1, GATE_A + H_A + h:GATE_A + H_A + h + 1]
                kk, qq, vk = kk_s[u, rows, :], qq_s[u, rows, :], vk_s[u, rows, :]
                k16 = kk[:, :half].astype(BF16)
                kq = jnp.concatenate([k16, qq[:, :half].astype(BF16)], axis=0)
                gram = _dot_nt(kq, jnp.concatenate([k16, k16], axis=0))
                neg_l = jnp.where(strict, gram[:CHUNK] * jnp.where(lo, -bff, -bbb) * decay, 0.0)
                a_s[u, rows, :] = jnp.where(incl, gram[CHUNK:] * decay, 0.0).astype(BF16)
                neg_d = jnp.where(diag_blk, neg_l, 0.0)
                pd = both(neg_d)
                t32_s[u] = eye + pd
                tp_s[u, :, :LANES] = (eye + pd).astype(BF16)
                tp_s[u, :, LANES:] = pd.astype(BF16)
                x_f = vk * jnp.where(lo, bff, bff * egf)
                x_b = vk * jnp.where(lo, bbb, bbb * egb)
                rp_s[u, :, :LANES] = jnp.concatenate([x_f, x_b], axis=0).astype(BF16)
                rp_s[u, :, LANES:] = both(neg_l - neg_d).astype(BF16)
                qd_s[u, rows, :] = (qq * jnp.where(lo, egb, egf)).astype(BF16)
                kd = kk * jnp.exp(jnp.where(lo, tot_b - cbb, tot_f - cff))
                kdt_s[u, c] = kd.T.astype(BF16)
                egt_s[u, c] = jnp.exp(jnp.where(lo_row, tot_f, tot_b))
        for u in range(n_unit):
            tp_s[u, :, LANES:] = _dot(tp_s[u, :, LANES:], tp_s[u, :, LANES:]).astype(BF16)
        for j in range(1, n_fac):
            for u in range(n_unit):
                if j + 1 < n_fac:
                    tp = _dot(tp_s[u, :, LANES:], tp_s[u])
                    t = t32_s[u] + tp[:, :LANES]
                    t32_s[u] = t
                    tp_s[u, :, :LANES] = t.astype(BF16)
                    tp_s[u, :, LANES:] = tp[:, LANES:].astype(BF16)
                else:
                    t = t32_s[u] + _dot(tp_s[u, :, LANES:], tp_s[u, :, :LANES])
                    tp_s[u, :, :LANES] = t.astype(BF16)
        for u in range(n_unit):
            cm = _dot(tp_s[u, :, :LANES], rp_s[u])
            c32_s[u] = cm[:, :LANES]
            rp_s[u] = cm.astype(BF16)
        assert CHUNK // SOLVE_BLOCK == 4
        for u in range(n_unit):
            mm = _dot(rp_s[u, :, LANES:], rp_s[u])
            y1 = c32_s[u] + mm[:, :LANES]
            c32_s[u] = y1
            x_s[u] = y1.astype(BF16)
            m_s[u] = mm[:, LANES:].astype(BF16)
        for u in range(n_unit):
            x = c32_s[u] + _dot(m_s[u], x_s[u])
            x_f, x_b = x[:CHUNK], pltpu.roll(x[CHUNK:], half, axis=1)
            u_s[u, rows, :] = jnp.where(lo, x_f, x_b)
            w_s[u, rows, :] = jnp.where(lo, x_b, x_f).astype(BF16)
        return carry

    lax.fori_loop(0, n_chunk, prep, 0)

    def scan(s, carry):
        rf = pl.ds(pl.multiple_of(s * CHUNK, CHUNK), CHUNK)
        sb = n_chunk - 1 - s
        rbk = pl.ds(pl.multiple_of(sb * CHUNK, CHUNK), CHUNK)
        for u in range(n_unit):
            w = jnp.where(lo16, w_s[u, rbk, :], w_s[u, rf, :])
            qd = jnp.where(lo16, qd_s[u, rbk, :], qd_s[u, rf, :])
            wq = _dot(jnp.concatenate([w, qd], axis=0), sa_s[u].astype(BF16))
            v = jnp.where(lo, u_s[u, rf, :], u_s[u, rbk, :]) - wq[:CHUNK]
            v_s[u] = v.astype(BF16)
            vbd_s[u] = both(v).astype(BF16)
            op_s[u] = wq[CHUNK:]
        for u in range(n_unit):
            a = jnp.where(lo16, a_s[u, rf, :], a_s[u, rbk, :])
            o = op_s[u] + _dot(a, vbd_s[u])
            o_s[u, rf, 0:half] = o[:, :half]
            o_s[u, rbk, half:LANES] = o[:, half:]
            kdt = jnp.concatenate([kdt_s[u, sb][:half], kdt_s[u, s][half:]], axis=0)
            egt = jnp.where(lo_row, egt_s[u, s], egt_s[u, sb])
            sa_s[u] = sa_s[u] * egt + jnp.where(anti, _dot(kdt, v_s[u]), 0.0)
        return carry

    lax.fori_loop(0, n_chunk, scan, 0)

    for sq in range(n_sub):
        for r0 in range(0, seq_len, rb):
            tr = sq * seq_len + r0
            for p in range(n_pair):
                nrm = []
                for e in range(2):
                    ofb = o_s[sq * H_A + 2 * p + e, r0:r0 + rb, :]
                    oo = ofb + pltpu.roll(ofb, half, axis=1)
                    nrm.append(oo * lax.rsqrt(jnp.sum(oo * oo, axis=-1, keepdims=True) * (0.5 / HEAD_DIM) + EPS))
                o_ref[tr:tr + rb, p * LANES:(p + 1) * LANES] = (
                    jnp.where(lo_rb, nrm[0], nrm[1]) * ng_ref[...]
                    * _silu(z_ref[tr:tr + rb, p * LANES:(p + 1) * LANES]))
        for h in range(H_A):
            st = sa_s[sq * H_A + h]
            sfin_ref[sq, 0, h] = st[half:, :half]
            sfin_ref[sq, 1, h] = st[:half, half:]


def delta_mixer(qkv, z, gate, cw, alog_row, dtb_row, ng2, s0, *, n_seq, seq_len, tok0, n_sub,
                all_states=None, layer=0):
    n_chunk = seq_len // CHUNK
    n_unit = n_sub * H_A
    blk = n_sub * seq_len
    b0 = tok0 // blk
    tok = lambda w: pl.BlockSpec((blk, w), lambda b: (b0 + b, 0))
    full = lambda a: pl.BlockSpec(a.shape, lambda b: (0,) * a.ndim)
    state = pl.BlockSpec((n_sub, 2, H_A, HEAD_DIM, HEAD_DIM), lambda b: (b, 0, 0, 0, 0))
    per_unit = lambda dt: pltpu.VMEM((n_unit, seq_len, LANES), dt)
    stage = lambda rows, cols, dt: pltpu.VMEM((n_unit, rows, cols), dt)
    aliased = all_states is not None
    in_specs = [tok(3 * A_W), tok(A_W), tok(LANES), full(cw), full(alog_row), full(dtb_row), full(ng2), state]
    args = [qkv, z, gate, cw, alog_row, dtb_row, ng2, s0]
    if aliased:
        in_specs.append(pl.BlockSpec(memory_space=pl.ANY))
        args.append(all_states)
        state_out = pl.BlockSpec((n_sub, None, 2, H_A, HEAD_DIM, HEAD_DIM), lambda b: (b, layer, 0, 0, 0, 0))
        state_shape = jax.ShapeDtypeStruct(all_states.shape, F32)
    else:
        state_out = state
        state_shape = jax.ShapeDtypeStruct((n_seq, 2, H_A, HEAD_DIM, HEAD_DIM), F32)
    return pl.pallas_call(
        functools.partial(_delta_kernel, seq_len=seq_len, n_sub=n_sub, aliased=aliased),
        grid=(n_seq // n_sub,),
        in_specs=in_specs,
        out_specs=[pl.BlockSpec((blk, A_W), lambda b: (b, 0)), state_out],
        out_shape=[jax.ShapeDtypeStruct((n_seq * seq_len, A_W), F32), state_shape],
        input_output_aliases={len(args) - 1: 1} if aliased else {},
        scratch_shapes=[pltpu.VMEM((seq_len + 16, 3 * A_W), F32),
                        per_unit(F32), per_unit(F32), per_unit(F32),
                        pltpu.VMEM((blk, LANES), F32),
                        pltpu.VMEM((blk, LANES), F32),
                        per_unit(F32), per_unit(BF16), per_unit(BF16), per_unit(BF16),
                        pltpu.VMEM((n_unit, n_chunk, LANES, HEAD_DIM), BF16),
                        pltpu.VMEM((n_unit, n_chunk, 1, LANES), F32),
                        stage(LANES, LANES, F32),
                        per_unit(F32),
                        stage(LANES, 2 * LANES, BF16),
                        stage(LANES, LANES, F32),
                        stage(LANES, 2 * LANES, BF16),
                        stage(LANES, LANES, F32),
                        stage(LANES, LANES, BF16),
                        stage(LANES, LANES, BF16),
                        stage(LANES, LANES, BF16),
                        stage(CHUNK, LANES, BF16),
                        stage(CHUNK, LANES, F32)],
        compiler_params=_cparams(("arbitrary",)),
        name=f"delta_mixer_{seq_len}",
    )(*args)


def _rope(x, cos, sin_signed, quarter):
    width = x.shape[-1]
    swapped = jnp.where((_lane(x.shape) % (2 * quarter)) < quarter,
                        pltpu.roll(x, width - quarter, axis=1), pltpu.roll(x, quarter, axis=1))
    return x * cos + swapped * sin_signed


def _rope_tables(rows, dim):
    nf = dim // 4
    inv = ROPE_THETA ** (-jnp.arange(nf, dtype=F32) / nf)
    r = jnp.repeat(jnp.arange(rows, dtype=F32), GRID_W)
    c = jnp.tile(jnp.arange(GRID_W, dtype=F32), rows)
    ar, ac = r[:, None] * inv, c[:, None] * inv
    cos = jnp.concatenate([jnp.cos(ar), jnp.cos(ar), jnp.cos(ac), jnp.cos(ac)], axis=1)
    sin = jnp.concatenate([-jnp.sin(ar), jnp.sin(ar), -jnp.sin(ac), jnp.sin(ac)], axis=1)
    return jnp.tile(cos, (1, LANES // dim)), jnp.tile(sin, (1, LANES // dim))


def _gqa_kernel(*refs, cached, n_sub):
    if cached:
        (q_ref, k_ref, v_ref, gq_ref, gk_ref, ck_ref, cv_ref, cosq_ref, sinq_ref, cosk_ref, sink_ref,
         o_ref, k_s, v_s) = refs
    else:
        q_ref, k_ref, v_ref, gq_ref, gk_ref, o_ref, kn_ref, k_s, v_s = refs
    gmat = _group_mean_matrix(LANES, HEAD_DIM)
    quarter = HEAD_DIM // 4
    n_k = k_ref.shape[0] // n_sub

    @pl.when(pl.program_id(1) == 0)
    def _():
        for sq in range(n_sub):
            k = _group_rms(k_ref[sq * n_k:(sq + 1) * n_k, :], gmat, gk_ref[...])
            v = v_ref[sq * n_k:(sq + 1) * n_k, :]
            if cached:
                k = _rope(k, cosk_ref[...], sink_ref[...], quarter)
                k = jnp.concatenate([ck_ref[...], k], axis=0)
                v = jnp.concatenate([cv_ref[...], v], axis=0)
            else:
                kn_ref[sq * n_k:(sq + 1) * n_k, :] = k
            k_s[sq] = k.astype(BF16)
            v_s[sq] = v.astype(BF16)

    tq = q_ref.shape[0] // n_sub
    lo = _lane((tq, LANES)) < HEAD_DIM
    group = H_B // KV_B
    q_scale = (HEAD_DIM ** -0.5) * math.log2(math.e)
    for sq in range(n_sub):
        k, v = k_s[sq], v_s[sq]
        qs = []
        for j in range(H_B // 2):
            q = _group_rms(q_ref[sq * tq:(sq + 1) * tq, j * LANES:(j + 1) * LANES], gmat, gq_ref[...])
            if cached:
                q = _rope(q, cosq_ref[...], sinq_ref[...], quarter)
            q = q * q_scale
            qs.append((q, pltpu.roll(q, HEAD_DIM, axis=1)))
        outs = [None] * H_B
        for kv in range(KV_B):
            keep = lo if kv == 0 else ~lo
            heads = range(kv * group, (kv + 1) * group)
            q3 = jnp.concatenate([jnp.where(keep, qs[h // 2][0 if h % 2 == kv else 1], 0.0).astype(BF16)
                                  for h in heads], axis=0)
            s = _dot_nt(q3, k)
            p = jnp.exp2(s - jnp.max(s, axis=-1, keepdims=True))
            r = _dot(p.astype(BF16), v) / jnp.sum(p, axis=-1, keepdims=True)
            for i, h in enumerate(heads):
                rh = r[i * tq:(i + 1) * tq]
                outs[h] = rh if h % 2 == kv else pltpu.roll(rh, HEAD_DIM, axis=1)
        for j in range(H_B // 2):
            o_ref[sq * tq:(sq + 1) * tq, j * LANES:(j + 1) * LANES] = jnp.where(lo, outs[2 * j], outs[2 * j + 1])


def gqa_mixer(q, k, v, gq, gk, *, n_seq, seq_len, tok0, cache=None, rope=None, layer=0):
    cached = cache is not None
    n_sub = 1 if cached else 2
    n_seq, seq_len = n_seq // n_sub, seq_len * n_sub
    tq = 256 * n_sub
    nq = seq_len // tq
    b0 = tok0 // seq_len
    q0 = tok0 // tq
    kvw = KV_B * HEAD_DIM
    in_specs = [pl.BlockSpec((tq, B_W), lambda b, i: (q0 + b * nq + i, 0)),
                pl.BlockSpec((seq_len, kvw), lambda b, i: (b0 + b, 0)),
                pl.BlockSpec((seq_len, kvw), lambda b, i: (b0 + b, 0)),
                pl.BlockSpec((1, LANES), lambda b, i: (0, 0)),
                pl.BlockSpec((1, LANES), lambda b, i: (0, 0))]
    args = [q, k, v, gq, gk]
    out_specs = [pl.BlockSpec((tq, B_W), lambda b, i: (b * nq + i, 0))]
    out_shape = [jax.ShapeDtypeStruct((n_seq * seq_len, B_W), F32)]
    if cached:
        cos, sin = rope
        cache_spec = pl.BlockSpec((None, None, PAST_LEN, kvw), lambda b, i: (b, layer, 0, 0))
        in_specs += [cache_spec, cache_spec,
                     pl.BlockSpec((tq, LANES), lambda b, i: (i, 0)),
                     pl.BlockSpec((tq, LANES), lambda b, i: (i, 0)),
                     pl.BlockSpec((seq_len, LANES), lambda b, i: (0, 0)),
                     pl.BlockSpec((seq_len, LANES), lambda b, i: (0, 0))]
        args += [cache[0], cache[1], cos, sin, cos, sin]
    else:
        out_specs.append(pl.BlockSpec((seq_len, kvw), lambda b, i: (b, 0)))
        out_shape.append(jax.ShapeDtypeStruct((n_seq * seq_len, kvw), F32))
    n_keys = seq_len // n_sub + (PAST_LEN if cached else 0)
    return pl.pallas_call(
        functools.partial(_gqa_kernel, cached=cached, n_sub=n_sub),
        grid=(n_seq, nq),
        in_specs=in_specs, out_specs=out_specs, out_shape=out_shape,
        scratch_shapes=[pltpu.VMEM((n_sub, n_keys, kvw), BF16), pltpu.VMEM((n_sub, n_keys, kvw), BF16)],
        compiler_params=_cparams(("arbitrary", "arbitrary")),
        name=f"gqa_mixer_{seq_len}",
    )(*args)


def _diff_kernel(*refs, cached, lam_init, n_sub):
    if cached:
        (q_ref, k_ref, v_ref, lam_ref, gn_ref, ck_ref, cv_ref, cosq_ref, sinq_ref, cosk_ref, sink_ref,
         o_ref) = refs
    else:
        q_ref, k_ref, v_ref, lam_ref, gn_ref, o_ref = refs
    quarter = DIFF_HD // 4
    lv = lam_ref[...]
    lam = (jnp.exp(jnp.sum(lv[0:1] * lv[1:2], axis=-1, keepdims=True))
           - jnp.exp(jnp.sum(lv[2:3] * lv[3:4], axis=-1, keepdims=True)) + lam_init)
    gmat = _group_mean_matrix(C_W, 2 * DIFF_HD)
    tq, n_k = q_ref.shape[0] // n_sub, k_ref.shape[0] // n_sub
    lane = _lane((tq, C_W))
    for sb in range(n_sub):
        q = q_ref[sb * tq:(sb + 1) * tq, :]
        k, v = k_ref[sb * n_k:(sb + 1) * n_k, :], v_ref[sb * n_k:(sb + 1) * n_k, :]
        if cached:
            cq = jnp.concatenate([cosq_ref[...]] * 2, axis=1)
            sq = jnp.concatenate([sinq_ref[...]] * 2, axis=1)
            ck = jnp.concatenate([cosk_ref[...]] * 2, axis=1)
            sk = jnp.concatenate([sink_ref[...]] * 2, axis=1)
            q = _rope(q, cq, sq, quarter)
            k = _rope(k, ck, sk, quarter)
            k = jnp.concatenate([ck_ref[...], k], axis=0)
            v = jnp.concatenate([cv_ref[...], v], axis=0)
        k = k.astype(BF16)
        v = v.astype(BF16)
        q = q * ((DIFF_HD ** -0.5) * math.log2(math.e))
        o = jnp.zeros(q.shape, F32)
        for h in range(H_C):
            ps, cs = [], []
            for m in range(2):
                qm = jnp.where(lane // DIFF_HD == 2 * h + m, q, 0.0).astype(BF16)
                s = _dot_nt(qm, k)
                p = jnp.exp2(s - jnp.max(s, axis=-1, keepdims=True))
                ps.append(p)
                cs.append((1.0 if m == 0 else lam) / jnp.sum(p, axis=-1, keepdims=True))
            a = ps[0] * cs[0] - ps[1] * cs[1]
            o = jnp.where(lane // (2 * DIFF_HD) == h, _dot(a.astype(BF16), v), o)
        o_ref[sb * tq:(sb + 1) * tq, :] = _group_rms(o, gmat, gn_ref[...]) * (1.0 - lam_init)


def diff_mixer(q, k, v, lam_p, gn, lam_init, *, n_seq, seq_len, tok0, cache=None, rope=None, layer=0):
    cached = cache is not None
    n_sub = 1 if cached else 2
    n_seq, seq_len = n_seq // n_sub, seq_len * n_sub
    tq = 256 * n_sub
    nq = seq_len // tq
    b0 = tok0 // seq_len
    q0 = tok0 // tq
    in_specs = [pl.BlockSpec((tq, C_W), lambda b, i: (q0 + b * nq + i, 0)),
                pl.BlockSpec((seq_len, C_W), lambda b, i: (b0 + b, 0)),
                pl.BlockSpec((seq_len, C_W), lambda b, i: (b0 + b, 0)),
                pl.BlockSpec((4, DIFF_HD), lambda b, i: (0, 0)),
                pl.BlockSpec((1, C_W), lambda b, i: (0, 0))]
    args = [q, k, v, lam_p, gn]
    if cached:
        cos, sin = rope
        cache_spec = pl.BlockSpec((None, None, PAST_LEN, C_W), lambda b, i: (b, layer, 0, 0))
        in_specs += [cache_spec, cache_spec,
                     pl.BlockSpec((tq, LANES), lambda b, i: (i, 0)),
                     pl.BlockSpec((tq, LANES), lambda b, i: (i, 0)),
                     pl.BlockSpec((seq_len, LANES), lambda b, i: (0, 0)),
                     pl.BlockSpec((seq_len, LANES), lambda b, i: (0, 0))]
        args += [cache[0], cache[1], cos, sin, cos, sin]
    return pl.pallas_call(
        functools.partial(_diff_kernel, cached=cached, lam_init=lam_init, n_sub=n_sub),
        grid=(n_seq, nq),
        in_specs=in_specs,
        out_specs=pl.BlockSpec((tq, C_W), lambda b, i: (b * nq + i, 0)),
        out_shape=jax.ShapeDtypeStruct((n_seq * seq_len, C_W), F32),
        compiler_params=_cparams(("arbitrary", "arbitrary")),
        name=f"diff_mixer_{seq_len}",
    )(*args)


def _outproj_kernel(*refs, routed, n_mix):
    x_ref = refs[0]
    mix_refs = refs[1:1 + 2 * n_mix]
    rest = refs[1 + 2 * n_mix:]
    if routed:
        w_ref, gm_ref, g_ref, sh_ref, sc_ref, r_ref, x1_ref, h_ref, lg_ref = rest
    else:
        w_ref, gm_ref, g_ref, sh_ref, sc_ref, x1_ref, h_ref = rest
    is_ctx = pl.program_id(0) < N_CTX // x_ref.shape[0]
    mixed = jnp.concatenate(
        [jnp.where(is_ctx, mix_refs[2 * m][...], mix_refs[2 * m + 1][...]) for m in range(n_mix)], axis=1)
    x1 = x_ref[...] + gm_ref[...] * _dot(mixed.astype(BF16), w_ref[...])
    x1_ref[...] = x1
    h = _rms(x1, g_ref[...]) * (1.0 + sc_ref[...]) + sh_ref[...]
    h_ref[...] = h.astype(h_ref.dtype)
    if routed:
        lg_ref[...] = _top2_route(_dot_split(h, r_ref[...]))


def out_projection(x, mixes, w_out_b, mod, norm_g, l, router_p=None):
    tm = 512
    n_ctx_tiles = N_CTX // tm
    n_lat_tiles = N_LAT // tm
    routed = router_p is not None
    mod_spec = lambda k: pl.BlockSpec((None, None, 1, D_MODEL), lambda i: (l, _cond_row(i, tm), 0, k))
    in_specs = [pl.BlockSpec((tm, D_MODEL), lambda i: (i, 0))]
    args = [x]
    for ctx_arr, lat_arr in mixes:
        w = ctx_arr.shape[1]
        in_specs.append(pl.BlockSpec((tm, w), lambda i: (jnp.minimum(i, n_ctx_tiles - 1), 0)))
        in_specs.append(pl.BlockSpec((tm, w), lambda i: (jnp.clip(i - n_ctx_tiles, 0, n_lat_tiles - 1), 0)))
        args += [ctx_arr, lat_arr]
    in_specs += [pl.BlockSpec((None, D_MODEL, D_MODEL), lambda i: (l, 0, 0)),
                 mod_spec(2),
                 pl.BlockSpec((None, 1, D_MODEL), lambda i: (l, 0, 0)),
                 mod_spec(3), mod_spec(4)]
    args += [w_out_b, mod, norm_g, mod, mod]
    out_specs = [pl.BlockSpec((tm, D_MODEL), lambda i: (i, 0)),
                 pl.BlockSpec((tm, D_MODEL), lambda i: (i, 0))]
    out_shape = [jax.ShapeDtypeStruct((N_TOK, D_MODEL), F32),
                 jax.ShapeDtypeStruct((N_TOK, D_MODEL), F32 if routed else BF16)]
    if routed:
        in_specs.append(pl.BlockSpec((D_MODEL, LANES), lambda i: (0, 0)))
        args.append(router_p)
        out_specs.append(pl.BlockSpec((tm, LANES), lambda i: (i, 0)))
        out_shape.append(jax.ShapeDtypeStruct((N_TOK, LANES), F32))
    return pl.pallas_call(
        functools.partial(_outproj_kernel, routed=routed, n_mix=len(mixes)),
        grid=(N_TOK // tm,),
        in_specs=in_specs, out_specs=out_specs, out_shape=out_shape,
        compiler_params=_cparams(("arbitrary",)),
        name="out_projection",
    )(*args)


def _top2_gate(logits):
    lane = _lane(logits.shape)
    neg = jnp.float32(-jnp.inf)
    lg = jnp.where(lane < N_EXP, logits, neg)
    t1 = jnp.max(lg, axis=-1, keepdims=True)
    i1 = jnp.min(jnp.where(lg == t1, lane, LANES), axis=-1, keepdims=True)
    lg2 = jnp.where(lane == i1, neg, lg)
    t2 = jnp.max(lg2, axis=-1, keepdims=True)
    i2 = jnp.min(jnp.where(lg2 == t2, lane, LANES), axis=-1, keepdims=True)
    e2 = jnp.exp(t2 - t1)
    w1 = 1.0 / (1.0 + e2)
    w2 = e2 / (1.0 + e2)
    return jnp.where(lane == i1, w1, 0.0) + jnp.where(lane == i2, w2, 0.0)


def _ffn_kernel(*refs, routed):
    if routed:
        h_ref, wg_ref, wu_ref, wd_ref, x_ref, gf_ref, lg_ref, o_ref, acc_ref = refs
    else:
        h_ref, wg_ref, wu_ref, wd_ref, x_ref, gf_ref, o_ref, acc_ref = refs
    j = pl.program_id(1)

    @pl.when(j == 0)
    def _():
        acc_ref[...] = jnp.zeros_like(acc_ref)

    h = h_ref[...]
    act = _silu(_dot(h, wg_ref[...])) * _dot(h, wu_ref[...])
    if routed:
        gate = _top2_gate(lg_ref[...])
        act = act * jnp.sum(jnp.where(_lane(gate.shape) == j, gate, 0.0), axis=-1, keepdims=True)
    acc_ref[...] += _dot(act.astype(BF16), wd_ref[...])

    @pl.when(j == pl.num_programs(1) - 1)
    def _():
        o_ref[...] = x_ref[...] + gf_ref[...] * acc_ref[...]


def ffn(h, w_gu, w_d, x1, mod, l, logits=None):
    tm = 512
    lyr = l // 2
    routed = logits is not None
    if routed:
        n_chunk = N_EXP
        gu_spec = lambda half: pl.BlockSpec((None, None, D_MODEL, FF_CHUNK), lambda i, j: (lyr, j, 0, half))
        d_spec = pl.BlockSpec((None, None, FF_CHUNK, D_MODEL), lambda i, j: (lyr, j, 0, 0))
    else:
        n_chunk = D_FF // FF_CHUNK
        gu_spec = lambda half: pl.BlockSpec((None, D_MODEL, FF_CHUNK), lambda i, j: (lyr, 0, half * n_chunk + j))
        d_spec = pl.BlockSpec((None, FF_CHUNK, D_MODEL), lambda i, j: (lyr, j, 0))
    in_specs = [pl.BlockSpec((tm, D_MODEL), lambda i, j: (i, 0)),
                gu_spec(0), gu_spec(1), d_spec,
                pl.BlockSpec((tm, D_MODEL), lambda i, j: (i, 0)),
                pl.BlockSpec((None, None, 1, D_MODEL), lambda i, j: (l, _cond_row(i, tm), 0, 5))]
    args = [h, w_gu, w_gu, w_d, x1, mod]
    if routed:
        in_specs.append(pl.BlockSpec((tm, LANES), lambda i, j: (i, 0)))
        args.append(logits)
    return pl.pallas_call(
        functools.partial(_ffn_kernel, routed=routed),
        grid=(N_TOK // tm, n_chunk),
        in_specs=in_specs,
        out_specs=pl.BlockSpec((tm, D_MODEL), lambda i, j: (i, 0)),
        out_shape=jax.ShapeDtypeStruct((N_TOK, D_MODEL), F32),
        scratch_shapes=[pltpu.VMEM((tm, D_MODEL), F32)],
        compiler_params=_cparams(("arbitrary", "arbitrary")),
        name="ffn_routed" if routed else "ffn_dense",
    )(*args)


MOE_TILE = 256
MOE_SLOTS = TOP_K * N_TOK + N_EXP * MOE_TILE
MOE_TILES = MOE_SLOTS // MOE_TILE
ROUTE_I1, ROUTE_I2, ROUTE_W1, ROUTE_W2 = 0, 1, 2, 3


def _top2_route(logits):
    lane = _lane(logits.shape)
    neg = jnp.float32(-jnp.inf)
    lg = jnp.where(lane < N_EXP, logits, neg)
    t1 = jnp.max(lg, axis=-1, keepdims=True)
    i1 = jnp.min(jnp.where(lg == t1, lane, LANES), axis=-1, keepdims=True)
    lg2 = jnp.where(lane == i1, neg, lg)
    t2 = jnp.max(lg2, axis=-1, keepdims=True)
    i2 = jnp.min(jnp.where(lg2 == t2, lane, LANES), axis=-1, keepdims=True)
    e2 = jnp.exp(t2 - t1)
    w1 = 1.0 / (1.0 + e2)
    w2 = e2 / (1.0 + e2)
    rec = jnp.where(lane == ROUTE_I1, i1.astype(F32), 0.0) + jnp.where(lane == ROUTE_I2, i2.astype(F32), 0.0)
    return rec + jnp.where(lane == ROUTE_W1, w1, 0.0) + jnp.where(lane == ROUTE_W2, w2, 0.0)


def _routing_tables(route):
    n_assign = TOP_K * N_TOK
    expert = jnp.concatenate([route[:, ROUTE_I1], route[:, ROUTE_I2]]).astype(jnp.int32)
    index = jnp.arange(n_assign, dtype=jnp.int32)
    onehot = (expert[:, None] == jnp.arange(N_EXP, dtype=jnp.int32)[None, :]).astype(jnp.int32)
    running = jnp.cumsum(onehot, axis=0)
    counts = running[-1]
    rank = jnp.sum(running * onehot, axis=1) - 1
    padded = (counts + MOE_TILE - 1) // MOE_TILE * MOE_TILE
    ends = jnp.cumsum(padded)
    starts = ends - padded
    slot = jnp.sum(onehot * starts[None, :], axis=1) + rank
    sorted_token = (jnp.sort(expert * n_assign + index) % n_assign) % N_TOK
    tile_start = jnp.arange(MOE_TILES, dtype=jnp.int32) * MOE_TILE
    tile_expert = jnp.minimum(jnp.sum(tile_start[:, None] >= ends[None, :], axis=1), N_EXP - 1).astype(jnp.int32)
    first_sorted = (jnp.cumsum(counts) - counts)[tile_expert] + tile_start - starts[tile_expert]
    n_used = (ends[-1] // MOE_TILE).astype(jnp.int32).reshape(1)
    return sorted_token, slot, tile_expert, first_sorted.astype(jnp.int32), n_used


def _row_copy(src_hbm, row, dst, dst_row, sem):
    return pltpu.make_async_copy(src_hbm.at[pl.ds(row, 1)], dst.at[pl.ds(dst_row, 1)], sem)


def _experts_kernel(tok_ref, exp_ref, first_ref, used_ref, h_hbm, wg_ref, wu_ref, wd_ref, y_ref, x_buf, sems,
                    wg_s, wu_s, wd_s):
    t = pl.program_id(0)
    n_used = used_ref[0]

    def start_gather(tile, half):
        first = first_ref[tile]
        for r in range(MOE_TILE):
            tok = tok_ref[jnp.minimum(first + r, TOP_K * N_TOK - 1)]
            _row_copy(h_hbm, tok, x_buf.at[half], r, sems.at[half]).start(priority=r % 2)

    @pl.when(t == 0)
    def _():
        start_gather(0, 0)

    @pl.when(t + 1 < n_used)
    def _():
        start_gather(t + 1, (t + 1) % 2)

    @pl.when(t < n_used)
    def _():
        half = t % 2
        for r in range(MOE_TILE):
            _row_copy(h_hbm, 0, x_buf.at[half], r, sems.at[half]).wait()
        @pl.when((t == 0) | (exp_ref[t] != exp_ref[jnp.maximum(t - 1, 0)]))
        def _():
            wg_s[...] = wg_ref[...].astype(BF16)
            wu_s[...] = wu_ref[...].astype(BF16)
            wd_s[...] = wd_ref[...].astype(BF16)

        x = x_buf[half].astype(BF16)
        act = _silu(_dot(x, wg_s[...])) * _dot(x, wu_s[...])
        y_ref[...] = _dot(act.astype(BF16), wd_s[...])

    @pl.when(t >= n_used)
    def _():
        y_ref[...] = jnp.zeros_like(y_ref)


def moe_experts(h, route, w_gu, w_d, l):
    lyr = l // 2
    sorted_token, slot, tile_expert, first_sorted, n_used = _routing_tables(route)
    gu_spec = lambda half: pl.BlockSpec((None, None, D_MODEL, FF_CHUNK),
                                        lambda t, tok, exp, first, used: (lyr, exp[t], 0, half))
    y = pl.pallas_call(
        _experts_kernel,
        grid_spec=pltpu.PrefetchScalarGridSpec(
            num_scalar_prefetch=4,
            grid=(MOE_TILES,),
            in_specs=[pl.BlockSpec(memory_space=pl.ANY),
                      gu_spec(0), gu_spec(1),
                      pl.BlockSpec((None, None, FF_CHUNK, D_MODEL),
                                   lambda t, tok, exp, first, used: (lyr, exp[t], 0, 0))],
            out_specs=pl.BlockSpec((MOE_TILE, D_MODEL), lambda t, tok, exp, first, used: (t, 0)),
            scratch_shapes=[pltpu.VMEM((2, MOE_TILE, D_MODEL), F32), pltpu.SemaphoreType.DMA((2,)),
                            pltpu.VMEM((D_MODEL, FF_CHUNK), BF16), pltpu.VMEM((D_MODEL, FF_CHUNK), BF16),
                            pltpu.VMEM((FF_CHUNK, D_MODEL), BF16)]),
        out_shape=jax.ShapeDtypeStruct((MOE_SLOTS, D_MODEL), F32),
        compiler_params=_cparams(("arbitrary",)),
        name="moe_experts",
    )(sorted_token, tile_expert, first_sorted, n_used, h, w_gu, w_gu, w_d)
    return y, slot


def _moe_combine_kernel(slot_ref, y_hbm, x_ref, gf_ref, route_ref, o_ref, y_buf, sems):
    i = pl.program_id(0)
    tm = x_ref.shape[0]

    def start_gather(tile, half):
        for k in range(TOP_K):
            for r in range(tm):
                _row_copy(y_hbm, slot_ref[k * N_TOK + tile * tm + r], y_buf.at[half, k], r,
                          sems.at[half]).start(priority=r % 2)

    @pl.when(i == 0)
    def _():
        start_gather(0, 0)

    @pl.when(i + 1 < pl.num_programs(0))
    def _():
        start_gather(i + 1, (i + 1) % 2)

    half = i % 2
    for k in range(TOP_K):
        for r in range(tm):
            _row_copy(y_hbm, 0, y_buf.at[half, k], r, sems.at[half]).wait()
    route = route_ref[...]
    f = (route[:, ROUTE_W1:ROUTE_W1 + 1] * y_buf[half, 0] + route[:, ROUTE_W2:ROUTE_W2 + 1] * y_buf[half, 1])
    o_ref[...] = x_ref[...] + gf_ref[...] * f


def moe_combine(y, slot, x1, mod, route, l):
    tm = 256
    return pl.pallas_call(
        _moe_combine_kernel,
        grid_spec=pltpu.PrefetchScalarGridSpec(
            num_scalar_prefetch=1,
            grid=(N_TOK // tm,),
            in_specs=[pl.BlockSpec(memory_space=pl.ANY),
                      pl.BlockSpec((tm, D_MODEL), lambda i, s: (i, 0)),
                      pl.BlockSpec((None, None, 1, D_MODEL), lambda i, s: (l, _cond_row(i, tm), 0, 5)),
                      pl.BlockSpec((tm, LANES), lambda i, s: (i, 0))],
            out_specs=pl.BlockSpec((tm, D_MODEL), lambda i, s: (i, 0)),
            scratch_shapes=[pltpu.VMEM((2, TOP_K, tm, D_MODEL), F32), pltpu.SemaphoreType.DMA((2,))]),
        out_shape=jax.ShapeDtypeStruct((N_TOK, D_MODEL), F32),
        compiler_params=_cparams(("arbitrary",)),
        name="moe_combine",
    )(slot, y, x1, mod, route)


def _final_norm_kernel(x_ref, g_ref, o_ref):
    o_ref[...] = _rms(x_ref[...], g_ref[...])


def final_norm_call(x, g, tok0, n_tok):
    tm = 1024
    i0 = tok0 // tm
    return pl.pallas_call(
        _final_norm_kernel,
        grid=(n_tok // tm,),
        in_specs=[pl.BlockSpec((tm, D_MODEL), lambda i: (i0 + i, 0)),
                  pl.BlockSpec((1, D_MODEL), lambda i: (0, 0))],
        out_specs=pl.BlockSpec((tm, D_MODEL), lambda i: (i, 0)),
        out_shape=jax.ShapeDtypeStruct((n_tok, D_MODEL), F32),
        compiler_params=_cparams(("arbitrary",)),
        name="final_norm",
    )(x, g)


def _cache_store_kernel(*refs):
    n = len(refs) // 3
    for src, dst in zip(refs[:n], refs[2 * n:]):
        dst[...] = src[...].reshape(dst.shape)


def cache_store(sources, all_layers, l):
    n_sq = 2
    tm = n_sq * SEQ
    return pl.pallas_call(
        _cache_store_kernel,
        grid=(N_CTX // tm,),
        in_specs=([pl.BlockSpec((tm, s.shape[1]), lambda i: (i, 0)) for s in sources]
                  + [pl.BlockSpec(memory_space=pl.ANY)] * len(all_layers)),
        out_specs=[pl.BlockSpec((n_sq, None, SEQ, a.shape[-1]), lambda i: (i, l, 0, 0)) for a in all_layers],
        out_shape=[jax.ShapeDtypeStruct(a.shape, F32) for a in all_layers],
        input_output_aliases={len(sources) + k: k for k in range(len(all_layers))},
        compiler_params=_cparams(("arbitrary",)),
        name="cache_store",
    )(*sources, *all_layers)


def _gate_row(p):
    return jnp.zeros((1, LANES), F32).at[0, GATE_A:GATE_A + 2 * H_A].set(p.reshape(-1))


def _tile_lanes(g, width):
    return jnp.tile(g, width // g.shape[0])[None, :]


def kernel(x_prompt, x_sample, state_delta, cache_gqa_k, cache_gqa_v, cache_diff_k, cache_diff_v, c, c_ctx,
           ada_w, ada_b, norm_mix, norm_ffn, w_in, conv_w, dn_a_log, dn_dt_bias, dn_norm, gqa_q_norm,
           gqa_k_norm, diff_lambda, diff_norm, w_out, ffn_w_gu, ffn_w_down, moe_router, moe_w_gu, moe_w_down,
           final_norm):
    cond8 = jnp.concatenate([c_ctx[None, :], c, jnp.zeros((8 - N_COND, D_MODEL), F32)], axis=0)
    gate_lo = 4 * A_W
    gate_hi = gate_lo + 4 * H_A
    w_in_p = jnp.concatenate(
        [w_in[:, :, :gate_lo], w_in[:, :, gate_hi:], w_in[:, :, gate_lo:gate_hi],
         jnp.zeros((DEPTH, D_MODEL, LANES - 4 * H_A), F32)], axis=-1).astype(BF16)
    w_out_b = w_out.astype(BF16)
    norm_mix3 = norm_mix.reshape(DEPTH, 1, D_MODEL)
    norm_ffn3 = norm_ffn.reshape(DEPTH, 1, D_MODEL)
    ffn_gu_b, ffn_d_b = ffn_w_gu.astype(BF16), ffn_w_down.astype(BF16)
    router_p = jnp.pad(moe_router, ((0, 0), (0, 0), (0, LANES - N_EXP)))
    rope_b = _rope_tables(DEC_SEQ // GRID_W, HEAD_DIM)
    rope_c = _rope_tables(DEC_SEQ // GRID_W, DIFF_HD)
    s0_ctx = jnp.zeros((BATCH, 2, H_A, HEAD_DIM, HEAD_DIM), F32)
    cache_b = (cache_gqa_k.reshape(DEC_BATCH, DEPTH, PAST_LEN, KV_B * HEAD_DIM),
               cache_gqa_v.reshape(DEC_BATCH, DEPTH, PAST_LEN, KV_B * HEAD_DIM))
    cache_c = (cache_diff_k.reshape(DEC_BATCH, DEPTH, PAST_LEN, C_W),
               cache_diff_v.reshape(DEC_BATCH, DEPTH, PAST_LEN, C_W))

    mod = ada_modulation(cond8, ada_w, ada_b)[:, :N_COND].reshape(DEPTH, N_COND, 1, 6 * D_MODEL)

    x = jnp.concatenate([x_prompt.reshape(N_CTX, D_MODEL), x_sample.reshape(N_LAT, D_MODEL)], axis=0)
    all_states = jnp.zeros((BATCH, DEPTH, 2, H_A, HEAD_DIM, HEAD_DIM), F32)
    ctx_layers = tuple(jnp.zeros((BATCH, DEPTH, SEQ, w), F32)
                       for w in (KV_B * HEAD_DIM, KV_B * HEAD_DIM, C_W, C_W))
    ctx = dict(n_seq=BATCH, seq_len=SEQ, tok0=0)
    lat = dict(n_seq=DEC_BATCH, seq_len=DEC_SEQ, tok0=N_CTX)
    for l in range(DEPTH):
        qkv_a, z_a, q_b, k_b, v_b, q_c, k_c, v_c, gate = in_projection(x, mod, norm_mix3, w_in_p, l)

        dn_args = (qkv_a, z_a, gate, conv_w[l], _gate_row(dn_a_log[l]), _gate_row(dn_dt_bias[l]),
                   _tile_lanes(dn_norm[l], LANES))
        oa_ctx, all_states = delta_mixer(*dn_args, s0_ctx, n_sub=2, all_states=all_states, layer=l, **ctx)
        oa_lat, _ = delta_mixer(*dn_args, state_delta[:, l], n_sub=1, **lat)

        gq, gk = _tile_lanes(gqa_q_norm[l], LANES), _tile_lanes(gqa_k_norm[l], LANES)
        ob_ctx, kn_ctx = gqa_mixer(q_b, k_b, v_b, gq, gk, **ctx)
        (ob_lat,) = gqa_mixer(q_b, k_b, v_b, gq, gk, cache=cache_b, rope=rope_b, layer=l, **lat)

        lam_init = 0.8 - 0.6 * math.exp(-0.3 * l)
        gn = _tile_lanes(diff_norm[l], C_W)
        oc_ctx = diff_mixer(q_c, k_c, v_c, diff_lambda[l], gn, lam_init, **ctx)
        oc_lat = diff_mixer(q_c, k_c, v_c, diff_lambda[l], gn, lam_init, cache=cache_c, rope=rope_c, layer=l, **lat)

        mixes = ((oa_ctx, oa_lat), (ob_ctx, ob_lat), (oc_ctx, oc_lat))
        if l % 2 == 0:
            x1, h2 = out_projection(x, mixes, w_out_b, mod, norm_ffn3, l)
            x = ffn(h2, ffn_gu_b, ffn_d_b, x1, mod, l)
        else:
            x1, h2, route = out_projection(x, mixes, w_out_b, mod, norm_ffn3, l, router_p[l // 2])
            y_slots, slot = moe_experts(h2, route, moe_w_gu, moe_w_down, l)
            x = moe_combine(y_slots, slot, x1, mod, route, l)

        ctx_layers = cache_store((kn_ctx, v_b, k_c, v_c), ctx_layers, l)

    y_prompt = final_norm_call(x, final_norm[None, :], 0, N_CTX).reshape(BATCH, SEQ, D_MODEL)
    y_sample = final_norm_call(x, final_norm[None, :], N_CTX, N_LAT).reshape(DEC_BATCH, DEC_SEQ, D_MODEL)
    new_gk, new_gv, new_dk, new_dv = ctx_layers
    return (y_prompt, y_sample, all_states,
            new_gk.reshape(BATCH, DEPTH, SEQ, KV_B, HEAD_DIM), new_gv.reshape(BATCH, DEPTH, SEQ, KV_B, HEAD_DIM),
            new_dk.reshape(BATCH, DEPTH, SEQ, H_C, 2, DIFF_HD), new_dv.reshape(BATCH, DEPTH, SEQ, H_C, 2 * DIFF_HD))
```

```python
import functools
import math

import jax
import jax.numpy as jnp
from jax import lax
from jax.experimental import pallas as pl
from jax.experimental.pallas import tpu as pltpu

D_MODEL = 1024
BATCH = 16
SEQ = 256
DEPTH = 4
DEC_BATCH = 2
DEC_SEQ = 1024
PAST_LEN = 512
GRID_W = 64
HEAD_DIM = 64
H_A = 6
A_W = H_A * HEAD_DIM
H_B = 6
KV_B = 2
B_W = H_B * HEAD_DIM
H_C = 4
DIFF_HD = 32
C_W = H_C * 2 * DIFF_HD
CHUNK = 64
ROPE_THETA = 10000.0
D_FF = 2816
N_EXP = 8
TOP_K = 2
D_FF_E = 1408
EPS = 1e-6

N_CTX = BATCH * SEQ
N_LAT = DEC_BATCH * DEC_SEQ
N_TOK = N_CTX + N_LAT
N_COND = 1 + DEC_BATCH
IN_PAD = 3072
FF_CHUNK = D_FF_E
LANES = 128
VMEM_LIMIT = 56 * 1024 * 1024

IN_SPLIT = (("qkv_a", 3 * A_W), ("z_a", A_W), ("q_b", B_W), ("k_b", KV_B * HEAD_DIM), ("v_b", KV_B * HEAD_DIM),
            ("q_c", C_W), ("k_c", C_W), ("v_c", C_W), ("gate", LANES))
GATE_BETA, GATE_A = 0, 2 * H_A

F32 = jnp.float32
BF16 = jnp.bfloat16


def _cparams(sem):
    return pltpu.CompilerParams(dimension_semantics=sem, vmem_limit_bytes=VMEM_LIMIT)


def _dot(a, b):
    return jnp.dot(a, b, preferred_element_type=F32)


def _dot_split(a, b):
    a_hi, b_hi = a.astype(BF16), b.astype(BF16)
    a_lo = (a - a_hi.astype(F32)).astype(BF16)
    b_lo = (b - b_hi.astype(F32)).astype(BF16)
    return _dot(jnp.concatenate([a_hi, a_lo, a_hi], axis=1), jnp.concatenate([b_hi, b_hi, b_lo], axis=0))


def _dot_nt(a, b):
    return lax.dot_general(a, b, (((1,), (1,)), ((), ())), preferred_element_type=F32)


def _silu(x):
    return x * jax.nn.sigmoid(x)


def _softplus(x):
    return jnp.maximum(x, 0.0) + jnp.log1p(jnp.exp(-jnp.abs(x)))


def _rms(x, g):
    return x * lax.rsqrt(jnp.mean(x * x, axis=-1, keepdims=True) + EPS) * g


def _lane(shape):
    return lax.broadcasted_iota(jnp.int32, shape, len(shape) - 1)


def _group_mean_matrix(width, group):
    i = lax.broadcasted_iota(jnp.int32, (width, width), 0)
    j = lax.broadcasted_iota(jnp.int32, (width, width), 1)
    return jnp.where(i // group == j // group, 1.0 / group, 0.0).astype(BF16)


def _group_rms(x, gmat, g):
    sq = x * x
    hi = sq.astype(BF16)
    lo = (sq - hi.astype(F32)).astype(BF16)
    ms = _dot(jnp.concatenate([hi, lo], axis=1), jnp.concatenate([gmat, gmat], axis=0))
    return x * lax.rsqrt(ms + EPS) * g


def _cond_row(i, tm):
    return jnp.maximum((i * tm) // DEC_SEQ - (N_CTX // DEC_SEQ - 1), 0)


def _ada_kernel(c_ref, w_ref, b_ref, o_ref):
    o_ref[...] = _dot_split(_silu(c_ref[...]), w_ref[...]) + b_ref[...]


def ada_modulation(cond8, ada_w, ada_b):
    tn = 1024
    return pl.pallas_call(
        _ada_kernel,
        grid=(DEPTH, 6 * D_MODEL // tn),
        in_specs=[pl.BlockSpec((8, D_MODEL), lambda l, j: (0, 0)),
                  pl.BlockSpec((None, D_MODEL, tn), lambda l, j: (l, 0, j)),
                  pl.BlockSpec((None, 1, tn), lambda l, j: (l, 0, j))],
        out_specs=pl.BlockSpec((None, 8, tn), lambda l, j: (l, 0, j)),
        out_shape=jax.ShapeDtypeStruct((DEPTH, 8, 6 * D_MODEL), F32),
        compiler_params=_cparams(("arbitrary", "arbitrary")),
        name="ada_modulation",
    )(cond8, ada_w, ada_b.reshape(DEPTH, 1, 6 * D_MODEL))


def _inproj_kernel(x_ref, g_ref, sh_ref, sc_ref, w_ref, *o_refs):
    h = _rms(x_ref[...], g_ref[...]) * (1.0 + sc_ref[...]) + sh_ref[...]
    acc = _dot(h.astype(BF16), w_ref[...])
    off = 0
    for o_ref, (_, width) in zip(o_refs, IN_SPLIT):
        o_ref[...] = acc[:, off:off + width]
        off += width


def in_projection(x, mod, norm_g, w_in_p, l):
    tm = 512
    return pl.pallas_call(
        _inproj_kernel,
        grid=(N_TOK // tm,),
        in_specs=[pl.BlockSpec((tm, D_MODEL), lambda i: (i, 0)),
                  pl.BlockSpec((None, 1, D_MODEL), lambda i: (l, 0, 0)),
                  pl.BlockSpec((None, None, 1, D_MODEL), lambda i: (l, _cond_row(i, tm), 0, 0)),
                  pl.BlockSpec((None, None, 1, D_MODEL), lambda i: (l, _cond_row(i, tm), 0, 1)),
                  pl.BlockSpec((None, D_MODEL, IN_PAD), lambda i: (l, 0, 0))],
        out_specs=[pl.BlockSpec((tm, w), lambda i: (i, 0)) for _, w in IN_SPLIT],
        out_shape=[jax.ShapeDtypeStruct((N_TOK, w), F32) for _, w in IN_SPLIT],
        compiler_params=_cparams(("arbitrary",)),
        name="in_projection",
    )(x, norm_g, mod, mod, w_in_p)


SOLVE_BLOCK = 16


def _delta_kernel(*refs, seq_len, n_sub, aliased):
    qkv_ref, z_ref, gate_ref, cw_ref, alog_ref, dtb_ref, ng_ref, s0_ref = refs[:8]
    (o_ref, sfin_ref, pad_s, kk_s, qq_s, vk_s, gc_s, beta_s, u_s, w_s, qd_s, a_s, kdt_s, egt_s, sa_s, o_s,
     tp_s, t32_s, rp_s, c32_s, m_s, x_s, vbd_s, v_s, op_s) = refs[9 if aliased else 8:]
    n_chunk = seq_len // CHUNK
    n_pair = H_A // 2
    n_unit = n_sub * H_A
    half = HEAD_DIM
    lo = _lane((CHUNK, LANES)) < half
    lo_row = _lane((1, LANES)) < half
    lo16 = _lane((CHUNK, LANES)).astype(F32).astype(BF16) < half
    row = lax.broadcasted_iota(jnp.int32, (CHUNK, LANES), 0)
    col = _lane((CHUNK, LANES)) % half
    ahead = jnp.where(lo, row - col, col - row)
    incl = ahead >= 0
    strict = ahead > 0
    diag_blk = row // SOLVE_BLOCK == col // SOLVE_BLOCK
    big_row = lax.broadcasted_iota(jnp.int32, (LANES, LANES), 0)
    anti = big_row // half + _lane((LANES, LANES)) // half == 1
    eye = (big_row == _lane((LANES, LANES))).astype(F32)

    def both(a):
        return jnp.concatenate([jnp.where(lo, a, 0.0), jnp.where(lo, 0.0, a)], axis=0)

    rb = 128
    lo_rb = _lane((rb, LANES)) < half
    zero8 = jnp.zeros((8, 3 * A_W), F32)
    pad_s[0:8, :] = zero8
    pad_s[8 + seq_len:16 + seq_len, :] = zero8

    def l2n(x):
        return x * lax.rsqrt(0.5 * jnp.sum(x * x, axis=-1, keepdims=True) + EPS)

    for sq in range(n_sub):
        pad_s[8:8 + seq_len, :] = qkv_ref[sq * seq_len:(sq + 1) * seq_len, :]
        for r0 in range(0, seq_len, rb):
            y = _silu(pad_s[7 + r0:7 + r0 + rb, :] * cw_ref[0:1, :] + pad_s[8 + r0:8 + r0 + rb, :] * cw_ref[1:2, :]
                      + pad_s[9 + r0:9 + r0 + rb, :] * cw_ref[2:3, :])

            def dup(part, p):
                c = y[:, part * A_W + p * LANES:part * A_W + (p + 1) * LANES]
                r = pltpu.roll(c, half, axis=1)
                return jnp.where(lo_rb, c, r), jnp.where(lo_rb, r, c)

            for p in range(n_pair):
                qs, ks, vs = dup(0, p), dup(1, p), dup(2, p)
                for e in range(2):
                    u = sq * H_A + 2 * p + e
                    kk = l2n(ks[e])
                    kk_s[u, r0:r0 + rb, :] = kk
                    qq_s[u, r0:r0 + rb, :] = l2n(qs[e]) * (HEAD_DIM ** -0.5)
                    vk_s[u, r0:r0 + rb, :] = jnp.where(lo_rb, vs[e], kk)

    cb = 256
    ci = lax.broadcasted_iota(jnp.int32, (cb, cb), 0)
    cj = lax.broadcasted_iota(jnp.int32, (cb, cb), 1)
    same = ci // CHUNK == cj // CHUNK
    m_pre = jnp.where(same & (ci >= cj), 1.0, 0.0).astype(BF16)
    m_suf = jnp.where(same & (ci <= cj), 1.0, 0.0).astype(BF16)
    fwd_cols = (_lane((cb, LANES)) >= GATE_A) & (_lane((cb, LANES)) < GATE_A + H_A)
    for r0 in range(0, n_sub * seq_len, cb):
        gt = gate_ref[r0:r0 + cb, :]
        beta_s[r0:r0 + cb, :] = jax.nn.sigmoid(gt)
        g = -jnp.exp(alog_ref[...]) * _softplus(gt + dtb_ref[...])
        g1 = g.astype(BF16)
        r1 = g - g1.astype(F32)
        g2 = r1.astype(BF16)
        g3 = (r1 - g2.astype(F32)).astype(BF16)
        blk = jnp.concatenate([g1, g2, g3], axis=1)
        pre, suf = _dot(m_pre, blk), _dot(m_suf, blk)
        pre = pre[:, :LANES] + pre[:, LANES:2 * LANES] + pre[:, 2 * LANES:]
        suf = suf[:, :LANES] + suf[:, LANES:2 * LANES] + suf[:, 2 * LANES:]
        gc_s[r0:r0 + cb, :] = jnp.where(fwd_cols, pre, suf)

    zero = jnp.zeros((half, half), F32)
    for sq in range(n_sub):
        for h in range(H_A):
            sa_s[sq * H_A + h] = jnp.concatenate([jnp.concatenate([zero, s0_ref[sq, 1, h]], axis=1),
                                                  jnp.concatenate([s0_ref[sq, 0, h], zero], axis=1)], axis=0)

    n_fac = int(math.log2(SOLVE_BLOCK))

    def prep(c, carry):
        rows = pl.ds(pl.multiple_of(c * CHUNK, CHUNK), CHUNK)
        for sq in range(n_sub):
            grow = pl.ds(pl.multiple_of(sq * seq_len + c * CHUNK, CHUNK), CHUNK)
            gcb = gc_s[grow, :]
            gct = gcb.T
            bb = beta_s[grow, :]
            for h in range(H_A):
                u = sq * H_A + h
                full = (CHUNK, LANES)
                cff = jnp.broadcast_to(gcb[:, GATE_A + h:GATE_A + h + 1], full)
                cbb = jnp.broadcast_to(gcb[:, GATE_A + H_A + h:GATE_A + H_A + h + 1], full)
                bff = jnp.broadcast_to(bb[:, GATE_BETA + h:GATE_BETA + h + 1], full)
                bbb = jnp.broadcast_to(bb[:, GATE_BETA + H_A + h:GATE_BETA + H_A + h + 1], full)
                gr_fb = jnp.concatenate([gct[GATE_A + h:GATE_A + h + 1, :],
                                         gct[GATE_A + H_A + h:GATE_A + H_A + h + 1, :]], axis=1)
                decay = jnp.where(incl, jnp.exp(jnp.where(incl, jnp.where(lo, cff, cbb) - gr_fb, 0.0)), 0.0)
                egf, egb = jnp.exp(cff), jnp.exp(cbb)
                tot_f = gcb[CHUNK - 1:CHUNK, GATE_A + h:GATE_A + h + 1]
                tot_b = gcb[0:1, GATE_A + H_A + h:GATE_A + H_A + h + 1]
                kk, qq, vk = kk_s[u, rows, :], qq_s[u, rows, :], vk_s[u, rows, :]
                k16 = kk[:, :half].astype(BF16)
                kq = jnp.concatenate([k16, qq[:, :half].astype(BF16)], axis=0)
                gram = _dot_nt(kq, jnp.concatenate([k16, k16], axis=0))
                neg_l = jnp.where(strict, gram[:CHUNK] * jnp.where(lo, -bff, -bbb) * decay, 0.0)
                a_s[u, rows, :] = jnp.where(incl, gram[CHUNK:] * decay, 0.0).astype(BF16)
                neg_d = jnp.where(diag_blk, neg_l, 0.0)
                pd = both(neg_d)
                t32_s[u] = eye + pd
                tp_s[u, :, :LANES] = (eye + pd).astype(BF16)
                tp_s[u, :, LANES:] = pd.astype(BF16)
                x_f = vk * jnp.where(lo, bff, bff * egf)
                x_b = vk * jnp.where(lo, bbb, bbb * egb)
                rp_s[u, :, :LANES] = jnp.concatenate([x_f, x_b], axis=0).astype(BF16)
                rp_s[u, :, LANES:] = both(neg_l - neg_d).astype(BF16)
                qd_s[u, rows, :] = (qq * jnp.where(lo, egb, egf)).astype(BF16)
                kd = kk * jnp.exp(jnp.where(lo, tot_b - cbb, tot_f - cff))
                kdt_s[u, c] = kd.T.astype(BF16)
                egt_s[u, c] = jnp.exp(jnp.where(lo_row, tot_f, tot_b))
        for u in range(n_unit):
            tp_s[u, :, LANES:] = _dot(tp_s[u, :, LANES:], tp_s[u, :, LANES:]).astype(BF16)
        for j in range(1, n_fac):
            for u in range(n_unit):
                if j + 1 < n_fac:
                    tp = _dot(tp_s[u, :, LANES:], tp_s[u])
                    t = t32_s[u] + tp[:, :LANES]
                    t32_s[u] = t
                    tp_s[u, :, :LANES] = t.astype(BF16)
                    tp_s[u, :, LANES:] = tp[:, LANES:].astype(BF16)
                else:
                    t = t32_s[u] + _dot(tp_s[u, :, LANES:], tp_s[u, :, :LANES])
                    tp_s[u, :, :LANES] = t.astype(BF16)
        for u in range(n_unit):
            cm = _dot(tp_s[u, :, :LANES], rp_s[u])
            c32_s[u] = cm[:, :LANES]
            rp_s[u] = cm.astype(BF16)
        assert CHUNK // SOLVE_BLOCK == 4
        for u in range(n_unit):
            mm = _dot(rp_s[u, :, LANES:], rp_s[u])
            y1 = c32_s[u] + mm[:, :LANES]
            c32_s[u] = y1
            x_s[u] = y1.astype(BF16)
            m_s[u] = mm[:, LANES:].astype(BF16)
        for u in range(n_unit):
            x = c32_s[u] + _dot(m_s[u], x_s[u])
            x_f, x_b = x[:CHUNK], pltpu.roll(x[CHUNK:], half, axis=1)
            u_s[u, rows, :] = jnp.where(lo, x_f, x_b)
            w_s[u, rows, :] = jnp.where(lo, x_b, x_f).astype(BF16)
        return carry

    lax.fori_loop(0, n_chunk, prep, 0)

    def scan(s, carry):
        rf = pl.ds(pl.multiple_of(s * CHUNK, CHUNK), CHUNK)
        sb = n_chunk - 1 - s
        rbk = pl.ds(pl.multiple_of(sb * CHUNK, CHUNK), CHUNK)
        for u in range(n_unit):
            w = jnp.where(lo16, w_s[u, rbk, :], w_s[u, rf, :])
            qd = jnp.where(lo16, qd_s[u, rbk, :], qd_s[u, rf, :])
            wq = _dot(jnp.concatenate([w, qd], axis=0), sa_s[u].astype(BF16))
            v = jnp.where(lo, u_s[u, rf, :], u_s[u, rbk, :]) - wq[:CHUNK]
            v_s[u] = v.astype(BF16)
            vbd_s[u] = both(v).astype(BF16)
            op_s[u] = wq[CHUNK:]
        for u in range(n_unit):
            a = jnp.where(lo16, a_s[u, rf, :], a_s[u, rbk, :])
            o = op_s[u] + _dot(a, vbd_s[u])
            o_s[u, rf, 0:half] = o[:, :half]
            o_s[u, rbk, half:LANES] = o[:, half:]
            kdt = jnp.concatenate([kdt_s[u, sb][:half], kdt_s[u, s][half:]], axis=0)
            egt = jnp.where(lo_row, egt_s[u, s], egt_s[u, sb])
            sa_s[u] = sa_s[u] * egt + jnp.where(anti, _dot(kdt, v_s[u]), 0.0)
        return carry

    lax.fori_loop(0, n_chunk, scan, 0)

    for sq in range(n_sub):
        for r0 in range(0, seq_len, rb):
            tr = sq * seq_len + r0
            for p in range(n_pair):
                nrm = []
                for e in range(2):
                    ofb = o_s[sq * H_A + 2 * p + e, r0:r0 + rb, :]
                    oo = ofb + pltpu.roll(ofb, half, axis=1)
                    nrm.append(oo * lax.rsqrt(jnp.sum(oo * oo, axis=-1, keepdims=True) * (0.5 / HEAD_DIM) + EPS))
                o_ref[tr:tr + rb, p * LANES:(p + 1) * LANES] = (
                    jnp.where(lo_rb, nrm[0], nrm[1]) * ng_ref[...]
                    * _silu(z_ref[tr:tr + rb, p * LANES:(p + 1) * LANES]))
        for h in range(H_A):
            st = sa_s[sq * H_A + h]
            sfin_ref[sq, 0, h] = st[half:, :half]
            sfin_ref[sq, 1, h] = st[:half, half:]


def delta_mixer(qkv, z, gate, cw, alog_row, dtb_row, ng2, s0, *, n_seq, seq_len, tok0, n_sub,
                all_states=None, layer=0):
    n_chunk = seq_len // CHUNK
    n_unit = n_sub * H_A
    blk = n_sub * seq_len
    b0 = tok0 // blk
    tok = lambda w: pl.BlockSpec((blk, w), lambda b: (b0 + b, 0))
    full = lambda a: pl.BlockSpec(a.shape, lambda b: (0,) * a.ndim)
    state = pl.BlockSpec((n_sub, 2, H_A, HEAD_DIM, HEAD_DIM), lambda b: (b, 0, 0, 0, 0))
    per_unit = lambda dt: pltpu.VMEM((n_unit, seq_len, LANES), dt)
    stage = lambda rows, cols, dt: pltpu.VMEM((n_unit, rows, cols), dt)
    aliased = all_states is not None
    in_specs = [tok(3 * A_W), tok(A_W), tok(LANES), full(cw), full(alog_row), full(dtb_row), full(ng2), state]
    args = [qkv, z, gate, cw, alog_row, dtb_row, ng2, s0]
    if aliased:
        in_specs.append(pl.BlockSpec(memory_space=pl.ANY))
        args.append(all_states)
        state_out = pl.BlockSpec((n_sub, None, 2, H_A, HEAD_DIM, HEAD_DIM), lambda b: (b, layer, 0, 0, 0, 0))
        state_shape = jax.ShapeDtypeStruct(all_states.shape, F32)
    else:
        state_out = state
        state_shape = jax.ShapeDtypeStruct((n_seq, 2, H_A, HEAD_DIM, HEAD_DIM), F32)
    return pl.pallas_call(
        functools.partial(_delta_kernel, seq_len=seq_len, n_sub=n_sub, aliased=aliased),
        grid=(n_seq // n_sub,),
        in_specs=in_specs,
        out_specs=[pl.BlockSpec((blk, A_W), lambda b: (b, 0)), state_out],
        out_shape=[jax.ShapeDtypeStruct((n_seq * seq_len, A_W), F32), state_shape],
        input_output_aliases={len(args) - 1: 1} if aliased else {},
        scratch_shapes=[pltpu.VMEM((seq_len + 16, 3 * A_W), F32),
                        per_unit(F32), per_unit(F32), per_unit(F32),
                        pltpu.VMEM((blk, LANES), F32),
                        pltpu.VMEM((blk, LANES), F32),
                        per_unit(F32), per_unit(BF16), per_unit(BF16), per_unit(BF16),
                        pltpu.VMEM((n_unit, n_chunk, LANES, HEAD_DIM), BF16),
                        pltpu.VMEM((n_unit, n_chunk, 1, LANES), F32),
                        stage(LANES, LANES, F32),
                        per_unit(F32),
                        stage(LANES, 2 * LANES, BF16),
                        stage(LANES, LANES, F32),
                        stage(LANES, 2 * LANES, BF16),
                        stage(LANES, LANES, F32),
                        stage(LANES, LANES, BF16),
                        stage(LANES, LANES, BF16),
                        stage(LANES, LANES, BF16),
                        stage(CHUNK, LANES, BF16),
                        stage(CHUNK, LANES, F32)],
        compiler_params=_cparams(("arbitrary",)),
        name=f"delta_mixer_{seq_len}",
    )(*args)


def _rope(x, cos, sin_signed, quarter):
    width = x.shape[-1]
    swapped = jnp.where((_lane(x.shape) % (2 * quarter)) < quarter,
                        pltpu.roll(x, width - quarter, axis=1), pltpu.roll(x, quarter, axis=1))
    return x * cos + swapped * sin_signed


def _rope_tables(rows, dim):
    nf = dim // 4
    inv = ROPE_THETA ** (-jnp.arange(nf, dtype=F32) / nf)
    r = jnp.repeat(jnp.arange(rows, dtype=F32), GRID_W)
    c = jnp.tile(jnp.arange(GRID_W, dtype=F32), rows)
    ar, ac = r[:, None] * inv, c[:, None] * inv
    cos = jnp.concatenate([jnp.cos(ar), jnp.cos(ar), jnp.cos(ac), jnp.cos(ac)], axis=1)
    sin = jnp.concatenate([-jnp.sin(ar), jnp.sin(ar), -jnp.sin(ac), jnp.sin(ac)], axis=1)
    return jnp.tile(cos, (1, LANES // dim)), jnp.tile(sin, (1, LANES // dim))


def _gqa_kernel(*refs, cached, n_sub):
    if cached:
        (q_ref, k_ref, v_ref, gq_ref, gk_ref, ck_ref, cv_ref, cosq_ref, sinq_ref, cosk_ref, sink_ref,
         o_ref, k_s, v_s) = refs
    else:
        q_ref, k_ref, v_ref, gq_ref, gk_ref, o_ref, kn_ref, k_s, v_s = refs
    gmat = _group_mean_matrix(LANES, HEAD_DIM)
    quarter = HEAD_DIM // 4
    n_k = k_ref.shape[0] // n_sub

    @pl.when(pl.program_id(1) == 0)
    def _():
        for sq in range(n_sub):
            k = _group_rms(k_ref[sq * n_k:(sq + 1) * n_k, :], gmat, gk_ref[...])
            v = v_ref[sq * n_k:(sq + 1) * n_k, :]
            if cached:
                k = _rope(k, cosk_ref[...], sink_ref[...], quarter)
                k = jnp.concatenate([ck_ref[...], k], axis=0)
                v = jnp.concatenate([cv_ref[...], v], axis=0)
            else:
                kn_ref[sq * n_k:(sq + 1) * n_k, :] = k
            k_s[sq] = k.astype(BF16)
            v_s[sq] = v.astype(BF16)

    tq = q_ref.shape[0] // n_sub
    lo = _lane((tq, LANES)) < HEAD_DIM
    group = H_B // KV_B
    q_scale = (HEAD_DIM ** -0.5) * math.log2(math.e)
    for sq in range(n_sub):
        k, v = k_s[sq], v_s[sq]
        qs = []
        for j in range(H_B // 2):
            q = _group_rms(q_ref[sq * tq:(sq + 1) * tq, j * LANES:(j + 1) * LANES], gmat, gq_ref[...])
            if cached:
                q = _rope(q, cosq_ref[...], sinq_ref[...], quarter)
            q = q * q_scale
            qs.append((q, pltpu.roll(q, HEAD_DIM, axis=1)))
        outs = [None] * H_B
        for kv in range(KV_B):
            keep = lo if kv == 0 else ~lo
            heads = range(kv * group, (kv + 1) * group)
            q3 = jnp.concatenate([jnp.where(keep, qs[h // 2][0 if h % 2 == kv else 1], 0.0).astype(BF16)
                                  for h in heads], axis=0)
            s = _dot_nt(q3, k)
            p = jnp.exp2(s - jnp.max(s, axis=-1, keepdims=True))
            r = _dot(p.astype(BF16), v) / jnp.sum(p, axis=-1, keepdims=True)
            for i, h in enumerate(heads):
                rh = r[i * tq:(i + 1) * tq]
                outs[h] = rh if h % 2 == kv else pltpu.roll(rh, HEAD_DIM, axis=1)
        for j in range(H_B // 2):
            o_ref[sq * tq:(sq + 1) * tq, j * LANES:(j + 1) * LANES] = jnp.where(lo, outs[2 * j], outs[2 * j + 1])


def gqa_mixer(q, k, v, gq, gk, *, n_seq, seq_len, tok0, cache=None, rope=None, layer=0):
    cached = cache is not None
    n_sub = 1 if cached else 2
    n_seq, seq_len = n_seq // n_sub, seq_len * n_sub
    tq = 256 * n_sub
    nq = seq_len // tq
    b0 = tok0 // seq_len
    q0 = tok0 // tq
    kvw = KV_B * HEAD_DIM
    in_specs = [pl.BlockSpec((tq, B_W), lambda b, i: (q0 + b * nq + i, 0)),
                pl.BlockSpec((seq_len, kvw), lambda b, i: (b0 + b, 0)),
                pl.BlockSpec((seq_len, kvw), lambda b, i: (b0 + b, 0)),
                pl.BlockSpec((1, LANES), lambda b, i: (0, 0)),
                pl.BlockSpec((1, LANES), lambda b, i: (0, 0))]
    args = [q, k, v, gq, gk]
    out_specs = [pl.BlockSpec((tq, B_W), lambda b, i: (b * nq + i, 0))]
    out_shape = [jax.ShapeDtypeStruct((n_seq * seq_len, B_W), F32)]
    if cached:
        cos, sin = rope
        cache_spec = pl.BlockSpec((None, None, PAST_LEN, kvw), lambda b, i: (b, layer, 0, 0))
        in_specs += [cache_spec, cache_spec,
                     pl.BlockSpec((tq, LANES), lambda b, i: (i, 0)),
                     pl.BlockSpec((tq, LANES), lambda b, i: (i, 0)),
                     pl.BlockSpec((seq_len, LANES), lambda b, i: (0, 0)),
                     pl.BlockSpec((seq_len, LANES), lambda b, i: (0, 0))]
        args += [cache[0], cache[1], cos, sin, cos, sin]
    else:
        out_specs.append(pl.BlockSpec((seq_len, kvw), lambda b, i: (b, 0)))
        out_shape.append(jax.ShapeDtypeStruct((n_seq * seq_len, kvw), F32))
    n_keys = seq_len // n_sub + (PAST_LEN if cached else 0)
    return pl.pallas_call(
        functools.partial(_gqa_kernel, cached=cached, n_sub=n_sub),
        grid=(n_seq, nq),
        in_specs=in_specs, out_specs=out_specs, out_shape=out_shape,
        scratch_shapes=[pltpu.VMEM((n_sub, n_keys, kvw), BF16), pltpu.VMEM((n_sub, n_keys, kvw), BF16)],
        compiler_params=_cparams(("arbitrary", "arbitrary")),
        name=f"gqa_mixer_{seq_len}",
    )(*args)


def _diff_kernel(*refs, cached, lam_init, n_sub):
    if cached:
        (q_ref, k_ref, v_ref, lam_ref, gn_ref, ck_ref, cv_ref, cosq_ref, sinq_ref, cosk_ref, sink_ref,
         o_ref) = refs
    else:
        q_ref, k_ref, v_ref, lam_ref, gn_ref, o_ref = refs
    quarter = DIFF_HD // 4
    lv = lam_ref[...]
    lam = (jnp.exp(jnp.sum(lv[0:1] * lv[1:2], axis=-1, keepdims=True))
           - jnp.exp(jnp.sum(lv[2:3] * lv[3:4], axis=-1, keepdims=True)) + lam_init)
    gmat = _group_mean_matrix(C_W, 2 * DIFF_HD)
    tq, n_k = q_ref.shape[0] // n_sub, k_ref.shape[0] // n_sub
    lane = _lane((tq, C_W))
    for sb in range(n_sub):
        q = q_ref[sb * tq:(sb + 1) * tq, :]
        k, v = k_ref[sb * n_k:(sb + 1) * n_k, :], v_ref[sb * n_k:(sb + 1) * n_k, :]
        if cached:
            cq = jnp.concatenate([cosq_ref[...]] * 2, axis=1)
            sq = jnp.concatenate([sinq_ref[...]] * 2, axis=1)
            ck = jnp.concatenate([cosk_ref[...]] * 2, axis=1)
            sk = jnp.concatenate([sink_ref[...]] * 2, axis=1)
            q = _rope(q, cq, sq, quarter)
            k = _rope(k, ck, sk, quarter)
            k = jnp.concatenate([ck_ref[...], k], axis=0)
            v = jnp.concatenate([cv_ref[...], v], axis=0)
        k = k.astype(BF16)
        v = v.astype(BF16)
        q = q * ((DIFF_HD ** -0.5) * math.log2(math.e))
        o = jnp.zeros(q.shape, F32)
        for h in range(H_C):
            ps, cs = [], []
            for m in range(2):
                qm = jnp.where(lane // DIFF_HD == 2 * h + m, q, 0.0).astype(BF16)
                s = _dot_nt(qm, k)
                p = jnp.exp2(s - jnp.max(s, axis=-1, keepdims=True))
                ps.append(p)
                cs.append((1.0 if m == 0 else lam) / jnp.sum(p, axis=-1, keepdims=True))
            a = ps[0] * cs[0] - ps[1] * cs[1]
            o = jnp.where(lane // (2 * DIFF_HD) == h, _dot(a.astype(BF16), v), o)
        o_ref[sb * tq:(sb + 1) * tq, :] = _group_rms(o, gmat, gn_ref[...]) * (1.0 - lam_init)


def diff_mixer(q, k, v, lam_p, gn, lam_init, *, n_seq, seq_len, tok0, cache=None, rope=None, layer=0):
    cached = cache is not None
    n_sub = 1 if cached else 2
    n_seq, seq_len = n_seq // n_sub, seq_len * n_sub
    tq = 256 * n_sub
    nq = seq_len // tq
    b0 = tok0 // seq_len
    q0 = tok0 // tq
    in_specs = [pl.BlockSpec((tq, C_W), lambda b, i: (q0 + b * nq + i, 0)),
                pl.BlockSpec((seq_len, C_W), lambda b, i: (b0 + b, 0)),
                pl.BlockSpec((seq_len, C_W), lambda b, i: (b0 + b, 0)),
                pl.BlockSpec((4, DIFF_HD), lambda b, i: (0, 0)),
                pl.BlockSpec((1, C_W), lambda b, i: (0, 0))]
    args = [q, k, v, lam_p, gn]
    if cached:
        cos, sin = rope
        cache_spec = pl.BlockSpec((None, None, PAST_LEN, C_W), lambda b, i: (b, layer, 0, 0))
        in_specs += [cache_spec, cache_spec,
                     pl.BlockSpec((tq, LANES), lambda b, i: (i, 0)),
                     pl.BlockSpec((tq, LANES), lambda b, i: (i, 0)),
                     pl.BlockSpec((seq_len, LANES), lambda b, i: (0, 0)),
                     pl.BlockSpec((seq_len, LANES), lambda b, i: (0, 0))]
        args += [cache[0], cache[1], cos, sin, cos, sin]
    return pl.pallas_call(
        functools.partial(_diff_kernel, cached=cached, lam_init=lam_init, n_sub=n_sub),
        grid=(n_seq, nq),
        in_specs=in_specs,
        out_specs=pl.BlockSpec((tq, C_W), lambda b, i: (b * nq + i, 0)),
        out_shape=jax.ShapeDtypeStruct((n_seq * seq_len, C_W), F32),
        compiler_params=_cparams(("arbitrary", "arbitrary")),
        name=f"diff_mixer_{seq_len}",
    )(*args)


def _outproj_kernel(*refs, routed, n_mix):
    x_ref = refs[0]
    mix_refs = refs[1:1 + 2 * n_mix]
    rest = refs[1 + 2 * n_mix:]
    if routed:
        w_ref, gm_ref, g_ref, sh_ref, sc_ref, r_ref, x1_ref, h_ref, lg_ref = rest
    else:
        w_ref, gm_ref, g_ref, sh_ref, sc_ref, x1_ref, h_ref = rest
    is_ctx = pl.program_id(0) < N_CTX // x_ref.shape[0]
    mixed = jnp.concatenate(
        [jnp.where(is_ctx, mix_refs[2 * m][...], mix_refs[2 * m + 1][...]) for m in range(n_mix)], axis=1)
    x1 = x_ref[...] + gm_ref[...] * _dot(mixed.astype(BF16), w_ref[...])
    x1_ref[...] = x1
    h = _rms(x1, g_ref[...]) * (1.0 + sc_ref[...]) + sh_ref[...]
    h_ref[...] = h.astype(h_ref.dtype)
    if routed:
        lg_ref[...] = _top2_route(_dot_split(h, r_ref[...]))


def out_projection(x, mixes, w_out_b, mod, norm_g, l, router_p=None):
    tm = 512
    n_ctx_tiles = N_CTX // tm
    n_lat_tiles = N_LAT // tm
    routed = router_p is not None
    mod_spec = lambda k: pl.BlockSpec((None, None, 1, D_MODEL), lambda i: (l, _cond_row(i, tm), 0, k))
    in_specs = [pl.BlockSpec((tm, D_MODEL), lambda i: (i, 0))]
    args = [x]
    for ctx_arr, lat_arr in mixes:
        w = ctx_arr.shape[1]
        in_specs.append(pl.BlockSpec((tm, w), lambda i: (jnp.minimum(i, n_ctx_tiles - 1), 0)))
        in_specs.append(pl.BlockSpec((tm, w), lambda i: (jnp.clip(i - n_ctx_tiles, 0, n_lat_tiles - 1), 0)))
        args += [ctx_arr, lat_arr]
    in_specs += [pl.BlockSpec((None, D_MODEL, D_MODEL), lambda i: (l, 0, 0)),
                 mod_spec(2),
                 pl.BlockSpec((None, 1, D_MODEL), lambda i: (l, 0, 0)),
                 mod_spec(3), mod_spec(4)]
    args += [w_out_b, mod, norm_g, mod, mod]
    out_specs = [pl.BlockSpec((tm, D_MODEL), lambda i: (i, 0)),
                 pl.BlockSpec((tm, D_MODEL), lambda i: (i, 0))]
    out_shape = [jax.ShapeDtypeStruct((N_TOK, D_MODEL), F32),
                 jax.ShapeDtypeStruct((N_TOK, D_MODEL), F32 if routed else BF16)]
    if routed:
        in_specs.append(pl.BlockSpec((D_MODEL, LANES), lambda i: (0, 0)))
        args.append(router_p)
        out_specs.append(pl.BlockSpec((tm, LANES), lambda i: (i, 0)))
        out_shape.append(jax.ShapeDtypeStruct((N_TOK, LANES), F32))
    return pl.pallas_call(
        functools.partial(_outproj_kernel, routed=routed, n_mix=len(mixes)),
        grid=(N_TOK // tm,),
        in_specs=in_specs, out_specs=out_specs, out_shape=out_shape,
        compiler_params=_cparams(("arbitrary",)),
        name="out_projection",
    )(*args)


def _ffn_kernel(h_ref, wg_ref, wu_ref, wd_ref, x_ref, gf_ref, o_ref, acc_ref):
    j = pl.program_id(1)

    @pl.when(j == 0)
    def _():
        acc_ref[...] = jnp.zeros_like(acc_ref)

    h = h_ref[...]
    act = _silu(_dot(h, wg_ref[...])) * _dot(h, wu_ref[...])
    acc_ref[...] += _dot(act.astype(BF16), wd_ref[...])

    @pl.when(j == pl.num_programs(1) - 1)
    def _():
        o_ref[...] = x_ref[...] + gf_ref[...] * acc_ref[...]


def ffn(h, w_gu, w_d, x1, mod, l):
    tm = 512
    lyr = l // 2
    n_chunk = D_FF // FF_CHUNK
    gu_spec = lambda half: pl.BlockSpec((None, D_MODEL, FF_CHUNK), lambda i, j: (lyr, 0, half * n_chunk + j))
    return pl.pallas_call(
        _ffn_kernel,
        grid=(N_TOK // tm, n_chunk),
        in_specs=[pl.BlockSpec((tm, D_MODEL), lambda i, j: (i, 0)),
                  gu_spec(0), gu_spec(1),
                  pl.BlockSpec((None, FF_CHUNK, D_MODEL), lambda i, j: (lyr, j, 0)),
                  pl.BlockSpec((tm, D_MODEL), lambda i, j: (i, 0)),
                  pl.BlockSpec((None, None, 1, D_MODEL), lambda i, j: (l, _cond_row(i, tm), 0, 5))],
        out_specs=pl.BlockSpec((tm, D_MODEL), lambda i, j: (i, 0)),
        out_shape=jax.ShapeDtypeStruct((N_TOK, D_MODEL), F32),
        scratch_shapes=[pltpu.VMEM((tm, D_MODEL), F32)],
        compiler_params=_cparams(("arbitrary", "arbitrary")),
        name="ffn_dense",
    )(h, w_gu, w_gu, w_d, x1, mod)


MOE_TILE = 256
MOE_SLOTS = TOP_K * N_TOK + N_EXP * MOE_TILE
MOE_TILES = MOE_SLOTS // MOE_TILE
ROUTE_I1, ROUTE_I2, ROUTE_W1, ROUTE_W2 = 0, 1, 2, 3


def _top2_route(logits):
    lane = _lane(logits.shape)
    neg = jnp.float32(-jnp.inf)
    lg = jnp.where(lane < N_EXP, logits, neg)
    t1 = jnp.max(lg, axis=-1, keepdims=True)
    i1 = jnp.min(jnp.where(lg == t1, lane, LANES), axis=-1, keepdims=True)
    lg2 = jnp.where(lane == i1, neg, lg)
    t2 = jnp.max(lg2, axis=-1, keepdims=True)
    i2 = jnp.min(jnp.where(lg2 == t2, lane, LANES), axis=-1, keepdims=True)
    e2 = jnp.exp(t2 - t1)
    w1 = 1.0 / (1.0 + e2)
    w2 = e2 / (1.0 + e2)
    rec = jnp.where(lane == ROUTE_I1, i1.astype(F32), 0.0) + jnp.where(lane == ROUTE_I2, i2.astype(F32), 0.0)
    return rec + jnp.where(lane == ROUTE_W1, w1, 0.0) + jnp.where(lane == ROUTE_W2, w2, 0.0)


def _routing_tables(route):
    n_assign = TOP_K * N_TOK
    expert = jnp.concatenate([route[:, ROUTE_I1], route[:, ROUTE_I2]]).astype(jnp.int32)
    index = jnp.arange(n_assign, dtype=jnp.int32)
    onehot = (expert[:, None] == jnp.arange(N_EXP, dtype=jnp.int32)[None, :]).astype(jnp.int32)
    running = jnp.cumsum(onehot, axis=0)
    counts = running[-1]
    rank = jnp.sum(running * onehot, axis=1) - 1
    padded = (counts + MOE_TILE - 1) // MOE_TILE * MOE_TILE
    ends = jnp.cumsum(padded)
    starts = ends - padded
    slot = jnp.sum(onehot * starts[None, :], axis=1) + rank
    sorted_token = (jnp.sort(expert * n_assign + index) % n_assign) % N_TOK
    tile_start = jnp.arange(MOE_TILES, dtype=jnp.int32) * MOE_TILE
    tile_expert = jnp.minimum(jnp.sum(tile_start[:, None] >= ends[None, :], axis=1), N_EXP - 1).astype(jnp.int32)
    first_sorted = (jnp.cumsum(counts) - counts)[tile_expert] + tile_start - starts[tile_expert]
    n_used = (ends[-1] // MOE_TILE).astype(jnp.int32).reshape(1)
    return sorted_token, slot, tile_expert, first_sorted.astype(jnp.int32), n_used


def _row_copy(src_hbm, row, dst, dst_row, sem):
    return pltpu.make_async_copy(src_hbm.at[pl.ds(row, 1)], dst.at[pl.ds(dst_row, 1)], sem)


def _experts_kernel(tok_ref, exp_ref, first_ref, used_ref, h_hbm, wg_ref, wu_ref, wd_ref, y_ref, x_buf, sems,
                    wg_s, wu_s, wd_s):
    t = pl.program_id(0)
    n_used = used_ref[0]

    def start_gather(tile, half):
        first = first_ref[tile]
        for r in range(MOE_TILE):
            tok = tok_ref[jnp.minimum(first + r, TOP_K * N_TOK - 1)]
            _row_copy(h_hbm, tok, x_buf.at[half], r, sems.at[half]).start(priority=r % 2)

    @pl.when(t == 0)
    def _():
        start_gather(0, 0)

    @pl.when(t + 1 < n_used)
    def _():
        start_gather(t + 1, (t + 1) % 2)

    @pl.when(t < n_used)
    def _():
        half = t % 2
        for r in range(MOE_TILE):
            _row_copy(h_hbm, 0, x_buf.at[half], r, sems.at[half]).wait()
        @pl.when((t == 0) | (exp_ref[t] != exp_ref[jnp.maximum(t - 1, 0)]))
        def _():
            wg_s[...] = wg_ref[...].astype(BF16)
            wu_s[...] = wu_ref[...].astype(BF16)
            wd_s[...] = wd_ref[...].astype(BF16)

        x = x_buf[half].astype(BF16)
        act = _silu(_dot(x, wg_s[...])) * _dot(x, wu_s[...])
        y_ref[...] = _dot(act.astype(BF16), wd_s[...])

    @pl.when(t >= n_used)
    def _():
        y_ref[...] = jnp.zeros_like(y_ref)


def moe_experts(h, route, w_gu, w_d, l):
    lyr = l // 2
    sorted_token, slot, tile_expert, first_sorted, n_used = _routing_tables(route)
    gu_spec = lambda half: pl.BlockSpec((None, None, D_MODEL, FF_CHUNK),
                                        lambda t, tok, exp, first, used: (lyr, exp[t], 0, half))
    y = pl.pallas_call(
        _experts_kernel,
        grid_spec=pltpu.PrefetchScalarGridSpec(
            num_scalar_prefetch=4,
            grid=(MOE_TILES,),
            in_specs=[pl.BlockSpec(memory_space=pl.ANY),
                      gu_spec(0), gu_spec(1),
                      pl.BlockSpec((None, None, FF_CHUNK, D_MODEL),
                                   lambda t, tok, exp, first, used: (lyr, exp[t], 0, 0))],
            out_specs=pl.BlockSpec((MOE_TILE, D_MODEL), lambda t, tok, exp, first, used: (t, 0)),
            scratch_shapes=[pltpu.VMEM((2, MOE_TILE, D_MODEL), F32), pltpu.SemaphoreType.DMA((2,)),
                            pltpu.VMEM((D_MODEL, FF_CHUNK), BF16), pltpu.VMEM((D_MODEL, FF_CHUNK), BF16),
                            pltpu.VMEM((FF_CHUNK, D_MODEL), BF16)]),
        out_shape=jax.ShapeDtypeStruct((MOE_SLOTS, D_MODEL), F32),
        compiler_params=_cparams(("arbitrary",)),
        name="moe_experts",
    )(sorted_token, tile_expert, first_sorted, n_used, h, w_gu, w_gu, w_d)
    return y, slot


def _moe_combine_kernel(slot_ref, y_hbm, x_ref, gf_ref, route_ref, o_ref, y_buf, sems):
    i = pl.program_id(0)
    tm = x_ref.shape[0]

    def start_gather(tile, half):
        for k in range(TOP_K):
            for r in range(tm):
                _row_copy(y_hbm, slot_ref[k * N_TOK + tile * tm + r], y_buf.at[half, k], r,
                          sems.at[half]).start(priority=r % 2)

    @pl.when(i == 0)
    def _():
        start_gather(0, 0)

    @pl.when(i + 1 < pl.num_programs(0))
    def _():
        start_gather(i + 1, (i + 1) % 2)

    half = i % 2
    for k in range(TOP_K):
        for r in range(tm):
            _row_copy(y_hbm, 0, y_buf.at[half, k], r, sems.at[half]).wait()
    route = route_ref[...]
    f = (route[:, ROUTE_W1:ROUTE_W1 + 1] * y_buf[half, 0] + route[:, ROUTE_W2:ROUTE_W2 + 1] * y_buf[half, 1])
    o_ref[...] = x_ref[...] + gf_ref[...] * f


def moe_combine(y, slot, x1, mod, route, l):
    tm = 256
    return pl.pallas_call(
        _moe_combine_kernel,
        grid_spec=pltpu.PrefetchScalarGridSpec(
            num_scalar_prefetch=1,
            grid=(N_TOK // tm,),
            in_specs=[pl.BlockSpec(memory_space=pl.ANY),
                      pl.BlockSpec((tm, D_MODEL), lambda i, s: (i, 0)),
                      pl.BlockSpec((None, None, 1, D_MODEL), lambda i, s: (l, _cond_row(i, tm), 0, 5)),
                      pl.BlockSpec((tm, LANES), lambda i, s: (i, 0))],
            out_specs=pl.BlockSpec((tm, D_MODEL), lambda i, s: (i, 0)),
            scratch_shapes=[pltpu.VMEM((2, TOP_K, tm, D_MODEL), F32), pltpu.SemaphoreType.DMA((2,))]),
        out_shape=jax.ShapeDtypeStruct((N_TOK, D_MODEL), F32),
        compiler_params=_cparams(("arbitrary",)),
        name="moe_combine",
    )(slot, y, x1, mod, route)


def _final_norm_kernel(x_ref, g_ref, o_ref):
    o_ref[...] = _rms(x_ref[...], g_ref[...])


def final_norm_call(x, g, tok0, n_tok):
    tm = 1024
    i0 = tok0 // tm
    return pl.pallas_call(
        _final_norm_kernel,
        grid=(n_tok // tm,),
        in_specs=[pl.BlockSpec((tm, D_MODEL), lambda i: (i0 + i, 0)),
                  pl.BlockSpec((1, D_MODEL), lambda i: (0, 0))],
        out_specs=pl.BlockSpec((tm, D_MODEL), lambda i: (i, 0)),
        out_shape=jax.ShapeDtypeStruct((n_tok, D_MODEL), F32),
        compiler_params=_cparams(("arbitrary",)),
        name="final_norm",
    )(x, g)


def _cache_store_kernel(*refs):
    n = len(refs) // 3
    for src, dst in zip(refs[:n], refs[2 * n:]):
        dst[...] = src[...].reshape(dst.shape)


def cache_store(sources, all_layers, l):
    n_sq = 2
    tm = n_sq * SEQ
    return pl.pallas_call(
        _cache_store_kernel,
        grid=(N_CTX // tm,),
        in_specs=([pl.BlockSpec((tm, s.shape[1]), lambda i: (i, 0)) for s in sources]
                  + [pl.BlockSpec(memory_space=pl.ANY)] * len(all_layers)),
        out_specs=[pl.BlockSpec((n_sq, None, SEQ, a.shape[-1]), lambda i: (i, l, 0, 0)) for a in all_layers],
        out_shape=[jax.ShapeDtypeStruct(a.shape, F32) for a in all_layers],
        input_output_aliases={len(sources) + k: k for k in range(len(all_layers))},
        compiler_params=_cparams(("arbitrary",)),
        name="cache_store",
    )(*sources, *all_layers)


def _gate_row(p):
    return jnp.zeros((1, LANES), F32).at[0, GATE_A:GATE_A + 2 * H_A].set(p.reshape(-1))


def _tile_lanes(g, width):
    return jnp.tile(g, width // g.shape[0])[None, :]


def kernel(x_prompt, x_sample, state_delta, cache_gqa_k, cache_gqa_v, cache_diff_k, cache_diff_v, c, c_ctx,
           ada_w, ada_b, norm_mix, norm_ffn, w_in, conv_w, dn_a_log, dn_dt_bias, dn_norm, gqa_q_norm,
           gqa_k_norm, diff_lambda, diff_norm, w_out, ffn_w_gu, ffn_w_down, moe_router, moe_w_gu, moe_w_down,
           final_norm):
    cond8 = jnp.concatenate([c_ctx[None, :], c, jnp.zeros((8 - N_COND, D_MODEL), F32)], axis=0)
    gate_lo = 4 * A_W
    gate_hi = gate_lo + 4 * H_A
    w_in_p = jnp.concatenate(
        [w_in[:, :, :gate_lo], w_in[:, :, gate_hi:], w_in[:, :, gate_lo:gate_hi],
         jnp.zeros((DEPTH, D_MODEL, LANES - 4 * H_A), F32)], axis=-1).astype(BF16)
    w_out_b = w_out.astype(BF16)
    norm_mix3 = norm_mix.reshape(DEPTH, 1, D_MODEL)
    norm_ffn3 = norm_ffn.reshape(DEPTH, 1, D_MODEL)
    ffn_gu_b, ffn_d_b = ffn_w_gu.astype(BF16), ffn_w_down.astype(BF16)
    router_p = jnp.pad(moe_router, ((0, 0), (0, 0), (0, LANES - N_EXP)))
    rope_b = _rope_tables(DEC_SEQ // GRID_W, HEAD_DIM)
    rope_c = _rope_tables(DEC_SEQ // GRID_W, DIFF_HD)
    s0_ctx = jnp.zeros((BATCH, 2, H_A, HEAD_DIM, HEAD_DIM), F32)
    cache_b = (cache_gqa_k.reshape(DEC_BATCH, DEPTH, PAST_LEN, KV_B * HEAD_DIM),
               cache_gqa_v.reshape(DEC_BATCH, DEPTH, PAST_LEN, KV_B * HEAD_DIM))
    cache_c = (cache_diff_k.reshape(DEC_BATCH, DEPTH, PAST_LEN, C_W),
               cache_diff_v.reshape(DEC_BATCH, DEPTH, PAST_LEN, C_W))

    mod = ada_modulation(cond8, ada_w, ada_b)[:, :N_COND].reshape(DEPTH, N_COND, 1, 6 * D_MODEL)

    x = jnp.concatenate([x_prompt.reshape(N_CTX, D_MODEL), x_sample.reshape(N_LAT, D_MODEL)], axis=0)
    all_states = jnp.zeros((BATCH, DEPTH, 2, H_A, HEAD_DIM, HEAD_DIM), F32)
    ctx_layers = tuple(jnp.zeros((BATCH, DEPTH, SEQ, w), F32)
                       for w in (KV_B * HEAD_DIM, KV_B * HEAD_DIM, C_W, C_W))
    ctx = dict(n_seq=BATCH, seq_len=SEQ, tok0=0)
    lat = dict(n_seq=DEC_BATCH, seq_len=DEC_SEQ, tok0=N_CTX)
    for l in range(DEPTH):
        qkv_a, z_a, q_b, k_b, v_b, q_c, k_c, v_c, gate = in_projection(x, mod, norm_mix3, w_in_p, l)

        dn_args = (qkv_a, z_a, gate, conv_w[l], _gate_row(dn_a_log[l]), _gate_row(dn_dt_bias[l]),
                   _tile_lanes(dn_norm[l], LANES))
        oa_ctx, all_states = delta_mixer(*dn_args, s0_ctx, n_sub=2, all_states=all_states, layer=l, **ctx)
        oa_lat, _ = delta_mixer(*dn_args, state_delta[:, l], n_sub=1, **lat)

        gq, gk = _tile_lanes(gqa_q_norm[l], LANES), _tile_lanes(gqa_k_norm[l], LANES)
        ob_ctx, kn_ctx = gqa_mixer(q_b, k_b, v_b, gq, gk, **ctx)
        (ob_lat,) = gqa_mixer(q_b, k_b, v_b, gq, gk, cache=cache_b, rope=rope_b, layer=l, **lat)

        lam_init = 0.8 - 0.6 * math.exp(-0.3 * l)
        gn = _tile_lanes(diff_norm[l], C_W)
        oc_ctx = diff_mixer(q_c, k_c, v_c, diff_lambda[l], gn, lam_init, **ctx)
        oc_lat = diff_mixer(q_c, k_c, v_c, diff_lambda[l], gn, lam_init, cache=cache_c, rope=rope_c, layer=l, **lat)

        mixes = ((oa_ctx, oa_lat), (ob_ctx, ob_lat), (oc_ctx, oc_lat))
        if l % 2 == 0:
            x1, h2 = out_projection(x, mixes, w_out_b, mod, norm_ffn3, l)
            x = ffn(h2, ffn_gu_b, ffn_d_b, x1, mod, l)
        else:
            x1, h2, route = out_projection(x, mixes, w_out_b, mod, norm_ffn3, l, router_p[l // 2])
            y_slots, slot = moe_experts(h2, route, moe_w_gu, moe_w_down, l)
            x = moe_combine(y_slots, slot, x1, mod, route, l)

        ctx_layers = cache_store((kn_ctx, v_b, k_c, v_c), ctx_layers, l)

    y_prompt = final_norm_call(x, final_norm[None, :], 0, N_CTX).reshape(BATCH, SEQ, D_MODEL)
    y_sample = final_norm_call(x, final_norm[None, :], N_CTX, N_LAT).reshape(DEC_BATCH, DEC_SEQ, D_MODEL)
    new_gk, new_gv, new_dk, new_dv = ctx_layers
    return (y_prompt, y_sample, all_states,
            new_gk.reshape(BATCH, DEPTH, SEQ, KV_B, HEAD_DIM), new_gv.reshape(BATCH, DEPTH, SEQ, KV_B, HEAD_DIM),
            new_dk.reshape(BATCH, DEPTH, SEQ, H_C, 2, DIFF_HD), new_dv.reshape(BATCH, DEPTH, SEQ, H_C, 2 * DIFF_HD))
```

```python
import functools
import math

import jax
import jax.numpy as jnp
from jax import lax
from jax.experimental import pallas as pl
from jax.experimental.pallas import tpu as pltpu

D_MODEL = 1024
BATCH = 16
SEQ = 256
DEPTH = 4
DEC_BATCH = 2
DEC_SEQ = 1024
PAST_LEN = 512
GRID_W = 64
HEAD_DIM = 64
H_A = 6
A_W = H_A * HEAD_DIM
H_B = 6
KV_B = 2
B_W = H_B * HEAD_DIM
H_C = 4
DIFF_HD = 32
C_W = H_C * 2 * DIFF_HD
CHUNK = 64
ROPE_THETA = 10000.0
D_FF = 2816
N_EXP = 8
TOP_K = 2
D_FF_E = 1408
EPS = 1e-6

N_CTX = BATCH * SEQ
N_LAT = DEC_BATCH * DEC_SEQ
N_TOK = N_CTX + N_LAT
N_COND = 1 + DEC_BATCH
IN_PAD = 3072
FF_CHUNK = D_FF_E
LANES = 128
VMEM_LIMIT = 56 * 1024 * 1024

IN_SPLIT = (("qkv_a", 3 * A_W), ("z_a", A_W), ("q_b", B_W), ("k_b", KV_B * HEAD_DIM), ("v_b", KV_B * HEAD_DIM),
            ("q_c", C_W), ("k_c", C_W), ("v_c", C_W), ("gate", LANES))
GATE_BETA, GATE_A = 0, 2 * H_A

F32 = jnp.float32
BF16 = jnp.bfloat16


def _cparams(sem):
    return pltpu.CompilerParams(dimension_semantics=sem, vmem_limit_bytes=VMEM_LIMIT)


def _dot(a, b):
    return jnp.dot(a, b, preferred_element_type=F32)


def _dot_split(a, b):
    a_hi, b_hi = a.astype(BF16), b.astype(BF16)
    a_lo = (a - a_hi.astype(F32)).astype(BF16)
    b_lo = (b - b_hi.astype(F32)).astype(BF16)
    return _dot(jnp.concatenate([a_hi, a_lo, a_hi], axis=1), jnp.concatenate([b_hi, b_hi, b_lo], axis=0))


def _dot_nt(a, b):
    return lax.dot_general(a, b, (((1,), (1,)), ((), ())), preferred_element_type=F32)


def _silu(x):
    return x * jax.nn.sigmoid(x)


def _softplus(x):
    return jnp.maximum(x, 0.0) + jnp.log1p(jnp.exp(-jnp.abs(x)))


def _rms(x, g):
    return x * lax.rsqrt(jnp.mean(x * x, axis=-1, keepdims=True) + EPS) * g


def _lane(shape):
    return lax.broadcasted_iota(jnp.int32, shape, len(shape) - 1)


def _group_mean_matrix(width, group):
    i = lax.broadcasted_iota(jnp.int32, (width, width), 0)
    j = lax.broadcasted_iota(jnp.int32, (width, width), 1)
    return jnp.where(i // group == j // group, 1.0 / group, 0.0).astype(BF16)


def _group_rms(x, gmat, g):
    sq = x * x
    hi = sq.astype(BF16)
    lo = (sq - hi.astype(F32)).astype(BF16)
    ms = _dot(jnp.concatenate([hi, lo], axis=1), jnp.concatenate([gmat, gmat], axis=0))
    return x * lax.rsqrt(ms + EPS) * g


def _cond_row(i, tm):
    return jnp.maximum((i * tm) // DEC_SEQ - (N_CTX // DEC_SEQ - 1), 0)


def _ada_kernel(c_ref, w_ref, b_ref, o_ref):
    o_ref[...] = _dot_split(_silu(c_ref[...]), w_ref[...]) + b_ref[...]


def ada_modulation(cond8, ada_w, ada_b):
    tn = 1024
    return pl.pallas_call(
        _ada_kernel,
        grid=(DEPTH, 6 * D_MODEL // tn),
        in_specs=[pl.BlockSpec((8, D_MODEL), lambda l, j: (0, 0)),
                  pl.BlockSpec((None, D_MODEL, tn), lambda l, j: (l, 0, j)),
                  pl.BlockSpec((None, 1, tn), lambda l, j: (l, 0, j))],
        out_specs=pl.BlockSpec((None, 8, tn), lambda l, j: (l, 0, j)),
        out_shape=jax.ShapeDtypeStruct((DEPTH, 8, 6 * D_MODEL), F32),
        compiler_params=_cparams(("arbitrary", "arbitrary")),
        name="ada_modulation",
    )(cond8, ada_w, ada_b.reshape(DEPTH, 1, 6 * D_MODEL))


GATE_LO = 4 * A_W
GATE_HI = GATE_LO + 4 * H_A
IN_COLS = GATE_HI + B_W + 2 * KV_B * HEAD_DIM + 3 * C_W


def _inproj_kernel(x_ref, g_ref, sh_ref, sc_ref, w_ref, *refs):
    o_refs, w_s = refs[:-1], refs[-1]

    @pl.when(pl.program_id(0) == 0)
    def _():
        w_s[:, :GATE_LO] = w_ref[:, :GATE_LO].astype(BF16)
        w_s[:, GATE_LO:IN_COLS - 4 * H_A] = w_ref[:, GATE_HI:].astype(BF16)
        tail = jnp.concatenate([w_ref[:, GATE_LO:GATE_HI], jnp.zeros((D_MODEL, LANES - 4 * H_A), F32)], axis=1)
        w_s[:, IN_COLS - 4 * H_A:] = tail.astype(BF16)

    h = _rms(x_ref[...], g_ref[...]) * (1.0 + sc_ref[...]) + sh_ref[...]
    acc = _dot(h.astype(BF16), w_s[...])
    off = 0
    for o_ref, (_, width) in zip(o_refs, IN_SPLIT):
        o_ref[...] = acc[:, off:off + width]
        off += width


def in_projection(x, mod, norm_g, w_in, l):
    tm = 512
    return pl.pallas_call(
        _inproj_kernel,
        grid=(N_TOK // tm,),
        in_specs=[pl.BlockSpec((tm, D_MODEL), lambda i: (i, 0)),
                  pl.BlockSpec((None, 1, D_MODEL), lambda i: (l, 0, 0)),
                  pl.BlockSpec((None, None, 1, D_MODEL), lambda i: (l, _cond_row(i, tm), 0, 0)),
                  pl.BlockSpec((None, None, 1, D_MODEL), lambda i: (l, _cond_row(i, tm), 0, 1)),
                  pl.BlockSpec((None, D_MODEL, IN_COLS), lambda i: (l, 0, 0))],
        out_specs=[pl.BlockSpec((tm, w), lambda i: (i, 0)) for _, w in IN_SPLIT],
        out_shape=[jax.ShapeDtypeStruct((N_TOK, w), F32) for _, w in IN_SPLIT],
        scratch_shapes=[pltpu.VMEM((D_MODEL, IN_PAD), BF16)],
        compiler_params=_cparams(("arbitrary",)),
        name="in_projection",
    )(x, norm_g, mod, mod, w_in)


SOLVE_BLOCK = 16


def _delta_kernel(*refs, seq_len, n_sub, aliased):
    qkv_ref, z_ref, gate_ref, cw_ref, alog_ref, dtb_ref, ng_ref, s0_ref = refs[:8]
    (o_ref, sfin_ref, pad_s, kk_s, qq_s, vk_s, gc_s, beta_s, u_s, w_s, qd_s, a_s, kdt_s, egt_s, sa_s, o_s,
     tp_s, t32_s, rp_s, c32_s, m_s, x_s, vbd_s, v_s, op_s) = refs[9 if aliased else 8:]
    n_chunk = seq_len // CHUNK
    n_pair = H_A // 2
    n_unit = n_sub * H_A
    half = HEAD_DIM
    lo = _lane((CHUNK, LANES)) < half
    lo_row = _lane((1, LANES)) < half
    lo16 = _lane((CHUNK, LANES)).astype(F32).astype(BF16) < half
    row = lax.broadcasted_iota(jnp.int32, (CHUNK, LANES), 0)
    col = _lane((CHUNK, LANES)) % half
    ahead = jnp.where(lo, row - col, col - row)
    incl = ahead >= 0
    strict = ahead > 0
    diag_blk = row // SOLVE_BLOCK == col // SOLVE_BLOCK
    big_row = lax.broadcasted_iota(jnp.int32, (LANES, LANES), 0)
    anti = big_row // half + _lane((LANES, LANES)) // half == 1
    eye = (big_row == _lane((LANES, LANES))).astype(F32)

    def both(a):
        return jnp.concatenate([jnp.where(lo, a, 0.0), jnp.where(lo, 0.0, a)], axis=0)

    rb = 128
    lo_rb = _lane((rb, LANES)) < half
    zero8 = jnp.zeros((8, 3 * A_W), F32)
    pad_s[0:8, :] = zero8
    pad_s[8 + seq_len:16 + seq_len, :] = zero8

    def l2n(x):
        return x * lax.rsqrt(0.5 * jnp.sum(x * x, axis=-1, keepdims=True) + EPS)

    for sq in range(n_sub):
        pad_s[8:8 + seq_len, :] = qkv_ref[sq * seq_len:(sq + 1) * seq_len, :]
        for r0 in range(0, seq_len, rb):
            y = _silu(pad_s[7 + r0:7 + r0 + rb, :] * cw_ref[0:1, :] + pad_s[8 + r0:8 + r0 + rb, :] * cw_ref[1:2, :]
                      + pad_s[9 + r0:9 + r0 + rb, :] * cw_ref[2:3, :])

            def dup(part, p):
                c = y[:, part * A_W + p * LANES:part * A_W + (p + 1) * LANES]
                r = pltpu.roll(c, half, axis=1)
                return jnp.where(lo_rb, c, r), jnp.where(lo_rb, r, c)

            for p in range(n_pair):
                qs, ks, vs = dup(0, p), dup(1, p), dup(2, p)
                for e in range(2):
                    u = sq * H_A + 2 * p + e
                    kk = l2n(ks[e])
                    kk_s[u, r0:r0 + rb, :] = kk
                    qq_s[u, r0:r0 + rb, :] = l2n(qs[e]) * (HEAD_DIM ** -0.5)
                    vk_s[u, r0:r0 + rb, :] = jnp.where(lo_rb, vs[e], kk)

    cb = 256
    ci = lax.broadcasted_iota(jnp.int32, (cb, cb), 0)
    cj = lax.broadcasted_iota(jnp.int32, (cb, cb), 1)
    same = ci // CHUNK == cj // CHUNK
    m_pre = jnp.where(same & (ci >= cj), 1.0, 0.0).astype(BF16)
    m_suf = jnp.where(same & (ci <= cj), 1.0, 0.0).astype(BF16)
    fwd_cols = (_lane((cb, LANES)) >= GATE_A) & (_lane((cb, LANES)) < GATE_A + H_A)
    for r0 in range(0, n_sub * seq_len, cb):
        gt = gate_ref[r0:r0 + cb, :]
        beta_s[r0:r0 + cb, :] = jax.nn.sigmoid(gt)
        g = -jnp.exp(alog_ref[...]) * _softplus(gt + dtb_ref[...])
        g1 = g.astype(BF16)
        r1 = g - g1.astype(F32)
        g2 = r1.astype(BF16)
        g3 = (r1 - g2.astype(F32)).astype(BF16)
        blk = jnp.concatenate([g1, g2, g3], axis=1)
        pre, suf = _dot(m_pre, blk), _dot(m_suf, blk)
        pre = pre[:, :LANES] + pre[:, LANES:2 * LANES] + pre[:, 2 * LANES:]
        suf = suf[:, :LANES] + suf[:, LANES:2 * LANES] + suf[:, 2 * LANES:]
        gc_s[r0:r0 + cb, :] = jnp.where(fwd_cols, pre, suf)

    zero = jnp.zeros((half, half), F32)
    for sq in range(n_sub):
        for h in range(H_A):
            sa_s[sq * H_A + h] = jnp.concatenate([jnp.concatenate([zero, s0_ref[sq, 1, h]], axis=1),
                                                  jnp.concatenate([s0_ref[sq, 0, h], zero], axis=1)], axis=0)

    n_fac = int(math.log2(SOLVE_BLOCK))

    def prep(c, carry):
        rows = pl.ds(pl.multiple_of(c * CHUNK, CHUNK), CHUNK)
        for sq in range(n_sub):
            grow = pl.ds(pl.multiple_of(sq * seq_len + c * CHUNK, CHUNK), CHUNK)
            gcb = gc_s[grow, :]
            gct = gcb.T
            bb = beta_s[grow, :]
            for h in range(H_A):
                u = sq * H_A + h
                full = (CHUNK, LANES)
                cff = jnp.broadcast_to(gcb[:, GATE_A + h:GATE_A + h + 1], full)
                cbb = jnp.broadcast_to(gcb[:, GATE_A + H_A + h:GATE_A + H_A + h + 1], full)
                bff = jnp.broadcast_to(bb[:, GATE_BETA + h:GATE_BETA + h + 1], full)
                bbb = jnp.broadcast_to(bb[:, GATE_BETA + H_A + h:GATE_BETA + H_A + h + 1], full)
                gr_fb = jnp.concatenate([gct[GATE_A + h:GATE_A + h + 1, :],
                                         gct[GATE_A + H_A + h:GATE_A + H_A + h + 1, :]], axis=1)
                decay = jnp.where(incl, jnp.exp(jnp.where(incl, jnp.where(lo, cff, cbb) - gr_fb, 0.0)), 0.0)
                egf, egb = jnp.exp(cff), jnp.exp(cbb)
                tot_f = gcb[CHUNK - 1:CHUNK, GATE_A + h:GATE_A + h + 1]
                tot_b = gcb[0:1, GATE_A + H_A + h:GATE_A + H_A + h + 1]
                kk, qq, vk = kk_s[u, rows, :], qq_s[u, rows, :], vk_s[u, rows, :]
                k16 = kk[:, :half].astype(BF16)
                kq = jnp.concatenate([k16, qq[:, :half].astype(BF16)], axis=0)
                gram = _dot_nt(kq, jnp.concatenate([k16, k16], axis=0))
                neg_l = jnp.where(strict, gram[:CHUNK] * jnp.where(lo, -bff, -bbb) * decay, 0.0)
                a_s[u, rows, :] = jnp.where(incl, gram[CHUNK:] * decay, 0.0).astype(BF16)
                neg_d = jnp.where(diag_blk, neg_l, 0.0)
                pd = both(neg_d)
                t32_s[u] = eye + pd
                tp_s[u, :, :LANES] = (eye + pd).astype(BF16)
                tp_s[u, :, LANES:] = pd.astype(BF16)
                x_f = vk * jnp.where(lo, bff, bff * egf)
                x_b = vk * jnp.where(lo, bbb, bbb * egb)
                rp_s[u, :, :LANES] = jnp.concatenate([x_f, x_b], axis=0).astype(BF16)
                rp_s[u, :, LANES:] = both(neg_l - neg_d).astype(BF16)
                qd_s[u, rows, :] = (qq * jnp.where(lo, egb, egf)).astype(BF16)
                kd = kk * jnp.exp(jnp.where(lo, tot_b - cbb, tot_f - cff))
                kdt_s[u, c] = kd.T.astype(BF16)
                egt_s[u, c] = jnp.exp(jnp.where(lo_row, tot_f, tot_b))
        for u in range(n_unit):
            tp_s[u, :, LANES:] = _dot(tp_s[u, :, LANES:], tp_s[u, :, LANES:]).astype(BF16)
        for j in range(1, n_fac):
            for u in range(n_unit):
                if j + 1 < n_fac:
                    tp = _dot(tp_s[u, :, LANES:], tp_s[u])
                    t = t32_s[u] + tp[:, :LANES]
                    t32_s[u] = t
                    tp_s[u, :, :LANES] = t.astype(BF16)
                    tp_s[u, :, LANES:] = tp[:, LANES:].astype(BF16)
                else:
                    t = t32_s[u] + _dot(tp_s[u, :, LANES:], tp_s[u, :, :LANES])
                    tp_s[u, :, :LANES] = t.astype(BF16)
        for u in range(n_unit):
            cm = _dot(tp_s[u, :, :LANES], rp_s[u])
            c32_s[u] = cm[:, :LANES]
            rp_s[u] = cm.astype(BF16)
        assert CHUNK // SOLVE_BLOCK == 4
        for u in range(n_unit):
            mm = _dot(rp_s[u, :, LANES:], rp_s[u])
            y1 = c32_s[u] + mm[:, :LANES]
            c32_s[u] = y1
            x_s[u] = y1.astype(BF16)
            m_s[u] = mm[:, LANES:].astype(BF16)
        for u in range(n_unit):
            x = c32_s[u] + _dot(m_s[u], x_s[u])
            x_f, x_b = x[:CHUNK], pltpu.roll(x[CHUNK:], half, axis=1)
            u_s[u, rows, :] = jnp.where(lo, x_f, x_b)
            w_s[u, rows, :] = jnp.where(lo, x_b, x_f).astype(BF16)
        return carry

    lax.fori_loop(0, n_chunk, prep, 0)

    def scan(s, carry):
        rf = pl.ds(pl.multiple_of(s * CHUNK, CHUNK), CHUNK)
        sb = n_chunk - 1 - s
        rbk = pl.ds(pl.multiple_of(sb * CHUNK, CHUNK), CHUNK)
        for u in range(n_unit):
            w = jnp.where(lo16, w_s[u, rbk, :], w_s[u, rf, :])
            qd = jnp.where(lo16, qd_s[u, rbk, :], qd_s[u, rf, :])
            wq = _dot(jnp.concatenate([w, qd], axis=0), sa_s[u].astype(BF16))
            v = jnp.where(lo, u_s[u, rf, :], u_s[u, rbk, :]) - wq[:CHUNK]
            v_s[u] = v.astype(BF16)
            vbd_s[u] = both(v).astype(BF16)
            op_s[u] = wq[CHUNK:]
        for u in range(n_unit):
            a = jnp.where(lo16, a_s[u, rf, :], a_s[u, rbk, :])
            o = op_s[u] + _dot(a, vbd_s[u])
            o_s[u, rf, 0:half] = o[:, :half]
            o_s[u, rbk, half:LANES] = o[:, half:]
            kdt = jnp.concatenate([kdt_s[u, sb][:half], kdt_s[u, s][half:]], axis=0)
            egt = jnp.where(lo_row, egt_s[u, s], egt_s[u, sb])
            sa_s[u] = sa_s[u] * egt + jnp.where(anti, _dot(kdt, v_s[u]), 0.0)
        return carry

    lax.fori_loop(0, n_chunk, scan, 0)

    for sq in range(n_sub):
        for r0 in range(0, seq_len, rb):
            tr = sq * seq_len + r0
            for p in range(n_pair):
                nrm = []
                for e in range(2):
                    ofb = o_s[sq * H_A + 2 * p + e, r0:r0 + rb, :]
                    oo = ofb + pltpu.roll(ofb, half, axis=1)
                    nrm.append(oo * lax.rsqrt(jnp.sum(oo * oo, axis=-1, keepdims=True) * (0.5 / HEAD_DIM) + EPS))
                o_ref[tr:tr + rb, p * LANES:(p + 1) * LANES] = (
                    jnp.where(lo_rb, nrm[0], nrm[1]) * ng_ref[...]
                    * _silu(z_ref[tr:tr + rb, p * LANES:(p + 1) * LANES]))
        for h in range(H_A):
            st = sa_s[sq * H_A + h]
            sfin_ref[sq, 0, h] = st[half:, :half]
            sfin_ref[sq, 1, h] = st[:half, half:]


def delta_mixer(qkv, z, gate, cw, alog_row, dtb_row, ng2, s0, *, n_seq, seq_len, tok0, n_sub,
                all_states=None, layer=0):
    n_chunk = seq_len // CHUNK
    n_unit = n_sub * H_A
    blk = n_sub * seq_len
    b0 = tok0 // blk
    tok = lambda w: pl.BlockSpec((blk, w), lambda b: (b0 + b, 0))
    full = lambda a: pl.BlockSpec(a.shape, lambda b: (0,) * a.ndim)
    state = pl.BlockSpec((n_sub, 2, H_A, HEAD_DIM, HEAD_DIM), lambda b: (b, 0, 0, 0, 0))
    per_unit = lambda dt: pltpu.VMEM((n_unit, seq_len, LANES), dt)
    stage = lambda rows, cols, dt: pltpu.VMEM((n_unit, rows, cols), dt)
    aliased = all_states is not None
    in_specs = [tok(3 * A_W), tok(A_W), tok(LANES), full(cw), full(alog_row), full(dtb_row), full(ng2), state]
    args = [qkv, z, gate, cw, alog_row, dtb_row, ng2, s0]
    if aliased:
        in_specs.append(pl.BlockSpec(memory_space=pl.ANY))
        args.append(all_states)
        state_out = pl.BlockSpec((n_sub, None, 2, H_A, HEAD_DIM, HEAD_DIM), lambda b: (b, layer, 0, 0, 0, 0))
        state_shape = jax.ShapeDtypeStruct(all_states.shape, F32)
    else:
        state_out = state
        state_shape = jax.ShapeDtypeStruct((n_seq, 2, H_A, HEAD_DIM, HEAD_DIM), F32)
    return pl.pallas_call(
        functools.partial(_delta_kernel, seq_len=seq_len, n_sub=n_sub, aliased=aliased),
        grid=(n_seq // n_sub,),
        in_specs=in_specs,
        out_specs=[pl.BlockSpec((blk, A_W), lambda b: (b, 0)), state_out],
        out_shape=[jax.ShapeDtypeStruct((n_seq * seq_len, A_W), F32), state_shape],
        input_output_aliases={len(args) - 1: 1} if aliased else {},
        scratch_shapes=[pltpu.VMEM((seq_len + 16, 3 * A_W), F32),
                        per_unit(F32), per_unit(F32), per_unit(F32),
                        pltpu.VMEM((blk, LANES), F32),
                        pltpu.VMEM((blk, LANES), F32),
                        per_unit(F32), per_unit(BF16), per_unit(BF16), per_unit(BF16),
                        pltpu.VMEM((n_unit, n_chunk, LANES, HEAD_DIM), BF16),
                        pltpu.VMEM((n_unit, n_chunk, 1, LANES), F32),
                        stage(LANES, LANES, F32),
                        per_unit(F32),
                        stage(LANES, 2 * LANES, BF16),
                        stage(LANES, LANES, F32),
                        stage(LANES, 2 * LANES, BF16),
                        stage(LANES, LANES, F32),
                        stage(LANES, LANES, BF16),
                        stage(LANES, LANES, BF16),
                        stage(LANES, LANES, BF16),
                        stage(CHUNK, LANES, BF16),
                        stage(CHUNK, LANES, F32)],
        compiler_params=_cparams(("arbitrary",)),
        name=f"delta_mixer_{seq_len}",
    )(*args)


def _rope(x, cos, sin_signed, quarter):
    width = x.shape[-1]
    swapped = jnp.where((_lane(x.shape) % (2 * quarter)) < quarter,
                        pltpu.roll(x, width - quarter, axis=1), pltpu.roll(x, quarter, axis=1))
    return x * cos + swapped * sin_signed


def _rope_tables(rows, dim):
    nf = dim // 4
    inv = ROPE_THETA ** (-jnp.arange(nf, dtype=F32) / nf)
    r = jnp.repeat(jnp.arange(rows, dtype=F32), GRID_W)
    c = jnp.tile(jnp.arange(GRID_W, dtype=F32), rows)
    ar, ac = r[:, None] * inv, c[:, None] * inv
    cos = jnp.concatenate([jnp.cos(ar), jnp.cos(ar), jnp.cos(ac), jnp.cos(ac)], axis=1)
    sin = jnp.concatenate([-jnp.sin(ar), jnp.sin(ar), -jnp.sin(ac), jnp.sin(ac)], axis=1)
    return jnp.tile(cos, (1, LANES // dim)), jnp.tile(sin, (1, LANES // dim))


def _gqa_kernel(*refs, cached, n_sub):
    if cached:
        (q_ref, k_ref, v_ref, gq_ref, gk_ref, ck_ref, cv_ref, cosq_ref, sinq_ref, cosk_ref, sink_ref,
         o_ref, k_s, v_s) = refs
    else:
        q_ref, k_ref, v_ref, gq_ref, gk_ref, o_ref, kn_ref, k_s, v_s = refs
    gmat = _group_mean_matrix(LANES, HEAD_DIM)
    quarter = HEAD_DIM // 4
    n_k = k_ref.shape[0] // n_sub

    @pl.when(pl.program_id(1) == 0)
    def _():
        for sq in range(n_sub):
            k = _group_rms(k_ref[sq * n_k:(sq + 1) * n_k, :], gmat, gk_ref[...])
            v = v_ref[sq * n_k:(sq + 1) * n_k, :]
            if cached:
                k = _rope(k, cosk_ref[...], sink_ref[...], quarter)
                k = jnp.concatenate([ck_ref[...], k], axis=0)
                v = jnp.concatenate([cv_ref[...], v], axis=0)
            else:
                kn_ref[sq * n_k:(sq + 1) * n_k, :] = k
            k_s[sq] = k.astype(BF16)
            v_s[sq] = v.astype(BF16)

    tq = q_ref.shape[0] // n_sub
    lo = _lane((tq, LANES)) < HEAD_DIM
    group = H_B // KV_B
    q_scale = (HEAD_DIM ** -0.5) * math.log2(math.e)
    for sq in range(n_sub):
        k, v = k_s[sq], v_s[sq]
        qs = []
        for j in range(H_B // 2):
            q = _group_rms(q_ref[sq * tq:(sq + 1) * tq, j * LANES:(j + 1) * LANES], gmat, gq_ref[...])
            if cached:
                q = _rope(q, cosq_ref[...], sinq_ref[...], quarter)
            q = q * q_scale
            qs.append((q, pltpu.roll(q, HEAD_DIM, axis=1)))
        outs = [None] * H_B
        for kv in range(KV_B):
            keep = lo if kv == 0 else ~lo
            heads = range(kv * group, (kv + 1) * group)
            q3 = jnp.concatenate([jnp.where(keep, qs[h // 2][0 if h % 2 == kv else 1], 0.0).astype(BF16)
                                  for h in heads], axis=0)
            s = _dot_nt(q3, k)
            p = jnp.exp2(s - jnp.max(s, axis=-1, keepdims=True))
            r = _dot(p.astype(BF16), v) / jnp.sum(p, axis=-1, keepdims=True)
            for i, h in enumerate(heads):
                rh = r[i * tq:(i + 1) * tq]
                outs[h] = rh if h % 2 == kv else pltpu.roll(rh, HEAD_DIM, axis=1)
        for j in range(H_B // 2):
            o_ref[sq * tq:(sq + 1) * tq, j * LANES:(j + 1) * LANES] = jnp.where(lo, outs[2 * j], outs[2 * j + 1])


def gqa_mixer(q, k, v, gq, gk, *, n_seq, seq_len, tok0, cache=None, rope=None, layer=0):
    cached = cache is not None
    n_sub = 1 if cached else 2
    n_seq, seq_len = n_seq // n_sub, seq_len * n_sub
    tq = 256 * n_sub
    nq = seq_len // tq
    b0 = tok0 // seq_len
    q0 = tok0 // tq
    kvw = KV_B * HEAD_DIM
    in_specs = [pl.BlockSpec((tq, B_W), lambda b, i: (q0 + b * nq + i, 0)),
                pl.BlockSpec((seq_len, kvw), lambda b, i: (b0 + b, 0)),
                pl.BlockSpec((seq_len, kvw), lambda b, i: (b0 + b, 0)),
                pl.BlockSpec((1, LANES), lambda b, i: (0, 0)),
                pl.BlockSpec((1, LANES), lambda b, i: (0, 0))]
    args = [q, k, v, gq, gk]
    out_specs = [pl.BlockSpec((tq, B_W), lambda b, i: (b * nq + i, 0))]
    out_shape = [jax.ShapeDtypeStruct((n_seq * seq_len, B_W), F32)]
    if cached:
        cos, sin = rope
        cache_spec = pl.BlockSpec((None, None, PAST_LEN, kvw), lambda b, i: (b, layer, 0, 0))
        in_specs += [cache_spec, cache_spec,
                     pl.BlockSpec((tq, LANES), lambda b, i: (i, 0)),
                     pl.BlockSpec((tq, LANES), lambda b, i: (i, 0)),
                     pl.BlockSpec((seq_len, LANES), lambda b, i: (0, 0)),
                     pl.BlockSpec((seq_len, LANES), lambda b, i: (0, 0))]
        args += [cache[0], cache[1], cos, sin, cos, sin]
    else:
        out_specs.append(pl.BlockSpec((seq_len, kvw), lambda b, i: (b, 0)))
        out_shape.append(jax.ShapeDtypeStruct((n_seq * seq_len, kvw), F32))
    n_keys = seq_len // n_sub + (PAST_LEN if cached else 0)
    return pl.pallas_call(
        functools.partial(_gqa_kernel, cached=cached, n_sub=n_sub),
        grid=(n_seq, nq),
        in_specs=in_specs, out_specs=out_specs, out_shape=out_shape,
        scratch_shapes=[pltpu.VMEM((n_sub, n_keys, kvw), BF16), pltpu.VMEM((n_sub, n_keys, kvw), BF16)],
        compiler_params=_cparams(("arbitrary", "arbitrary")),
        name=f"gqa_mixer_{seq_len}",
    )(*args)


def _diff_kernel(*refs, cached, lam_init, n_sub):
    if cached:
        (q_ref, k_ref, v_ref, lam_ref, gn_ref, ck_ref, cv_ref, cosq_ref, sinq_ref, cosk_ref, sink_ref,
         o_ref) = refs
    else:
        q_ref, k_ref, v_ref, lam_ref, gn_ref, o_ref = refs
    quarter = DIFF_HD // 4
    lv = lam_ref[...]
    lam = (jnp.exp(jnp.sum(lv[0:1] * lv[1:2], axis=-1, keepdims=True))
           - jnp.exp(jnp.sum(lv[2:3] * lv[3:4], axis=-1, keepdims=True)) + lam_init)
    gmat = _group_mean_matrix(C_W, 2 * DIFF_HD)
    tq, n_k = q_ref.shape[0] // n_sub, k_ref.shape[0] // n_sub
    lane = _lane((tq, C_W))
    for sb in range(n_sub):
        q = q_ref[sb * tq:(sb + 1) * tq, :]
        k, v = k_ref[sb * n_k:(sb + 1) * n_k, :], v_ref[sb * n_k:(sb + 1) * n_k, :]
        if cached:
            cq = jnp.concatenate([cosq_ref[...]] * 2, axis=1)
            sq = jnp.concatenate([sinq_ref[...]] * 2, axis=1)
            ck = jnp.concatenate([cosk_ref[...]] * 2, axis=1)
            sk = jnp.concatenate([sink_ref[...]] * 2, axis=1)
            q = _rope(q, cq, sq, quarter)
            k = _rope(k, ck, sk, quarter)
            k = jnp.concatenate([ck_ref[...], k], axis=0)
            v = jnp.concatenate([cv_ref[...], v], axis=0)
        k = k.astype(BF16)
        v = v.astype(BF16)
        q = q * ((DIFF_HD ** -0.5) * math.log2(math.e))
        o = jnp.zeros(q.shape, F32)
        for h in range(H_C):
            ps, cs = [], []
            for m in range(2):
                qm = jnp.where(lane // DIFF_HD == 2 * h + m, q, 0.0).astype(BF16)
                s = _dot_nt(qm, k)
                p = jnp.exp2(s - jnp.max(s, axis=-1, keepdims=True))
                ps.append(p)
                cs.append((1.0 if m == 0 else lam) / jnp.sum(p, axis=-1, keepdims=True))
            a = ps[0] * cs[0] - ps[1] * cs[1]
            o = jnp.where(lane // (2 * DIFF_HD) == h, _dot(a.astype(BF16), v), o)
        o_ref[sb * tq:(sb + 1) * tq, :] = _group_rms(o, gmat, gn_ref[...]) * (1.0 - lam_init)


def diff_mixer(q, k, v, lam_p, gn, lam_init, *, n_seq, seq_len, tok0, cache=None, rope=None, layer=0):
    cached = cache is not None
    n_sub = 1 if cached else 2
    n_seq, seq_len = n_seq // n_sub, seq_len * n_sub
    tq = 256 * n_sub
    nq = seq_len // tq
    b0 = tok0 // seq_len
    q0 = tok0 // tq
    in_specs = [pl.BlockSpec((tq, C_W), lambda b, i: (q0 + b * nq + i, 0)),
                pl.BlockSpec((seq_len, C_W), lambda b, i: (b0 + b, 0)),
                pl.BlockSpec((seq_len, C_W), lambda b, i: (b0 + b, 0)),
                pl.BlockSpec((4, DIFF_HD), lambda b, i: (0, 0)),
                pl.BlockSpec((1, C_W), lambda b, i: (0, 0))]
    args = [q, k, v, lam_p, gn]
    if cached:
        cos, sin = rope
        cache_spec = pl.BlockSpec((None, None, PAST_LEN, C_W), lambda b, i: (b, layer, 0, 0))
        in_specs += [cache_spec, cache_spec,
                     pl.BlockSpec((tq, LANES), lambda b, i: (i, 0)),
                     pl.BlockSpec((tq, LANES), lambda b, i: (i, 0)),
                     pl.BlockSpec((seq_len, LANES), lambda b, i: (0, 0)),
                     pl.BlockSpec((seq_len, LANES), lambda b, i: (0, 0))]
        args += [cache[0], cache[1], cos, sin, cos, sin]
    return pl.pallas_call(
        functools.partial(_diff_kernel, cached=cached, lam_init=lam_init, n_sub=n_sub),
        grid=(n_seq, nq),
        in_specs=in_specs,
        out_specs=pl.BlockSpec((tq, C_W), lambda b, i: (b * nq + i, 0)),
        out_shape=jax.ShapeDtypeStruct((n_seq * seq_len, C_W), F32),
        compiler_params=_cparams(("arbitrary", "arbitrary")),
        name=f"diff_mixer_{seq_len}",
    )(*args)


def _outproj_kernel(*refs, routed, n_mix):
    x_ref = refs[0]
    mix_refs = refs[1:1 + 2 * n_mix]
    rest = refs[1 + 2 * n_mix:]
    if routed:
        w_ref, gm_ref, g_ref, sh_ref, sc_ref, r_ref, x1_ref, h_ref, lg_ref = rest
    else:
        w_ref, gm_ref, g_ref, sh_ref, sc_ref, x1_ref, h_ref = rest
    is_ctx = pl.program_id(0) < N_CTX // x_ref.shape[0]
    mixed = jnp.concatenate(
        [jnp.where(is_ctx, mix_refs[2 * m][...], mix_refs[2 * m + 1][...]) for m in range(n_mix)], axis=1)
    x1 = x_ref[...] + gm_ref[...] * _dot(mixed.astype(BF16), w_ref[...])
    x1_ref[...] = x1
    h = _rms(x1, g_ref[...]) * (1.0 + sc_ref[...]) + sh_ref[...]
    h_ref[...] = h.astype(h_ref.dtype)
    if routed:
        lg_ref[...] = _top2_route(_dot_split(h, r_ref[...]))


def out_projection(x, mixes, w_out_b, mod, norm_g, l, router_p=None):
    tm = 512
    n_ctx_tiles = N_CTX // tm
    n_lat_tiles = N_LAT // tm
    routed = router_p is not None
    mod_spec = lambda k: pl.BlockSpec((None, None, 1, D_MODEL), lambda i: (l, _cond_row(i, tm), 0, k))
    in_specs = [pl.BlockSpec((tm, D_MODEL), lambda i: (i, 0))]
    args = [x]
    for ctx_arr, lat_arr in mixes:
        w = ctx_arr.shape[1]
        in_specs.append(pl.BlockSpec((tm, w), lambda i: (jnp.minimum(i, n_ctx_tiles - 1), 0)))
        in_specs.append(pl.BlockSpec((tm, w), lambda i: (jnp.clip(i - n_ctx_tiles, 0, n_lat_tiles - 1), 0)))
        args += [ctx_arr, lat_arr]
    in_specs += [pl.BlockSpec((None, D_MODEL, D_MODEL), lambda i: (l, 0, 0)),
                 mod_spec(2),
                 pl.BlockSpec((None, 1, D_MODEL), lambda i: (l, 0, 0)),
                 mod_spec(3), mod_spec(4)]
    args += [w_out_b, mod, norm_g, mod, mod]
    out_specs = [pl.BlockSpec((tm, D_MODEL), lambda i: (i, 0)),
                 pl.BlockSpec((tm, D_MODEL), lambda i: (i, 0))]
    out_shape = [jax.ShapeDtypeStruct((N_TOK, D_MODEL), F32),
                 jax.ShapeDtypeStruct((N_TOK, D_MODEL), F32 if routed else BF16)]
    if routed:
        in_specs.append(pl.BlockSpec((D_MODEL, LANES), lambda i: (0, 0)))
        args.append(router_p)
        out_specs.append(pl.BlockSpec((tm, LANES), lambda i: (i, 0)))
        out_shape.append(jax.ShapeDtypeStruct((N_TOK, LANES), F32))
    return pl.pallas_call(
        functools.partial(_outproj_kernel, routed=routed, n_mix=len(mixes)),
        grid=(N_TOK // tm,),
        in_specs=in_specs, out_specs=out_specs, out_shape=out_shape,
        compiler_params=_cparams(("arbitrary",)),
        name="out_projection",
    )(*args)


def _ffn_kernel(h_ref, wg_ref, wu_ref, wd_ref, x_ref, gf_ref, o_ref, acc_ref):
    j = pl.program_id(1)

    @pl.when(j == 0)
    def _():
        acc_ref[...] = jnp.zeros_like(acc_ref)

    h = h_ref[...]
    act = _silu(_dot(h, wg_ref[...])) * _dot(h, wu_ref[...])
    acc_ref[...] += _dot(act.astype(BF16), wd_ref[...])

    @pl.when(j == pl.num_programs(1) - 1)
    def _():
        o_ref[...] = x_ref[...] + gf_ref[...] * acc_ref[...]


def ffn(h, w_gu, w_d, x1, mod, l):
    tm = 512
    lyr = l // 2
    n_chunk = D_FF // FF_CHUNK
    gu_spec = lambda half: pl.BlockSpec((None, D_MODEL, FF_CHUNK), lambda i, j: (lyr, 0, half * n_chunk + j))
    return pl.pallas_call(
        _ffn_kernel,
        grid=(N_TOK // tm, n_chunk),
        in_specs=[pl.BlockSpec((tm, D_MODEL), lambda i, j: (i, 0)),
                  gu_spec(0), gu_spec(1),
                  pl.BlockSpec((None, FF_CHUNK, D_MODEL), lambda i, j: (lyr, j, 0)),
                  pl.BlockSpec((tm, D_MODEL), lambda i, j: (i, 0)),
                  pl.BlockSpec((None, None, 1, D_MODEL), lambda i, j: (l, _cond_row(i, tm), 0, 5))],
        out_specs=pl.BlockSpec((tm, D_MODEL), lambda i, j: (i, 0)),
        out_shape=jax.ShapeDtypeStruct((N_TOK, D_MODEL), F32),
        scratch_shapes=[pltpu.VMEM((tm, D_MODEL), F32)],
        compiler_params=_cparams(("arbitrary", "arbitrary")),
        name="ffn_dense",
    )(h, w_gu, w_gu, w_d, x1, mod)


MOE_TILE = 256
MOE_SLOTS = TOP_K * N_TOK + N_EXP * MOE_TILE
MOE_TILES = MOE_SLOTS // MOE_TILE
ROUTE_I1, ROUTE_I2, ROUTE_W1, ROUTE_W2 = 0, 1, 2, 3


def _top2_route(logits):
    lane = _lane(logits.shape)
    neg = jnp.float32(-jnp.inf)
    lg = jnp.where(lane < N_EXP, logits, neg)
    t1 = jnp.max(lg, axis=-1, keepdims=True)
    i1 = jnp.min(jnp.where(lg == t1, lane, LANES), axis=-1, keepdims=True)
    lg2 = jnp.where(lane == i1, neg, lg)
    t2 = jnp.max(lg2, axis=-1, keepdims=True)
    i2 = jnp.min(jnp.where(lg2 == t2, lane, LANES), axis=-1, keepdims=True)
    e2 = jnp.exp(t2 - t1)
    w1 = 1.0 / (1.0 + e2)
    w2 = e2 / (1.0 + e2)
    rec = jnp.where(lane == ROUTE_I1, i1.astype(F32), 0.0) + jnp.where(lane == ROUTE_I2, i2.astype(F32), 0.0)
    return rec + jnp.where(lane == ROUTE_W1, w1, 0.0) + jnp.where(lane == ROUTE_W2, w2, 0.0)


def _routing_tables(route):
    n_assign = TOP_K * N_TOK
    expert = jnp.concatenate([route[:, ROUTE_I1], route[:, ROUTE_I2]]).astype(jnp.int32)
    index = jnp.arange(n_assign, dtype=jnp.int32)
    onehot = (expert[:, None] == jnp.arange(N_EXP, dtype=jnp.int32)[None, :]).astype(jnp.int32)
    running = jnp.cumsum(onehot, axis=0)
    counts = running[-1]
    rank = jnp.sum(running * onehot, axis=1) - 1
    padded = (counts + MOE_TILE - 1) // MOE_TILE * MOE_TILE
    ends = jnp.cumsum(padded)
    starts = ends - padded
    slot = jnp.sum(onehot * starts[None, :], axis=1) + rank
    sorted_token = (jnp.sort(expert * n_assign + index) % n_assign) % N_TOK
    tile_start = jnp.arange(MOE_TILES, dtype=jnp.int32) * MOE_TILE
    tile_expert = jnp.minimum(jnp.sum(tile_start[:, None] >= ends[None, :], axis=1), N_EXP - 1).astype(jnp.int32)
    first_sorted = (jnp.cumsum(counts) - counts)[tile_expert] + tile_start - starts[tile_expert]
    n_used = (ends[-1] // MOE_TILE).astype(jnp.int32).reshape(1)
    return sorted_token, slot, tile_expert, first_sorted.astype(jnp.int32), n_used


def _row_copy(src_hbm, row, dst, dst_row, sem):
    return pltpu.make_async_copy(src_hbm.at[pl.ds(row, 1)], dst.at[pl.ds(dst_row, 1)], sem)


def _experts_kernel(tok_ref, exp_ref, first_ref, used_ref, h_hbm, wg_ref, wu_ref, wd_ref, y_ref, x_buf, sems,
                    wg_s, wu_s, wd_s):
    t = pl.program_id(0)
    n_used = used_ref[0]

    def start_gather(tile, half):
        first = first_ref[tile]
        for r in range(MOE_TILE):
            tok = tok_ref[jnp.minimum(first + r, TOP_K * N_TOK - 1)]
            _row_copy(h_hbm, tok, x_buf.at[half], r, sems.at[half]).start(priority=r % 2)

    @pl.when(t == 0)
    def _():
        start_gather(0, 0)

    @pl.when(t + 1 < n_used)
    def _():
        start_gather(t + 1, (t + 1) % 2)

    @pl.when(t < n_used)
    def _():
        half = t % 2
        for r in range(MOE_TILE):
            _row_copy(h_hbm, 0, x_buf.at[half], r, sems.at[half]).wait()
        @pl.when((t == 0) | (exp_ref[t] != exp_ref[jnp.maximum(t - 1, 0)]))
        def _():
            wg_s[...] = wg_ref[...].astype(BF16)
            wu_s[...] = wu_ref[...].astype(BF16)
            wd_s[...] = wd_ref[...].astype(BF16)

        x = x_buf[half].astype(BF16)
        act = _silu(_dot(x, wg_s[...])) * _dot(x, wu_s[...])
        y_ref[...] = _dot(act.astype(BF16), wd_s[...])

    @pl.when(t >= n_used)
    def _():
        y_ref[...] = jnp.zeros_like(y_ref)


def moe_experts(h, route, w_gu, w_d, l):
    lyr = l // 2
    sorted_token, slot, tile_expert, first_sorted, n_used = _routing_tables(route)
    gu_spec = lambda half: pl.BlockSpec((None, None, D_MODEL, FF_CHUNK),
                                        lambda t, tok, exp, first, used: (lyr, exp[t], 0, half))
    y = pl.pallas_call(
        _experts_kernel,
        grid_spec=pltpu.PrefetchScalarGridSpec(
            num_scalar_prefetch=4,
            grid=(MOE_TILES,),
            in_specs=[pl.BlockSpec(memory_space=pl.ANY),
                      gu_spec(0), gu_spec(1),
                      pl.BlockSpec((None, None, FF_CHUNK, D_MODEL),
                                   lambda t, tok, exp, first, used: (lyr, exp[t], 0, 0))],
            out_specs=pl.BlockSpec((MOE_TILE, D_MODEL), lambda t, tok, exp, first, used: (t, 0)),
            scratch_shapes=[pltpu.VMEM((2, MOE_TILE, D_MODEL), F32), pltpu.SemaphoreType.DMA((2,)),
                            pltpu.VMEM((D_MODEL, FF_CHUNK), BF16), pltpu.VMEM((D_MODEL, FF_CHUNK), BF16),
                            pltpu.VMEM((FF_CHUNK, D_MODEL), BF16)]),
        out_shape=jax.ShapeDtypeStruct((MOE_SLOTS, D_MODEL), F32),
        compiler_params=_cparams(("arbitrary",)),
        name="moe_experts",
    )(sorted_token, tile_expert, first_sorted, n_used, h, w_gu, w_gu, w_d)
    return y, slot


def _moe_combine_kernel(slot_ref, y_hbm, x_ref, gf_ref, route_ref, o_ref, y_buf, sems):
    i = pl.program_id(0)
    tm = x_ref.shape[0]

    def start_gather(tile, half):
        for k in range(TOP_K):
            for r in range(tm):
                _row_copy(y_hbm, slot_ref[k * N_TOK + tile * tm + r], y_buf.at[half, k], r,
                          sems.at[half]).start(priority=r % 2)

    @pl.when(i == 0)
    def _():
        start_gather(0, 0)

    @pl.when(i + 1 < pl.num_programs(0))
    def _():
        start_gather(i + 1, (i + 1) % 2)

    half = i % 2
    for k in range(TOP_K):
        for r in range(tm):
            _row_copy(y_hbm, 0, y_buf.at[half, k], r, sems.at[half]).wait()
    route = route_ref[...]
    f = (route[:, ROUTE_W1:ROUTE_W1 + 1] * y_buf[half, 0] + route[:, ROUTE_W2:ROUTE_W2 + 1] * y_buf[half, 1])
    o_ref[...] = x_ref[...] + gf_ref[...] * f


def moe_combine(y, slot, x1, mod, route, l):
    tm = 256
    return pl.pallas_call(
        _moe_combine_kernel,
        grid_spec=pltpu.PrefetchScalarGridSpec(
            num_scalar_prefetch=1,
            grid=(N_TOK // tm,),
            in_specs=[pl.BlockSpec(memory_space=pl.ANY),
                      pl.BlockSpec((tm, D_MODEL), lambda i, s: (i, 0)),
                      pl.BlockSpec((None, None, 1, D_MODEL), lambda i, s: (l, _cond_row(i, tm), 0, 5)),
                      pl.BlockSpec((tm, LANES), lambda i, s: (i, 0))],
            out_specs=pl.BlockSpec((tm, D_MODEL), lambda i, s: (i, 0)),
            scratch_shapes=[pltpu.VMEM((2, TOP_K, tm, D_MODEL), F32), pltpu.SemaphoreType.DMA((2,))]),
        out_shape=jax.ShapeDtypeStruct((N_TOK, D_MODEL), F32),
        compiler_params=_cparams(("arbitrary",)),
        name="moe_combine",
    )(slot, y, x1, mod, route)


def _final_norm_kernel(x_ref, g_ref, o_ref):
    o_ref[...] = _rms(x_ref[...], g_ref[...])


def final_norm_call(x, g, tok0, n_tok):
    tm = 1024
    i0 = tok0 // tm
    return pl.pallas_call(
        _final_norm_kernel,
        grid=(n_tok // tm,),
        in_specs=[pl.BlockSpec((tm, D_MODEL), lambda i: (i0 + i, 0)),
                  pl.BlockSpec((1, D_MODEL), lambda i: (0, 0))],
        out_specs=pl.BlockSpec((tm, D_MODEL), lambda i: (i, 0)),
        out_shape=jax.ShapeDtypeStruct((n_tok, D_MODEL), F32),
        compiler_params=_cparams(("arbitrary",)),
        name="final_norm",
    )(x, g)


def _cache_store_kernel(*refs):
    n = len(refs) // 3
    for src, dst in zip(refs[:n], refs[2 * n:]):
        dst[...] = src[...].reshape(dst.shape)


def cache_store(sources, all_layers, l):
    n_sq = 2
    tm = n_sq * SEQ
    return pl.pallas_call(
        _cache_store_kernel,
        grid=(N_CTX // tm,),
        in_specs=([pl.BlockSpec((tm, s.shape[1]), lambda i: (i, 0)) for s in sources]
                  + [pl.BlockSpec(memory_space=pl.ANY)] * len(all_layers)),
        out_specs=[pl.BlockSpec((n_sq, None, SEQ, a.shape[-1]), lambda i: (i, l, 0, 0)) for a in all_layers],
        out_shape=[jax.ShapeDtypeStruct(a.shape, F32) for a in all_layers],
        input_output_aliases={len(sources) + k: k for k in range(len(all_layers))},
        compiler_params=_cparams(("arbitrary",)),
        name="cache_store",
    )(*sources, *all_layers)


def _gate_row(p):
    return jnp.zeros((1, LANES), F32).at[0, GATE_A:GATE_A + 2 * H_A].set(p.reshape(-1))


def _tile_lanes(g, width):
    return jnp.tile(g, width // g.shape[0])[None, :]


def kernel(x_prompt, x_sample, state_delta, cache_gqa_k, cache_gqa_v, cache_diff_k, cache_diff_v, c, c_ctx,
           ada_w, ada_b, norm_mix, norm_ffn, w_in, conv_w, dn_a_log, dn_dt_bias, dn_norm, gqa_q_norm,
           gqa_k_norm, diff_lambda, diff_norm, w_out, ffn_w_gu, ffn_w_down, moe_router, moe_w_gu, moe_w_down,
           final_norm):
    cond8 = jnp.concatenate([c_ctx[None, :], c, jnp.zeros((8 - N_COND, D_MODEL), F32)], axis=0)
    w_out_b = w_out.astype(BF16)
    norm_mix3 = norm_mix.reshape(DEPTH, 1, D_MODEL)
    norm_ffn3 = norm_ffn.reshape(DEPTH, 1, D_MODEL)
    ffn_gu_b, ffn_d_b = ffn_w_gu.astype(BF16), ffn_w_down.astype(BF16)
    router_p = jnp.pad(moe_router, ((0, 0), (0, 0), (0, LANES - N_EXP)))
    rope_b = _rope_tables(DEC_SEQ // GRID_W, HEAD_DIM)
    rope_c = _rope_tables(DEC_SEQ // GRID_W, DIFF_HD)
    s0_ctx = jnp.zeros((BATCH, 2, H_A, HEAD_DIM, HEAD_DIM), F32)
    cache_b = (cache_gqa_k.reshape(DEC_BATCH, DEPTH, PAST_LEN, KV_B * HEAD_DIM),
               cache_gqa_v.reshape(DEC_BATCH, DEPTH, PAST_LEN, KV_B * HEAD_DIM))
    cache_c = (cache_diff_k.reshape(DEC_BATCH, DEPTH, PAST_LEN, C_W),
               cache_diff_v.reshape(DEC_BATCH, DEPTH, PAST_LEN, C_W))

    mod = ada_modulation(cond8, ada_w, ada_b)[:, :N_COND].reshape(DEPTH, N_COND, 1, 6 * D_MODEL)

    x = jnp.concatenate([x_prompt.reshape(N_CTX, D_MODEL), x_sample.reshape(N_LAT, D_MODEL)], axis=0)
    all_states = jnp.zeros((BATCH, DEPTH, 2, H_A, HEAD_DIM, HEAD_DIM), F32)
    ctx_layers = tuple(jnp.zeros((BATCH, DEPTH, SEQ, w), F32)
                       for w in (KV_B * HEAD_DIM, KV_B * HEAD_DIM, C_W, C_W))
    ctx = dict(n_seq=BATCH, seq_len=SEQ, tok0=0)
    lat = dict(n_seq=DEC_BATCH, seq_len=DEC_SEQ, tok0=N_CTX)
    for l in range(DEPTH):
        qkv_a, z_a, q_b, k_b, v_b, q_c, k_c, v_c, gate = in_projection(x, mod, norm_mix3, w_in, l)

        dn_args = (qkv_a, z_a, gate, conv_w[l], _gate_row(dn_a_log[l]), _gate_row(dn_dt_bias[l]),
                   _tile_lanes(dn_norm[l], LANES))
        oa_ctx, all_states = delta_mixer(*dn_args, s0_ctx, n_sub=2, all_states=all_states, layer=l, **ctx)
        oa_lat, _ = delta_mixer(*dn_args, state_delta[:, l], n_sub=1, **lat)

        gq, gk = _tile_lanes(gqa_q_norm[l], LANES), _tile_lanes(gqa_k_norm[l], LANES)
        ob_ctx, kn_ctx = gqa_mixer(q_b, k_b, v_b, gq, gk, **ctx)
        (ob_lat,) = gqa_mixer(q_b, k_b, v_b, gq, gk, cache=cache_b, rope=rope_b, layer=l, **lat)

        lam_init = 0.8 - 0.6 * math.exp(-0.3 * l)
        gn = _tile_lanes(diff_norm[l], C_W)
        oc_ctx = diff_mixer(q_c, k_c, v_c, diff_lambda[l], gn, lam_init, **ctx)
        oc_lat = diff_mixer(q_c, k_c, v_c, diff_lambda[l], gn, lam_init, cache=cache_c, rope=rope_c, layer=l, **lat)

        mixes = ((oa_ctx, oa_lat), (ob_ctx, ob_lat), (oc_ctx, oc_lat))
        if l % 2 == 0:
            x1, h2 = out_projection(x, mixes, w_out_b, mod, norm_ffn3, l)
            x = ffn(h2, ffn_gu_b, ffn_d_b, x1, mod, l)
        else:
            x1, h2, route = out_projection(x, mixes, w_out_b, mod, norm_ffn3, l, router_p[l // 2])
            y_slots, slot = moe_experts(h2, route, moe_w_gu, moe_w_down, l)
            x = moe_combine(y_slots, slot, x1, mod, route, l)

        ctx_layers = cache_store((kn_ctx, v_b, k_c, v_c), ctx_layers, l)

    y_prompt = final_norm_call(x, final_norm[None, :], 0, N_CTX).reshape(BATCH, SEQ, D_MODEL)
    y_sample = final_norm_call(x, final_norm[None, :], N_CTX, N_LAT).reshape(DEC_BATCH, DEC_SEQ, D_MODEL)
    new_gk, new_gv, new_dk, new_dv = ctx_layers
    return (y_prompt, y_sample, all_states,
            new_gk.reshape(BATCH, DEPTH, SEQ, KV_B, HEAD_DIM), new_gv.reshape(BATCH, DEPTH, SEQ, KV_B, HEAD_DIM),
            new_dk.reshape(BATCH, DEPTH, SEQ, H_C, 2, DIFF_HD), new_dv.reshape(BATCH, DEPTH, SEQ, H_C, 2 * DIFF_HD))
```

```python
import functools
import math

import jax
import jax.numpy as jnp
from jax import lax
from jax.experimental import pallas as pl
from jax.experimental.pallas import tpu as pltpu

D_MODEL = 1024
BATCH = 16
SEQ = 256
DEPTH = 4
DEC_BATCH = 2
DEC_SEQ = 1024
PAST_LEN = 512
GRID_W = 64
HEAD_DIM = 64
H_A = 6
A_W = H_A * HEAD_DIM
H_B = 6
KV_B = 2
B_W = H_B * HEAD_DIM
H_C = 4
DIFF_HD = 32
C_W = H_C * 2 * DIFF_HD
CHUNK = 64
ROPE_THETA = 10000.0
D_FF = 2816
N_EXP = 8
TOP_K = 2
D_FF_E = 1408
EPS = 1e-6

N_CTX = BATCH * SEQ
N_LAT = DEC_BATCH * DEC_SEQ
N_TOK = N_CTX + N_LAT
N_COND = 1 + DEC_BATCH
IN_PAD = 3072
FF_CHUNK = D_FF_E
LANES = 128
VMEM_LIMIT = 56 * 1024 * 1024

IN_SPLIT = (("qkv_a", 3 * A_W), ("z_a", A_W), ("q_b", B_W), ("k_b", KV_B * HEAD_DIM), ("v_b", KV_B * HEAD_DIM),
            ("q_c", C_W), ("k_c", C_W), ("v_c", C_W), ("gate", LANES))
GATE_BETA, GATE_A = 0, 2 * H_A

F32 = jnp.float32
BF16 = jnp.bfloat16


def _cparams(sem):
    return pltpu.CompilerParams(dimension_semantics=sem, vmem_limit_bytes=VMEM_LIMIT)


def _dot(a, b):
    return jnp.dot(a, b, preferred_element_type=F32)


def _dot_split(a, b):
    a_hi, b_hi = a.astype(BF16), b.astype(BF16)
    a_lo = (a - a_hi.astype(F32)).astype(BF16)
    b_lo = (b - b_hi.astype(F32)).astype(BF16)
    return _dot(jnp.concatenate([a_hi, a_lo, a_hi], axis=1), jnp.concatenate([b_hi, b_hi, b_lo], axis=0))


def _dot_nt(a, b):
    return lax.dot_general(a, b, (((1,), (1,)), ((), ())), preferred_element_type=F32)


def _silu(x):
    return x * jax.nn.sigmoid(x)


def _softplus(x):
    return jnp.maximum(x, 0.0) + jnp.log1p(jnp.exp(-jnp.abs(x)))


def _rms(x, g):
    return x * lax.rsqrt(jnp.mean(x * x, axis=-1, keepdims=True) + EPS) * g


def _lane(shape):
    return lax.broadcasted_iota(jnp.int32, shape, len(shape) - 1)


def _group_mean_matrix(width, group):
    i = lax.broadcasted_iota(jnp.int32, (width, width), 0)
    j = lax.broadcasted_iota(jnp.int32, (width, width), 1)
    return jnp.where(i // group == j // group, 1.0 / group, 0.0).astype(BF16)


def _group_rms(x, gmat, g):
    sq = x * x
    hi = sq.astype(BF16)
    lo = (sq - hi.astype(F32)).astype(BF16)
    ms = _dot(jnp.concatenate([hi, lo], axis=1), jnp.concatenate([gmat, gmat], axis=0))
    return x * lax.rsqrt(ms + EPS) * g


def _cond_row(i, tm):
    return jnp.maximum((i * tm) // DEC_SEQ - (N_CTX // DEC_SEQ - 1), 0)


def _ada_kernel(c_ref, w_ref, b_ref, o_ref):
    o_ref[...] = _dot_split(_silu(c_ref[...]), w_ref[...]) + b_ref[...]


def ada_modulation(cond8, ada_w, ada_b):
    tn = 1024
    return pl.pallas_call(
        _ada_kernel,
        grid=(DEPTH, 6 * D_MODEL // tn),
        in_specs=[pl.BlockSpec((8, D_MODEL), lambda l, j: (0, 0)),
                  pl.BlockSpec((None, D_MODEL, tn), lambda l, j: (l, 0, j)),
                  pl.BlockSpec((None, 1, tn), lambda l, j: (l, 0, j))],
        out_specs=pl.BlockSpec((None, 8, tn), lambda l, j: (l, 0, j)),
        out_shape=jax.ShapeDtypeStruct((DEPTH, 8, 6 * D_MODEL), F32),
        compiler_params=_cparams(("arbitrary", "arbitrary")),
        name="ada_modulation",
    )(cond8, ada_w, ada_b.reshape(DEPTH, 1, 6 * D_MODEL))


GATE_LO = 4 * A_W
GATE_HI = GATE_LO + 4 * H_A
IN_COLS = GATE_HI + B_W + 2 * KV_B * HEAD_DIM + 3 * C_W


def _inproj_kernel(x_ref, g_ref, sh_ref, sc_ref, w_ref, *refs):
    o_refs, w_s = refs[:-1], refs[-1]

    @pl.when(pl.program_id(0) == 0)
    def _():
        w_s[:, :GATE_LO] = w_ref[:, :GATE_LO].astype(BF16)
        w_s[:, GATE_LO:IN_COLS - 4 * H_A] = w_ref[:, GATE_HI:].astype(BF16)
        tail = jnp.concatenate([w_ref[:, GATE_LO:GATE_HI], jnp.zeros((D_MODEL, LANES - 4 * H_A), F32)], axis=1)
        w_s[:, IN_COLS - 4 * H_A:] = tail.astype(BF16)

    h = _rms(x_ref[...], g_ref[...]) * (1.0 + sc_ref[...]) + sh_ref[...]
    acc = _dot(h.astype(BF16), w_s[...])
    off = 0
    for o_ref, (_, width) in zip(o_refs, IN_SPLIT):
        o_ref[...] = acc[:, off:off + width]
        off += width


def in_projection(x, mod, norm_g, w_in, l):
    tm = 512
    return pl.pallas_call(
        _inproj_kernel,
        grid=(N_TOK // tm,),
        in_specs=[pl.BlockSpec((tm, D_MODEL), lambda i: (i, 0)),
                  pl.BlockSpec((None, 1, D_MODEL), lambda i: (l, 0, 0)),
                  pl.BlockSpec((None, None, 1, D_MODEL), lambda i: (l, _cond_row(i, tm), 0, 0)),
                  pl.BlockSpec((None, None, 1, D_MODEL), lambda i: (l, _cond_row(i, tm), 0, 1)),
                  pl.BlockSpec((None, D_MODEL, IN_COLS), lambda i: (l, 0, 0))],
        out_specs=[pl.BlockSpec((tm, w), lambda i: (i, 0)) for _, w in IN_SPLIT],
        out_shape=[jax.ShapeDtypeStruct((N_TOK, w), F32) for _, w in IN_SPLIT],
        scratch_shapes=[pltpu.VMEM((D_MODEL, IN_PAD), BF16)],
        compiler_params=_cparams(("arbitrary",)),
        name="in_projection",
    )(x, norm_g, mod, mod, w_in)


SOLVE_BLOCK = 16


def _delta_kernel(*refs, seq_len, n_sub, n_cp, aliased):
    qkv_ref, z_ref, gate_ref, cw_ref, alog_ref, dtb_ref, ng_ref, s0_ref = refs[:8]
    (o_ref, sfin_ref, pad_s, kk_s, qq_s, vk_s, gc_s, beta_s, u_s, w_s, qd_s, a_s, kdt_s, egt_s, sa_s, o_s,
     tp_s, t32_s, rp_s, c32_s, m_s, x_s, vbd_s, v_s, op_s) = refs[9 if aliased else 8:]
    n_chunk = seq_len // CHUNK
    n_pair = H_A // 2
    n_unit = n_sub * H_A
    half = HEAD_DIM
    lo = _lane((CHUNK, LANES)) < half
    lo_row = _lane((1, LANES)) < half
    lo16 = _lane((CHUNK, LANES)).astype(F32).astype(BF16) < half
    row = lax.broadcasted_iota(jnp.int32, (CHUNK, LANES), 0)
    col = _lane((CHUNK, LANES)) % half
    ahead = jnp.where(lo, row - col, col - row)
    incl = ahead >= 0
    strict = ahead > 0
    diag_blk = row // SOLVE_BLOCK == col // SOLVE_BLOCK
    big_row = lax.broadcasted_iota(jnp.int32, (LANES, LANES), 0)
    anti = big_row // half + _lane((LANES, LANES)) // half == 1
    eye = (big_row == _lane((LANES, LANES))).astype(F32)

    def both(a):
        return jnp.concatenate([jnp.where(lo, a, 0.0), jnp.where(lo, 0.0, a)], axis=0)

    rb = seq_len
    lo_rb = _lane((rb, LANES)) < half
    zero8 = jnp.zeros((8, 3 * A_W), F32)
    pad_s[0:8, :] = zero8
    pad_s[8 + seq_len:16 + seq_len, :] = zero8

    def l2n(x):
        return x * lax.rsqrt(0.5 * jnp.sum(x * x, axis=-1, keepdims=True) + EPS)

    for sq in range(n_sub):
        pad_s[8:8 + seq_len, :] = qkv_ref[sq * seq_len:(sq + 1) * seq_len, :]
        for r0 in range(0, seq_len, rb):
            y = _silu(pad_s[7 + r0:7 + r0 + rb, :] * cw_ref[0:1, :] + pad_s[8 + r0:8 + r0 + rb, :] * cw_ref[1:2, :]
                      + pad_s[9 + r0:9 + r0 + rb, :] * cw_ref[2:3, :])

            def dup(part, p):
                c = y[:, part * A_W + p * LANES:part * A_W + (p + 1) * LANES]
                r = pltpu.roll(c, half, axis=1)
                return jnp.where(lo_rb, c, r), jnp.where(lo_rb, r, c)

            for p in range(n_pair):
                qs, ks, vs = dup(0, p), dup(1, p), dup(2, p)
                for e in range(2):
                    u = sq * H_A + 2 * p + e
                    kk = l2n(ks[e])
                    kk_s[u, r0:r0 + rb, :] = kk
                    qq_s[u, r0:r0 + rb, :] = l2n(qs[e]) * (HEAD_DIM ** -0.5)
                    vk_s[u, r0:r0 + rb, :] = jnp.where(lo_rb, vs[e], kk)

    cb = 256
    ci = lax.broadcasted_iota(jnp.int32, (cb, cb), 0)
    cj = lax.broadcasted_iota(jnp.int32, (cb, cb), 1)
    same = ci // CHUNK == cj // CHUNK
    m_pre = jnp.where(same & (ci >= cj), 1.0, 0.0).astype(BF16)
    m_suf = jnp.where(same & (ci <= cj), 1.0, 0.0).astype(BF16)
    fwd_cols = (_lane((cb, LANES)) >= GATE_A) & (_lane((cb, LANES)) < GATE_A + H_A)
    for r0 in range(0, n_sub * seq_len, cb):
        gt = gate_ref[r0:r0 + cb, :]
        beta_s[r0:r0 + cb, :] = jax.nn.sigmoid(gt)
        g = -jnp.exp(alog_ref[...]) * _softplus(gt + dtb_ref[...])
        g1 = g.astype(BF16)
        r1 = g - g1.astype(F32)
        g2 = r1.astype(BF16)
        g3 = (r1 - g2.astype(F32)).astype(BF16)
        blk = jnp.concatenate([g1, g2, g3], axis=1)
        pre, suf = _dot(m_pre, blk), _dot(m_suf, blk)
        pre = pre[:, :LANES] + pre[:, LANES:2 * LANES] + pre[:, 2 * LANES:]
        suf = suf[:, :LANES] + suf[:, LANES:2 * LANES] + suf[:, 2 * LANES:]
        gc_s[r0:r0 + cb, :] = jnp.where(fwd_cols, pre, suf)

    zero = jnp.zeros((half, half), F32)
    for sq in range(n_sub):
        for h in range(H_A):
            sa_s[sq * H_A + h] = jnp.concatenate([jnp.concatenate([zero, s0_ref[sq, 1, h]], axis=1),
                                                  jnp.concatenate([s0_ref[sq, 0, h], zero], axis=1)], axis=0)

    n_fac = int(math.log2(SOLVE_BLOCK))

    def fill_stage(c, rows, cc):
        for sq in range(n_sub):
            grow = pl.ds(pl.multiple_of(sq * seq_len + c * CHUNK, CHUNK), CHUNK)
            gcb = gc_s[grow, :]
            gct = gcb.T
            bb = beta_s[grow, :]
            for h in range(H_A):
                u = sq * H_A + h
                w = cc * n_unit + u
                full = (CHUNK, LANES)
                cff = jnp.broadcast_to(gcb[:, GATE_A + h:GATE_A + h + 1], full)
                cbb = jnp.broadcast_to(gcb[:, GATE_A + H_A + h:GATE_A + H_A + h + 1], full)
                bff = jnp.broadcast_to(bb[:, GATE_BETA + h:GATE_BETA + h + 1], full)
                bbb = jnp.broadcast_to(bb[:, GATE_BETA + H_A + h:GATE_BETA + H_A + h + 1], full)
                gr_fb = jnp.concatenate([gct[GATE_A + h:GATE_A + h + 1, :],
                                         gct[GATE_A + H_A + h:GATE_A + H_A + h + 1, :]], axis=1)
                decay = jnp.where(incl, jnp.exp(jnp.where(incl, jnp.where(lo, cff, cbb) - gr_fb, 0.0)), 0.0)
                egf, egb = jnp.exp(cff), jnp.exp(cbb)
                tot_f = gcb[CHUNK - 1:CHUNK, GATE_A + h:GATE_A + h + 1]
                tot_b = gcb[0:1, GATE_A + H_A + h:GATE_A + H_A + h + 1]
                kk, qq, vk = kk_s[u, rows, :], qq_s[u, rows, :], vk_s[u, rows, :]
                k16 = kk[:, :half].astype(BF16)
                kq = jnp.concatenate([k16, qq[:, :half].astype(BF16)], axis=0)
                gram = _dot_nt(kq, jnp.concatenate([k16, k16], axis=0))
                neg_l = jnp.where(strict, gram[:CHUNK] * jnp.where(lo, -bff, -bbb) * decay, 0.0)
                a_s[u, rows, :] = jnp.where(incl, gram[CHUNK:] * decay, 0.0).astype(BF16)
                neg_d = jnp.where(diag_blk, neg_l, 0.0)
                pd = both(neg_d)
                t32_s[w] = eye + pd
                tp_s[w, :, :LANES] = (eye + pd).astype(BF16)
                tp_s[w, :, LANES:] = pd.astype(BF16)
                x_f = vk * jnp.where(lo, bff, bff * egf)
                x_b = vk * jnp.where(lo, bbb, bbb * egb)
                rp_s[w, :, :LANES] = jnp.concatenate([x_f, x_b], axis=0).astype(BF16)
                rp_s[w, :, LANES:] = both(neg_l - neg_d).astype(BF16)
                qd_s[u, rows, :] = (qq * jnp.where(lo, egb, egf)).astype(BF16)
                kd = kk * jnp.exp(jnp.where(lo, tot_b - cbb, tot_f - cff))
                kdt_s[u, c] = kd.T.astype(BF16)
                egt_s[u, c] = jnp.exp(jnp.where(lo_row, tot_f, tot_b))

    def prep(ci, carry):
        rows_of = []
        for cc in range(n_cp):
            c = ci * n_cp + cc
            rows_of.append(pl.ds(pl.multiple_of(c * CHUNK, CHUNK), CHUNK))
            fill_stage(c, rows_of[cc], cc)
        n_slot = n_cp * n_unit
        for w in range(n_slot):
            tp_s[w, :, LANES:] = _dot(tp_s[w, :, LANES:], tp_s[w, :, LANES:]).astype(BF16)
        for j in range(1, n_fac):
            for w in range(n_slot):
                if j + 1 < n_fac:
                    tp = _dot(tp_s[w, :, LANES:], tp_s[w])
                    t = t32_s[w] + tp[:, :LANES]
                    t32_s[w] = t
                    tp_s[w, :, :LANES] = t.astype(BF16)
                    tp_s[w, :, LANES:] = tp[:, LANES:].astype(BF16)
                else:
                    t = t32_s[w] + _dot(tp_s[w, :, LANES:], tp_s[w, :, :LANES])
                    tp_s[w, :, :LANES] = t.astype(BF16)
        for w in range(n_slot):
            cm = _dot(tp_s[w, :, :LANES], rp_s[w])
            c32_s[w] = cm[:, :LANES]
            rp_s[w] = cm.astype(BF16)
        assert CHUNK // SOLVE_BLOCK == 4
        for w in range(n_slot):
            mm = _dot(rp_s[w, :, LANES:], rp_s[w])
            y1 = c32_s[w] + mm[:, :LANES]
            c32_s[w] = y1
            x_s[w] = y1.astype(BF16)
            m_s[w] = mm[:, LANES:].astype(BF16)
        for w in range(n_slot):
            x = c32_s[w] + _dot(m_s[w], x_s[w])
            x_f, x_b = x[:CHUNK], pltpu.roll(x[CHUNK:], half, axis=1)
            u_s[w % n_unit, rows_of[w // n_unit], :] = jnp.where(lo, x_f, x_b)
            w_s[w % n_unit, rows_of[w // n_unit], :] = jnp.where(lo, x_b, x_f).astype(BF16)
        return carry

    lax.fori_loop(0, n_chunk // n_cp, prep, 0)

    def scan(s, carry):
        rf = pl.ds(pl.multiple_of(s * CHUNK, CHUNK), CHUNK)
        sb = n_chunk - 1 - s
        rbk = pl.ds(pl.multiple_of(sb * CHUNK, CHUNK), CHUNK)
        for u in range(n_unit):
            w = jnp.where(lo16, w_s[u, rbk, :], w_s[u, rf, :])
            qd = jnp.where(lo16, qd_s[u, rbk, :], qd_s[u, rf, :])
            wq = _dot(jnp.concatenate([w, qd], axis=0), sa_s[u].astype(BF16))
            v = jnp.where(lo, u_s[u, rf, :], u_s[u, rbk, :]) - wq[:CHUNK]
            v_s[u] = v.astype(BF16)
            vbd_s[u] = both(v).astype(BF16)
            op_s[u] = wq[CHUNK:]
        for u in range(n_unit):
            a = jnp.where(lo16, a_s[u, rf, :], a_s[u, rbk, :])
            o = op_s[u] + _dot(a, vbd_s[u])
            o_s[u, rf, 0:half] = o[:, :half]
            o_s[u, rbk, half:LANES] = o[:, half:]
            kdt = jnp.concatenate([kdt_s[u, sb][:half], kdt_s[u, s][half:]], axis=0)
            egt = jnp.where(lo_row, egt_s[u, s], egt_s[u, sb])
            sa_s[u] = sa_s[u] * egt + jnp.where(anti, _dot(kdt, v_s[u]), 0.0)
        return carry

    lax.fori_loop(0, n_chunk, scan, 0)

    for sq in range(n_sub):
        for r0 in range(0, seq_len, rb):
            tr = sq * seq_len + r0
            for p in range(n_pair):
                nrm = []
                for e in range(2):
                    ofb = o_s[sq * H_A + 2 * p + e, r0:r0 + rb, :]
                    oo = ofb + pltpu.roll(ofb, half, axis=1)
                    nrm.append(oo * lax.rsqrt(jnp.sum(oo * oo, axis=-1, keepdims=True) * (0.5 / HEAD_DIM) + EPS))
                o_ref[tr:tr + rb, p * LANES:(p + 1) * LANES] = (
                    jnp.where(lo_rb, nrm[0], nrm[1]) * ng_ref[...]
                    * _silu(z_ref[tr:tr + rb, p * LANES:(p + 1) * LANES]))
        for h in range(H_A):
            st = sa_s[sq * H_A + h]
            sfin_ref[sq, 0, h] = st[half:, :half]
            sfin_ref[sq, 1, h] = st[:half, half:]


def delta_mixer(qkv, z, gate, cw, alog_row, dtb_row, ng2, s0, *, n_seq, seq_len, tok0, n_sub,
                all_states=None, layer=0):
    n_chunk = seq_len // CHUNK
    n_unit = n_sub * H_A
    blk = n_sub * seq_len
    b0 = tok0 // blk
    tok = lambda w: pl.BlockSpec((blk, w), lambda b: (b0 + b, 0))
    full = lambda a: pl.BlockSpec(a.shape, lambda b: (0,) * a.ndim)
    state = pl.BlockSpec((n_sub, 2, H_A, HEAD_DIM, HEAD_DIM), lambda b: (b, 0, 0, 0, 0))
    per_unit = lambda dt: pltpu.VMEM((n_unit, seq_len, LANES), dt)
    n_cp = 2 // n_sub
    stage = lambda rows, cols, dt, n=n_unit: pltpu.VMEM((n, rows, cols), dt)
    aliased = all_states is not None
    in_specs = [tok(3 * A_W), tok(A_W), tok(LANES), full(cw), full(alog_row), full(dtb_row), full(ng2), state]
    args = [qkv, z, gate, cw, alog_row, dtb_row, ng2, s0]
    if aliased:
        in_specs.append(pl.BlockSpec(memory_space=pl.ANY))
        args.append(all_states)
        state_out = pl.BlockSpec((n_sub, None, 2, H_A, HEAD_DIM, HEAD_DIM), lambda b: (b, layer, 0, 0, 0, 0))
        state_shape = jax.ShapeDtypeStruct(all_states.shape, F32)
    else:
        state_out = state
        state_shape = jax.ShapeDtypeStruct((n_seq, 2, H_A, HEAD_DIM, HEAD_DIM), F32)
    return pl.pallas_call(
        functools.partial(_delta_kernel, seq_len=seq_len, n_sub=n_sub, n_cp=n_cp, aliased=aliased),
        grid=(n_seq // n_sub,),
        in_specs=in_specs,
        out_specs=[pl.BlockSpec((blk, A_W), lambda b: (b, 0)), state_out],
        out_shape=[jax.ShapeDtypeStruct((n_seq * seq_len, A_W), F32), state_shape],
        input_output_aliases={len(args) - 1: 1} if aliased else {},
        scratch_shapes=[pltpu.VMEM((seq_len + 16, 3 * A_W), F32),
                        per_unit(F32), per_unit(F32), per_unit(F32),
                        pltpu.VMEM((blk, LANES), F32),
                        pltpu.VMEM((blk, LANES), F32),
                        per_unit(F32), per_unit(BF16), per_unit(BF16), per_unit(BF16),
                        pltpu.VMEM((n_unit, n_chunk, LANES, HEAD_DIM), BF16),
                        pltpu.VMEM((n_unit, n_chunk, 1, LANES), F32),
                        stage(LANES, LANES, F32),
                        per_unit(F32),
                        stage(LANES, 2 * LANES, BF16, n_cp * n_unit),
                        stage(LANES, LANES, F32, n_cp * n_unit),
                        stage(LANES, 2 * LANES, BF16, n_cp * n_unit),
                        stage(LANES, LANES, F32, n_cp * n_unit),
                        stage(LANES, LANES, BF16, n_cp * n_unit),
                        stage(LANES, LANES, BF16, n_cp * n_unit),
                        stage(LANES, LANES, BF16),
                        stage(CHUNK, LANES, BF16),
                        stage(CHUNK, LANES, F32)],
        compiler_params=_cparams(("arbitrary",)),
        name=f"delta_mixer_{seq_len}",
    )(*args)


def _rope(x, cos, sin_signed, quarter):
    width = x.shape[-1]
    swapped = jnp.where((_lane(x.shape) % (2 * quarter)) < quarter,
                        pltpu.roll(x, width - quarter, axis=1), pltpu.roll(x, quarter, axis=1))
    return x * cos + swapped * sin_signed


def _rope_tables(rows, dim):
    nf = dim // 4
    inv = ROPE_THETA ** (-jnp.arange(nf, dtype=F32) / nf)
    r = jnp.repeat(jnp.arange(rows, dtype=F32), GRID_W)
    c = jnp.tile(jnp.arange(GRID_W, dtype=F32), rows)
    ar, ac = r[:, None] * inv, c[:, None] * inv
    cos = jnp.concatenate([jnp.cos(ar), jnp.cos(ar), jnp.cos(ac), jnp.cos(ac)], axis=1)
    sin = jnp.concatenate([-jnp.sin(ar), jnp.sin(ar), -jnp.sin(ac), jnp.sin(ac)], axis=1)
    return jnp.tile(cos, (1, LANES // dim)), jnp.tile(sin, (1, LANES // dim))


def _gqa_kernel(*refs, cached, n_sub):
    if cached:
        (q_ref, k_ref, v_ref, gq_ref, gk_ref, ck_ref, cv_ref, cosq_ref, sinq_ref, cosk_ref, sink_ref,
         o_ref, k_s, v_s) = refs
    else:
        q_ref, k_ref, v_ref, gq_ref, gk_ref, o_ref, kn_ref, k_s, v_s = refs
    gmat = _group_mean_matrix(LANES, HEAD_DIM)
    quarter = HEAD_DIM // 4
    n_k = k_ref.shape[0] // n_sub

    @pl.when(pl.program_id(1) == 0)
    def _():
        for sq in range(n_sub):
            k = _group_rms(k_ref[sq * n_k:(sq + 1) * n_k, :], gmat, gk_ref[...])
            v = v_ref[sq * n_k:(sq + 1) * n_k, :]
            if cached:
                k = _rope(k, cosk_ref[...], sink_ref[...], quarter)
                k = jnp.concatenate([ck_ref[...], k], axis=0)
                v = jnp.concatenate([cv_ref[...], v], axis=0)
            else:
                kn_ref[sq * n_k:(sq + 1) * n_k, :] = k
            k_s[sq] = k.astype(BF16)
            v_s[sq] = v.astype(BF16)

    tq = q_ref.shape[0] // n_sub
    lo = _lane((tq, LANES)) < HEAD_DIM
    group = H_B // KV_B
    q_scale = (HEAD_DIM ** -0.5) * math.log2(math.e)
    for sq in range(n_sub):
        k, v = k_s[sq], v_s[sq]
        qs = []
        for j in range(H_B // 2):
            q = _group_rms(q_ref[sq * tq:(sq + 1) * tq, j * LANES:(j + 1) * LANES], gmat, gq_ref[...])
            if cached:
                q = _rope(q, cosq_ref[...], sinq_ref[...], quarter)
            q = q * q_scale
            qs.append((q, pltpu.roll(q, HEAD_DIM, axis=1)))
        outs = [None] * H_B
        for kv in range(KV_B):
            keep = lo if kv == 0 else ~lo
            heads = range(kv * group, (kv + 1) * group)
            q3 = jnp.concatenate([jnp.where(keep, qs[h // 2][0 if h % 2 == kv else 1], 0.0).astype(BF16)
                                  for h in heads], axis=0)
            s = _dot_nt(q3, k)
            p = jnp.exp2(s - jnp.max(s, axis=-1, keepdims=True))
            r = _dot(p.astype(BF16), v) / jnp.sum(p, axis=-1, keepdims=True)
            for i, h in enumerate(heads):
                rh = r[i * tq:(i + 1) * tq]
                outs[h] = rh if h % 2 == kv else pltpu.roll(rh, HEAD_DIM, axis=1)
        for j in range(H_B // 2):
            o_ref[sq * tq:(sq + 1) * tq, j * LANES:(j + 1) * LANES] = jnp.where(lo, outs[2 * j], outs[2 * j + 1])


def gqa_mixer(q, k, v, gq, gk, *, n_seq, seq_len, tok0, cache=None, rope=None, layer=0):
    cached = cache is not None
    n_sub = 1 if cached else 2
    n_seq, seq_len = n_seq // n_sub, seq_len * n_sub
    tq = 256 * n_sub
    nq = seq_len // tq
    b0 = tok0 // seq_len
    q0 = tok0 // tq
    kvw = KV_B * HEAD_DIM
    in_specs = [pl.BlockSpec((tq, B_W), lambda b, i: (q0 + b * nq + i, 0)),
                pl.BlockSpec((seq_len, kvw), lambda b, i: (b0 + b, 0)),
                pl.BlockSpec((seq_len, kvw), lambda b, i: (b0 + b, 0)),
                pl.BlockSpec((1, LANES), lambda b, i: (0, 0)),
                pl.BlockSpec((1, LANES), lambda b, i: (0, 0))]
    args = [q, k, v, gq, gk]
    out_specs = [pl.BlockSpec((tq, B_W), lambda b, i: (b * nq + i, 0))]
    out_shape = [jax.ShapeDtypeStruct((n_seq * seq_len, B_W), F32)]
    if cached:
        cos, sin = rope
        cache_spec = pl.BlockSpec((None, None, PAST_LEN, kvw), lambda b, i: (b, layer, 0, 0))
        in_specs += [cache_spec, cache_spec,
                     pl.BlockSpec((tq, LANES), lambda b, i: (i, 0)),
                     pl.BlockSpec((tq, LANES), lambda b, i: (i, 0)),
                     pl.BlockSpec((seq_len, LANES), lambda b, i: (0, 0)),
                     pl.BlockSpec((seq_len, LANES), lambda b, i: (0, 0))]
        args += [cache[0], cache[1], cos, sin, cos, sin]
    else:
        out_specs.append(pl.BlockSpec((seq_len, kvw), lambda b, i: (b, 0)))
        out_shape.append(jax.ShapeDtypeStruct((n_seq * seq_len, kvw), F32))
    n_keys = seq_len // n_sub + (PAST_LEN if cached else 0)
    return pl.pallas_call(
        functools.partial(_gqa_kernel, cached=cached, n_sub=n_sub),
        grid=(n_seq, nq),
        in_specs=in_specs, out_specs=out_specs, out_shape=out_shape,
        scratch_shapes=[pltpu.VMEM((n_sub, n_keys, kvw), BF16), pltpu.VMEM((n_sub, n_keys, kvw), BF16)],
        compiler_params=_cparams(("arbitrary", "arbitrary")),
        name=f"gqa_mixer_{seq_len}",
    )(*args)


def _diff_kernel(*refs, cached, lam_init, n_sub):
    if cached:
        (q_ref, k_ref, v_ref, lam_ref, gn_ref, ck_ref, cv_ref, cosq_ref, sinq_ref, cosk_ref, sink_ref,
         o_ref) = refs
    else:
        q_ref, k_ref, v_ref, lam_ref, gn_ref, o_ref = refs
    quarter = DIFF_HD // 4
    lv = lam_ref[...]
    lam = (jnp.exp(jnp.sum(lv[0:1] * lv[1:2], axis=-1, keepdims=True))
           - jnp.exp(jnp.sum(lv[2:3] * lv[3:4], axis=-1, keepdims=True)) + lam_init)
    gmat = _group_mean_matrix(C_W, 2 * DIFF_HD)
    tq, n_k = q_ref.shape[0] // n_sub, k_ref.shape[0] // n_sub
    lane = _lane((tq, C_W))
    for sb in range(n_sub):
        q = q_ref[sb * tq:(sb + 1) * tq, :]
        k, v = k_ref[sb * n_k:(sb + 1) * n_k, :], v_ref[sb * n_k:(sb + 1) * n_k, :]
        if cached:
            cq = jnp.concatenate([cosq_ref[...]] * 2, axis=1)
            sq = jnp.concatenate([sinq_ref[...]] * 2, axis=1)
            ck = jnp.concatenate([cosk_ref[...]] * 2, axis=1)
            sk = jnp.concatenate([sink_ref[...]] * 2, axis=1)
            q = _rope(q, cq, sq, quarter)
            k = _rope(k, ck, sk, quarter)
            k = jnp.concatenate([ck_ref[...], k], axis=0)
            v = jnp.concatenate([cv_ref[...], v], axis=0)
        k = k.astype(BF16)
        v = v.astype(BF16)
        q = q * ((DIFF_HD ** -0.5) * math.log2(math.e))
        o = jnp.zeros(q.shape, F32)
        for h in range(H_C):
            ps, cs = [], []
            for m in range(2):
                qm = jnp.where(lane // DIFF_HD == 2 * h + m, q, 0.0).astype(BF16)
                s = _dot_nt(qm, k)
                p = jnp.exp2(s - jnp.max(s, axis=-1, keepdims=True))
                ps.append(p)
                cs.append((1.0 if m == 0 else lam) / jnp.sum(p, axis=-1, keepdims=True))
            a = ps[0] * cs[0] - ps[1] * cs[1]
            o = jnp.where(lane // (2 * DIFF_HD) == h, _dot(a.astype(BF16), v), o)
        o_ref[sb * tq:(sb + 1) * tq, :] = _group_rms(o, gmat, gn_ref[...]) * (1.0 - lam_init)


def diff_mixer(q, k, v, lam_p, gn, lam_init, *, n_seq, seq_len, tok0, cache=None, rope=None, layer=0):
    cached = cache is not None
    n_sub = 1 if cached else 2
    n_seq, seq_len = n_seq // n_sub, seq_len * n_sub
    tq = 256 * n_sub
    nq = seq_len // tq
    b0 = tok0 // seq_len
    q0 = tok0 // tq
    in_specs = [pl.BlockSpec((tq, C_W), lambda b, i: (q0 + b * nq + i, 0)),
                pl.BlockSpec((seq_len, C_W), lambda b, i: (b0 + b, 0)),
                pl.BlockSpec((seq_len, C_W), lambda b, i: (b0 + b, 0)),
                pl.BlockSpec((4, DIFF_HD), lambda b, i: (0, 0)),
                pl.BlockSpec((1, C_W), lambda b, i: (0, 0))]
    args = [q, k, v, lam_p, gn]
    if cached:
        cos, sin = rope
        cache_spec = pl.BlockSpec((None, None, PAST_LEN, C_W), lambda b, i: (b, layer, 0, 0))
        in_specs += [cache_spec, cache_spec,
                     pl.BlockSpec((tq, LANES), lambda b, i: (i, 0)),
                     pl.BlockSpec((tq, LANES), lambda b, i: (i, 0)),
                     pl.BlockSpec((seq_len, LANES), lambda b, i: (0, 0)),
                     pl.BlockSpec((seq_len, LANES), lambda b, i: (0, 0))]
        args += [cache[0], cache[1], cos, sin, cos, sin]
    return pl.pallas_call(
        functools.partial(_diff_kernel, cached=cached, lam_init=lam_init, n_sub=n_sub),
        grid=(n_seq, nq),
        in_specs=in_specs,
        out_specs=pl.BlockSpec((tq, C_W), lambda b, i: (b * nq + i, 0)),
        out_shape=jax.ShapeDtypeStruct((n_seq * seq_len, C_W), F32),
        compiler_params=_cparams(("arbitrary", "arbitrary")),
        name=f"diff_mixer_{seq_len}",
    )(*args)


def _outproj_kernel(*refs, routed, n_mix):
    x_ref = refs[0]
    mix_refs = refs[1:1 + 2 * n_mix]
    rest = refs[1 + 2 * n_mix:]
    if routed:
        w_ref, gm_ref, g_ref, sh_ref, sc_ref, r_ref, x1_ref, h_ref, lg_ref = rest
    else:
        w_ref, gm_ref, g_ref, sh_ref, sc_ref, x1_ref, h_ref = rest
    is_ctx = pl.program_id(0) < N_CTX // x_ref.shape[0]
    mixed = jnp.concatenate(
        [jnp.where(is_ctx, mix_refs[2 * m][...], mix_refs[2 * m + 1][...]) for m in range(n_mix)], axis=1)
    x1 = x_ref[...] + gm_ref[...] * _dot(mixed.astype(BF16), w_ref[...])
    x1_ref[...] = x1
    h = _rms(x1, g_ref[...]) * (1.0 + sc_ref[...]) + sh_ref[...]
    h_ref[...] = h.astype(h_ref.dtype)
    if routed:
        lg_ref[...] = _top2_route(_dot_split(h, r_ref[...]))


def out_projection(x, mixes, w_out_b, mod, norm_g, l, router_p=None):
    tm = 512
    n_ctx_tiles = N_CTX // tm
    n_lat_tiles = N_LAT // tm
    routed = router_p is not None
    mod_spec = lambda k: pl.BlockSpec((None, None, 1, D_MODEL), lambda i: (l, _cond_row(i, tm), 0, k))
    in_specs = [pl.BlockSpec((tm, D_MODEL), lambda i: (i, 0))]
    args = [x]
    for ctx_arr, lat_arr in mixes:
        w = ctx_arr.shape[1]
        in_specs.append(pl.BlockSpec((tm, w), lambda i: (jnp.minimum(i, n_ctx_tiles - 1), 0)))
        in_specs.append(pl.BlockSpec((tm, w), lambda i: (jnp.clip(i - n_ctx_tiles, 0, n_lat_tiles - 1), 0)))
        args += [ctx_arr, lat_arr]
    in_specs += [pl.BlockSpec((None, D_MODEL, D_MODEL), lambda i: (l, 0, 0)),
                 mod_spec(2),
                 pl.BlockSpec((None, 1, D_MODEL), lambda i: (l, 0, 0)),
                 mod_spec(3), mod_spec(4)]
    args += [w_out_b, mod, norm_g, mod, mod]
    out_specs = [pl.BlockSpec((tm, D_MODEL), lambda i: (i, 0)),
                 pl.BlockSpec((tm, D_MODEL), lambda i: (i, 0))]
    out_shape = [jax.ShapeDtypeStruct((N_TOK, D_MODEL), F32),
                 jax.ShapeDtypeStruct((N_TOK, D_MODEL), F32 if routed else BF16)]
    if routed:
        in_specs.append(pl.BlockSpec((D_MODEL, LANES), lambda i: (0, 0)))
        args.append(router_p)
        out_specs.append(pl.BlockSpec((tm, LANES), lambda i: (i, 0)))
        out_shape.append(jax.ShapeDtypeStruct((N_TOK, LANES), F32))
    return pl.pallas_call(
        functools.partial(_outproj_kernel, routed=routed, n_mix=len(mixes)),
        grid=(N_TOK // tm,),
        in_specs=in_specs, out_specs=out_specs, out_shape=out_shape,
        compiler_params=_cparams(("arbitrary",)),
        name="out_projection",
    )(*args)


def _ffn_kernel(h_ref, wg_ref, wu_ref, wd_ref, x_ref, gf_ref, o_ref, acc_ref):
    j = pl.program_id(1)

    @pl.when(j == 0)
    def _():
        acc_ref[...] = jnp.zeros_like(acc_ref)

    h = h_ref[...]
    act = _silu(_dot(h, wg_ref[...])) * _dot(h, wu_ref[...])
    acc_ref[...] += _dot(act.astype(BF16), wd_ref[...])

    @pl.when(j == pl.num_programs(1) - 1)
    def _():
        o_ref[...] = x_ref[...] + gf_ref[...] * acc_ref[...]


def ffn(h, w_gu, w_d, x1, mod, l):
    tm = 512
    lyr = l // 2
    n_chunk = D_FF // FF_CHUNK
    gu_spec = lambda half: pl.BlockSpec((None, D_MODEL, FF_CHUNK), lambda i, j: (lyr, 0, half * n_chunk + j))
    return pl.pallas_call(
        _ffn_kernel,
        grid=(N_TOK // tm, n_chunk),
        in_specs=[pl.BlockSpec((tm, D_MODEL), lambda i, j: (i, 0)),
                  gu_spec(0), gu_spec(1),
                  pl.BlockSpec((None, FF_CHUNK, D_MODEL), lambda i, j: (lyr, j, 0)),
                  pl.BlockSpec((tm, D_MODEL), lambda i, j: (i, 0)),
                  pl.BlockSpec((None, None, 1, D_MODEL), lambda i, j: (l, _cond_row(i, tm), 0, 5))],
        out_specs=pl.BlockSpec((tm, D_MODEL), lambda i, j: (i, 0)),
        out_shape=jax.ShapeDtypeStruct((N_TOK, D_MODEL), F32),
        scratch_shapes=[pltpu.VMEM((tm, D_MODEL), F32)],
        compiler_params=_cparams(("arbitrary", "arbitrary")),
        name="ffn_dense",
    )(h, w_gu, w_gu, w_d, x1, mod)


MOE_TILE = 256
MOE_SLOTS = TOP_K * N_TOK + N_EXP * MOE_TILE
MOE_TILES = MOE_SLOTS // MOE_TILE
ROUTE_I1, ROUTE_I2, ROUTE_W1, ROUTE_W2 = 0, 1, 2, 3


def _top2_route(logits):
    lane = _lane(logits.shape)
    neg = jnp.float32(-jnp.inf)
    lg = jnp.where(lane < N_EXP, logits, neg)
    t1 = jnp.max(lg, axis=-1, keepdims=True)
    i1 = jnp.min(jnp.where(lg == t1, lane, LANES), axis=-1, keepdims=True)
    lg2 = jnp.where(lane == i1, neg, lg)
    t2 = jnp.max(lg2, axis=-1, keepdims=True)
    i2 = jnp.min(jnp.where(lg2 == t2, lane, LANES), axis=-1, keepdims=True)
    e2 = jnp.exp(t2 - t1)
    w1 = 1.0 / (1.0 + e2)
    w2 = e2 / (1.0 + e2)
    rec = jnp.where(lane == ROUTE_I1, i1.astype(F32), 0.0) + jnp.where(lane == ROUTE_I2, i2.astype(F32), 0.0)
    return rec + jnp.where(lane == ROUTE_W1, w1, 0.0) + jnp.where(lane == ROUTE_W2, w2, 0.0)


def _routing_tables(route):
    n_assign = TOP_K * N_TOK
    expert = jnp.concatenate([route[:, ROUTE_I1], route[:, ROUTE_I2]]).astype(jnp.int32)
    index = jnp.arange(n_assign, dtype=jnp.int32)
    onehot = (expert[:, None] == jnp.arange(N_EXP, dtype=jnp.int32)[None, :]).astype(jnp.int32)
    running = jnp.cumsum(onehot, axis=0)
    counts = running[-1]
    rank = jnp.sum(running * onehot, axis=1) - 1
    padded = (counts + MOE_TILE - 1) // MOE_TILE * MOE_TILE
    ends = jnp.cumsum(padded)
    starts = ends - padded
    slot = jnp.sum(onehot * starts[None, :], axis=1) + rank
    sorted_token = (jnp.sort(expert * n_assign + index) % n_assign) % N_TOK
    tile_start = jnp.arange(MOE_TILES, dtype=jnp.int32) * MOE_TILE
    tile_expert = jnp.minimum(jnp.sum(tile_start[:, None] >= ends[None, :], axis=1), N_EXP - 1).astype(jnp.int32)
    first_sorted = (jnp.cumsum(counts) - counts)[tile_expert] + tile_start - starts[tile_expert]
    n_used = (ends[-1] // MOE_TILE).astype(jnp.int32).reshape(1)
    return sorted_token, slot, tile_expert, first_sorted.astype(jnp.int32), n_used


def _row_copy(src_hbm, row, dst, dst_row, sem):
    return pltpu.make_async_copy(src_hbm.at[pl.ds(row, 1)], dst.at[pl.ds(dst_row, 1)], sem)


def _experts_kernel(tok_ref, exp_ref, first_ref, used_ref, h_hbm, wg_ref, wu_ref, wd_ref, y_ref, x_buf, sems,
                    wg_s, wu_s, wd_s):
    t = pl.program_id(0)
    n_used = used_ref[0]

    def start_gather(tile, half):
        first = first_ref[tile]
        for r in range(MOE_TILE):
            tok = tok_ref[jnp.minimum(first + r, TOP_K * N_TOK - 1)]
            _row_copy(h_hbm, tok, x_buf.at[half], r, sems.at[half]).start(priority=r % 2)

    @pl.when(t == 0)
    def _():
        start_gather(0, 0)

    @pl.when(t + 1 < n_used)
    def _():
        start_gather(t + 1, (t + 1) % 2)

    @pl.when(t < n_used)
    def _():
        half = t % 2
        for r in range(MOE_TILE):
            _row_copy(h_hbm, 0, x_buf.at[half], r, sems.at[half]).wait()
        @pl.when((t == 0) | (exp_ref[t] != exp_ref[jnp.maximum(t - 1, 0)]))
        def _():
            wg_s[...] = wg_ref[...].astype(BF16)
            wu_s[...] = wu_ref[...].astype(BF16)
            wd_s[...] = wd_ref[...].astype(BF16)

        x = x_buf[half].astype(BF16)
        act = _silu(_dot(x, wg_s[...])) * _dot(x, wu_s[...])
        y_ref[...] = _dot(act.astype(BF16), wd_s[...])

    @pl.when(t >= n_used)
    def _():
        y_ref[...] = jnp.zeros_like(y_ref)


def moe_experts(h, route, w_gu, w_d, l):
    lyr = l // 2
    sorted_token, slot, tile_expert, first_sorted, n_used = _routing_tables(route)
    gu_spec = lambda half: pl.BlockSpec((None, None, D_MODEL, FF_CHUNK),
                                        lambda t, tok, exp, first, used: (lyr, exp[t], 0, half))
    y = pl.pallas_call(
        _experts_kernel,
        grid_spec=pltpu.PrefetchScalarGridSpec(
            num_scalar_prefetch=4,
            grid=(MOE_TILES,),
            in_specs=[pl.BlockSpec(memory_space=pl.ANY),
                      gu_spec(0), gu_spec(1),
                      pl.BlockSpec((None, None, FF_CHUNK, D_MODEL),
                                   lambda t, tok, exp, first, used: (lyr, exp[t], 0, 0))],
            out_specs=pl.BlockSpec((MOE_TILE, D_MODEL), lambda t, tok, exp, first, used: (t, 0)),
            scratch_shapes=[pltpu.VMEM((2, MOE_TILE, D_MODEL), F32), pltpu.SemaphoreType.DMA((2,)),
                            pltpu.VMEM((D_MODEL, FF_CHUNK), BF16), pltpu.VMEM((D_MODEL, FF_CHUNK), BF16),
                            pltpu.VMEM((FF_CHUNK, D_MODEL), BF16)]),
        out_shape=jax.ShapeDtypeStruct((MOE_SLOTS, D_MODEL), F32),
        compiler_params=_cparams(("arbitrary",)),
        name="moe_experts",
    )(sorted_token, tile_expert, first_sorted, n_used, h, w_gu, w_gu, w_d)
    return y, slot


def _moe_combine_kernel(slot_ref, y_hbm, x_ref, gf_ref, route_ref, o_ref, y_buf, sems):
    i = pl.program_id(0)
    tm = x_ref.shape[0]

    def start_gather(tile, half):
        for k in range(TOP_K):
            for r in range(tm):
                _row_copy(y_hbm, slot_ref[k * N_TOK + tile * tm + r], y_buf.at[half, k], r,
                          sems.at[half]).start(priority=r % 2)

    @pl.when(i == 0)
    def _():
        start_gather(0, 0)

    @pl.when(i + 1 < pl.num_programs(0))
    def _():
        start_gather(i + 1, (i + 1) % 2)

    half = i % 2
    for k in range(TOP_K):
        for r in range(tm):
            _row_copy(y_hbm, 0, y_buf.at[half, k], r, sems.at[half]).wait()
    route = route_ref[...]
    f = (route[:, ROUTE_W1:ROUTE_W1 + 1] * y_buf[half, 0] + route[:, ROUTE_W2:ROUTE_W2 + 1] * y_buf[half, 1])
    o_ref[...] = x_ref[...] + gf_ref[...] * f


def moe_combine(y, slot, x1, mod, route, l):
    tm = 256
    return pl.pallas_call(
        _moe_combine_kernel,
        grid_spec=pltpu.PrefetchScalarGridSpec(
            num_scalar_prefetch=1,
            grid=(N_TOK // tm,),
            in_specs=[pl.BlockSpec(memory_space=pl.ANY),
                      pl.BlockSpec((tm, D_MODEL), lambda i, s: (i, 0)),
                      pl.BlockSpec((None, None, 1, D_MODEL), lambda i, s: (l, _cond_row(i, tm), 0, 5)),
                      pl.BlockSpec((tm, LANES), lambda i, s: (i, 0))],
            out_specs=pl.BlockSpec((tm, D_MODEL), lambda i, s: (i, 0)),
            scratch_shapes=[pltpu.VMEM((2, TOP_K, tm, D_MODEL), F32), pltpu.SemaphoreType.DMA((2,))]),
        out_shape=jax.ShapeDtypeStruct((N_TOK, D_MODEL), F32),
        compiler_params=_cparams(("arbitrary",)),
        name="moe_combine",
    )(slot, y, x1, mod, route)


def _final_norm_kernel(x_ref, g_ref, o_ref):
    o_ref[...] = _rms(x_ref[...], g_ref[...])


def final_norm_call(x, g, tok0, n_tok):
    tm = 1024
    i0 = tok0 // tm
    return pl.pallas_call(
        _final_norm_kernel,
        grid=(n_tok // tm,),
        in_specs=[pl.BlockSpec((tm, D_MODEL), lambda i: (i0 + i, 0)),
                  pl.BlockSpec((1, D_MODEL), lambda i: (0, 0))],
        out_specs=pl.BlockSpec((tm, D_MODEL), lambda i: (i, 0)),
        out_shape=jax.ShapeDtypeStruct((n_tok, D_MODEL), F32),
        compiler_params=_cparams(("arbitrary",)),
        name="final_norm",
    )(x, g)


def _cache_store_kernel(*refs):
    n = len(refs) // 3
    for src, dst in zip(refs[:n], refs[2 * n:]):
        dst[...] = src[...].reshape(dst.shape)


def cache_store(sources, all_layers, l):
    n_sq = 2
    tm = n_sq * SEQ
    return pl.pallas_call(
        _cache_store_kernel,
        grid=(N_CTX // tm,),
        in_specs=([pl.BlockSpec((tm, s.shape[1]), lambda i: (i, 0)) for s in sources]
                  + [pl.BlockSpec(memory_space=pl.ANY)] * len(all_layers)),
        out_specs=[pl.BlockSpec((n_sq, None, SEQ, a.shape[-1]), lambda i: (i, l, 0, 0)) for a in all_layers],
        out_shape=[jax.ShapeDtypeStruct(a.shape, F32) for a in all_layers],
        input_output_aliases={len(sources) + k: k for k in range(len(all_layers))},
        compiler_params=_cparams(("arbitrary",)),
        name="cache_store",
    )(*sources, *all_layers)


def _gate_row(p):
    return jnp.zeros((1, LANES), F32).at[0, GATE_A:GATE_A + 2 * H_A].set(p.reshape(-1))


def _tile_lanes(g, width):
    return jnp.tile(g, width // g.shape[0])[None, :]


def kernel(x_prompt, x_sample, state_delta, cache_gqa_k, cache_gqa_v, cache_diff_k, cache_diff_v, c, c_ctx,
           ada_w, ada_b, norm_mix, norm_ffn, w_in, conv_w, dn_a_log, dn_dt_bias, dn_norm, gqa_q_norm,
           gqa_k_norm, diff_lambda, diff_norm, w_out, ffn_w_gu, ffn_w_down, moe_router, moe_w_gu, moe_w_down,
           final_norm):
    cond8 = jnp.concatenate([c_ctx[None, :], c, jnp.zeros((8 - N_COND, D_MODEL), F32)], axis=0)
    w_out_b = w_out.astype(BF16)
    norm_mix3 = norm_mix.reshape(DEPTH, 1, D_MODEL)
    norm_ffn3 = norm_ffn.reshape(DEPTH, 1, D_MODEL)
    ffn_gu_b, ffn_d_b = ffn_w_gu.astype(BF16), ffn_w_down.astype(BF16)
    router_p = jnp.pad(moe_router, ((0, 0), (0, 0), (0, LANES - N_EXP)))
    rope_b = _rope_tables(DEC_SEQ // GRID_W, HEAD_DIM)
    rope_c = _rope_tables(DEC_SEQ // GRID_W, DIFF_HD)
    s0_ctx = jnp.zeros((BATCH, 2, H_A, HEAD_DIM, HEAD_DIM), F32)
    cache_b = (cache_gqa_k.reshape(DEC_BATCH, DEPTH, PAST_LEN, KV_B * HEAD_DIM),
               cache_gqa_v.reshape(DEC_BATCH, DEPTH, PAST_LEN, KV_B * HEAD_DIM))
    cache_c = (cache_diff_k.reshape(DEC_BATCH, DEPTH, PAST_LEN, C_W),
               cache_diff_v.reshape(DEC_BATCH, DEPTH, PAST_LEN, C_W))

    mod = ada_modulation(cond8, ada_w, ada_b)[:, :N_COND].reshape(DEPTH, N_COND, 1, 6 * D_MODEL)

    x = jnp.concatenate([x_prompt.reshape(N_CTX, D_MODEL), x_sample.reshape(N_LAT, D_MODEL)], axis=0)
    all_states = jnp.zeros((BATCH, DEPTH, 2, H_A, HEAD_DIM, HEAD_DIM), F32)
    ctx_layers = tuple(jnp.zeros((BATCH, DEPTH, SEQ, w), F32)
                       for w in (KV_B * HEAD_DIM, KV_B * HEAD_DIM, C_W, C_W))
    ctx = dict(n_seq=BATCH, seq_len=SEQ, tok0=0)
    lat = dict(n_seq=DEC_BATCH, seq_len=DEC_SEQ, tok0=N_CTX)
    for l in range(DEPTH):
        qkv_a, z_a, q_b, k_b, v_b, q_c, k_c, v_c, gate = in_projection(x, mod, norm_mix3, w_in, l)

        dn_args = (qkv_a, z_a, gate, conv_w[l], _gate_row(dn_a_log[l]), _gate_row(dn_dt_bias[l]),
                   _tile_lanes(dn_norm[l], LANES))
        oa_ctx, all_states = delta_mixer(*dn_args, s0_ctx, n_sub=2, all_states=all_states, layer=l, **ctx)
        oa_lat, _ = delta_mixer(*dn_args, state_delta[:, l], n_sub=1, **lat)

        gq, gk = _tile_lanes(gqa_q_norm[l], LANES), _tile_lanes(gqa_k_norm[l], LANES)
        ob_ctx, kn_ctx = gqa_mixer(q_b, k_b, v_b, gq, gk, **ctx)
        (ob_lat,) = gqa_mixer(q_b, k_b, v_b, gq, gk, cache=cache_b, rope=rope_b, layer=l, **lat)

        lam_init = 0.8 - 0.6 * math.exp(-0.3 * l)
        gn = _tile_lanes(diff_norm[l], C_W)
        oc_ctx = diff_mixer(q_c, k_c, v_c, diff_lambda[l], gn, lam_init, **ctx)
        oc_lat = diff_mixer(q_c, k_c, v_c, diff_lambda[l], gn, lam_init, cache=cache_c, rope=rope_c, layer=l, **lat)

        mixes = ((oa_ctx, oa_lat), (ob_ctx, ob_lat), (oc_ctx, oc_lat))
        if l % 2 == 0:
            x1, h2 = out_projection(x, mixes, w_out_b, mod, norm_ffn3, l)
            x = ffn(h2, ffn_gu_b, ffn_d_b, x1, mod, l)
        else:
            x1, h2, route = out_projection(x, mixes, w_out_b, mod, norm_ffn3, l, router_p[l // 2])
            y_slots, slot = moe_experts(h2, route, moe_w_gu, moe_w_down, l)
            x = moe_combine(y_slots, slot, x1, mod, route, l)

        ctx_layers = cache_store((kn_ctx, v_b, k_c, v_c), ctx_layers, l)

    y_prompt = final_norm_call(x, final_norm[None, :], 0, N_CTX).reshape(BATCH, SEQ, D_MODEL)
    y_sample = final_norm_call(x, final_norm[None, :], N_CTX, N_LAT).reshape(DEC_BATCH, DEC_SEQ, D_MODEL)
    new_gk, new_gv, new_dk, new_dv = ctx_layers
    return (y_prompt, y_sample, all_states,
            new_gk.reshape(BATCH, DEPTH, SEQ, KV_B, HEAD_DIM), new_gv.reshape(BATCH, DEPTH, SEQ, KV_B, HEAD_DIM),
            new_dk.reshape(BATCH, DEPTH, SEQ, H_C, 2, DIFF_HD), new_dv.reshape(BATCH, DEPTH, SEQ, H_C, 2 * DIFF_HD))
```

```python
import functools
import math

import jax
import jax.numpy as jnp
from jax import lax
from jax.experimental import pallas as pl
from jax.experimental.pallas import tpu as pltpu

D_MODEL = 1024
BATCH = 16
SEQ = 256
DEPTH = 4
DEC_BATCH = 2
DEC_SEQ = 1024
PAST_LEN = 512
GRID_W = 64
HEAD_DIM = 64
H_A = 6
A_W = H_A * HEAD_DIM
H_B = 6
KV_B = 2
B_W = H_B * HEAD_DIM
H_C = 4
DIFF_HD = 32
C_W = H_C * 2 * DIFF_HD
CHUNK = 64
ROPE_THETA = 10000.0
D_FF = 2816
N_EXP = 8
TOP_K = 2
D_FF_E = 1408
EPS = 1e-6

N_CTX = BATCH * SEQ
N_LAT = DEC_BATCH * DEC_SEQ
N_TOK = N_CTX + N_LAT
N_COND = 1 + DEC_BATCH
IN_PAD = 3072
FF_CHUNK = D_FF_E
LANES = 128
VMEM_LIMIT = 56 * 1024 * 1024

IN_SPLIT = (("qkv_a", 3 * A_W), ("z_a", A_W), ("q_b", B_W), ("k_b", KV_B * HEAD_DIM), ("v_b", KV_B * HEAD_DIM),
            ("q_c", C_W), ("k_c", C_W), ("v_c", C_W), ("gate", LANES))
GATE_BETA, GATE_A = 0, 2 * H_A

F32 = jnp.float32
BF16 = jnp.bfloat16


def _cparams(sem):
    return pltpu.CompilerParams(dimension_semantics=sem, vmem_limit_bytes=VMEM_LIMIT)


def _dot(a, b):
    return jnp.dot(a, b, preferred_element_type=F32)


def _dot_split(a, b):
    a_hi, b_hi = a.astype(BF16), b.astype(BF16)
    a_lo = (a - a_hi.astype(F32)).astype(BF16)
    b_lo = (b - b_hi.astype(F32)).astype(BF16)
    return _dot(jnp.concatenate([a_hi, a_lo, a_hi], axis=1), jnp.concatenate([b_hi, b_hi, b_lo], axis=0))


def _dot_nt(a, b):
    return lax.dot_general(a, b, (((1,), (1,)), ((), ())), preferred_element_type=F32)


def _silu(x):
    return x * jax.nn.sigmoid(x)


def _softplus(x):
    return jnp.maximum(x, 0.0) + jnp.log1p(jnp.exp(-jnp.abs(x)))


def _rms(x, g):
    return x * lax.rsqrt(jnp.mean(x * x, axis=-1, keepdims=True) + EPS) * g


def _lane(shape):
    return lax.broadcasted_iota(jnp.int32, shape, len(shape) - 1)


def _group_mean_matrix(width, group):
    i = lax.broadcasted_iota(jnp.int32, (width, width), 0)
    j = lax.broadcasted_iota(jnp.int32, (width, width), 1)
    return jnp.where(i // group == j // group, 1.0 / group, 0.0).astype(BF16)


def _group_rms(x, gmat, g):
    sq = x * x
    hi = sq.astype(BF16)
    lo = (sq - hi.astype(F32)).astype(BF16)
    ms = _dot(jnp.concatenate([hi, lo], axis=1), jnp.concatenate([gmat, gmat], axis=0))
    return x * lax.rsqrt(ms + EPS) * g


def _cond_row(i, tm):
    return jnp.maximum((i * tm) // DEC_SEQ - (N_CTX // DEC_SEQ - 1), 0)


def _ada_kernel(c_ref, w_ref, b_ref, o_ref):
    o_ref[...] = _dot_split(_silu(c_ref[...]), w_ref[...]) + b_ref[...]


def ada_modulation(cond8, ada_w, ada_b):
    tn = 2048
    return pl.pallas_call(
        _ada_kernel,
        grid=(DEPTH, 6 * D_MODEL // tn),
        in_specs=[pl.BlockSpec((8, D_MODEL), lambda l, j: (0, 0)),
                  pl.BlockSpec((None, D_MODEL, tn), lambda l, j: (l, 0, j)),
                  pl.BlockSpec((None, 1, tn), lambda l, j: (l, 0, j))],
        out_specs=pl.BlockSpec((None, 8, tn), lambda l, j: (l, 0, j)),
        out_shape=jax.ShapeDtypeStruct((DEPTH, 8, 6 * D_MODEL), F32),
        compiler_params=_cparams(("arbitrary", "arbitrary")),
        name="ada_modulation",
    )(cond8, ada_w, ada_b.reshape(DEPTH, 1, 6 * D_MODEL))


GATE_LO = 4 * A_W
GATE_HI = GATE_LO + 4 * H_A
IN_COLS = GATE_HI + B_W + 2 * KV_B * HEAD_DIM + 3 * C_W


def _inproj_kernel(x_ref, g_ref, sh_ref, sc_ref, w_ref, *refs):
    o_refs, w_s = refs[:-1], refs[-1]

    @pl.when(pl.program_id(0) == 0)
    def _():
        w_s[:, :GATE_LO] = w_ref[:, :GATE_LO].astype(BF16)
        w_s[:, GATE_LO:IN_COLS - 4 * H_A] = w_ref[:, GATE_HI:].astype(BF16)
        tail = jnp.concatenate([w_ref[:, GATE_LO:GATE_HI], jnp.zeros((D_MODEL, LANES - 4 * H_A), F32)], axis=1)
        w_s[:, IN_COLS - 4 * H_A:] = tail.astype(BF16)

    h = _rms(x_ref[...], g_ref[...]) * (1.0 + sc_ref[...]) + sh_ref[...]
    acc = _dot(h.astype(BF16), w_s[...])
    off = 0
    for o_ref, (_, width) in zip(o_refs, IN_SPLIT):
        o_ref[...] = acc[:, off:off + width]
        off += width


def in_projection(x, mod, norm_g, w_in, l):
    tm = 512
    return pl.pallas_call(
        _inproj_kernel,
        grid=(N_TOK // tm,),
        in_specs=[pl.BlockSpec((tm, D_MODEL), lambda i: (i, 0)),
                  pl.BlockSpec((None, 1, D_MODEL), lambda i: (l, 0, 0)),
                  pl.BlockSpec((None, None, 1, D_MODEL), lambda i: (l, _cond_row(i, tm), 0, 0)),
                  pl.BlockSpec((None, None, 1, D_MODEL), lambda i: (l, _cond_row(i, tm), 0, 1)),
                  pl.BlockSpec((None, D_MODEL, IN_COLS), lambda i: (l, 0, 0))],
        out_specs=[pl.BlockSpec((tm, w), lambda i: (i, 0)) for _, w in IN_SPLIT],
        out_shape=[jax.ShapeDtypeStruct((N_TOK, w), F32) for _, w in IN_SPLIT],
        scratch_shapes=[pltpu.VMEM((D_MODEL, IN_PAD), BF16)],
        compiler_params=_cparams(("arbitrary",)),
        name="in_projection",
    )(x, norm_g, mod, mod, w_in)


SOLVE_BLOCK = 16


def _delta_kernel(*refs, seq_len, n_sub, n_cp, aliased):
    qkv_ref, z_ref, gate_ref, cw_ref, alog_ref, dtb_ref, ng_ref, s0_ref = refs[:8]
    (o_ref, sfin_ref, pad_s, kk_s, qq_s, vk_s, gc_s, beta_s, u_s, w_s, qd_s, a_s, kdt_s, egt_s, sa_s, o_s,
     tp_s, t32_s, rp_s, c32_s, m_s, x_s, vbd_s, v_s, op_s) = refs[9 if aliased else 8:]
    n_chunk = seq_len // CHUNK
    n_pair = H_A // 2
    n_unit = n_sub * H_A
    half = HEAD_DIM
    lo = _lane((CHUNK, LANES)) < half
    lo_row = _lane((1, LANES)) < half
    lo16 = _lane((CHUNK, LANES)).astype(F32).astype(BF16) < half
    row = lax.broadcasted_iota(jnp.int32, (CHUNK, LANES), 0)
    col = _lane((CHUNK, LANES)) % half
    ahead = jnp.where(lo, row - col, col - row)
    incl = ahead >= 0
    strict = ahead > 0
    diag_blk = row // SOLVE_BLOCK == col // SOLVE_BLOCK
    big_row = lax.broadcasted_iota(jnp.int32, (LANES, LANES), 0)
    anti = big_row // half + _lane((LANES, LANES)) // half == 1
    eye = (big_row == _lane((LANES, LANES))).astype(F32)

    def both(a):
        return jnp.concatenate([jnp.where(lo, a, 0.0), jnp.where(lo, 0.0, a)], axis=0)

    rb = seq_len
    lo_rb = _lane((rb, LANES)) < half
    zero8 = jnp.zeros((8, 3 * A_W), F32)
    pad_s[0:8, :] = zero8
    pad_s[8 + seq_len:16 + seq_len, :] = zero8

    def l2n(x):
        return x * lax.rsqrt(0.5 * jnp.sum(x * x, axis=-1, keepdims=True) + EPS)

    for sq in range(n_sub):
        pad_s[8:8 + seq_len, :] = qkv_ref[sq * seq_len:(sq + 1) * seq_len, :]
        for r0 in range(0, seq_len, rb):
            y = _silu(pad_s[7 + r0:7 + r0 + rb, :] * cw_ref[0:1, :] + pad_s[8 + r0:8 + r0 + rb, :] * cw_ref[1:2, :]
                      + pad_s[9 + r0:9 + r0 + rb, :] * cw_ref[2:3, :])

            def dup(part, p):
                c = y[:, part * A_W + p * LANES:part * A_W + (p + 1) * LANES]
                r = pltpu.roll(c, half, axis=1)
                return jnp.where(lo_rb, c, r), jnp.where(lo_rb, r, c)

            for p in range(n_pair):
                qs, ks, vs = dup(0, p), dup(1, p), dup(2, p)
                for e in range(2):
                    u = sq * H_A + 2 * p + e
                    kk = l2n(ks[e])
                    kk_s[u, r0:r0 + rb, :] = kk
                    qq_s[u, r0:r0 + rb, :] = l2n(qs[e]) * (HEAD_DIM ** -0.5)
                    vk_s[u, r0:r0 + rb, :] = jnp.where(lo_rb, vs[e], kk)

    cb = 256
    ci = lax.broadcasted_iota(jnp.int32, (cb, cb), 0)
    cj = lax.broadcasted_iota(jnp.int32, (cb, cb), 1)
    same = ci // CHUNK == cj // CHUNK
    m_pre = jnp.where(same & (ci >= cj), 1.0, 0.0).astype(BF16)
    m_suf = jnp.where(same & (ci <= cj), 1.0, 0.0).astype(BF16)
    fwd_cols = (_lane((cb, LANES)) >= GATE_A) & (_lane((cb, LANES)) < GATE_A + H_A)
    for r0 in range(0, n_sub * seq_len, cb):
        gt = gate_ref[r0:r0 + cb, :]
        beta_s[r0:r0 + cb, :] = jax.nn.sigmoid(gt)
        g = -jnp.exp(alog_ref[...]) * _softplus(gt + dtb_ref[...])
        g1 = g.astype(BF16)
        r1 = g - g1.astype(F32)
        g2 = r1.astype(BF16)
        g3 = (r1 - g2.astype(F32)).astype(BF16)
        blk = jnp.concatenate([g1, g2, g3], axis=1)
        pre, suf = _dot(m_pre, blk), _dot(m_suf, blk)
        pre = pre[:, :LANES] + pre[:, LANES:2 * LANES] + pre[:, 2 * LANES:]
        suf = suf[:, :LANES] + suf[:, LANES:2 * LANES] + suf[:, 2 * LANES:]
        gc_s[r0:r0 + cb, :] = jnp.where(fwd_cols, pre, suf)

    zero = jnp.zeros((half, half), F32)
    for sq in range(n_sub):
        for h in range(H_A):
            sa_s[sq * H_A + h] = jnp.concatenate([jnp.concatenate([zero, s0_ref[sq, 1, h]], axis=1),
                                                  jnp.concatenate([s0_ref[sq, 0, h], zero], axis=1)], axis=0)

    n_fac = int(math.log2(SOLVE_BLOCK))

    def fill_stage(c, rows, cc):
        for sq in range(n_sub):
            grow = pl.ds(pl.multiple_of(sq * seq_len + c * CHUNK, CHUNK), CHUNK)
            gcb = gc_s[grow, :]
            gct = gcb.T
            bb = beta_s[grow, :]
            for h in range(H_A):
                u = sq * H_A + h
                w = cc * n_unit + u
                full = (CHUNK, LANES)
                cff = jnp.broadcast_to(gcb[:, GATE_A + h:GATE_A + h + 1], full)
                cbb = jnp.broadcast_to(gcb[:, GATE_A + H_A + h:GATE_A + H_A + h + 1], full)
                bff = jnp.broadcast_to(bb[:, GATE_BETA + h:GATE_BETA + h + 1], full)
                bbb = jnp.broadcast_to(bb[:, GATE_BETA + H_A + h:GATE_BETA + H_A + h + 1], full)
                gr_fb = jnp.concatenate([gct[GATE_A + h:GATE_A + h + 1, :],
                                         gct[GATE_A + H_A + h:GATE_A + H_A + h + 1, :]], axis=1)
                decay = jnp.where(incl, jnp.exp(jnp.where(incl, jnp.where(lo, cff, cbb) - gr_fb, 0.0)), 0.0)
                egf, egb = jnp.exp(cff), jnp.exp(cbb)
                tot_f = gcb[CHUNK - 1:CHUNK, GATE_A + h:GATE_A + h + 1]
                tot_b = gcb[0:1, GATE_A + H_A + h:GATE_A + H_A + h + 1]
                kk, qq, vk = kk_s[u, rows, :], qq_s[u, rows, :], vk_s[u, rows, :]
                k16 = kk[:, :half].astype(BF16)
                kq = jnp.concatenate([k16, qq[:, :half].astype(BF16)], axis=0)
                gram = _dot_nt(kq, jnp.concatenate([k16, k16], axis=0))
                neg_l = jnp.where(strict, gram[:CHUNK] * jnp.where(lo, -bff, -bbb) * decay, 0.0)
                a_s[u, rows, :] = jnp.where(incl, gram[CHUNK:] * decay, 0.0).astype(BF16)
                neg_d = jnp.where(diag_blk, neg_l, 0.0)
                pd = both(neg_d)
                t32_s[w] = eye + pd
                tp_s[w, :, :LANES] = (eye + pd).astype(BF16)
                tp_s[w, :, LANES:] = pd.astype(BF16)
                x_f = vk * jnp.where(lo, bff, bff * egf)
                x_b = vk * jnp.where(lo, bbb, bbb * egb)
                rp_s[w, :, :LANES] = jnp.concatenate([x_f, x_b], axis=0).astype(BF16)
                rp_s[w, :, LANES:] = both(neg_l - neg_d).astype(BF16)
                qd_s[u, rows, :] = (qq * jnp.where(lo, egb, egf)).astype(BF16)
                kd = kk * jnp.exp(jnp.where(lo, tot_b - cbb, tot_f - cff))
                kdt_s[u, c] = kd.T.astype(BF16)
                egt_s[u, c] = jnp.exp(jnp.where(lo_row, tot_f, tot_b))

    def prep(ci, carry):
        rows_of = []
        for cc in range(n_cp):
            c = ci * n_cp + cc
            rows_of.append(pl.ds(pl.multiple_of(c * CHUNK, CHUNK), CHUNK))
            fill_stage(c, rows_of[cc], cc)
        n_slot = n_cp * n_unit
        for w in range(n_slot):
            tp_s[w, :, LANES:] = _dot(tp_s[w, :, LANES:], tp_s[w, :, LANES:]).astype(BF16)
        for j in range(1, n_fac):
            for w in range(n_slot):
                if j + 1 < n_fac:
                    tp = _dot(tp_s[w, :, LANES:], tp_s[w])
                    t = t32_s[w] + tp[:, :LANES]
                    t32_s[w] = t
                    tp_s[w, :, :LANES] = t.astype(BF16)
                    tp_s[w, :, LANES:] = tp[:, LANES:].astype(BF16)
                else:
                    t = t32_s[w] + _dot(tp_s[w, :, LANES:], tp_s[w, :, :LANES])
                    tp_s[w, :, :LANES] = t.astype(BF16)
        for w in range(n_slot):
            cm = _dot(tp_s[w, :, :LANES], rp_s[w])
            c32_s[w] = cm[:, :LANES]
            rp_s[w] = cm.astype(BF16)
        assert CHUNK // SOLVE_BLOCK == 4
        for w in range(n_slot):
            mm = _dot(rp_s[w, :, LANES:], rp_s[w])
            y1 = c32_s[w] + mm[:, :LANES]
            c32_s[w] = y1
            x_s[w] = y1.astype(BF16)
            m_s[w] = mm[:, LANES:].astype(BF16)
        for w in range(n_slot):
            x = c32_s[w] + _dot(m_s[w], x_s[w])
            x_f, x_b = x[:CHUNK], pltpu.roll(x[CHUNK:], half, axis=1)
            u_s[w % n_unit, rows_of[w // n_unit], :] = jnp.where(lo, x_f, x_b)
            w_s[w % n_unit, rows_of[w // n_unit], :] = jnp.where(lo, x_b, x_f).astype(BF16)
        return carry

    lax.fori_loop(0, n_chunk // n_cp, prep, 0)

    def scan(s, carry):
        rf = pl.ds(pl.multiple_of(s * CHUNK, CHUNK), CHUNK)
        sb = n_chunk - 1 - s
        rbk = pl.ds(pl.multiple_of(sb * CHUNK, CHUNK), CHUNK)
        for u in range(n_unit):
            w = jnp.where(lo16, w_s[u, rbk, :], w_s[u, rf, :])
            qd = jnp.where(lo16, qd_s[u, rbk, :], qd_s[u, rf, :])
            wq = _dot(jnp.concatenate([w, qd], axis=0), sa_s[u].astype(BF16))
            v = jnp.where(lo, u_s[u, rf, :], u_s[u, rbk, :]) - wq[:CHUNK]
            v_s[u] = v.astype(BF16)
            vbd_s[u] = both(v).astype(BF16)
            op_s[u] = wq[CHUNK:]
        for u in range(n_unit):
            a = jnp.where(lo16, a_s[u, rf, :], a_s[u, rbk, :])
            o = op_s[u] + _dot(a, vbd_s[u])
            o_s[u, rf, 0:half] = o[:, :half]
            o_s[u, rbk, half:LANES] = o[:, half:]
            kdt = jnp.concatenate([kdt_s[u, sb][:half], kdt_s[u, s][half:]], axis=0)
            egt = jnp.where(lo_row, egt_s[u, s], egt_s[u, sb])
            sa_s[u] = sa_s[u] * egt + jnp.where(anti, _dot(kdt, v_s[u]), 0.0)
        return carry

    lax.fori_loop(0, n_chunk, scan, 0)

    for sq in range(n_sub):
        for r0 in range(0, seq_len, rb):
            tr = sq * seq_len + r0
            for p in range(n_pair):
                nrm = []
                for e in range(2):
                    ofb = o_s[sq * H_A + 2 * p + e, r0:r0 + rb, :]
                    oo = ofb + pltpu.roll(ofb, half, axis=1)
                    nrm.append(oo * lax.rsqrt(jnp.sum(oo * oo, axis=-1, keepdims=True) * (0.5 / HEAD_DIM) + EPS))
                o_ref[tr:tr + rb, p * LANES:(p + 1) * LANES] = (
                    jnp.where(lo_rb, nrm[0], nrm[1]) * ng_ref[...]
                    * _silu(z_ref[tr:tr + rb, p * LANES:(p + 1) * LANES]))
        for h in range(H_A):
            st = sa_s[sq * H_A + h]
            sfin_ref[sq, 0, h] = st[half:, :half]
            sfin_ref[sq, 1, h] = st[:half, half:]


def delta_mixer(qkv, z, gate, cw, alog_row, dtb_row, ng2, s0, *, n_seq, seq_len, tok0, n_sub,
                all_states=None, layer=0):
    n_chunk = seq_len // CHUNK
    n_unit = n_sub * H_A
    blk = n_sub * seq_len
    b0 = tok0 // blk
    tok = lambda w: pl.BlockSpec((blk, w), lambda b: (b0 + b, 0))
    full = lambda a: pl.BlockSpec(a.shape, lambda b: (0,) * a.ndim)
    state = pl.BlockSpec((n_sub, 2, H_A, HEAD_DIM, HEAD_DIM), lambda b: (b, 0, 0, 0, 0))
    per_unit = lambda dt: pltpu.VMEM((n_unit, seq_len, LANES), dt)
    n_cp = 2
    stage = lambda rows, cols, dt, n=n_unit: pltpu.VMEM((n, rows, cols), dt)
    aliased = all_states is not None
    in_specs = [tok(3 * A_W), tok(A_W), tok(LANES), full(cw), full(alog_row), full(dtb_row), full(ng2), state]
    args = [qkv, z, gate, cw, alog_row, dtb_row, ng2, s0]
    if aliased:
        in_specs.append(pl.BlockSpec(memory_space=pl.ANY))
        args.append(all_states)
        state_out = pl.BlockSpec((n_sub, None, 2, H_A, HEAD_DIM, HEAD_DIM), lambda b: (b, layer, 0, 0, 0, 0))
        state_shape = jax.ShapeDtypeStruct(all_states.shape, F32)
    else:
        state_out = state
        state_shape = jax.ShapeDtypeStruct((n_seq, 2, H_A, HEAD_DIM, HEAD_DIM), F32)
    return pl.pallas_call(
        functools.partial(_delta_kernel, seq_len=seq_len, n_sub=n_sub, n_cp=n_cp, aliased=aliased),
        grid=(n_seq // n_sub,),
        in_specs=in_specs,
        out_specs=[pl.BlockSpec((blk, A_W), lambda b: (b, 0)), state_out],
        out_shape=[jax.ShapeDtypeStruct((n_seq * seq_len, A_W), F32), state_shape],
        input_output_aliases={len(args) - 1: 1} if aliased else {},
        scratch_shapes=[pltpu.VMEM((seq_len + 16, 3 * A_W), F32),
                        per_unit(F32), per_unit(F32), per_unit(F32),
                        pltpu.VMEM((blk, LANES), F32),
                        pltpu.VMEM((blk, LANES), F32),
                        per_unit(F32), per_unit(BF16), per_unit(BF16), per_unit(BF16),
                        pltpu.VMEM((n_unit, n_chunk, LANES, HEAD_DIM), BF16),
                        pltpu.VMEM((n_unit, n_chunk, 1, LANES), F32),
                        stage(LANES, LANES, F32),
                        per_unit(F32),
                        stage(LANES, 2 * LANES, BF16, n_cp * n_unit),
                        stage(LANES, LANES, F32, n_cp * n_unit),
                        stage(LANES, 2 * LANES, BF16, n_cp * n_unit),
                        stage(LANES, LANES, F32, n_cp * n_unit),
                        stage(LANES, LANES, BF16, n_cp * n_unit),
                        stage(LANES, LANES, BF16, n_cp * n_unit),
                        stage(LANES, LANES, BF16),
                        stage(CHUNK, LANES, BF16),
                        stage(CHUNK, LANES, F32)],
        compiler_params=_cparams(("arbitrary",)),
        name=f"delta_mixer_{seq_len}",
    )(*args)


def _rope(x, cos, sin_signed, quarter):
    width = x.shape[-1]
    swapped = jnp.where((_lane(x.shape) % (2 * quarter)) < quarter,
                        pltpu.roll(x, width - quarter, axis=1), pltpu.roll(x, quarter, axis=1))
    return x * cos + swapped * sin_signed


def _rope_tables(rows, dim):
    nf = dim // 4
    inv = ROPE_THETA ** (-jnp.arange(nf, dtype=F32) / nf)
    r = jnp.repeat(jnp.arange(rows, dtype=F32), GRID_W)
    c = jnp.tile(jnp.arange(GRID_W, dtype=F32), rows)
    ar, ac = r[:, None] * inv, c[:, None] * inv
    cos = jnp.concatenate([jnp.cos(ar), jnp.cos(ar), jnp.cos(ac), jnp.cos(ac)], axis=1)
    sin = jnp.concatenate([-jnp.sin(ar), jnp.sin(ar), -jnp.sin(ac), jnp.sin(ac)], axis=1)
    return jnp.tile(cos, (1, LANES // dim)), jnp.tile(sin, (1, LANES // dim))


def _gqa_kernel(*refs, cached, n_sub):
    if cached:
        (q_ref, k_ref, v_ref, gq_ref, gk_ref, ck_ref, cv_ref, cosq_ref, sinq_ref, cosk_ref, sink_ref,
         o_ref, k_s, v_s) = refs
    else:
        q_ref, k_ref, v_ref, gq_ref, gk_ref, o_ref, kn_ref, k_s, v_s = refs
    gmat = _group_mean_matrix(LANES, HEAD_DIM)
    quarter = HEAD_DIM // 4
    n_k = k_ref.shape[0] // n_sub

    @pl.when(pl.program_id(1) == 0)
    def _():
        for sq in range(n_sub):
            k = _group_rms(k_ref[sq * n_k:(sq + 1) * n_k, :], gmat, gk_ref[...])
            v = v_ref[sq * n_k:(sq + 1) * n_k, :]
            if cached:
                k = _rope(k, cosk_ref[...], sink_ref[...], quarter)
                k = jnp.concatenate([ck_ref[...], k], axis=0)
                v = jnp.concatenate([cv_ref[...], v], axis=0)
            else:
                kn_ref[sq * n_k:(sq + 1) * n_k, :] = k
            k_s[sq] = k.astype(BF16)
            v_s[sq] = v.astype(BF16)

    tq = q_ref.shape[0] // n_sub
    lo = _lane((tq, LANES)) < HEAD_DIM
    group = H_B // KV_B
    q_scale = (HEAD_DIM ** -0.5) * math.log2(math.e)
    for sq in range(n_sub):
        k, v = k_s[sq], v_s[sq]
        qs = []
        for j in range(H_B // 2):
            q = _group_rms(q_ref[sq * tq:(sq + 1) * tq, j * LANES:(j + 1) * LANES], gmat, gq_ref[...])
            if cached:
                q = _rope(q, cosq_ref[...], sinq_ref[...], quarter)
            q = q * q_scale
            qs.append((q, pltpu.roll(q, HEAD_DIM, axis=1)))
        outs = [None] * H_B
        for kv in range(KV_B):
            keep = lo if kv == 0 else ~lo
            heads = range(kv * group, (kv + 1) * group)
            q3 = jnp.concatenate([jnp.where(keep, qs[h // 2][0 if h % 2 == kv else 1], 0.0).astype(BF16)
                                  for h in heads], axis=0)
            s = _dot_nt(q3, k)
            p = jnp.exp2(s - jnp.max(s, axis=-1, keepdims=True))
            r = _dot(p.astype(BF16), v) / jnp.sum(p, axis=-1, keepdims=True)
            for i, h in enumerate(heads):
                rh = r[i * tq:(i + 1) * tq]
                outs[h] = rh if h % 2 == kv else pltpu.roll(rh, HEAD_DIM, axis=1)
        for j in range(H_B // 2):
            o_ref[sq * tq:(sq + 1) * tq, j * LANES:(j + 1) * LANES] = jnp.where(lo, outs[2 * j], outs[2 * j + 1])


def gqa_mixer(q, k, v, gq, gk, *, n_seq, seq_len, tok0, cache=None, rope=None, layer=0):
    cached = cache is not None
    n_sub = 1 if cached else 2
    n_seq, seq_len = n_seq // n_sub, seq_len * n_sub
    tq = 256 * n_sub
    nq = seq_len // tq
    b0 = tok0 // seq_len
    q0 = tok0 // tq
    kvw = KV_B * HEAD_DIM
    in_specs = [pl.BlockSpec((tq, B_W), lambda b, i: (q0 + b * nq + i, 0)),
                pl.BlockSpec((seq_len, kvw), lambda b, i: (b0 + b, 0)),
                pl.BlockSpec((seq_len, kvw), lambda b, i: (b0 + b, 0)),
                pl.BlockSpec((1, LANES), lambda b, i: (0, 0)),
                pl.BlockSpec((1, LANES), lambda b, i: (0, 0))]
    args = [q, k, v, gq, gk]
    out_specs = [pl.BlockSpec((tq, B_W), lambda b, i: (b * nq + i, 0))]
    out_shape = [jax.ShapeDtypeStruct((n_seq * seq_len, B_W), F32)]
    if cached:
        cos, sin = rope
        cache_spec = pl.BlockSpec((None, None, PAST_LEN, kvw), lambda b, i: (b, layer, 0, 0))
        in_specs += [cache_spec, cache_spec,
                     pl.BlockSpec((tq, LANES), lambda b, i: (i, 0)),
                     pl.BlockSpec((tq, LANES), lambda b, i: (i, 0)),
                     pl.BlockSpec((seq_len, LANES), lambda b, i: (0, 0)),
                     pl.BlockSpec((seq_len, LANES), lambda b, i: (0, 0))]
        args += [cache[0], cache[1], cos, sin, cos, sin]
    else:
        out_specs.append(pl.BlockSpec((seq_len, kvw), lambda b, i: (b, 0)))
        out_shape.append(jax.ShapeDtypeStruct((n_seq * seq_len, kvw), F32))
    n_keys = seq_len // n_sub + (PAST_LEN if cached else 0)
    return pl.pallas_call(
        functools.partial(_gqa_kernel, cached=cached, n_sub=n_sub),
        grid=(n_seq, nq),
        in_specs=in_specs, out_specs=out_specs, out_shape=out_shape,
        scratch_shapes=[pltpu.VMEM((n_sub, n_keys, kvw), BF16), pltpu.VMEM((n_sub, n_keys, kvw), BF16)],
        compiler_params=_cparams(("arbitrary", "arbitrary")),
        name=f"gqa_mixer_{seq_len}",
    )(*args)


def _diff_kernel(*refs, cached, lam_init, n_sub):
    if cached:
        (q_ref, k_ref, v_ref, lam_ref, gn_ref, ck_ref, cv_ref, cosq_ref, sinq_ref, cosk_ref, sink_ref,
         o_ref) = refs
    else:
        q_ref, k_ref, v_ref, lam_ref, gn_ref, o_ref = refs
    quarter = DIFF_HD // 4
    lv = lam_ref[...]
    lam = (jnp.exp(jnp.sum(lv[0:1] * lv[1:2], axis=-1, keepdims=True))
           - jnp.exp(jnp.sum(lv[2:3] * lv[3:4], axis=-1, keepdims=True)) + lam_init)
    gmat = _group_mean_matrix(C_W, 2 * DIFF_HD)
    tq, n_k = q_ref.shape[0] // n_sub, k_ref.shape[0] // n_sub
    lane = _lane((tq, C_W))
    for sb in range(n_sub):
        q = q_ref[sb * tq:(sb + 1) * tq, :]
        k, v = k_ref[sb * n_k:(sb + 1) * n_k, :], v_ref[sb * n_k:(sb + 1) * n_k, :]
        if cached:
            cq = jnp.concatenate([cosq_ref[...]] * 2, axis=1)
            sq = jnp.concatenate([sinq_ref[...]] * 2, axis=1)
            ck = jnp.concatenate([cosk_ref[...]] * 2, axis=1)
            sk = jnp.concatenate([sink_ref[...]] * 2, axis=1)
            q = _rope(q, cq, sq, quarter)
            k = _rope(k, ck, sk, quarter)
            k = jnp.concatenate([ck_ref[...], k], axis=0)
            v = jnp.concatenate([cv_ref[...], v], axis=0)
        k = k.astype(BF16)
        v = v.astype(BF16)
        q = q * ((DIFF_HD ** -0.5) * math.log2(math.e))
        o = jnp.zeros(q.shape, F32)
        for h in range(H_C):
            ps, cs = [], []
            for m in range(2):
                qm = jnp.where(lane // DIFF_HD == 2 * h + m, q, 0.0).astype(BF16)
                s = _dot_nt(qm, k)
                p = jnp.exp2(s - jnp.max(s, axis=-1, keepdims=True))
                ps.append(p)
                cs.append((1.0 if m == 0 else lam) / jnp.sum(p, axis=-1, keepdims=True))
            a = ps[0] * cs[0] - ps[1] * cs[1]
            o = jnp.where(lane // (2 * DIFF_HD) == h, _dot(a.astype(BF16), v), o)
        o_ref[sb * tq:(sb + 1) * tq, :] = _group_rms(o, gmat, gn_ref[...]) * (1.0 - lam_init)


def diff_mixer(q, k, v, lam_p, gn, lam_init, *, n_seq, seq_len, tok0, cache=None, rope=None, layer=0):
    cached = cache is not None
    n_sub = 1 if cached else 2
    n_seq, seq_len = n_seq // n_sub, seq_len * n_sub
    tq = 256 * n_sub
    nq = seq_len // tq
    b0 = tok0 // seq_len
    q0 = tok0 // tq
    in_specs = [pl.BlockSpec((tq, C_W), lambda b, i: (q0 + b * nq + i, 0)),
                pl.BlockSpec((seq_len, C_W), lambda b, i: (b0 + b, 0)),
                pl.BlockSpec((seq_len, C_W), lambda b, i: (b0 + b, 0)),
                pl.BlockSpec((4, DIFF_HD), lambda b, i: (0, 0)),
                pl.BlockSpec((1, C_W), lambda b, i: (0, 0))]
    args = [q, k, v, lam_p, gn]
    if cached:
        cos, sin = rope
        cache_spec = pl.BlockSpec((None, None, PAST_LEN, C_W), lambda b, i: (b, layer, 0, 0))
        in_specs += [cache_spec, cache_spec,
                     pl.BlockSpec((tq, LANES), lambda b, i: (i, 0)),
                     pl.BlockSpec((tq, LANES), lambda b, i: (i, 0)),
                     pl.BlockSpec((seq_len, LANES), lambda b, i: (0, 0)),
                     pl.BlockSpec((seq_len, LANES), lambda b, i: (0, 0))]
        args += [cache[0], cache[1], cos, sin, cos, sin]
    return pl.pallas_call(
        functools.partial(_diff_kernel, cached=cached, lam_init=lam_init, n_sub=n_sub),
        grid=(n_seq, nq),
        in_specs=in_specs,
        out_specs=pl.BlockSpec((tq, C_W), lambda b, i: (b * nq + i, 0)),
        out_shape=jax.ShapeDtypeStruct((n_seq * seq_len, C_W), F32),
        compiler_params=_cparams(("arbitrary", "arbitrary")),
        name=f"diff_mixer_{seq_len}",
    )(*args)


def _outproj_kernel(*refs, routed, n_mix):
    x_ref = refs[0]
    mix_refs = refs[1:1 + 2 * n_mix]
    rest = refs[1 + 2 * n_mix:]
    if routed:
        w_ref, gm_ref, g_ref, sh_ref, sc_ref, r_ref, x1_ref, h_ref, lg_ref = rest
    else:
        w_ref, gm_ref, g_ref, sh_ref, sc_ref, x1_ref, h_ref = rest
    is_ctx = pl.program_id(0) < N_CTX // x_ref.shape[0]
    mixed = jnp.concatenate(
        [jnp.where(is_ctx, mix_refs[2 * m][...], mix_refs[2 * m + 1][...]) for m in range(n_mix)], axis=1)
    x1 = x_ref[...] + gm_ref[...] * _dot(mixed.astype(BF16), w_ref[...])
    x1_ref[...] = x1
    h = _rms(x1, g_ref[...]) * (1.0 + sc_ref[...]) + sh_ref[...]
    h_ref[...] = h.astype(h_ref.dtype)
    if routed:
        lg_ref[...] = _top2_route(_dot_split(h, r_ref[...]))


def out_projection(x, mixes, w_out_b, mod, norm_g, l, router_p=None):
    tm = 512
    n_ctx_tiles = N_CTX // tm
    n_lat_tiles = N_LAT // tm
    routed = router_p is not None
    mod_spec = lambda k: pl.BlockSpec((None, None, 1, D_MODEL), lambda i: (l, _cond_row(i, tm), 0, k))
    in_specs = [pl.BlockSpec((tm, D_MODEL), lambda i: (i, 0))]
    args = [x]
    for ctx_arr, lat_arr in mixes:
        w = ctx_arr.shape[1]
        in_specs.append(pl.BlockSpec((tm, w), lambda i: (jnp.minimum(i, n_ctx_tiles - 1), 0)))
        in_specs.append(pl.BlockSpec((tm, w), lambda i: (jnp.clip(i - n_ctx_tiles, 0, n_lat_tiles - 1), 0)))
        args += [ctx_arr, lat_arr]
    in_specs += [pl.BlockSpec((None, D_MODEL, D_MODEL), lambda i: (l, 0, 0)),
                 mod_spec(2),
                 pl.BlockSpec((None, 1, D_MODEL), lambda i: (l, 0, 0)),
                 mod_spec(3), mod_spec(4)]
    args += [w_out_b, mod, norm_g, mod, mod]
    out_specs = [pl.BlockSpec((tm, D_MODEL), lambda i: (i, 0)),
                 pl.BlockSpec((tm, D_MODEL), lambda i: (i, 0))]
    out_shape = [jax.ShapeDtypeStruct((N_TOK, D_MODEL), F32),
                 jax.ShapeDtypeStruct((N_TOK, D_MODEL), F32 if routed else BF16)]
    if routed:
        in_specs.append(pl.BlockSpec((D_MODEL, LANES), lambda i: (0, 0)))
        args.append(router_p)
        out_specs.append(pl.BlockSpec((tm, LANES), lambda i: (i, 0)))
        out_shape.append(jax.ShapeDtypeStruct((N_TOK, LANES), F32))
    return pl.pallas_call(
        functools.partial(_outproj_kernel, routed=routed, n_mix=len(mixes)),
        grid=(N_TOK // tm,),
        in_specs=in_specs, out_specs=out_specs, out_shape=out_shape,
        compiler_params=_cparams(("arbitrary",)),
        name="out_projection",
    )(*args)


def _ffn_kernel(h_ref, wg_ref, wu_ref, wd_ref, x_ref, gf_ref, o_ref, acc_ref):
    j = pl.program_id(1)

    @pl.when(j == 0)
    def _():
        acc_ref[...] = jnp.zeros_like(acc_ref)

    h = h_ref[...]
    act = _silu(_dot(h, wg_ref[...])) * _dot(h, wu_ref[...])
    acc_ref[...] += _dot(act.astype(BF16), wd_ref[...])

    @pl.when(j == pl.num_programs(1) - 1)
    def _():
        o_ref[...] = x_ref[...] + gf_ref[...] * acc_ref[...]


def ffn(h, w_gu, w_d, x1, mod, l):
    tm = 512
    lyr = l // 2
    n_chunk = D_FF // FF_CHUNK
    gu_spec = lambda half: pl.BlockSpec((None, D_MODEL, FF_CHUNK), lambda i, j: (lyr, 0, half * n_chunk + j))
    return pl.pallas_call(
        _ffn_kernel,
        grid=(N_TOK // tm, n_chunk),
        in_specs=[pl.BlockSpec((tm, D_MODEL), lambda i, j: (i, 0)),
                  gu_spec(0), gu_spec(1),
                  pl.BlockSpec((None, FF_CHUNK, D_MODEL), lambda i, j: (lyr, j, 0)),
                  pl.BlockSpec((tm, D_MODEL), lambda i, j: (i, 0)),
                  pl.BlockSpec((None, None, 1, D_MODEL), lambda i, j: (l, _cond_row(i, tm), 0, 5))],
        out_specs=pl.BlockSpec((tm, D_MODEL), lambda i, j: (i, 0)),
        out_shape=jax.ShapeDtypeStruct((N_TOK, D_MODEL), F32),
        scratch_shapes=[pltpu.VMEM((tm, D_MODEL), F32)],
        compiler_params=_cparams(("arbitrary", "arbitrary")),
        name="ffn_dense",
    )(h, w_gu, w_gu, w_d, x1, mod)


MOE_TILE = 256
MOE_SLOTS = TOP_K * N_TOK + N_EXP * MOE_TILE
MOE_TILES = MOE_SLOTS // MOE_TILE
ROUTE_I1, ROUTE_I2, ROUTE_W1, ROUTE_W2 = 0, 1, 2, 3


def _top2_route(logits):
    lane = _lane(logits.shape)
    neg = jnp.float32(-jnp.inf)
    lg = jnp.where(lane < N_EXP, logits, neg)
    t1 = jnp.max(lg, axis=-1, keepdims=True)
    i1 = jnp.min(jnp.where(lg == t1, lane, LANES), axis=-1, keepdims=True)
    lg2 = jnp.where(lane == i1, neg, lg)
    t2 = jnp.max(lg2, axis=-1, keepdims=True)
    i2 = jnp.min(jnp.where(lg2 == t2, lane, LANES), axis=-1, keepdims=True)
    e2 = jnp.exp(t2 - t1)
    w1 = 1.0 / (1.0 + e2)
    w2 = e2 / (1.0 + e2)
    rec = jnp.where(lane == ROUTE_I1, i1.astype(F32), 0.0) + jnp.where(lane == ROUTE_I2, i2.astype(F32), 0.0)
    return rec + jnp.where(lane == ROUTE_W1, w1, 0.0) + jnp.where(lane == ROUTE_W2, w2, 0.0)


def _routing_tables(route):
    n_assign = TOP_K * N_TOK
    expert = jnp.concatenate([route[:, ROUTE_I1], route[:, ROUTE_I2]]).astype(jnp.int32)
    index = jnp.arange(n_assign, dtype=jnp.int32)
    onehot = (expert[:, None] == jnp.arange(N_EXP, dtype=jnp.int32)[None, :]).astype(jnp.int32)
    running = jnp.cumsum(onehot, axis=0)
    counts = running[-1]
    rank = jnp.sum(running * onehot, axis=1) - 1
    padded = (counts + MOE_TILE - 1) // MOE_TILE * MOE_TILE
    ends = jnp.cumsum(padded)
    starts = ends - padded
    slot = jnp.sum(onehot * starts[None, :], axis=1) + rank
    sorted_token = (jnp.sort(expert * n_assign + index) % n_assign) % N_TOK
    tile_start = jnp.arange(MOE_TILES, dtype=jnp.int32) * MOE_TILE
    tile_expert = jnp.minimum(jnp.sum(tile_start[:, None] >= ends[None, :], axis=1), N_EXP - 1).astype(jnp.int32)
    first_sorted = (jnp.cumsum(counts) - counts)[tile_expert] + tile_start - starts[tile_expert]
    n_used = (ends[-1] // MOE_TILE).astype(jnp.int32).reshape(1)
    return sorted_token, slot, tile_expert, first_sorted.astype(jnp.int32), n_used


def _row_copy(src_hbm, row, dst, dst_row, sem):
    return pltpu.make_async_copy(src_hbm.at[pl.ds(row, 1)], dst.at[pl.ds(dst_row, 1)], sem)


def _experts_kernel(tok_ref, exp_ref, first_ref, used_ref, h_hbm, wg_ref, wu_ref, wd_ref, y_ref, x_buf, sems,
                    wg_s, wu_s, wd_s):
    t = pl.program_id(0)
    n_used = used_ref[0]

    def start_gather(tile, half):
        first = first_ref[tile]
        for r in range(MOE_TILE):
            tok = tok_ref[jnp.minimum(first + r, TOP_K * N_TOK - 1)]
            _row_copy(h_hbm, tok, x_buf.at[half], r, sems.at[half]).start(priority=r % 2)

    @pl.when(t == 0)
    def _():
        start_gather(0, 0)

    @pl.when(t + 1 < n_used)
    def _():
        start_gather(t + 1, (t + 1) % 2)

    @pl.when(t < n_used)
    def _():
        half = t % 2
        for r in range(MOE_TILE):
            _row_copy(h_hbm, 0, x_buf.at[half], r, sems.at[half]).wait()
        @pl.when((t == 0) | (exp_ref[t] != exp_ref[jnp.maximum(t - 1, 0)]))
        def _():
            wg_s[...] = wg_ref[...].astype(BF16)
            wu_s[...] = wu_ref[...].astype(BF16)
            wd_s[...] = wd_ref[...].astype(BF16)

        x = x_buf[half].astype(BF16)
        act = _silu(_dot(x, wg_s[...])) * _dot(x, wu_s[...])
        y_ref[...] = _dot(act.astype(BF16), wd_s[...])

    @pl.when(t >= n_used)
    def _():
        y_ref[...] = jnp.zeros_like(y_ref)


def moe_experts(h, route, w_gu, w_d, l):
    lyr = l // 2
    sorted_token, slot, tile_expert, first_sorted, n_used = _routing_tables(route)
    gu_spec = lambda half: pl.BlockSpec((None, None, D_MODEL, FF_CHUNK),
                                        lambda t, tok, exp, first, used: (lyr, exp[t], 0, half))
    y = pl.pallas_call(
        _experts_kernel,
        grid_spec=pltpu.PrefetchScalarGridSpec(
            num_scalar_prefetch=4,
            grid=(MOE_TILES,),
            in_specs=[pl.BlockSpec(memory_space=pl.ANY),
                      gu_spec(0), gu_spec(1),
                      pl.BlockSpec((None, None, FF_CHUNK, D_MODEL),
                                   lambda t, tok, exp, first, used: (lyr, exp[t], 0, 0))],
            out_specs=pl.BlockSpec((MOE_TILE, D_MODEL), lambda t, tok, exp, first, used: (t, 0)),
            scratch_shapes=[pltpu.VMEM((2, MOE_TILE, D_MODEL), F32), pltpu.SemaphoreType.DMA((2,)),
                            pltpu.VMEM((D_MODEL, FF_CHUNK), BF16), pltpu.VMEM((D_MODEL, FF_CHUNK), BF16),
                            pltpu.VMEM((FF_CHUNK, D_MODEL), BF16)]),
        out_shape=jax.ShapeDtypeStruct((MOE_SLOTS, D_MODEL), F32),
        compiler_params=_cparams(("arbitrary",)),
        name="moe_experts",
    )(sorted_token, tile_expert, first_sorted, n_used, h, w_gu, w_gu, w_d)
    return y, slot


def _moe_combine_kernel(slot_ref, y_hbm, x_ref, gf_ref, route_ref, o_ref, y_buf, sems):
    i = pl.program_id(0)
    tm = x_ref.shape[0]

    def start_gather(tile, half):
        for k in range(TOP_K):
            for r in range(tm):
                _row_copy(y_hbm, slot_ref[k * N_TOK + tile * tm + r], y_buf.at[half, k], r,
                          sems.at[half]).start(priority=r % 2)

    @pl.when(i == 0)
    def _():
        start_gather(0, 0)

    @pl.when(i + 1 < pl.num_programs(0))
    def _():
        start_gather(i + 1, (i + 1) % 2)

    half = i % 2
    for k in range(TOP_K):
        for r in range(tm):
            _row_copy(y_hbm, 0, y_buf.at[half, k], r, sems.at[half]).wait()
    route = route_ref[...]
    f = (route[:, ROUTE_W1:ROUTE_W1 + 1] * y_buf[half, 0] + route[:, ROUTE_W2:ROUTE_W2 + 1] * y_buf[half, 1])
    o_ref[...] = x_ref[...] + gf_ref[...] * f


def moe_combine(y, slot, x1, mod, route, l):
    tm = 256
    return pl.pallas_call(
        _moe_combine_kernel,
        grid_spec=pltpu.PrefetchScalarGridSpec(
            num_scalar_prefetch=1,
            grid=(N_TOK // tm,),
            in_specs=[pl.BlockSpec(memory_space=pl.ANY),
                      pl.BlockSpec((tm, D_MODEL), lambda i, s: (i, 0)),
                      pl.BlockSpec((None, None, 1, D_MODEL), lambda i, s: (l, _cond_row(i, tm), 0, 5)),
                      pl.BlockSpec((tm, LANES), lambda i, s: (i, 0))],
            out_specs=pl.BlockSpec((tm, D_MODEL), lambda i, s: (i, 0)),
            scratch_shapes=[pltpu.VMEM((2, TOP_K, tm, D_MODEL), F32), pltpu.SemaphoreType.DMA((2,))]),
        out_shape=jax.ShapeDtypeStruct((N_TOK, D_MODEL), F32),
        compiler_params=_cparams(("arbitrary",)),
        name="moe_combine",
    )(slot, y, x1, mod, route)


def _final_norm_kernel(x_ref, g_ref, o_ref):
    o_ref[...] = _rms(x_ref[...], g_ref[...])


def final_norm_call(x, g, tok0, n_tok):
    tm = 1024
    i0 = tok0 // tm
    return pl.pallas_call(
        _final_norm_kernel,
        grid=(n_tok // tm,),
        in_specs=[pl.BlockSpec((tm, D_MODEL), lambda i: (i0 + i, 0)),
                  pl.BlockSpec((1, D_MODEL), lambda i: (0, 0))],
        out_specs=pl.BlockSpec((tm, D_MODEL), lambda i: (i, 0)),
        out_shape=jax.ShapeDtypeStruct((n_tok, D_MODEL), F32),
        compiler_params=_cparams(("arbitrary",)),
        name="final_norm",
    )(x, g)


def _cache_store_kernel(*refs):
    n = len(refs) // 3
    for src, dst in zip(refs[:n], refs[2 * n:]):
        dst[...] = src[...].reshape(dst.shape)


def cache_store(sources, all_layers, l):
    n_sq = 2
    tm = n_sq * SEQ
    return pl.pallas_call(
        _cache_store_kernel,
        grid=(N_CTX // tm,),
        in_specs=([pl.BlockSpec((tm, s.shape[1]), lambda i: (i, 0)) for s in sources]
                  + [pl.BlockSpec(memory_space=pl.ANY)] * len(all_layers)),
        out_specs=[pl.BlockSpec((n_sq, None, SEQ, a.shape[-1]), lambda i: (i, l, 0, 0)) for a in all_layers],
        out_shape=[jax.ShapeDtypeStruct(a.shape, F32) for a in all_layers],
        input_output_aliases={len(sources) + k: k for k in range(len(all_layers))},
        compiler_params=_cparams(("arbitrary",)),
        name="cache_store",
    )(*sources, *all_layers)


def _gate_row(p):
    return jnp.zeros((1, LANES), F32).at[0, GATE_A:GATE_A + 2 * H_A].set(p.reshape(-1))


def _tile_lanes(g, width):
    return jnp.tile(g, width // g.shape[0])[None, :]


def kernel(x_prompt, x_sample, state_delta, cache_gqa_k, cache_gqa_v, cache_diff_k, cache_diff_v, c, c_ctx,
           ada_w, ada_b, norm_mix, norm_ffn, w_in, conv_w, dn_a_log, dn_dt_bias, dn_norm, gqa_q_norm,
           gqa_k_norm, diff_lambda, diff_norm, w_out, ffn_w_gu, ffn_w_down, moe_router, moe_w_gu, moe_w_down,
           final_norm):
    cond8 = jnp.concatenate([c_ctx[None, :], c, jnp.zeros((8 - N_COND, D_MODEL), F32)], axis=0)
    w_out_b = w_out.astype(BF16)
    norm_mix3 = norm_mix.reshape(DEPTH, 1, D_MODEL)
    norm_ffn3 = norm_ffn.reshape(DEPTH, 1, D_MODEL)
    ffn_gu_b, ffn_d_b = ffn_w_gu.astype(BF16), ffn_w_down.astype(BF16)
    router_p = jnp.pad(moe_router, ((0, 0), (0, 0), (0, LANES - N_EXP)))
    rope_b = _rope_tables(DEC_SEQ // GRID_W, HEAD_DIM)
    rope_c = _rope_tables(DEC_SEQ // GRID_W, DIFF_HD)
    s0_ctx = jnp.zeros((BATCH, 2, H_A, HEAD_DIM, HEAD_DIM), F32)
    cache_b = (cache_gqa_k.reshape(DEC_BATCH, DEPTH, PAST_LEN, KV_B * HEAD_DIM),
               cache_gqa_v.reshape(DEC_BATCH, DEPTH, PAST_LEN, KV_B * HEAD_DIM))
    cache_c = (cache_diff_k.reshape(DEC_BATCH, DEPTH, PAST_LEN, C_W),
               cache_diff_v.reshape(DEC_BATCH, DEPTH, PAST_LEN, C_W))

    mod = ada_modulation(cond8, ada_w, ada_b)[:, :N_COND].reshape(DEPTH, N_COND, 1, 6 * D_MODEL)

    x = jnp.concatenate([x_prompt.reshape(N_CTX, D_MODEL), x_sample.reshape(N_LAT, D_MODEL)], axis=0)
    all_states = jnp.zeros((BATCH, DEPTH, 2, H_A, HEAD_DIM, HEAD_DIM), F32)
    ctx_layers = tuple(jnp.zeros((BATCH, DEPTH, SEQ, w), F32)
                       for w in (KV_B * HEAD_DIM, KV_B * HEAD_DIM, C_W, C_W))
    ctx = dict(n_seq=BATCH, seq_len=SEQ, tok0=0)
    lat = dict(n_seq=DEC_BATCH, seq_len=DEC_SEQ, tok0=N_CTX)
    for l in range(DEPTH):
        qkv_a, z_a, q_b, k_b, v_b, q_c, k_c, v_c, gate = in_projection(x, mod, norm_mix3, w_in, l)

        dn_args = (qkv_a, z_a, gate, conv_w[l], _gate_row(dn_a_log[l]), _gate_row(dn_dt_bias[l]),
                   _tile_lanes(dn_norm[l], LANES))
        oa_ctx, all_states = delta_mixer(*dn_args, s0_ctx, n_sub=2, all_states=all_states, layer=l, **ctx)
        oa_lat, _ = delta_mixer(*dn_args, state_delta[:, l], n_sub=1, **lat)

        gq, gk = _tile_lanes(gqa_q_norm[l], LANES), _tile_lanes(gqa_k_norm[l], LANES)
        ob_ctx, kn_ctx = gqa_mixer(q_b, k_b, v_b, gq, gk, **ctx)
        (ob_lat,) = gqa_mixer(q_b, k_b, v_b, gq, gk, cache=cache_b, rope=rope_b, layer=l, **lat)

        lam_init = 0.8 - 0.6 * math.exp(-0.3 * l)
        gn = _tile_lanes(diff_norm[l], C_W)
        oc_ctx = diff_mixer(q_c, k_c, v_c, diff_lambda[l], gn, lam_init, **ctx)
        oc_lat = diff_mixer(q_c, k_c, v_c, diff_lambda[l], gn, lam_init, cache=cache_c, rope=rope_c, layer=l, **lat)

        mixes = ((oa_ctx, oa_lat), (ob_ctx, ob_lat), (oc_ctx, oc_lat))
        if l % 2 == 0:
            x1, h2 = out_projection(x, mixes, w_out_b, mod, norm_ffn3, l)
            x = ffn(h2, ffn_gu_b, ffn_d_b, x1, mod, l)
        else:
            x1, h2, route = out_projection(x, mixes, w_out_b, mod, norm_ffn3, l, router_p[l // 2])
            y_slots, slot = moe_experts(h2, route, moe_w_gu, moe_w_down, l)
            x = moe_combine(y_slots, slot, x1, mod, route, l)

        ctx_layers = cache_store((kn_ctx, v_b, k_c, v_c), ctx_layers, l)

    y_prompt = final_norm_call(x, final_norm[None, :], 0, N_CTX).reshape(BATCH, SEQ, D_MODEL)
    y_sample = final_norm_call(x, final_norm[None, :], N_CTX, N_LAT).reshape(DEC_BATCH, DEC_SEQ, D_MODEL)
    new_gk, new_gv, new_dk, new_dv = ctx_layers
    return (y_prompt, y_sample, all_states,
            new_gk.reshape(BATCH, DEPTH, SEQ, KV_B, HEAD_DIM), new_gv.reshape(BATCH, DEPTH, SEQ, KV_B, HEAD_DIM),
            new_dk.reshape(BATCH, DEPTH, SEQ, H_C, 2, DIFF_HD), new_dv.reshape(BATCH, DEPTH, SEQ, H_C, 2 * DIFF_HD))
```

```python
import functools
import math

import jax
import jax.numpy as jnp
from jax import lax
from jax.experimental import pallas as pl
from jax.experimental.pallas import tpu as pltpu

D_MODEL = 1024
BATCH = 16
SEQ = 256
DEPTH = 4
DEC_BATCH = 2
DEC_SEQ = 1024
PAST_LEN = 512
GRID_W = 64
HEAD_DIM = 64
H_A = 6
A_W = H_A * HEAD_DIM
H_B = 6
KV_B = 2
B_W = H_B * HEAD_DIM
H_C = 4
DIFF_HD = 32
C_W = H_C * 2 * DIFF_HD
CHUNK = 64
ROPE_THETA = 10000.0
D_FF = 2816
N_EXP = 8
TOP_K = 2
D_FF_E = 1408
EPS = 1e-6

N_CTX = BATCH * SEQ
N_LAT = DEC_BATCH * DEC_SEQ
N_TOK = N_CTX + N_LAT
N_COND = 1 + DEC_BATCH
IN_PAD = 3072
FF_CHUNK = D_FF_E
LANES = 128
VMEM_LIMIT = 56 * 1024 * 1024

IN_SPLIT = (("qkv_a", 3 * A_W), ("z_a", A_W), ("q_b", B_W), ("k_b", KV_B * HEAD_DIM), ("v_b", KV_B * HEAD_DIM),
            ("q_c", C_W), ("k_c", C_W), ("v_c", C_W), ("gate", LANES))
GATE_BETA, GATE_A = 0, 2 * H_A

F32 = jnp.float32
BF16 = jnp.bfloat16


def _cparams(sem):
    return pltpu.CompilerParams(dimension_semantics=sem, vmem_limit_bytes=VMEM_LIMIT)


def _dot(a, b):
    return jnp.dot(a, b, preferred_element_type=F32)


def _dot_split(a, b):
    a_hi, b_hi = a.astype(BF16), b.astype(BF16)
    a_lo = (a - a_hi.astype(F32)).astype(BF16)
    b_lo = (b - b_hi.astype(F32)).astype(BF16)
    return _dot(jnp.concatenate([a_hi, a_lo, a_hi], axis=1), jnp.concatenate([b_hi, b_hi, b_lo], axis=0))


def _dot_nt(a, b):
    return lax.dot_general(a, b, (((1,), (1,)), ((), ())), preferred_element_type=F32)


def _silu(x):
    return x * jax.nn.sigmoid(x)


def _softplus(x):
    return jnp.maximum(x, 0.0) + jnp.log1p(jnp.exp(-jnp.abs(x)))


def _rms(x, g):
    return x * lax.rsqrt(jnp.mean(x * x, axis=-1, keepdims=True) + EPS) * g


def _lane(shape):
    return lax.broadcasted_iota(jnp.int32, shape, len(shape) - 1)


def _group_mean_matrix(width, group):
    i = lax.broadcasted_iota(jnp.int32, (width, width), 0)
    j = lax.broadcasted_iota(jnp.int32, (width, width), 1)
    return jnp.where(i // group == j // group, 1.0 / group, 0.0).astype(BF16)


def _group_rms(x, gmat, g):
    sq = x * x
    hi = sq.astype(BF16)
    lo = (sq - hi.astype(F32)).astype(BF16)
    ms = _dot(jnp.concatenate([hi, lo], axis=1), jnp.concatenate([gmat, gmat], axis=0))
    return x * lax.rsqrt(ms + EPS) * g


def _cond_row(i, tm):
    return jnp.maximum((i * tm) // DEC_SEQ - (N_CTX // DEC_SEQ - 1), 0)


def _ada_kernel(c_ref, w_ref, b_ref, o_ref):
    o_ref[...] = _dot_split(_silu(c_ref[...]), w_ref[...]) + b_ref[...]


def ada_modulation(cond8, ada_w, ada_b):
    tn = 2048
    return pl.pallas_call(
        _ada_kernel,
        grid=(DEPTH, 6 * D_MODEL // tn),
        in_specs=[pl.BlockSpec((8, D_MODEL), lambda l, j: (0, 0)),
                  pl.BlockSpec((None, D_MODEL, tn), lambda l, j: (l, 0, j)),
                  pl.BlockSpec((None, 1, tn), lambda l, j: (l, 0, j))],
        out_specs=pl.BlockSpec((None, 8, tn), lambda l, j: (l, 0, j)),
        out_shape=jax.ShapeDtypeStruct((DEPTH, 8, 6 * D_MODEL), F32),
        compiler_params=_cparams(("arbitrary", "arbitrary")),
        name="ada_modulation",
    )(cond8, ada_w, ada_b.reshape(DEPTH, 1, 6 * D_MODEL))


GATE_LO = 4 * A_W
GATE_HI = GATE_LO + 4 * H_A
IN_COLS = GATE_HI + B_W + 2 * KV_B * HEAD_DIM + 3 * C_W


def _inproj_kernel(x_ref, g_ref, sh_ref, sc_ref, w_ref, *refs):
    o_refs, w_s = refs[:-1], refs[-1]

    @pl.when(pl.program_id(0) == 0)
    def _():
        w_s[:, :GATE_LO] = w_ref[:, :GATE_LO].astype(BF16)
        w_s[:, GATE_LO:IN_COLS - 4 * H_A] = w_ref[:, GATE_HI:].astype(BF16)
        tail = jnp.concatenate([w_ref[:, GATE_LO:GATE_HI], jnp.zeros((D_MODEL, LANES - 4 * H_A), F32)], axis=1)
        w_s[:, IN_COLS - 4 * H_A:] = tail.astype(BF16)

    h = _rms(x_ref[...], g_ref[...]) * (1.0 + sc_ref[...]) + sh_ref[...]
    acc = _dot(h.astype(BF16), w_s[...])
    off = 0
    for o_ref, (_, width) in zip(o_refs, IN_SPLIT):
        o_ref[...] = acc[:, off:off + width]
        off += width


def in_projection(x, mod, norm_g, w_in, l):
    tm = 512
    return pl.pallas_call(
        _inproj_kernel,
        grid=(N_TOK // tm,),
        in_specs=[pl.BlockSpec((tm, D_MODEL), lambda i: (i, 0)),
                  pl.BlockSpec((None, 1, D_MODEL), lambda i: (l, 0, 0)),
                  pl.BlockSpec((None, None, 1, D_MODEL), lambda i: (l, _cond_row(i, tm), 0, 0)),
                  pl.BlockSpec((None, None, 1, D_MODEL), lambda i: (l, _cond_row(i, tm), 0, 1)),
                  pl.BlockSpec((None, D_MODEL, IN_COLS), lambda i: (l, 0, 0))],
        out_specs=[pl.BlockSpec((tm, w), lambda i: (i, 0)) for _, w in IN_SPLIT],
        out_shape=[jax.ShapeDtypeStruct((N_TOK, w), F32) for _, w in IN_SPLIT],
        scratch_shapes=[pltpu.VMEM((D_MODEL, IN_PAD), BF16)],
        compiler_params=_cparams(("arbitrary",)),
        name="in_projection",
    )(x, norm_g, mod, mod, w_in)


SOLVE_BLOCK = 16


def _delta_kernel(*refs, seq_len, n_sub, n_cp, aliased):
    qkv_ref, z_ref, gate_ref, cw_ref, alog_ref, dtb_ref, ng_ref, s0_ref = refs[:8]
    (o_ref, sfin_ref, pad_s, kk_s, qq_s, vk_s, gc_s, beta_s, u_s, w_s, qd_s, a_s, kdt_s, egt_s, sa_s, o_s,
     tp_s, t32_s, rp_s, c32_s, m_s, x_s, vbd_s, v_s, op_s) = refs[9 if aliased else 8:]
    n_chunk = seq_len // CHUNK
    n_pair = H_A // 2
    n_unit = n_sub * H_A
    half = HEAD_DIM
    lo = _lane((CHUNK, LANES)) < half
    lo_row = _lane((1, LANES)) < half
    lo16 = _lane((CHUNK, LANES)).astype(F32).astype(BF16) < half
    row = lax.broadcasted_iota(jnp.int32, (CHUNK, LANES), 0)
    col = _lane((CHUNK, LANES)) % half
    ahead = jnp.where(lo, row - col, col - row)
    incl = ahead >= 0
    strict = ahead > 0
    diag_blk = row // SOLVE_BLOCK == col // SOLVE_BLOCK
    big_row = lax.broadcasted_iota(jnp.int32, (LANES, LANES), 0)
    anti = big_row // half + _lane((LANES, LANES)) // half == 1
    eye = (big_row == _lane((LANES, LANES))).astype(F32)

    def both(a):
        return jnp.concatenate([jnp.where(lo, a, 0.0), jnp.where(lo, 0.0, a)], axis=0)

    rb = seq_len
    lo_rb = _lane((rb, LANES)) < half
    zero8 = jnp.zeros((8, 3 * A_W), F32)
    pad_s[0:8, :] = zero8
    pad_s[8 + seq_len:16 + seq_len, :] = zero8

    def l2n(x):
        return x * lax.rsqrt(0.5 * jnp.sum(x * x, axis=-1, keepdims=True) + EPS)

    for sq in range(n_sub):
        pad_s[8:8 + seq_len, :] = qkv_ref[sq * seq_len:(sq + 1) * seq_len, :]
        for r0 in range(0, seq_len, rb):
            y = _silu(pad_s[7 + r0:7 + r0 + rb, :] * cw_ref[0:1, :] + pad_s[8 + r0:8 + r0 + rb, :] * cw_ref[1:2, :]
                      + pad_s[9 + r0:9 + r0 + rb, :] * cw_ref[2:3, :])

            def dup(part, p):
                c = y[:, part * A_W + p * LANES:part * A_W + (p + 1) * LANES]
                r = pltpu.roll(c, half, axis=1)
                return jnp.where(lo_rb, c, r), jnp.where(lo_rb, r, c)

            for p in range(n_pair):
                qs, ks, vs = dup(0, p), dup(1, p), dup(2, p)
                for e in range(2):
                    u = sq * H_A + 2 * p + e
                    kk = l2n(ks[e])
                    kk_s[u, r0:r0 + rb, :] = kk
                    qq_s[u, r0:r0 + rb, :] = l2n(qs[e]) * (HEAD_DIM ** -0.5)
                    vk_s[u, r0:r0 + rb, :] = jnp.where(lo_rb, vs[e], kk)

    cb = 256
    ci = lax.broadcasted_iota(jnp.int32, (cb, cb), 0)
    cj = lax.broadcasted_iota(jnp.int32, (cb, cb), 1)
    same = ci // CHUNK == cj // CHUNK
    m_pre = jnp.where(same & (ci >= cj), 1.0, 0.0).astype(BF16)
    m_suf = jnp.where(same & (ci <= cj), 1.0, 0.0).astype(BF16)
    fwd_cols = (_lane((cb, LANES)) >= GATE_A) & (_lane((cb, LANES)) < GATE_A + H_A)
    for r0 in range(0, n_sub * seq_len, cb):
        gt = gate_ref[r0:r0 + cb, :]
        beta_s[r0:r0 + cb, :] = jax.nn.sigmoid(gt)
        g = -jnp.exp(alog_ref[...]) * _softplus(gt + dtb_ref[...])
        g1 = g.astype(BF16)
        r1 = g - g1.astype(F32)
        g2 = r1.astype(BF16)
        g3 = (r1 - g2.astype(F32)).astype(BF16)
        blk = jnp.concatenate([g1, g2, g3], axis=1)
        pre, suf = _dot(m_pre, blk), _dot(m_suf, blk)
        pre = pre[:, :LANES] + pre[:, LANES:2 * LANES] + pre[:, 2 * LANES:]
        suf = suf[:, :LANES] + suf[:, LANES:2 * LANES] + suf[:, 2 * LANES:]
        gc_s[r0:r0 + cb, :] = jnp.where(fwd_cols, pre, suf)

    zero = jnp.zeros((half, half), F32)
    for sq in range(n_sub):
        for h in range(H_A):
            sa_s[sq * H_A + h] = jnp.concatenate([jnp.concatenate([zero, s0_ref[sq, 1, h]], axis=1),
                                                  jnp.concatenate([s0_ref[sq, 0, h], zero], axis=1)], axis=0)

    n_fac = int(math.log2(SOLVE_BLOCK))

    def fill_stage(c, rows, cc):
        for sq in range(n_sub):
            grow = pl.ds(pl.multiple_of(sq * seq_len + c * CHUNK, CHUNK), CHUNK)
            gcb = gc_s[grow, :]
            gct = gcb.T
            bb = beta_s[grow, :]
            for h in range(H_A):
                u = sq * H_A + h
                w = cc * n_unit + u
                full = (CHUNK, LANES)
                cff = jnp.broadcast_to(gcb[:, GATE_A + h:GATE_A + h + 1], full)
                cbb = jnp.broadcast_to(gcb[:, GATE_A + H_A + h:GATE_A + H_A + h + 1], full)
                bff = jnp.broadcast_to(bb[:, GATE_BETA + h:GATE_BETA + h + 1], full)
                bbb = jnp.broadcast_to(bb[:, GATE_BETA + H_A + h:GATE_BETA + H_A + h + 1], full)
                gr_fb = jnp.concatenate([gct[GATE_A + h:GATE_A + h + 1, :],
                                         gct[GATE_A + H_A + h:GATE_A + H_A + h + 1, :]], axis=1)
                decay = jnp.where(incl, jnp.exp(jnp.where(incl, jnp.where(lo, cff, cbb) - gr_fb, 0.0)), 0.0)
                egf, egb = jnp.exp(cff), jnp.exp(cbb)
                tot_f = gcb[CHUNK - 1:CHUNK, GATE_A + h:GATE_A + h + 1]
                tot_b = gcb[0:1, GATE_A + H_A + h:GATE_A + H_A + h + 1]
                kk, qq, vk = kk_s[u, rows, :], qq_s[u, rows, :], vk_s[u, rows, :]
                k16 = kk[:, :half].astype(BF16)
                kq = jnp.concatenate([k16, qq[:, :half].astype(BF16)], axis=0)
                gram = _dot_nt(kq, jnp.concatenate([k16, k16], axis=0))
                neg_l = jnp.where(strict, gram[:CHUNK] * jnp.where(lo, -bff, -bbb) * decay, 0.0)
                a_s[u, rows, :] = jnp.where(incl, gram[CHUNK:] * decay, 0.0).astype(BF16)
                neg_d = jnp.where(diag_blk, neg_l, 0.0)
                pd = both(neg_d)
                t32_s[w] = eye + pd
                tp_s[w, :, :LANES] = (eye + pd).astype(BF16)
                tp_s[w, :, LANES:] = pd.astype(BF16)
                x_f = vk * jnp.where(lo, bff, bff * egf)
                x_b = vk * jnp.where(lo, bbb, bbb * egb)
                rp_s[w, :, :LANES] = jnp.concatenate([x_f, x_b], axis=0).astype(BF16)
                rp_s[w, :, LANES:] = both(neg_l - neg_d).astype(BF16)
                qd_s[u, rows, :] = (qq * jnp.where(lo, egb, egf)).astype(BF16)
                kd = kk * jnp.exp(jnp.where(lo, tot_b - cbb, tot_f - cff))
                kdt_s[u, c] = kd.T.astype(BF16)
                egt_s[u, c] = jnp.exp(jnp.where(lo_row, tot_f, tot_b))

    def prep(ci, carry):
        rows_of = []
        for cc in range(n_cp):
            c = ci * n_cp + cc
            rows_of.append(pl.ds(pl.multiple_of(c * CHUNK, CHUNK), CHUNK))
            fill_stage(c, rows_of[cc], cc)
        n_slot = n_cp * n_unit
        for w in range(n_slot):
            tp_s[w, :, LANES:] = _dot(tp_s[w, :, LANES:], tp_s[w, :, LANES:]).astype(BF16)
        for j in range(1, n_fac):
            for w in range(n_slot):
                if j + 1 < n_fac:
                    tp = _dot(tp_s[w, :, LANES:], tp_s[w])
                    t = t32_s[w] + tp[:, :LANES]
                    t32_s[w] = t
                    tp_s[w, :, :LANES] = t.astype(BF16)
                    tp_s[w, :, LANES:] = tp[:, LANES:].astype(BF16)
                else:
                    t = t32_s[w] + _dot(tp_s[w, :, LANES:], tp_s[w, :, :LANES])
                    tp_s[w, :, :LANES] = t.astype(BF16)
        for w in range(n_slot):
            cm = _dot(tp_s[w, :, :LANES], rp_s[w])
            c32_s[w] = cm[:, :LANES]
            rp_s[w] = cm.astype(BF16)
        assert CHUNK // SOLVE_BLOCK == 4
        for w in range(n_slot):
            mm = _dot(rp_s[w, :, LANES:], rp_s[w])
            y1 = c32_s[w] + mm[:, :LANES]
            c32_s[w] = y1
            x_s[w] = y1.astype(BF16)
            m_s[w] = mm[:, LANES:].astype(BF16)
        for w in range(n_slot):
            x = c32_s[w] + _dot(m_s[w], x_s[w])
            x_f, x_b = x[:CHUNK], pltpu.roll(x[CHUNK:], half, axis=1)
            u_s[w % n_unit, rows_of[w // n_unit], :] = jnp.where(lo, x_f, x_b)
            w_s[w % n_unit, rows_of[w // n_unit], :] = jnp.where(lo, x_b, x_f).astype(BF16)
        return carry

    lax.fori_loop(0, n_chunk // n_cp, prep, 0)

    def scan(s, carry):
        rf = pl.ds(pl.multiple_of(s * CHUNK, CHUNK), CHUNK)
        sb = n_chunk - 1 - s
        rbk = pl.ds(pl.multiple_of(sb * CHUNK, CHUNK), CHUNK)
        for u in range(n_unit):
            w = jnp.where(lo16, w_s[u, rbk, :], w_s[u, rf, :])
            qd = jnp.where(lo16, qd_s[u, rbk, :], qd_s[u, rf, :])
            wq = _dot(jnp.concatenate([w, qd], axis=0), sa_s[u].astype(BF16))
            v = jnp.where(lo, u_s[u, rf, :], u_s[u, rbk, :]) - wq[:CHUNK]
            v_s[u] = v.astype(BF16)
            vbd_s[u] = both(v).astype(BF16)
            op_s[u] = wq[CHUNK:]
        for u in range(n_unit):
            a = jnp.where(lo16, a_s[u, rf, :], a_s[u, rbk, :])
            o = op_s[u] + _dot(a, vbd_s[u])
            o_s[u, rf, 0:half] = o[:, :half]
            o_s[u, rbk, half:LANES] = o[:, half:]
            kdt = jnp.concatenate([kdt_s[u, sb][:half], kdt_s[u, s][half:]], axis=0)
            egt = jnp.where(lo_row, egt_s[u, s], egt_s[u, sb])
            sa_s[u] = sa_s[u] * egt + jnp.where(anti, _dot(kdt, v_s[u]), 0.0)
        return carry

    lax.fori_loop(0, n_chunk, scan, 0)

    for sq in range(n_sub):
        for r0 in range(0, seq_len, rb):
            tr = sq * seq_len + r0
            for p in range(n_pair):
                nrm = []
                for e in range(2):
                    ofb = o_s[sq * H_A + 2 * p + e, r0:r0 + rb, :]
                    oo = ofb + pltpu.roll(ofb, half, axis=1)
                    nrm.append(oo * lax.rsqrt(jnp.sum(oo * oo, axis=-1, keepdims=True) * (0.5 / HEAD_DIM) + EPS))
                o_ref[tr:tr + rb, p * LANES:(p + 1) * LANES] = (
                    jnp.where(lo_rb, nrm[0], nrm[1]) * ng_ref[...]
                    * _silu(z_ref[tr:tr + rb, p * LANES:(p + 1) * LANES]))
        for h in range(H_A):
            st = sa_s[sq * H_A + h]
            sfin_ref[sq, 0, h] = st[half:, :half]
            sfin_ref[sq, 1, h] = st[:half, half:]


def delta_mixer(qkv, z, gate, cw, alog_row, dtb_row, ng2, s0, *, n_seq, seq_len, tok0, n_sub,
                all_states=None, layer=0):
    n_chunk = seq_len // CHUNK
    n_unit = n_sub * H_A
    blk = n_sub * seq_len
    b0 = tok0 // blk
    tok = lambda w: pl.BlockSpec((blk, w), lambda b: (b0 + b, 0))
    full = lambda a: pl.BlockSpec(a.shape, lambda b: (0,) * a.ndim)
    state = pl.BlockSpec((n_sub, 2, H_A, HEAD_DIM, HEAD_DIM), lambda b: (b, 0, 0, 0, 0))
    per_unit = lambda dt: pltpu.VMEM((n_unit, seq_len, LANES), dt)
    n_cp = 2
    stage = lambda rows, cols, dt, n=n_unit: pltpu.VMEM((n, rows, cols), dt)
    aliased = all_states is not None
    in_specs = [tok(3 * A_W), tok(A_W), tok(LANES), full(cw), full(alog_row), full(dtb_row), full(ng2), state]
    args = [qkv, z, gate, cw, alog_row, dtb_row, ng2, s0]
    if aliased:
        in_specs.append(pl.BlockSpec(memory_space=pl.ANY))
        args.append(all_states)
        state_out = pl.BlockSpec((n_sub, None, 2, H_A, HEAD_DIM, HEAD_DIM), lambda b: (b, layer, 0, 0, 0, 0))
        state_shape = jax.ShapeDtypeStruct(all_states.shape, F32)
    else:
        state_out = state
        state_shape = jax.ShapeDtypeStruct((n_seq, 2, H_A, HEAD_DIM, HEAD_DIM), F32)
    return pl.pallas_call(
        functools.partial(_delta_kernel, seq_len=seq_len, n_sub=n_sub, n_cp=n_cp, aliased=aliased),
        grid=(n_seq // n_sub,),
        in_specs=in_specs,
        out_specs=[pl.BlockSpec((blk, A_W), lambda b: (b, 0)), state_out],
        out_shape=[jax.ShapeDtypeStruct((n_seq * seq_len, A_W), F32), state_shape],
        input_output_aliases={len(args) - 1: 1} if aliased else {},
        scratch_shapes=[pltpu.VMEM((seq_len + 16, 3 * A_W), F32),
                        per_unit(F32), per_unit(F32), per_unit(F32),
                        pltpu.VMEM((blk, LANES), F32),
                        pltpu.VMEM((blk, LANES), F32),
                        per_unit(F32), per_unit(BF16), per_unit(BF16), per_unit(BF16),
                        pltpu.VMEM((n_unit, n_chunk, LANES, HEAD_DIM), BF16),
                        pltpu.VMEM((n_unit, n_chunk, 1, LANES), F32),
                        stage(LANES, LANES, F32),
                        per_unit(F32),
                        stage(LANES, 2 * LANES, BF16, n_cp * n_unit),
                        stage(LANES, LANES, F32, n_cp * n_unit),
                        stage(LANES, 2 * LANES, BF16, n_cp * n_unit),
                        stage(LANES, LANES, F32, n_cp * n_unit),
                        stage(LANES, LANES, BF16, n_cp * n_unit),
                        stage(LANES, LANES, BF16, n_cp * n_unit),
                        stage(LANES, LANES, BF16),
                        stage(CHUNK, LANES, BF16),
                        stage(CHUNK, LANES, F32)],
        compiler_params=_cparams(("arbitrary",)),
        name=f"delta_mixer_{seq_len}",
    )(*args)


def _rope(x, cos, sin_signed, quarter):
    width = x.shape[-1]
    swapped = jnp.where((_lane(x.shape) % (2 * quarter)) < quarter,
                        pltpu.roll(x, width - quarter, axis=1), pltpu.roll(x, quarter, axis=1))
    return x * cos + swapped * sin_signed


def _rope_tables(rows, dim):
    nf = dim // 4
    inv = ROPE_THETA ** (-jnp.arange(nf, dtype=F32) / nf)
    r = jnp.repeat(jnp.arange(rows, dtype=F32), GRID_W)
    c = jnp.tile(jnp.arange(GRID_W, dtype=F32), rows)
    ar, ac = r[:, None] * inv, c[:, None] * inv
    cos = jnp.concatenate([jnp.cos(ar), jnp.cos(ar), jnp.cos(ac), jnp.cos(ac)], axis=1)
    sin = jnp.concatenate([-jnp.sin(ar), jnp.sin(ar), -jnp.sin(ac), jnp.sin(ac)], axis=1)
    return jnp.tile(cos, (1, LANES // dim)), jnp.tile(sin, (1, LANES // dim))


def _gqa_kernel(*refs, cached, n_sub):
    if cached:
        (q_ref, k_ref, v_ref, gq_ref, gk_ref, ck_ref, cv_ref, cosq_ref, sinq_ref, cosk_ref, sink_ref,
         o_ref, k_s, v_s) = refs
    else:
        q_ref, k_ref, v_ref, gq_ref, gk_ref, o_ref, kn_ref, k_s, v_s = refs
    gmat = _group_mean_matrix(LANES, HEAD_DIM)
    quarter = HEAD_DIM // 4
    n_k = k_ref.shape[0] // n_sub

    @pl.when(pl.program_id(1) == 0)
    def _():
        for sq in range(n_sub):
            k = _group_rms(k_ref[sq * n_k:(sq + 1) * n_k, :], gmat, gk_ref[...])
            v = v_ref[sq * n_k:(sq + 1) * n_k, :]
            if cached:
                k = _rope(k, cosk_ref[...], sink_ref[...], quarter)
                k = jnp.concatenate([ck_ref[...], k], axis=0)
                v = jnp.concatenate([cv_ref[...], v], axis=0)
            else:
                kn_ref[sq * n_k:(sq + 1) * n_k, :] = k
            k_s[sq] = k.astype(BF16)
            v_s[sq] = v.astype(BF16)

    tq = q_ref.shape[0] // n_sub
    lo = _lane((tq, LANES)) < HEAD_DIM
    group = H_B // KV_B
    q_scale = (HEAD_DIM ** -0.5) * math.log2(math.e)
    for sq in range(n_sub):
        k, v = k_s[sq], v_s[sq]
        qs = []
        for j in range(H_B // 2):
            q = _group_rms(q_ref[sq * tq:(sq + 1) * tq, j * LANES:(j + 1) * LANES], gmat, gq_ref[...])
            if cached:
                q = _rope(q, cosq_ref[...], sinq_ref[...], quarter)
            q = q * q_scale
            qs.append((q, pltpu.roll(q, HEAD_DIM, axis=1)))
        outs = [None] * H_B
        for kv in range(KV_B):
            keep = lo if kv == 0 else ~lo
            heads = range(kv * group, (kv + 1) * group)
            q3 = jnp.concatenate([jnp.where(keep, qs[h // 2][0 if h % 2 == kv else 1], 0.0).astype(BF16)
                                  for h in heads], axis=0)
            s = _dot_nt(q3, k)
            p = jnp.exp2(s - jnp.max(s, axis=-1, keepdims=True))
            r = _dot(p.astype(BF16), v) / jnp.sum(p, axis=-1, keepdims=True)
            for i, h in enumerate(heads):
                rh = r[i * tq:(i + 1) * tq]
                outs[h] = rh if h % 2 == kv else pltpu.roll(rh, HEAD_DIM, axis=1)
        for j in range(H_B // 2):
            o_ref[sq * tq:(sq + 1) * tq, j * LANES:(j + 1) * LANES] = jnp.where(lo, outs[2 * j], outs[2 * j + 1])


def gqa_mixer(q, k, v, gq, gk, *, n_seq, seq_len, tok0, cache=None, rope=None, layer=0):
    cached = cache is not None
    n_sub = 1 if cached else 2
    n_seq, seq_len = n_seq // n_sub, seq_len * n_sub
    tq = 256 * n_sub
    nq = seq_len // tq
    b0 = tok0 // seq_len
    q0 = tok0 // tq
    kvw = KV_B * HEAD_DIM
    in_specs = [pl.BlockSpec((tq, B_W), lambda b, i: (q0 + b * nq + i, 0)),
                pl.BlockSpec((seq_len, kvw), lambda b, i: (b0 + b, 0)),
                pl.BlockSpec((seq_len, kvw), lambda b, i: (b0 + b, 0)),
                pl.BlockSpec((1, LANES), lambda b, i: (0, 0)),
                pl.BlockSpec((1, LANES), lambda b, i: (0, 0))]
    args = [q, k, v, gq, gk]
    out_specs = [pl.BlockSpec((tq, B_W), lambda b, i: (b * nq + i, 0))]
    out_shape = [jax.ShapeDtypeStruct((n_seq * seq_len, B_W), F32)]
    if cached:
        cos, sin = rope
        cache_spec = pl.BlockSpec((None, None, PAST_LEN, kvw), lambda b, i: (b, layer, 0, 0))
        in_specs += [cache_spec, cache_spec,
                     pl.BlockSpec((tq, LANES), lambda b, i: (i, 0)),
                     pl.BlockSpec((tq, LANES), lambda b, i: (i, 0)),
                     pl.BlockSpec((seq_len, LANES), lambda b, i: (0, 0)),
                     pl.BlockSpec((seq_len, LANES), lambda b, i: (0, 0))]
        args += [cache[0], cache[1], cos, sin, cos, sin]
    else:
        out_specs.append(pl.BlockSpec((seq_len, kvw), lambda b, i: (b, 0)))
        out_shape.append(jax.ShapeDtypeStruct((n_seq * seq_len, kvw), F32))
    n_keys = seq_len // n_sub + (PAST_LEN if cached else 0)
    return pl.pallas_call(
        functools.partial(_gqa_kernel, cached=cached, n_sub=n_sub),
        grid=(n_seq, nq),
        in_specs=in_specs, out_specs=out_specs, out_shape=out_shape,
        scratch_shapes=[pltpu.VMEM((n_sub, n_keys, kvw), BF16), pltpu.VMEM((n_sub, n_keys, kvw), BF16)],
        compiler_params=_cparams(("arbitrary", "arbitrary")),
        name=f"gqa_mixer_{seq_len}",
    )(*args)


def _diff_kernel(*refs, cached, lam_init, n_sub):
    if cached:
        (q_ref, k_ref, v_ref, lam_ref, gn_ref, ck_ref, cv_ref, cosq_ref, sinq_ref, cosk_ref, sink_ref,
         o_ref) = refs
    else:
        q_ref, k_ref, v_ref, lam_ref, gn_ref, o_ref = refs
    quarter = DIFF_HD // 4
    lv = lam_ref[...]
    lam = (jnp.exp(jnp.sum(lv[0:1] * lv[1:2], axis=-1, keepdims=True))
           - jnp.exp(jnp.sum(lv[2:3] * lv[3:4], axis=-1, keepdims=True)) + lam_init)
    gmat = _group_mean_matrix(C_W, 2 * DIFF_HD)
    tq, n_k = q_ref.shape[0] // n_sub, k_ref.shape[0] // n_sub
    lane = _lane((tq, C_W))
    for sb in range(n_sub):
        q = q_ref[sb * tq:(sb + 1) * tq, :]
        k, v = k_ref[sb * n_k:(sb + 1) * n_k, :], v_ref[sb * n_k:(sb + 1) * n_k, :]
        if cached:
            cq = jnp.concatenate([cosq_ref[...]] * 2, axis=1)
            sq = jnp.concatenate([sinq_ref[...]] * 2, axis=1)
            ck = jnp.concatenate([cosk_ref[...]] * 2, axis=1)
            sk = jnp.concatenate([sink_ref[...]] * 2, axis=1)
            q = _rope(q, cq, sq, quarter)
            k = _rope(k, ck, sk, quarter)
            k = jnp.concatenate([ck_ref[...], k], axis=0)
            v = jnp.concatenate([cv_ref[...], v], axis=0)
        k = k.astype(BF16)
        v = v.astype(BF16)
        q = q * ((DIFF_HD ** -0.5) * math.log2(math.e))
        o = jnp.zeros(q.shape, F32)
        for h in range(H_C):
            ps, cs = [], []
            for m in range(2):
                qm = jnp.where(lane // DIFF_HD == 2 * h + m, q, 0.0).astype(BF16)
                s = _dot_nt(qm, k)
                p = jnp.exp2(s - jnp.max(s, axis=-1, keepdims=True))
                ps.append(p)
                cs.append((1.0 if m == 0 else lam) / jnp.sum(p, axis=-1, keepdims=True))
            a = ps[0] * cs[0] - ps[1] * cs[1]
            o = jnp.where(lane // (2 * DIFF_HD) == h, _dot(a.astype(BF16), v), o)
        o_ref[sb * tq:(sb + 1) * tq, :] = _group_rms(o, gmat, gn_ref[...]) * (1.0 - lam_init)


def diff_mixer(q, k, v, lam_p, gn, lam_init, *, n_seq, seq_len, tok0, cache=None, rope=None, layer=0):
    cached = cache is not None
    n_sub = 1 if cached else 2
    n_seq, seq_len = n_seq // n_sub, seq_len * n_sub
    tq = 256 * n_sub
    nq = seq_len // tq
    b0 = tok0 // seq_len
    q0 = tok0 // tq
    in_specs = [pl.BlockSpec((tq, C_W), lambda b, i: (q0 + b * nq + i, 0)),
                pl.BlockSpec((seq_len, C_W), lambda b, i: (b0 + b, 0)),
                pl.BlockSpec((seq_len, C_W), lambda b, i: (b0 + b, 0)),
                pl.BlockSpec((4, DIFF_HD), lambda b, i: (0, 0)),
                pl.BlockSpec((1, C_W), lambda b, i: (0, 0))]
    args = [q, k, v, lam_p, gn]
    if cached:
        cos, sin = rope
        cache_spec = pl.BlockSpec((None, None, PAST_LEN, C_W), lambda b, i: (b, layer, 0, 0))
        in_specs += [cache_spec, cache_spec,
                     pl.BlockSpec((tq, LANES), lambda b, i: (i, 0)),
                     pl.BlockSpec((tq, LANES), lambda b, i: (i, 0)),
                     pl.BlockSpec((seq_len, LANES), lambda b, i: (0, 0)),
                     pl.BlockSpec((seq_len, LANES), lambda b, i: (0, 0))]
        args += [cache[0], cache[1], cos, sin, cos, sin]
    return pl.pallas_call(
        functools.partial(_diff_kernel, cached=cached, lam_init=lam_init, n_sub=n_sub),
        grid=(n_seq, nq),
        in_specs=in_specs,
        out_specs=pl.BlockSpec((tq, C_W), lambda b, i: (b * nq + i, 0)),
        out_shape=jax.ShapeDtypeStruct((n_seq * seq_len, C_W), F32),
        compiler_params=_cparams(("arbitrary", "arbitrary")),
        name=f"diff_mixer_{seq_len}",
    )(*args)


def _outproj_kernel(*refs, routed, n_mix):
    x_ref = refs[0]
    mix_refs = refs[1:1 + 2 * n_mix]
    rest = refs[1 + 2 * n_mix:]
    if routed:
        w_ref, gm_ref, g_ref, sh_ref, sc_ref, r_ref, x1_ref, h_ref, lg_ref = rest
    else:
        w_ref, gm_ref, g_ref, sh_ref, sc_ref, x1_ref, h_ref = rest
    is_ctx = pl.program_id(0) < N_CTX // x_ref.shape[0]
    mixed = jnp.concatenate(
        [jnp.where(is_ctx, mix_refs[2 * m][...], mix_refs[2 * m + 1][...]) for m in range(n_mix)], axis=1)
    x1 = x_ref[...] + gm_ref[...] * _dot(mixed.astype(BF16), w_ref[...])
    x1_ref[...] = x1
    h = _rms(x1, g_ref[...]) * (1.0 + sc_ref[...]) + sh_ref[...]
    h_ref[...] = h.astype(h_ref.dtype)
    if routed:
        lg_ref[...] = _top2_route(_dot_split(h, r_ref[...]))


def out_projection(x, mixes, w_out_b, mod, norm_g, l, router_p=None):
    tm = 512
    n_ctx_tiles = N_CTX // tm
    n_lat_tiles = N_LAT // tm
    routed = router_p is not None
    mod_spec = lambda k: pl.BlockSpec((None, None, 1, D_MODEL), lambda i: (l, _cond_row(i, tm), 0, k))
    in_specs = [pl.BlockSpec((tm, D_MODEL), lambda i: (i, 0))]
    args = [x]
    for ctx_arr, lat_arr in mixes:
        w = ctx_arr.shape[1]
        in_specs.append(pl.BlockSpec((tm, w), lambda i: (jnp.minimum(i, n_ctx_tiles - 1), 0)))
        in_specs.append(pl.BlockSpec((tm, w), lambda i: (jnp.clip(i - n_ctx_tiles, 0, n_lat_tiles - 1), 0)))
        args += [ctx_arr, lat_arr]
    in_specs += [pl.BlockSpec((None, D_MODEL, D_MODEL), lambda i: (l, 0, 0)),
                 mod_spec(2),
                 pl.BlockSpec((None, 1, D_MODEL), lambda i: (l, 0, 0)),
                 mod_spec(3), mod_spec(4)]
    args += [w_out_b, mod, norm_g, mod, mod]
    out_specs = [pl.BlockSpec((tm, D_MODEL), lambda i: (i, 0)),
                 pl.BlockSpec((tm, D_MODEL), lambda i: (i, 0))]
    out_shape = [jax.ShapeDtypeStruct((N_TOK, D_MODEL), F32),
                 jax.ShapeDtypeStruct((N_TOK, D_MODEL), F32 if routed else BF16)]
    if routed:
        in_specs.append(pl.BlockSpec((D_MODEL, LANES), lambda i: (0, 0)))
        args.append(router_p)
        out_specs.append(pl.BlockSpec((tm, LANES), lambda i: (i, 0)))
        out_shape.append(jax.ShapeDtypeStruct((N_TOK, LANES), F32))
    return pl.pallas_call(
        functools.partial(_outproj_kernel, routed=routed, n_mix=len(mixes)),
        grid=(N_TOK // tm,),
        in_specs=in_specs, out_specs=out_specs, out_shape=out_shape,
        compiler_params=_cparams(("arbitrary",)),
        name="out_projection",
    )(*args)


def _ffn_kernel(h_ref, wg_ref, wu_ref, wd_ref, x_ref, gf_ref, o_ref, acc_ref):
    j = pl.program_id(1)

    @pl.when(j == 0)
    def _():
        acc_ref[...] = jnp.zeros_like(acc_ref)

    h = h_ref[...]
    act = _silu(_dot(h, wg_ref[...])) * _dot(h, wu_ref[...])
    acc_ref[...] += _dot(act.astype(BF16), wd_ref[...])

    @pl.when(j == pl.num_programs(1) - 1)
    def _():
        o_ref[...] = x_ref[...] + gf_ref[...] * acc_ref[...]


def ffn(h, w_gu, w_d, x1, mod, l):
    tm = 512
    lyr = l // 2
    n_chunk = D_FF // FF_CHUNK
    gu_spec = lambda half: pl.BlockSpec((None, D_MODEL, FF_CHUNK), lambda i, j: (lyr, 0, half * n_chunk + j))
    return pl.pallas_call(
        _ffn_kernel,
        grid=(N_TOK // tm, n_chunk),
        in_specs=[pl.BlockSpec((tm, D_MODEL), lambda i, j: (i, 0)),
                  gu_spec(0), gu_spec(1),
                  pl.BlockSpec((None, FF_CHUNK, D_MODEL), lambda i, j: (lyr, j, 0)),
                  pl.BlockSpec((tm, D_MODEL), lambda i, j: (i, 0)),
                  pl.BlockSpec((None, None, 1, D_MODEL), lambda i, j: (l, _cond_row(i, tm), 0, 5))],
        out_specs=pl.BlockSpec((tm, D_MODEL), lambda i, j: (i, 0)),
        out_shape=jax.ShapeDtypeStruct((N_TOK, D_MODEL), F32),
        scratch_shapes=[pltpu.VMEM((tm, D_MODEL), F32)],
        compiler_params=_cparams(("arbitrary", "arbitrary")),
        name="ffn_dense",
    )(h, w_gu, w_gu, w_d, x1, mod)


MOE_TILE = 256
MOE_SLOTS = TOP_K * N_TOK + N_EXP * MOE_TILE
MOE_TILES = MOE_SLOTS // MOE_TILE
ROUTE_I1, ROUTE_I2, ROUTE_W1, ROUTE_W2 = 0, 1, 2, 3


def _top2_route(logits):
    lane = _lane(logits.shape)
    neg = jnp.float32(-jnp.inf)
    lg = jnp.where(lane < N_EXP, logits, neg)
    t1 = jnp.max(lg, axis=-1, keepdims=True)
    i1 = jnp.min(jnp.where(lg == t1, lane, LANES), axis=-1, keepdims=True)
    lg2 = jnp.where(lane == i1, neg, lg)
    t2 = jnp.max(lg2, axis=-1, keepdims=True)
    i2 = jnp.min(jnp.where(lg2 == t2, lane, LANES), axis=-1, keepdims=True)
    e2 = jnp.exp(t2 - t1)
    w1 = 1.0 / (1.0 + e2)
    w2 = e2 / (1.0 + e2)
    rec = jnp.where(lane == ROUTE_I1, i1.astype(F32), 0.0) + jnp.where(lane == ROUTE_I2, i2.astype(F32), 0.0)
    return rec + jnp.where(lane == ROUTE_W1, w1, 0.0) + jnp.where(lane == ROUTE_W2, w2, 0.0)


def _routing_tables(route):
    n_assign = TOP_K * N_TOK
    expert = jnp.concatenate([route[:, ROUTE_I1], route[:, ROUTE_I2]]).astype(jnp.int32)
    index = jnp.arange(n_assign, dtype=jnp.int32)
    onehot = (expert[:, None] == jnp.arange(N_EXP, dtype=jnp.int32)[None, :]).astype(jnp.int32)
    running = jnp.cumsum(onehot, axis=0)
    counts = running[-1]
    rank = jnp.sum(running * onehot, axis=1) - 1
    padded = (counts + MOE_TILE - 1) // MOE_TILE * MOE_TILE
    ends = jnp.cumsum(padded)
    starts = ends - padded
    slot = jnp.sum(onehot * starts[None, :], axis=1) + rank
    sorted_token = (jnp.sort(expert * n_assign + index) % n_assign) % N_TOK
    tile_start = jnp.arange(MOE_TILES, dtype=jnp.int32) * MOE_TILE
    tile_expert = jnp.minimum(jnp.sum(tile_start[:, None] >= ends[None, :], axis=1), N_EXP - 1).astype(jnp.int32)
    first_sorted = (jnp.cumsum(counts) - counts)[tile_expert] + tile_start - starts[tile_expert]
    n_used = (ends[-1] // MOE_TILE).astype(jnp.int32).reshape(1)
    return sorted_token, slot, tile_expert, first_sorted.astype(jnp.int32), n_used


def _row_copy(src_hbm, row, dst, dst_row, sem):
    return pltpu.make_async_copy(src_hbm.at[pl.ds(row, 1)], dst.at[pl.ds(dst_row, 1)], sem)


def _experts_kernel(tok_ref, exp_ref, first_ref, used_ref, h_hbm, wg_ref, wu_ref, wd_ref, y_ref, x_buf, sems,
                    wg_s, wu_s, wd_s):
    t = pl.program_id(0)
    n_used = used_ref[0]

    def start_gather(tile, half):
        first = first_ref[tile]
        for r in range(MOE_TILE):
            tok = tok_ref[jnp.minimum(first + r, TOP_K * N_TOK - 1)]
            _row_copy(h_hbm, tok, x_buf.at[half], r, sems.at[half]).start(priority=r % 2)

    @pl.when(t == 0)
    def _():
        start_gather(0, 0)

    @pl.when(t + 1 < n_used)
    def _():
        start_gather(t + 1, (t + 1) % 2)

    @pl.when(t < n_used)
    def _():
        half = t % 2
        for r in range(MOE_TILE):
            _row_copy(h_hbm, 0, x_buf.at[half], r, sems.at[half]).wait()
        @pl.when((t == 0) | (exp_ref[t] != exp_ref[jnp.maximum(t - 1, 0)]))
        def _():
            wg_s[...] = wg_ref[...].astype(BF16)
            wu_s[...] = wu_ref[...].astype(BF16)
            wd_s[...] = wd_ref[...].astype(BF16)

        x = x_buf[half].astype(BF16)
        act = _silu(_dot(x, wg_s[...])) * _dot(x, wu_s[...])
        y_ref[...] = _dot(act.astype(BF16), wd_s[...])

    @pl.when(t >= n_used)
    def _():
        y_ref[...] = jnp.zeros_like(y_ref)


def moe_experts(h, route, w_gu, w_d, l):
    lyr = l // 2
    sorted_token, slot, tile_expert, first_sorted, n_used = _routing_tables(route)
    gu_spec = lambda half: pl.BlockSpec((None, None, D_MODEL, FF_CHUNK),
                                        lambda t, tok, exp, first, used: (lyr, exp[t], 0, half))
    y = pl.pallas_call(
        _experts_kernel,
        grid_spec=pltpu.PrefetchScalarGridSpec(
            num_scalar_prefetch=4,
            grid=(MOE_TILES,),
            in_specs=[pl.BlockSpec(memory_space=pl.ANY),
                      gu_spec(0), gu_spec(1),
                      pl.BlockSpec((None, None, FF_CHUNK, D_MODEL),
                                   lambda t, tok, exp, first, used: (lyr, exp[t], 0, 0))],
            out_specs=pl.BlockSpec((MOE_TILE, D_MODEL), lambda t, tok, exp, first, used: (t, 0)),
            scratch_shapes=[pltpu.VMEM((2, MOE_TILE, D_MODEL), F32), pltpu.SemaphoreType.DMA((2,)),
                            pltpu.VMEM((D_MODEL, FF_CHUNK), BF16), pltpu.VMEM((D_MODEL, FF_CHUNK), BF16),
                            pltpu.VMEM((FF_CHUNK, D_MODEL), BF16)]),
        out_shape=jax.ShapeDtypeStruct((MOE_SLOTS, D_MODEL), F32),
        compiler_params=_cparams(("arbitrary",)),
        name="moe_experts",
    )(sorted_token, tile_expert, first_sorted, n_used, h, w_gu, w_gu, w_d)
    return y, slot


def _moe_combine_kernel(slot_ref, y_hbm, x_ref, gf_ref, route_ref, o_ref, y_buf, sems):
    i = pl.program_id(0)
    tm = x_ref.shape[0]

    def start_gather(tile, half):
        for k in range(TOP_K):
            for r in range(tm):
                _row_copy(y_hbm, slot_ref[k * N_TOK + tile * tm + r], y_buf.at[half, k], r,
                          sems.at[half]).start(priority=r % 2)

    @pl.when(i == 0)
    def _():
        start_gather(0, 0)

    @pl.when(i + 1 < pl.num_programs(0))
    def _():
        start_gather(i + 1, (i + 1) % 2)

    half = i % 2
    for k in range(TOP_K):
        for r in range(tm):
            _row_copy(y_hbm, 0, y_buf.at[half, k], r, sems.at[half]).wait()
    route = route_ref[...]
    f = (route[:, ROUTE_W1:ROUTE_W1 + 1] * y_buf[half, 0] + route[:, ROUTE_W2:ROUTE_W2 + 1] * y_buf[half, 1])
    o_ref[...] = x_ref[...] + gf_ref[...] * f


def moe_combine(y, slot, x1, mod, route, l):
    tm = 512
    return pl.pallas_call(
        _moe_combine_kernel,
        grid_spec=pltpu.PrefetchScalarGridSpec(
            num_scalar_prefetch=1,
            grid=(N_TOK // tm,),
            in_specs=[pl.BlockSpec(memory_space=pl.ANY),
                      pl.BlockSpec((tm, D_MODEL), lambda i, s: (i, 0)),
                      pl.BlockSpec((None, None, 1, D_MODEL), lambda i, s: (l, _cond_row(i, tm), 0, 5)),
                      pl.BlockSpec((tm, LANES), lambda i, s: (i, 0))],
            out_specs=pl.BlockSpec((tm, D_MODEL), lambda i, s: (i, 0)),
            scratch_shapes=[pltpu.VMEM((2, TOP_K, tm, D_MODEL), F32), pltpu.SemaphoreType.DMA((2,))]),
        out_shape=jax.ShapeDtypeStruct((N_TOK, D_MODEL), F32),
        compiler_params=_cparams(("arbitrary",)),
        name="moe_combine",
    )(slot, y, x1, mod, route)


def _final_norm_kernel(x_ref, g_ref, o_ref):
    o_ref[...] = _rms(x_ref[...], g_ref[...])


def final_norm_call(x, g, tok0, n_tok):
    tm = 1024
    i0 = tok0 // tm
    return pl.pallas_call(
        _final_norm_kernel,
        grid=(n_tok // tm,),
        in_specs=[pl.BlockSpec((tm, D_MODEL), lambda i: (i0 + i, 0)),
                  pl.BlockSpec((1, D_MODEL), lambda i: (0, 0))],
        out_specs=pl.BlockSpec((tm, D_MODEL), lambda i: (i, 0)),
        out_shape=jax.ShapeDtypeStruct((n_tok, D_MODEL), F32),
        compiler_params=_cparams(("arbitrary",)),
        name="final_norm",
    )(x, g)


def _cache_store_kernel(*refs):
    n = len(refs) // 3
    for src, dst in zip(refs[:n], refs[2 * n:]):
        dst[...] = src[...].reshape(dst.shape)


def cache_store(sources, all_layers, l):
    n_sq = 4
    tm = n_sq * SEQ
    return pl.pallas_call(
        _cache_store_kernel,
        grid=(N_CTX // tm,),
        in_specs=([pl.BlockSpec((tm, s.shape[1]), lambda i: (i, 0)) for s in sources]
                  + [pl.BlockSpec(memory_space=pl.ANY)] * len(all_layers)),
        out_specs=[pl.BlockSpec((n_sq, None, SEQ, a.shape[-1]), lambda i: (i, l, 0, 0)) for a in all_layers],
        out_shape=[jax.ShapeDtypeStruct(a.shape, F32) for a in all_layers],
        input_output_aliases={len(sources) + k: k for k in range(len(all_layers))},
        compiler_params=_cparams(("arbitrary",)),
        name="cache_store",
    )(*sources, *all_layers)


def _gate_row(p):
    return jnp.zeros((1, LANES), F32).at[0, GATE_A:GATE_A + 2 * H_A].set(p.reshape(-1))


def _tile_lanes(g, width):
    return jnp.tile(g, width // g.shape[0])[None, :]


def kernel(x_prompt, x_sample, state_delta, cache_gqa_k, cache_gqa_v, cache_diff_k, cache_diff_v, c, c_ctx,
           ada_w, ada_b, norm_mix, norm_ffn, w_in, conv_w, dn_a_log, dn_dt_bias, dn_norm, gqa_q_norm,
           gqa_k_norm, diff_lambda, diff_norm, w_out, ffn_w_gu, ffn_w_down, moe_router, moe_w_gu, moe_w_down,
           final_norm):
    cond8 = jnp.concatenate([c_ctx[None, :], c, jnp.zeros((8 - N_COND, D_MODEL), F32)], axis=0)
    w_out_b = w_out.astype(BF16)
    norm_mix3 = norm_mix.reshape(DEPTH, 1, D_MODEL)
    norm_ffn3 = norm_ffn.reshape(DEPTH, 1, D_MODEL)
    ffn_gu_b, ffn_d_b = ffn_w_gu.astype(BF16), ffn_w_down.astype(BF16)
    router_p = jnp.pad(moe_router, ((0, 0), (0, 0), (0, LANES - N_EXP)))
    rope_b = _rope_tables(DEC_SEQ // GRID_W, HEAD_DIM)
    rope_c = _rope_tables(DEC_SEQ // GRID_W, DIFF_HD)
    s0_ctx = jnp.zeros((BATCH, 2, H_A, HEAD_DIM, HEAD_DIM), F32)
    cache_b = (cache_gqa_k.reshape(DEC_BATCH, DEPTH, PAST_LEN, KV_B * HEAD_DIM),
               cache_gqa_v.reshape(DEC_BATCH, DEPTH, PAST_LEN, KV_B * HEAD_DIM))
    cache_c = (cache_diff_k.reshape(DEC_BATCH, DEPTH, PAST_LEN, C_W),
               cache_diff_v.reshape(DEC_BATCH, DEPTH, PAST_LEN, C_W))

    mod = ada_modulation(cond8, ada_w, ada_b)[:, :N_COND].reshape(DEPTH, N_COND, 1, 6 * D_MODEL)

    x = jnp.concatenate([x_prompt.reshape(N_CTX, D_MODEL), x_sample.reshape(N_LAT, D_MODEL)], axis=0)
    all_states = jnp.zeros((BATCH, DEPTH, 2, H_A, HEAD_DIM, HEAD_DIM), F32)
    ctx_layers = tuple(jnp.zeros((BATCH, DEPTH, SEQ, w), F32)
                       for w in (KV_B * HEAD_DIM, KV_B * HEAD_DIM, C_W, C_W))
    ctx = dict(n_seq=BATCH, seq_len=SEQ, tok0=0)
    lat = dict(n_seq=DEC_BATCH, seq_len=DEC_SEQ, tok0=N_CTX)
    for l in range(DEPTH):
        qkv_a, z_a, q_b, k_b, v_b, q_c, k_c, v_c, gate = in_projection(x, mod, norm_mix3, w_in, l)

        dn_args = (qkv_a, z_a, gate, conv_w[l], _gate_row(dn_a_log[l]), _gate_row(dn_dt_bias[l]),
                   _tile_lanes(dn_norm[l], LANES))
        oa_ctx, all_states = delta_mixer(*dn_args, s0_ctx, n_sub=2, all_states=all_states, layer=l, **ctx)
        oa_lat, _ = delta_mixer(*dn_args, state_delta[:, l], n_sub=1, **lat)

        gq, gk = _tile_lanes(gqa_q_norm[l], LANES), _tile_lanes(gqa_k_norm[l], LANES)
        ob_ctx, kn_ctx = gqa_mixer(q_b, k_b, v_b, gq, gk, **ctx)
        (ob_lat,) = gqa_mixer(q_b, k_b, v_b, gq, gk, cache=cache_b, rope=rope_b, layer=l, **lat)

        lam_init = 0.8 - 0.6 * math.exp(-0.3 * l)
        gn = _tile_lanes(diff_norm[l], C_W)
        oc_ctx = diff_mixer(q_c, k_c, v_c, diff_lambda[l], gn, lam_init, **ctx)
        oc_lat = diff_mixer(q_c, k_c, v_c, diff_lambda[l], gn, lam_init, cache=cache_c, rope=rope_c, layer=l, **lat)

        mixes = ((oa_ctx, oa_lat), (ob_ctx, ob_lat), (oc_ctx, oc_lat))
        if l % 2 == 0:
            x1, h2 = out_projection(x, mixes, w_out_b, mod, norm_ffn3, l)
            x = ffn(h2, ffn_gu_b, ffn_d_b, x1, mod, l)
        else:
            x1, h2, route = out_projection(x, mixes, w_out_b, mod, norm_ffn3, l, router_p[l // 2])
            y_slots, slot = moe_experts(h2, route, moe_w_gu, moe_w_down, l)
            x = moe_combine(y_slots, slot, x1, mod, route, l)

        ctx_layers = cache_store((kn_ctx, v_b, k_c, v_c), ctx_layers, l)

    y_prompt = final_norm_call(x, final_norm[None, :], 0, N_CTX).reshape(BATCH, SEQ, D_MODEL)
    y_sample = final_norm_call(x, final_norm[None, :], N_CTX, N_LAT).reshape(DEC_BATCH, DEC_SEQ, D_MODEL)
    new_gk, new_gv, new_dk, new_dv = ctx_layers
    return (y_prompt, y_sample, all_states,
            new_gk.reshape(BATCH, DEPTH, SEQ, KV_B, HEAD_DIM), new_gv.reshape(BATCH, DEPTH, SEQ, KV_B, HEAD_DIM),
            new_dk.reshape(BATCH, DEPTH, SEQ, H_C, 2, DIFF_HD), new_dv.reshape(BATCH, DEPTH, SEQ, H_C, 2 * DIFF_HD))
```

```python
import functools
import math

import jax
import jax.numpy as jnp
from jax import lax
from jax.experimental import pallas as pl
from jax.experimental.pallas import tpu as pltpu

D_MODEL = 1024
BATCH = 16
SEQ = 256
DEPTH = 4
DEC_BATCH = 2
DEC_SEQ = 1024
PAST_LEN = 512
GRID_W = 64
HEAD_DIM = 64
H_A = 6
A_W = H_A * HEAD_DIM
H_B = 6
KV_B = 2
B_W = H_B * HEAD_DIM
H_C = 4
DIFF_HD = 32
C_W = H_C * 2 * DIFF_HD
CHUNK = 64
ROPE_THETA = 10000.0
D_FF = 2816
N_EXP = 8
TOP_K = 2
D_FF_E = 1408
EPS = 1e-6

N_CTX = BATCH * SEQ
N_LAT = DEC_BATCH * DEC_SEQ
N_TOK = N_CTX + N_LAT
N_COND = 1 + DEC_BATCH
IN_PAD = 3072
FF_CHUNK = D_FF_E
LANES = 128
VMEM_LIMIT = 56 * 1024 * 1024

IN_SPLIT = (("qkv_a", 3 * A_W), ("z_a", A_W), ("q_b", B_W), ("k_b", KV_B * HEAD_DIM), ("v_b", KV_B * HEAD_DIM),
            ("q_c", C_W), ("k_c", C_W), ("v_c", C_W), ("gate", LANES))
GATE_BETA, GATE_A = 0, 2 * H_A

F32 = jnp.float32
BF16 = jnp.bfloat16


def _cparams(sem):
    return pltpu.CompilerParams(dimension_semantics=sem, vmem_limit_bytes=VMEM_LIMIT)


def _dot(a, b):
    return jnp.dot(a, b, preferred_element_type=F32)


def _dot_split(a, b):
    a_hi, b_hi = a.astype(BF16), b.astype(BF16)
    a_lo = (a - a_hi.astype(F32)).astype(BF16)
    b_lo = (b - b_hi.astype(F32)).astype(BF16)
    return _dot(jnp.concatenate([a_hi, a_lo, a_hi], axis=1), jnp.concatenate([b_hi, b_hi, b_lo], axis=0))


def _dot_nt(a, b):
    return lax.dot_general(a, b, (((1,), (1,)), ((), ())), preferred_element_type=F32)


def _silu(x):
    return x * jax.nn.sigmoid(x)


def _softplus(x):
    return jnp.maximum(x, 0.0) + jnp.log1p(jnp.exp(-jnp.abs(x)))


def _rms(x, g):
    return x * lax.rsqrt(jnp.mean(x * x, axis=-1, keepdims=True) + EPS) * g


def _lane(shape):
    return lax.broadcasted_iota(jnp.int32, shape, len(shape) - 1)


def _group_mean_matrix(width, group):
    i = lax.broadcasted_iota(jnp.int32, (width, width), 0)
    j = lax.broadcasted_iota(jnp.int32, (width, width), 1)
    return jnp.where(i // group == j // group, 1.0 / group, 0.0).astype(BF16)


def _group_rms(x, gmat, g):
    sq = x * x
    hi = sq.astype(BF16)
    lo = (sq - hi.astype(F32)).astype(BF16)
    ms = _dot(jnp.concatenate([hi, lo], axis=1), jnp.concatenate([gmat, gmat], axis=0))
    return x * lax.rsqrt(ms + EPS) * g


def _cond_row(i, tm):
    return jnp.maximum((i * tm) // DEC_SEQ - (N_CTX // DEC_SEQ - 1), 0)


def _ada_kernel(c_ref, w_ref, b_ref, o_ref):
    o_ref[...] = _dot_split(_silu(c_ref[...]), w_ref[...]) + b_ref[...]


def ada_modulation(cond8, ada_w, ada_b):
    tn = 2048
    return pl.pallas_call(
        _ada_kernel,
        grid=(DEPTH, 6 * D_MODEL // tn),
        in_specs=[pl.BlockSpec((8, D_MODEL), lambda l, j: (0, 0)),
                  pl.BlockSpec((None, D_MODEL, tn), lambda l, j: (l, 0, j)),
                  pl.BlockSpec((None, 1, tn), lambda l, j: (l, 0, j))],
        out_specs=pl.BlockSpec((None, 8, tn), lambda l, j: (l, 0, j)),
        out_shape=jax.ShapeDtypeStruct((DEPTH, 8, 6 * D_MODEL), F32),
        compiler_params=_cparams(("arbitrary", "arbitrary")),
        name="ada_modulation",
    )(cond8, ada_w, ada_b.reshape(DEPTH, 1, 6 * D_MODEL))


GATE_LO = 4 * A_W
GATE_HI = GATE_LO + 4 * H_A
IN_COLS = GATE_HI + B_W + 2 * KV_B * HEAD_DIM + 3 * C_W


def _inproj_kernel(x_ref, g_ref, sh_ref, sc_ref, w_ref, *refs):
    o_refs, w_s = refs[:-1], refs[-1]

    @pl.when(pl.program_id(0) == 0)
    def _():
        w_s[:, :GATE_LO] = w_ref[:, :GATE_LO].astype(BF16)
        w_s[:, GATE_LO:IN_COLS - 4 * H_A] = w_ref[:, GATE_HI:].astype(BF16)
        tail = jnp.concatenate([w_ref[:, GATE_LO:GATE_HI], jnp.zeros((D_MODEL, LANES - 4 * H_A), F32)], axis=1)
        w_s[:, IN_COLS - 4 * H_A:] = tail.astype(BF16)

    h = _rms(x_ref[...], g_ref[...]) * (1.0 + sc_ref[...]) + sh_ref[...]
    acc = _dot(h.astype(BF16), w_s[...])
    off = 0
    for o_ref, (_, width) in zip(o_refs, IN_SPLIT):
        o_ref[...] = acc[:, off:off + width]
        off += width


def in_projection(x, mod, norm_g, w_in, l):
    tm = 512
    return pl.pallas_call(
        _inproj_kernel,
        grid=(N_TOK // tm,),
        in_specs=[pl.BlockSpec((tm, D_MODEL), lambda i: (i, 0)),
                  pl.BlockSpec((None, 1, D_MODEL), lambda i: (l, 0, 0)),
                  pl.BlockSpec((None, None, 1, D_MODEL), lambda i: (l, _cond_row(i, tm), 0, 0)),
                  pl.BlockSpec((None, None, 1, D_MODEL), lambda i: (l, _cond_row(i, tm), 0, 1)),
                  pl.BlockSpec((None, D_MODEL, IN_COLS), lambda i: (l, 0, 0))],
        out_specs=[pl.BlockSpec((tm, w), lambda i: (i, 0)) for _, w in IN_SPLIT],
        out_shape=[jax.ShapeDtypeStruct((N_TOK, w), F32) for _, w in IN_SPLIT],
        scratch_shapes=[pltpu.VMEM((D_MODEL, IN_PAD), BF16)],
        compiler_params=_cparams(("arbitrary",)),
        name="in_projection",
    )(x, norm_g, mod, mod, w_in)


SOLVE_BLOCK = 16


def _delta_kernel(*refs, seq_len, n_sub, n_cp, aliased):
    qkv_ref, z_ref, gate_ref, cw_ref, alog_ref, dtb_ref, ng_ref, s0_ref = refs[:8]
    (o_ref, sfin_ref, pad_s, kk_s, qq_s, vk_s, gc_s, beta_s, u_s, w_s, qd_s, a_s, kdt_s, egt_s, sa_s, o_s,
     tp_s, t32_s, rp_s, c32_s, m_s, x_s, vbd_s, v_s, op_s) = refs[9 if aliased else 8:]
    n_chunk = seq_len // CHUNK
    n_pair = H_A // 2
    n_unit = n_sub * H_A
    half = HEAD_DIM
    lo = _lane((CHUNK, LANES)) < half
    lo_row = _lane((1, LANES)) < half
    lo16 = _lane((CHUNK, LANES)).astype(F32).astype(BF16) < half
    row = lax.broadcasted_iota(jnp.int32, (CHUNK, LANES), 0)
    col = _lane((CHUNK, LANES)) % half
    ahead = jnp.where(lo, row - col, col - row)
    incl = ahead >= 0
    strict = ahead > 0
    diag_blk = row // SOLVE_BLOCK == col // SOLVE_BLOCK
    big_row = lax.broadcasted_iota(jnp.int32, (LANES, LANES), 0)
    anti = big_row // half + _lane((LANES, LANES)) // half == 1
    eye = (big_row == _lane((LANES, LANES))).astype(F32)

    def both(a):
        return jnp.concatenate([jnp.where(lo, a, 0.0), jnp.where(lo, 0.0, a)], axis=0)

    rb = seq_len
    lo_rb = _lane((rb, LANES)) < half
    zero8 = jnp.zeros((8, 3 * A_W), F32)
    pad_s[0:8, :] = zero8
    pad_s[8 + seq_len:16 + seq_len, :] = zero8

    def l2n(x):
        return x * lax.rsqrt(0.5 * jnp.sum(x * x, axis=-1, keepdims=True) + EPS)

    for sq in range(n_sub):
        pad_s[8:8 + seq_len, :] = qkv_ref[sq * seq_len:(sq + 1) * seq_len, :]
        for r0 in range(0, seq_len, rb):
            y = _silu(pad_s[7 + r0:7 + r0 + rb, :] * cw_ref[0:1, :] + pad_s[8 + r0:8 + r0 + rb, :] * cw_ref[1:2, :]
                      + pad_s[9 + r0:9 + r0 + rb, :] * cw_ref[2:3, :])

            def dup(part, p):
                c = y[:, part * A_W + p * LANES:part * A_W + (p + 1) * LANES]
                r = pltpu.roll(c, half, axis=1)
                return jnp.where(lo_rb, c, r), jnp.where(lo_rb, r, c)

            for p in range(n_pair):
                qs, ks, vs = dup(0, p), dup(1, p), dup(2, p)
                for e in range(2):
                    u = sq * H_A + 2 * p + e
                    kk = l2n(ks[e])
                    kk_s[u, r0:r0 + rb, :] = kk
                    qq_s[u, r0:r0 + rb, :] = l2n(qs[e]) * (HEAD_DIM ** -0.5)
                    vk_s[u, r0:r0 + rb, :] = jnp.where(lo_rb, vs[e], kk)

    cb = 256
    ci = lax.broadcasted_iota(jnp.int32, (cb, cb), 0)
    cj = lax.broadcasted_iota(jnp.int32, (cb, cb), 1)
    same = ci // CHUNK == cj // CHUNK
    m_pre = jnp.where(same & (ci >= cj), 1.0, 0.0).astype(BF16)
    m_suf = jnp.where(same & (ci <= cj), 1.0, 0.0).astype(BF16)
    fwd_cols = (_lane((cb, LANES)) >= GATE_A) & (_lane((cb, LANES)) < GATE_A + H_A)
    for r0 in range(0, n_sub * seq_len, cb):
        gt = gate_ref[r0:r0 + cb, :]
        beta_s[r0:r0 + cb, :] = jax.nn.sigmoid(gt)
        g = -jnp.exp(alog_ref[...]) * _softplus(gt + dtb_ref[...])
        g1 = g.astype(BF16)
        r1 = g - g1.astype(F32)
        g2 = r1.astype(BF16)
        g3 = (r1 - g2.astype(F32)).astype(BF16)
        blk = jnp.concatenate([g1, g2, g3], axis=1)
        pre, suf = _dot(m_pre, blk), _dot(m_suf, blk)
        pre = pre[:, :LANES] + pre[:, LANES:2 * LANES] + pre[:, 2 * LANES:]
        suf = suf[:, :LANES] + suf[:, LANES:2 * LANES] + suf[:, 2 * LANES:]
        gc_s[r0:r0 + cb, :] = jnp.where(fwd_cols, pre, suf)

    zero = jnp.zeros((half, half), F32)
    for sq in range(n_sub):
        for h in range(H_A):
            sa_s[sq * H_A + h] = jnp.concatenate([jnp.concatenate([zero, s0_ref[sq, 1, h]], axis=1),
                                                  jnp.concatenate([s0_ref[sq, 0, h], zero], axis=1)], axis=0)

    n_fac = int(math.log2(SOLVE_BLOCK))

    def fill_stage(c, rows, cc):
        for sq in range(n_sub):
            grow = pl.ds(pl.multiple_of(sq * seq_len + c * CHUNK, CHUNK), CHUNK)
            gcb = gc_s[grow, :]
            gct = gcb.T
            bb = beta_s[grow, :]
            for h in range(H_A):
                u = sq * H_A + h
                w = cc * n_unit + u
                full = (CHUNK, LANES)
                cff = jnp.broadcast_to(gcb[:, GATE_A + h:GATE_A + h + 1], full)
                cbb = jnp.broadcast_to(gcb[:, GATE_A + H_A + h:GATE_A + H_A + h + 1], full)
                bff = jnp.broadcast_to(bb[:, GATE_BETA + h:GATE_BETA + h + 1], full)
                bbb = jnp.broadcast_to(bb[:, GATE_BETA + H_A + h:GATE_BETA + H_A + h + 1], full)
                gr_fb = jnp.concatenate([gct[GATE_A + h:GATE_A + h + 1, :],
                                         gct[GATE_A + H_A + h:GATE_A + H_A + h + 1, :]], axis=1)
                decay = jnp.where(incl, jnp.exp(jnp.where(incl, jnp.where(lo, cff, cbb) - gr_fb, 0.0)), 0.0)
                egf, egb = jnp.exp(cff), jnp.exp(cbb)
                tot_f = gcb[CHUNK - 1:CHUNK, GATE_A + h:GATE_A + h + 1]
                tot_b = gcb[0:1, GATE_A + H_A + h:GATE_A + H_A + h + 1]
                kk, qq, vk = kk_s[u, rows, :], qq_s[u, rows, :], vk_s[u, rows, :]
                k16 = kk[:, :half].astype(BF16)
                kq = jnp.concatenate([k16, qq[:, :half].astype(BF16)], axis=0)
                gram = _dot_nt(kq, jnp.concatenate([k16, k16], axis=0))
                neg_l = jnp.where(strict, gram[:CHUNK] * jnp.where(lo, -bff, -bbb) * decay, 0.0)
                a_s[u, rows, :] = jnp.where(incl, gram[CHUNK:] * decay, 0.0).astype(BF16)
                neg_d = jnp.where(diag_blk, neg_l, 0.0)
                pd = both(neg_d)
                t32_s[w] = eye + pd
                tp_s[w, :, :LANES] = (eye + pd).astype(BF16)
                tp_s[w, :, LANES:] = pd.astype(BF16)
                x_f = vk * jnp.where(lo, bff, bff * egf)
                x_b = vk * jnp.where(lo, bbb, bbb * egb)
                rp_s[w, :, :LANES] = jnp.concatenate([x_f, x_b], axis=0).astype(BF16)
                rp_s[w, :, LANES:] = both(neg_l - neg_d).astype(BF16)
                qd_s[u, rows, :] = (qq * jnp.where(lo, egb, egf)).astype(BF16)
                kd = kk * jnp.exp(jnp.where(lo, tot_b - cbb, tot_f - cff))
                kdt_s[u, c] = kd.T.astype(BF16)
                egt_s[u, c] = jnp.exp(jnp.where(lo_row, tot_f, tot_b))

    def prep(ci, carry):
        rows_of = []
        for cc in range(n_cp):
            c = ci * n_cp + cc
            rows_of.append(pl.ds(pl.multiple_of(c * CHUNK, CHUNK), CHUNK))
            fill_stage(c, rows_of[cc], cc)
        n_slot = n_cp * n_unit
        for w in range(n_slot):
            tp_s[w, :, LANES:] = _dot(tp_s[w, :, LANES:], tp_s[w, :, LANES:]).astype(BF16)
        for j in range(1, n_fac):
            for w in range(n_slot):
                if j + 1 < n_fac:
                    tp = _dot(tp_s[w, :, LANES:], tp_s[w])
                    t = t32_s[w] + tp[:, :LANES]
                    t32_s[w] = t
                    tp_s[w, :, :LANES] = t.astype(BF16)
                    tp_s[w, :, LANES:] = tp[:, LANES:].astype(BF16)
                else:
                    t = t32_s[w] + _dot(tp_s[w, :, LANES:], tp_s[w, :, :LANES])
                    tp_s[w, :, :LANES] = t.astype(BF16)
        for w in range(n_slot):
            cm = _dot(tp_s[w, :, :LANES], rp_s[w])
            c32_s[w] = cm[:, :LANES]
            rp_s[w] = cm.astype(BF16)
        assert CHUNK // SOLVE_BLOCK == 4
        for w in range(n_slot):
            mm = _dot(rp_s[w, :, LANES:], rp_s[w])
            y1 = c32_s[w] + mm[:, :LANES]
            c32_s[w] = y1
            x_s[w] = y1.astype(BF16)
            m_s[w] = mm[:, LANES:].astype(BF16)
        for w in range(n_slot):
            x = c32_s[w] + _dot(m_s[w], x_s[w])
            x_f, x_b = x[:CHUNK], pltpu.roll(x[CHUNK:], half, axis=1)
            u_s[w % n_unit, rows_of[w // n_unit], :] = jnp.where(lo, x_f, x_b)
            w_s[w % n_unit, rows_of[w // n_unit], :] = jnp.where(lo, x_b, x_f).astype(BF16)
        return carry

    lax.fori_loop(0, n_chunk // n_cp, prep, 0)

    def scan(s, carry):
        rf = pl.ds(pl.multiple_of(s * CHUNK, CHUNK), CHUNK)
        sb = n_chunk - 1 - s
        rbk = pl.ds(pl.multiple_of(sb * CHUNK, CHUNK), CHUNK)
        for u in range(n_unit):
            w = jnp.where(lo16, w_s[u, rbk, :], w_s[u, rf, :])
            qd = jnp.where(lo16, qd_s[u, rbk, :], qd_s[u, rf, :])
            wq = _dot(jnp.concatenate([w, qd], axis=0), sa_s[u].astype(BF16))
            v = jnp.where(lo, u_s[u, rf, :], u_s[u, rbk, :]) - wq[:CHUNK]
            v_s[u] = v.astype(BF16)
            vbd_s[u] = both(v).astype(BF16)
            op_s[u] = wq[CHUNK:]
        for u in range(n_unit):
            a = jnp.where(lo16, a_s[u, rf, :], a_s[u, rbk, :])
            o = op_s[u] + _dot(a, vbd_s[u])
            o_s[u, rf, 0:half] = o[:, :half]
            o_s[u, rbk, half:LANES] = o[:, half:]
            kdt = jnp.concatenate([kdt_s[u, sb][:half], kdt_s[u, s][half:]], axis=0)
            egt = jnp.where(lo_row, egt_s[u, s], egt_s[u, sb])
            sa_s[u] = sa_s[u] * egt + jnp.where(anti, _dot(kdt, v_s[u]), 0.0)
        return carry

    lax.fori_loop(0, n_chunk, scan, 0)

    for sq in range(n_sub):
        for r0 in range(0, seq_len, rb):
            tr = sq * seq_len + r0
            for p in range(n_pair):
                nrm = []
                for e in range(2):
                    ofb = o_s[sq * H_A + 2 * p + e, r0:r0 + rb, :]
                    oo = ofb + pltpu.roll(ofb, half, axis=1)
                    nrm.append(oo * lax.rsqrt(jnp.sum(oo * oo, axis=-1, keepdims=True) * (0.5 / HEAD_DIM) + EPS))
                o_ref[tr:tr + rb, p * LANES:(p + 1) * LANES] = (
                    jnp.where(lo_rb, nrm[0], nrm[1]) * ng_ref[...]
                    * _silu(z_ref[tr:tr + rb, p * LANES:(p + 1) * LANES]))
        for h in range(H_A):
            st = sa_s[sq * H_A + h]
            sfin_ref[sq, 0, h] = st[half:, :half]
            sfin_ref[sq, 1, h] = st[:half, half:]


def delta_mixer(qkv, z, gate, cw, alog_row, dtb_row, ng2, s0, *, n_seq, seq_len, tok0, n_sub,
                all_states=None, layer=0):
    n_chunk = seq_len // CHUNK
    n_unit = n_sub * H_A
    blk = n_sub * seq_len
    b0 = tok0 // blk
    tok = lambda w: pl.BlockSpec((blk, w), lambda b: (b0 + b, 0))
    full = lambda a: pl.BlockSpec(a.shape, lambda b: (0,) * a.ndim)
    state = pl.BlockSpec((n_sub, 2, H_A, HEAD_DIM, HEAD_DIM), lambda b: (b, 0, 0, 0, 0))
    per_unit = lambda dt: pltpu.VMEM((n_unit, seq_len, LANES), dt)
    n_cp = 2
    stage = lambda rows, cols, dt, n=n_unit: pltpu.VMEM((n, rows, cols), dt)
    aliased = all_states is not None
    in_specs = [tok(3 * A_W), tok(A_W), tok(LANES), full(cw), full(alog_row), full(dtb_row), full(ng2), state]
    args = [qkv, z, gate, cw, alog_row, dtb_row, ng2, s0]
    if aliased:
        in_specs.append(pl.BlockSpec(memory_space=pl.ANY))
        args.append(all_states)
        state_out = pl.BlockSpec((n_sub, None, 2, H_A, HEAD_DIM, HEAD_DIM), lambda b: (b, layer, 0, 0, 0, 0))
        state_shape = jax.ShapeDtypeStruct(all_states.shape, F32)
    else:
        state_out = state
        state_shape = jax.ShapeDtypeStruct((n_seq, 2, H_A, HEAD_DIM, HEAD_DIM), F32)
    return pl.pallas_call(
        functools.partial(_delta_kernel, seq_len=seq_len, n_sub=n_sub, n_cp=n_cp, aliased=aliased),
        grid=(n_seq // n_sub,),
        in_specs=in_specs,
        out_specs=[pl.BlockSpec((blk, A_W), lambda b: (b, 0)), state_out],
        out_shape=[jax.ShapeDtypeStruct((n_seq * seq_len, A_W), F32), state_shape],
        input_output_aliases={len(args) - 1: 1} if aliased else {},
        scratch_shapes=[pltpu.VMEM((seq_len + 16, 3 * A_W), F32),
                        per_unit(F32), per_unit(F32), per_unit(F32),
                        pltpu.VMEM((blk, LANES), F32),
                        pltpu.VMEM((blk, LANES), F32),
                        per_unit(F32), per_unit(BF16), per_unit(BF16), per_unit(BF16),
                        pltpu.VMEM((n_unit, n_chunk, LANES, HEAD_DIM), BF16),
                        pltpu.VMEM((n_unit, n_chunk, 1, LANES), F32),
                        stage(LANES, LANES, F32),
                        per_unit(F32),
                        stage(LANES, 2 * LANES, BF16, n_cp * n_unit),
                        stage(LANES, LANES, F32, n_cp * n_unit),
                        stage(LANES, 2 * LANES, BF16, n_cp * n_unit),
                        stage(LANES, LANES, F32, n_cp * n_unit),
                        stage(LANES, LANES, BF16, n_cp * n_unit),
                        stage(LANES, LANES, BF16, n_cp * n_unit),
                        stage(LANES, LANES, BF16),
                        stage(CHUNK, LANES, BF16),
                        stage(CHUNK, LANES, F32)],
        compiler_params=_cparams(("arbitrary",)),
        name=f"delta_mixer_{seq_len}",
    )(*args)


def _rope(x, cos, sin_signed, quarter):
    width = x.shape[-1]
    swapped = jnp.where((_lane(x.shape) % (2 * quarter)) < quarter,
                        pltpu.roll(x, width - quarter, axis=1), pltpu.roll(x, quarter, axis=1))
    return x * cos + swapped * sin_signed


def _rope_tables(rows, dim):
    nf = dim // 4
    inv = ROPE_THETA ** (-jnp.arange(nf, dtype=F32) / nf)
    r = jnp.repeat(jnp.arange(rows, dtype=F32), GRID_W)
    c = jnp.tile(jnp.arange(GRID_W, dtype=F32), rows)
    ar, ac = r[:, None] * inv, c[:, None] * inv
    cos = jnp.concatenate([jnp.cos(ar), jnp.cos(ar), jnp.cos(ac), jnp.cos(ac)], axis=1)
    sin = jnp.concatenate([-jnp.sin(ar), jnp.sin(ar), -jnp.sin(ac), jnp.sin(ac)], axis=1)
    return jnp.tile(cos, (1, LANES // dim)), jnp.tile(sin, (1, LANES // dim))


def _gqa_kernel(*refs, cached, n_sub):
    if cached:
        (q_ref, k_ref, v_ref, gq_ref, gk_ref, ck_ref, cv_ref, cosq_ref, sinq_ref, cosk_ref, sink_ref,
         o_ref, k_s, v_s) = refs
    else:
        q_ref, k_ref, v_ref, gq_ref, gk_ref, o_ref, kn_ref, k_s, v_s = refs
    gmat = _group_mean_matrix(LANES, HEAD_DIM)
    quarter = HEAD_DIM // 4
    n_k = k_ref.shape[0] // n_sub

    @pl.when(pl.program_id(1) == 0)
    def _():
        for sq in range(n_sub):
            k = _group_rms(k_ref[sq * n_k:(sq + 1) * n_k, :], gmat, gk_ref[...])
            v = v_ref[sq * n_k:(sq + 1) * n_k, :]
            if cached:
                k = _rope(k, cosk_ref[...], sink_ref[...], quarter)
                k = jnp.concatenate([ck_ref[...], k], axis=0)
                v = jnp.concatenate([cv_ref[...], v], axis=0)
            else:
                kn_ref[sq * n_k:(sq + 1) * n_k, :] = k
            k_s[sq] = k.astype(BF16)
            v_s[sq] = v.astype(BF16)

    tq = q_ref.shape[0] // n_sub
    lo = _lane((tq, LANES)) < HEAD_DIM
    group = H_B // KV_B
    q_scale = (HEAD_DIM ** -0.5) * math.log2(math.e)
    for sq in range(n_sub):
        k, v = k_s[sq], v_s[sq]
        qs = []
        for j in range(H_B // 2):
            q = _group_rms(q_ref[sq * tq:(sq + 1) * tq, j * LANES:(j + 1) * LANES], gmat, gq_ref[...])
            if cached:
                q = _rope(q, cosq_ref[...], sinq_ref[...], quarter)
            q = q * q_scale
            qs.append((q, pltpu.roll(q, HEAD_DIM, axis=1)))
        outs = [None] * H_B
        for kv in range(KV_B):
            keep = lo if kv == 0 else ~lo
            heads = range(kv * group, (kv + 1) * group)
            q3 = jnp.concatenate([jnp.where(keep, qs[h // 2][0 if h % 2 == kv else 1], 0.0).astype(BF16)
                                  for h in heads], axis=0)
            s = _dot_nt(q3, k)
            p = jnp.exp2(s - jnp.max(s, axis=-1, keepdims=True))
            r = _dot(p.astype(BF16), v) / jnp.sum(p, axis=-1, keepdims=True)
            for i, h in enumerate(heads):
                rh = r[i * tq:(i + 1) * tq]
                outs[h] = rh if h % 2 == kv else pltpu.roll(rh, HEAD_DIM, axis=1)
        for j in range(H_B // 2):
            o_ref[sq * tq:(sq + 1) * tq, j * LANES:(j + 1) * LANES] = jnp.where(lo, outs[2 * j], outs[2 * j + 1])


def gqa_mixer(q, k, v, gq, gk, *, n_seq, seq_len, tok0, cache=None, rope=None, layer=0):
    cached = cache is not None
    n_sub = 1 if cached else 2
    n_seq, seq_len = n_seq // n_sub, seq_len * n_sub
    tq = 256 * n_sub
    nq = seq_len // tq
    b0 = tok0 // seq_len
    q0 = tok0 // tq
    kvw = KV_B * HEAD_DIM
    in_specs = [pl.BlockSpec((tq, B_W), lambda b, i: (q0 + b * nq + i, 0)),
                pl.BlockSpec((seq_len, kvw), lambda b, i: (b0 + b, 0)),
                pl.BlockSpec((seq_len, kvw), lambda b, i: (b0 + b, 0)),
                pl.BlockSpec((1, LANES), lambda b, i: (0, 0)),
                pl.BlockSpec((1, LANES), lambda b, i: (0, 0))]
    args = [q, k, v, gq, gk]
    out_specs = [pl.BlockSpec((tq, B_W), lambda b, i: (b * nq + i, 0))]
    out_shape = [jax.ShapeDtypeStruct((n_seq * seq_len, B_W), F32)]
    if cached:
        cos, sin = rope
        cache_spec = pl.BlockSpec((None, None, PAST_LEN, kvw), lambda b, i: (b, layer, 0, 0))
        in_specs += [cache_spec, cache_spec,
                     pl.BlockSpec((tq, LANES), lambda b, i: (i, 0)),
                     pl.BlockSpec((tq, LANES), lambda b, i: (i, 0)),
                     pl.BlockSpec((seq_len, LANES), lambda b, i: (0, 0)),
                     pl.BlockSpec((seq_len, LANES), lambda b, i: (0, 0))]
        args += [cache[0], cache[1], cos, sin, cos, sin]
    else:
        out_specs.append(pl.BlockSpec((seq_len, kvw), lambda b, i: (b, 0)))
        out_shape.append(jax.ShapeDtypeStruct((n_seq * seq_len, kvw), F32))
    n_keys = seq_len // n_sub + (PAST_LEN if cached else 0)
    return pl.pallas_call(
        functools.partial(_gqa_kernel, cached=cached, n_sub=n_sub),
        grid=(n_seq, nq),
        in_specs=in_specs, out_specs=out_specs, out_shape=out_shape,
        scratch_shapes=[pltpu.VMEM((n_sub, n_keys, kvw), BF16), pltpu.VMEM((n_sub, n_keys, kvw), BF16)],
        compiler_params=_cparams(("arbitrary", "arbitrary")),
        name=f"gqa_mixer_{seq_len}",
    )(*args)


def _diff_kernel(*refs, cached, lam_init, n_sub):
    if cached:
        (q_ref, k_ref, v_ref, lam_ref, gn_ref, ck_ref, cv_ref, cosq_ref, sinq_ref, cosk_ref, sink_ref,
         o_ref) = refs
    else:
        q_ref, k_ref, v_ref, lam_ref, gn_ref, o_ref = refs
    quarter = DIFF_HD // 4
    lv = lam_ref[...]
    lam = (jnp.exp(jnp.sum(lv[0:1] * lv[1:2], axis=-1, keepdims=True))
           - jnp.exp(jnp.sum(lv[2:3] * lv[3:4], axis=-1, keepdims=True)) + lam_init)
    gmat = _group_mean_matrix(C_W, 2 * DIFF_HD)
    tq, n_k = q_ref.shape[0] // n_sub, k_ref.shape[0] // n_sub
    lane = _lane((tq, C_W))
    for sb in range(n_sub):
        q = q_ref[sb * tq:(sb + 1) * tq, :]
        k, v = k_ref[sb * n_k:(sb + 1) * n_k, :], v_ref[sb * n_k:(sb + 1) * n_k, :]
        if cached:
            cq = jnp.concatenate([cosq_ref[...]] * 2, axis=1)
            sq = jnp.concatenate([sinq_ref[...]] * 2, axis=1)
            ck = jnp.concatenate([cosk_ref[...]] * 2, axis=1)
            sk = jnp.concatenate([sink_ref[...]] * 2, axis=1)
            q = _rope(q, cq, sq, quarter)
            k = _rope(k, ck, sk, quarter)
            k = jnp.concatenate([ck_ref[...], k], axis=0)
            v = jnp.concatenate([cv_ref[...], v], axis=0)
        k = k.astype(BF16)
        v = v.astype(BF16)
        q = q * ((DIFF_HD ** -0.5) * math.log2(math.e))
        o = jnp.zeros(q.shape, F32)
        for h in range(H_C):
            ps, cs = [], []
            for m in range(2):
                qm = jnp.where(lane // DIFF_HD == 2 * h + m, q, 0.0).astype(BF16)
                s = _dot_nt(qm, k)
                p = jnp.exp2(s - jnp.max(s, axis=-1, keepdims=True))
                ps.append(p)
                cs.append((1.0 if m == 0 else lam) / jnp.sum(p, axis=-1, keepdims=True))
            a = ps[0] * cs[0] - ps[1] * cs[1]
            o = jnp.where(lane // (2 * DIFF_HD) == h, _dot(a.astype(BF16), v), o)
        o_ref[sb * tq:(sb + 1) * tq, :] = _group_rms(o, gmat, gn_ref[...]) * (1.0 - lam_init)


def diff_mixer(q, k, v, lam_p, gn, lam_init, *, n_seq, seq_len, tok0, cache=None, rope=None, layer=0):
    cached = cache is not None
    n_sub = 1 if cached else 2
    n_seq, seq_len = n_seq // n_sub, seq_len * n_sub
    tq = 256 * n_sub
    nq = seq_len // tq
    b0 = tok0 // seq_len
    q0 = tok0 // tq
    in_specs = [pl.BlockSpec((tq, C_W), lambda b, i: (q0 + b * nq + i, 0)),
                pl.BlockSpec((seq_len, C_W), lambda b, i: (b0 + b, 0)),
                pl.BlockSpec((seq_len, C_W), lambda b, i: (b0 + b, 0)),
                pl.BlockSpec((4, DIFF_HD), lambda b, i: (0, 0)),
                pl.BlockSpec((1, C_W), lambda b, i: (0, 0))]
    args = [q, k, v, lam_p, gn]
    if cached:
        cos, sin = rope
        cache_spec = pl.BlockSpec((None, None, PAST_LEN, C_W), lambda b, i: (b, layer, 0, 0))
        in_specs += [cache_spec, cache_spec,
                     pl.BlockSpec((tq, LANES), lambda b, i: (i, 0)),
                     pl.BlockSpec((tq, LANES), lambda b, i: (i, 0)),
                     pl.BlockSpec((seq_len, LANES), lambda b, i: (0, 0)),
                     pl.BlockSpec((seq_len, LANES), lambda b, i: (0, 0))]
        args += [cache[0], cache[1], cos, sin, cos, sin]
    return pl.pallas_call(
        functools.partial(_diff_kernel, cached=cached, lam_init=lam_init, n_sub=n_sub),
        grid=(n_seq, nq),
        in_specs=in_specs,
        out_specs=pl.BlockSpec((tq, C_W), lambda b, i: (b * nq + i, 0)),
        out_shape=jax.ShapeDtypeStruct((n_seq * seq_len, C_W), F32),
        compiler_params=_cparams(("arbitrary", "arbitrary")),
        name=f"diff_mixer_{seq_len}",
    )(*args)


def _outproj_kernel(*refs, routed, n_mix):
    x_ref = refs[0]
    mix_refs = refs[1:1 + 2 * n_mix]
    rest = refs[1 + 2 * n_mix:]
    if routed:
        w_ref, gm_ref, g_ref, sh_ref, sc_ref, r_ref, x1_ref, h_ref, lg_ref = rest
    else:
        w_ref, gm_ref, g_ref, sh_ref, sc_ref, x1_ref, h_ref = rest
    is_ctx = pl.program_id(0) < N_CTX // x_ref.shape[0]
    mixed = jnp.concatenate(
        [jnp.where(is_ctx, mix_refs[2 * m][...], mix_refs[2 * m + 1][...]) for m in range(n_mix)], axis=1)
    x1 = x_ref[...] + gm_ref[...] * _dot(mixed.astype(BF16), w_ref[...])
    x1_ref[...] = x1
    h = _rms(x1, g_ref[...]) * (1.0 + sc_ref[...]) + sh_ref[...]
    h_ref[...] = h.astype(h_ref.dtype)
    if routed:
        lg_ref[...] = _top2_route(_dot_split(h, r_ref[...]))


def out_projection(x, mixes, w_out_b, mod, norm_g, l, router_p=None):
    tm = 512
    n_ctx_tiles = N_CTX // tm
    n_lat_tiles = N_LAT // tm
    routed = router_p is not None
    mod_spec = lambda k: pl.BlockSpec((None, None, 1, D_MODEL), lambda i: (l, _cond_row(i, tm), 0, k))
    in_specs = [pl.BlockSpec((tm, D_MODEL), lambda i: (i, 0))]
    args = [x]
    for ctx_arr, lat_arr in mixes:
        w = ctx_arr.shape[1]
        in_specs.append(pl.BlockSpec((tm, w), lambda i: (jnp.minimum(i, n_ctx_tiles - 1), 0)))
        in_specs.append(pl.BlockSpec((tm, w), lambda i: (jnp.clip(i - n_ctx_tiles, 0, n_lat_tiles - 1), 0)))
        args += [ctx_arr, lat_arr]
    in_specs += [pl.BlockSpec((None, D_MODEL, D_MODEL), lambda i: (l, 0, 0)),
                 mod_spec(2),
                 pl.BlockSpec((None, 1, D_MODEL), lambda i: (l, 0, 0)),
                 mod_spec(3), mod_spec(4)]
    args += [w_out_b, mod, norm_g, mod, mod]
    out_specs = [pl.BlockSpec((tm, D_MODEL), lambda i: (i, 0)),
                 pl.BlockSpec((tm, D_MODEL), lambda i: (i, 0))]
    out_shape = [jax.ShapeDtypeStruct((N_TOK, D_MODEL), F32),
                 jax.ShapeDtypeStruct((N_TOK, D_MODEL), F32 if routed else BF16)]
    if routed:
        in_specs.append(pl.BlockSpec((D_MODEL, LANES), lambda i: (0, 0)))
        args.append(router_p)
        out_specs.append(pl.BlockSpec((tm, LANES), lambda i: (i, 0)))
        out_shape.append(jax.ShapeDtypeStruct((N_TOK, LANES), F32))
    return pl.pallas_call(
        functools.partial(_outproj_kernel, routed=routed, n_mix=len(mixes)),
        grid=(N_TOK // tm,),
        in_specs=in_specs, out_specs=out_specs, out_shape=out_shape,
        compiler_params=_cparams(("arbitrary",)),
        name="out_projection",
    )(*args)


def _outproj_ffn_kernel(*refs, n_mix):
    x_ref = refs[0]
    mix_refs = refs[1:1 + 2 * n_mix]
    (w_ref, gm_ref, g_ref, sh_ref, sc_ref, gf_ref, wg_ref, wu_ref, wd_ref,
     o_ref, x1_s, h_s, acc_ref) = refs[1 + 2 * n_mix:]
    j = pl.program_id(1)

    @pl.when(j == 0)
    def _():
        is_ctx = pl.program_id(0) < N_CTX // x_ref.shape[0]
        mixed = jnp.concatenate(
            [jnp.where(is_ctx, mix_refs[2 * m][...], mix_refs[2 * m + 1][...]) for m in range(n_mix)], axis=1)
        x1 = x_ref[...] + gm_ref[...] * _dot(mixed.astype(BF16), w_ref[...])
        x1_s[...] = x1
        h_s[...] = (_rms(x1, g_ref[...]) * (1.0 + sc_ref[...]) + sh_ref[...]).astype(BF16)
        acc_ref[...] = jnp.zeros_like(acc_ref)

    h = h_s[...]
    act = _silu(_dot(h, wg_ref[...])) * _dot(h, wu_ref[...])
    acc_ref[...] += _dot(act.astype(BF16), wd_ref[...])

    @pl.when(j == pl.num_programs(1) - 1)
    def _():
        o_ref[...] = x1_s[...] + gf_ref[...] * acc_ref[...]


def out_projection_ffn(x, mixes, w_out_b, mod, norm_g, w_gu, w_d, l):
    tm = 512
    lyr = l // 2
    n_chunk = D_FF // FF_CHUNK
    n_ctx_tiles, n_lat_tiles = N_CTX // tm, N_LAT // tm
    mod_spec = lambda k: pl.BlockSpec((None, None, 1, D_MODEL), lambda i, j: (l, _cond_row(i, tm), 0, k))
    gu_spec = lambda half: pl.BlockSpec((None, D_MODEL, FF_CHUNK), lambda i, j: (lyr, 0, half * n_chunk + j))
    in_specs = [pl.BlockSpec((tm, D_MODEL), lambda i, j: (i, 0))]
    args = [x]
    for ctx_arr, lat_arr in mixes:
        w = ctx_arr.shape[1]
        in_specs.append(pl.BlockSpec((tm, w), lambda i, j: (jnp.minimum(i, n_ctx_tiles - 1), 0)))
        in_specs.append(pl.BlockSpec((tm, w), lambda i, j: (jnp.clip(i - n_ctx_tiles, 0, n_lat_tiles - 1), 0)))
        args += [ctx_arr, lat_arr]
    in_specs += [pl.BlockSpec((None, D_MODEL, D_MODEL), lambda i, j: (l, 0, 0)),
                 mod_spec(2),
                 pl.BlockSpec((None, 1, D_MODEL), lambda i, j: (l, 0, 0)),
                 mod_spec(3), mod_spec(4), mod_spec(5),
                 gu_spec(0), gu_spec(1),
                 pl.BlockSpec((None, FF_CHUNK, D_MODEL), lambda i, j: (lyr, j, 0))]
    args += [w_out_b, mod, norm_g, mod, mod, mod, w_gu, w_gu, w_d]
    return pl.pallas_call(
        functools.partial(_outproj_ffn_kernel, n_mix=len(mixes)),
        grid=(N_TOK // tm, n_chunk),
        in_specs=in_specs,
        out_specs=pl.BlockSpec((tm, D_MODEL), lambda i, j: (i, 0)),
        out_shape=jax.ShapeDtypeStruct((N_TOK, D_MODEL), F32),
        scratch_shapes=[pltpu.VMEM((tm, D_MODEL), F32), pltpu.VMEM((tm, D_MODEL), BF16),
                        pltpu.VMEM((tm, D_MODEL), F32)],
        compiler_params=_cparams(("arbitrary", "arbitrary")),
        name="out_projection_ffn",
    )(*args)


MOE_TILE = 256
MOE_SLOTS = TOP_K * N_TOK + N_EXP * MOE_TILE
MOE_TILES = MOE_SLOTS // MOE_TILE
ROUTE_I1, ROUTE_I2, ROUTE_W1, ROUTE_W2 = 0, 1, 2, 3


def _top2_route(logits):
    lane = _lane(logits.shape)
    neg = jnp.float32(-jnp.inf)
    lg = jnp.where(lane < N_EXP, logits, neg)
    t1 = jnp.max(lg, axis=-1, keepdims=True)
    i1 = jnp.min(jnp.where(lg == t1, lane, LANES), axis=-1, keepdims=True)
    lg2 = jnp.where(lane == i1, neg, lg)
    t2 = jnp.max(lg2, axis=-1, keepdims=True)
    i2 = jnp.min(jnp.where(lg2 == t2, lane, LANES), axis=-1, keepdims=True)
    e2 = jnp.exp(t2 - t1)
    w1 = 1.0 / (1.0 + e2)
    w2 = e2 / (1.0 + e2)
    rec = jnp.where(lane == ROUTE_I1, i1.astype(F32), 0.0) + jnp.where(lane == ROUTE_I2, i2.astype(F32), 0.0)
    return rec + jnp.where(lane == ROUTE_W1, w1, 0.0) + jnp.where(lane == ROUTE_W2, w2, 0.0)


def _routing_tables(route):
    n_assign = TOP_K * N_TOK
    expert = jnp.concatenate([route[:, ROUTE_I1], route[:, ROUTE_I2]]).astype(jnp.int32)
    index = jnp.arange(n_assign, dtype=jnp.int32)
    onehot = (expert[:, None] == jnp.arange(N_EXP, dtype=jnp.int32)[None, :]).astype(jnp.int32)
    running = jnp.cumsum(onehot, axis=0)
    counts = running[-1]
    rank = jnp.sum(running * onehot, axis=1) - 1
    padded = (counts + MOE_TILE - 1) // MOE_TILE * MOE_TILE
    ends = jnp.cumsum(padded)
    starts = ends - padded
    slot = jnp.sum(onehot * starts[None, :], axis=1) + rank
    sorted_token = (jnp.sort(expert * n_assign + index) % n_assign) % N_TOK
    tile_start = jnp.arange(MOE_TILES, dtype=jnp.int32) * MOE_TILE
    tile_expert = jnp.minimum(jnp.sum(tile_start[:, None] >= ends[None, :], axis=1), N_EXP - 1).astype(jnp.int32)
    first_sorted = (jnp.cumsum(counts) - counts)[tile_expert] + tile_start - starts[tile_expert]
    n_used = (ends[-1] // MOE_TILE).astype(jnp.int32).reshape(1)
    return sorted_token, slot, tile_expert, first_sorted.astype(jnp.int32), n_used


def _row_copy(src_hbm, row, dst, dst_row, sem):
    return pltpu.make_async_copy(src_hbm.at[pl.ds(row, 1)], dst.at[pl.ds(dst_row, 1)], sem)


def _experts_kernel(tok_ref, exp_ref, first_ref, used_ref, h_hbm, wg_ref, wu_ref, wd_ref, y_ref, x_buf, sems,
                    wg_s, wu_s, wd_s):
    t = pl.program_id(0)
    n_used = used_ref[0]

    def start_gather(tile, half):
        first = first_ref[tile]
        for r in range(MOE_TILE):
            tok = tok_ref[jnp.minimum(first + r, TOP_K * N_TOK - 1)]
            _row_copy(h_hbm, tok, x_buf.at[half], r, sems.at[half]).start(priority=r % 2)

    @pl.when(t == 0)
    def _():
        start_gather(0, 0)

    @pl.when(t + 1 < n_used)
    def _():
        start_gather(t + 1, (t + 1) % 2)

    @pl.when(t < n_used)
    def _():
        half = t % 2
        for r in range(MOE_TILE):
            _row_copy(h_hbm, 0, x_buf.at[half], r, sems.at[half]).wait()
        @pl.when((t == 0) | (exp_ref[t] != exp_ref[jnp.maximum(t - 1, 0)]))
        def _():
            wg_s[...] = wg_ref[...].astype(BF16)
            wu_s[...] = wu_ref[...].astype(BF16)
            wd_s[...] = wd_ref[...].astype(BF16)

        x = x_buf[half].astype(BF16)
        act = _silu(_dot(x, wg_s[...])) * _dot(x, wu_s[...])
        y_ref[...] = _dot(act.astype(BF16), wd_s[...])

    @pl.when(t >= n_used)
    def _():
        y_ref[...] = jnp.zeros_like(y_ref)


def moe_experts(h, route, w_gu, w_d, l):
    lyr = l // 2
    sorted_token, slot, tile_expert, first_sorted, n_used = _routing_tables(route)
    gu_spec = lambda half: pl.BlockSpec((None, None, D_MODEL, FF_CHUNK),
                                        lambda t, tok, exp, first, used: (lyr, exp[t], 0, half))
    y = pl.pallas_call(
        _experts_kernel,
        grid_spec=pltpu.PrefetchScalarGridSpec(
            num_scalar_prefetch=4,
            grid=(MOE_TILES,),
            in_specs=[pl.BlockSpec(memory_space=pl.ANY),
                      gu_spec(0), gu_spec(1),
                      pl.BlockSpec((None, None, FF_CHUNK, D_MODEL),
                                   lambda t, tok, exp, first, used: (lyr, exp[t], 0, 0))],
            out_specs=pl.BlockSpec((MOE_TILE, D_MODEL), lambda t, tok, exp, first, used: (t, 0)),
            scratch_shapes=[pltpu.VMEM((2, MOE_TILE, D_MODEL), F32), pltpu.SemaphoreType.DMA((2,)),
                            pltpu.VMEM((D_MODEL, FF_CHUNK), BF16), pltpu.VMEM((D_MODEL, FF_CHUNK), BF16),
                            pltpu.VMEM((FF_CHUNK, D_MODEL), BF16)]),
        out_shape=jax.ShapeDtypeStruct((MOE_SLOTS, D_MODEL), F32),
        compiler_params=_cparams(("arbitrary",)),
        name="moe_experts",
    )(sorted_token, tile_expert, first_sorted, n_used, h, w_gu, w_gu, w_d)
    return y, slot


def _moe_combine_kernel(slot_ref, y_hbm, x_ref, gf_ref, route_ref, o_ref, y_buf, sems):
    i = pl.program_id(0)
    tm = x_ref.shape[0]

    def start_gather(tile, half):
        for k in range(TOP_K):
            for r in range(tm):
                _row_copy(y_hbm, slot_ref[k * N_TOK + tile * tm + r], y_buf.at[half, k], r,
                          sems.at[half]).start(priority=r % 2)

    @pl.when(i == 0)
    def _():
        start_gather(0, 0)

    @pl.when(i + 1 < pl.num_programs(0))
    def _():
        start_gather(i + 1, (i + 1) % 2)

    half = i % 2
    for k in range(TOP_K):
        for r in range(tm):
            _row_copy(y_hbm, 0, y_buf.at[half, k], r, sems.at[half]).wait()
    route = route_ref[...]
    f = (route[:, ROUTE_W1:ROUTE_W1 + 1] * y_buf[half, 0] + route[:, ROUTE_W2:ROUTE_W2 + 1] * y_buf[half, 1])
    o_ref[...] = x_ref[...] + gf_ref[...] * f


def moe_combine(y, slot, x1, mod, route, l):
    tm = 512
    return pl.pallas_call(
        _moe_combine_kernel,
        grid_spec=pltpu.PrefetchScalarGridSpec(
            num_scalar_prefetch=1,
            grid=(N_TOK // tm,),
            in_specs=[pl.BlockSpec(memory_space=pl.ANY),
                      pl.BlockSpec((tm, D_MODEL), lambda i, s: (i, 0)),
                      pl.BlockSpec((None, None, 1, D_MODEL), lambda i, s: (l, _cond_row(i, tm), 0, 5)),
                      pl.BlockSpec((tm, LANES), lambda i, s: (i, 0))],
            out_specs=pl.BlockSpec((tm, D_MODEL), lambda i, s: (i, 0)),
            scratch_shapes=[pltpu.VMEM((2, TOP_K, tm, D_MODEL), F32), pltpu.SemaphoreType.DMA((2,))]),
        out_shape=jax.ShapeDtypeStruct((N_TOK, D_MODEL), F32),
        compiler_params=_cparams(("arbitrary",)),
        name="moe_combine",
    )(slot, y, x1, mod, route)


def _final_norm_kernel(x_ref, g_ref, o_ref):
    o_ref[...] = _rms(x_ref[...], g_ref[...])


def final_norm_call(x, g, tok0, n_tok):
    tm = 1024
    i0 = tok0 // tm
    return pl.pallas_call(
        _final_norm_kernel,
        grid=(n_tok // tm,),
        in_specs=[pl.BlockSpec((tm, D_MODEL), lambda i: (i0 + i, 0)),
                  pl.BlockSpec((1, D_MODEL), lambda i: (0, 0))],
        out_specs=pl.BlockSpec((tm, D_MODEL), lambda i: (i, 0)),
        out_shape=jax.ShapeDtypeStruct((n_tok, D_MODEL), F32),
        compiler_params=_cparams(("arbitrary",)),
        name="final_norm",
    )(x, g)


def _cache_store_kernel(*refs):
    n = len(refs) // 3
    for src, dst in zip(refs[:n], refs[2 * n:]):
        dst[...] = src[...].reshape(dst.shape)


def cache_store(sources, all_layers, l):
    n_sq = 4
    tm = n_sq * SEQ
    return pl.pallas_call(
        _cache_store_kernel,
        grid=(N_CTX // tm,),
        in_specs=([pl.BlockSpec((tm, s.shape[1]), lambda i: (i, 0)) for s in sources]
                  + [pl.BlockSpec(memory_space=pl.ANY)] * len(all_layers)),
        out_specs=[pl.BlockSpec((n_sq, None, SEQ, a.shape[-1]), lambda i: (i, l, 0, 0)) for a in all_layers],
        out_shape=[jax.ShapeDtypeStruct(a.shape, F32) for a in all_layers],
        input_output_aliases={len(sources) + k: k for k in range(len(all_layers))},
        compiler_params=_cparams(("arbitrary",)),
        name="cache_store",
    )(*sources, *all_layers)


def _gate_row(p):
    return jnp.zeros((1, LANES), F32).at[0, GATE_A:GATE_A + 2 * H_A].set(p.reshape(-1))


def _tile_lanes(g, width):
    return jnp.tile(g, width // g.shape[0])[None, :]


def kernel(x_prompt, x_sample, state_delta, cache_gqa_k, cache_gqa_v, cache_diff_k, cache_diff_v, c, c_ctx,
           ada_w, ada_b, norm_mix, norm_ffn, w_in, conv_w, dn_a_log, dn_dt_bias, dn_norm, gqa_q_norm,
           gqa_k_norm, diff_lambda, diff_norm, w_out, ffn_w_gu, ffn_w_down, moe_router, moe_w_gu, moe_w_down,
           final_norm):
    cond8 = jnp.concatenate([c_ctx[None, :], c, jnp.zeros((8 - N_COND, D_MODEL), F32)], axis=0)
    w_out_b = w_out.astype(BF16)
    norm_mix3 = norm_mix.reshape(DEPTH, 1, D_MODEL)
    norm_ffn3 = norm_ffn.reshape(DEPTH, 1, D_MODEL)
    ffn_gu_b, ffn_d_b = ffn_w_gu.astype(BF16), ffn_w_down.astype(BF16)
    router_p = jnp.pad(moe_router, ((0, 0), (0, 0), (0, LANES - N_EXP)))
    rope_b = _rope_tables(DEC_SEQ // GRID_W, HEAD_DIM)
    rope_c = _rope_tables(DEC_SEQ // GRID_W, DIFF_HD)
    s0_ctx = jnp.zeros((BATCH, 2, H_A, HEAD_DIM, HEAD_DIM), F32)
    cache_b = (cache_gqa_k.reshape(DEC_BATCH, DEPTH, PAST_LEN, KV_B * HEAD_DIM),
               cache_gqa_v.reshape(DEC_BATCH, DEPTH, PAST_LEN, KV_B * HEAD_DIM))
    cache_c = (cache_diff_k.reshape(DEC_BATCH, DEPTH, PAST_LEN, C_W),
               cache_diff_v.reshape(DEC_BATCH, DEPTH, PAST_LEN, C_W))

    mod = ada_modulation(cond8, ada_w, ada_b)[:, :N_COND].reshape(DEPTH, N_COND, 1, 6 * D_MODEL)

    x = jnp.concatenate([x_prompt.reshape(N_CTX, D_MODEL), x_sample.reshape(N_LAT, D_MODEL)], axis=0)
    all_states = jnp.zeros((BATCH, DEPTH, 2, H_A, HEAD_DIM, HEAD_DIM), F32)
    ctx_layers = tuple(jnp.zeros((BATCH, DEPTH, SEQ, w), F32)
                       for w in (KV_B * HEAD_DIM, KV_B * HEAD_DIM, C_W, C_W))
    ctx = dict(n_seq=BATCH, seq_len=SEQ, tok0=0)
    lat = dict(n_seq=DEC_BATCH, seq_len=DEC_SEQ, tok0=N_CTX)
    for l in range(DEPTH):
        qkv_a, z_a, q_b, k_b, v_b, q_c, k_c, v_c, gate = in_projection(x, mod, norm_mix3, w_in, l)

        dn_args = (qkv_a, z_a, gate, conv_w[l], _gate_row(dn_a_log[l]), _gate_row(dn_dt_bias[l]),
                   _tile_lanes(dn_norm[l], LANES))
        oa_ctx, all_states = delta_mixer(*dn_args, s0_ctx, n_sub=2, all_states=all_states, layer=l, **ctx)
        oa_lat, _ = delta_mixer(*dn_args, state_delta[:, l], n_sub=1, **lat)

        gq, gk = _tile_lanes(gqa_q_norm[l], LANES), _tile_lanes(gqa_k_norm[l], LANES)
        ob_ctx, kn_ctx = gqa_mixer(q_b, k_b, v_b, gq, gk, **ctx)
        (ob_lat,) = gqa_mixer(q_b, k_b, v_b, gq, gk, cache=cache_b, rope=rope_b, layer=l, **lat)

        lam_init = 0.8 - 0.6 * math.exp(-0.3 * l)
        gn = _tile_lanes(diff_norm[l], C_W)
        oc_ctx = diff_mixer(q_c, k_c, v_c, diff_lambda[l], gn, lam_init, **ctx)
        oc_lat = diff_mixer(q_c, k_c, v_c, diff_lambda[l], gn, lam_init, cache=cache_c, rope=rope_c, layer=l, **lat)

        mixes = ((oa_ctx, oa_lat), (ob_ctx, ob_lat), (oc_ctx, oc_lat))
        if l % 2 == 0:
            x = out_projection_ffn(x, mixes, w_out_b, mod, norm_ffn3, ffn_gu_b, ffn_d_b, l)
        else:
            x1, h2, route = out_projection(x, mixes, w_out_b, mod, norm_ffn3, l, router_p[l // 2])
            y_slots, slot = moe_experts(h2, route, moe_w_gu, moe_w_down, l)
            x = moe_combine(y_slots, slot, x1, mod, route, l)

        ctx_layers = cache_store((kn_ctx, v_b, k_c, v_c), ctx_layers, l)

    y_prompt = final_norm_call(x, final_norm[None, :], 0, N_CTX).reshape(BATCH, SEQ, D_MODEL)
    y_sample = final_norm_call(x, final_norm[None, :], N_CTX, N_LAT).reshape(DEC_BATCH, DEC_SEQ, D_MODEL)
    new_gk, new_gv, new_dk, new_dv = ctx_layers
    return (y_prompt, y_sample, all_states,
            new_gk.reshape(BATCH, DEPTH, SEQ, KV_B, HEAD_DIM), new_gv.reshape(BATCH, DEPTH, SEQ, KV_B, HEAD_DIM),
            new_dk.reshape(BATCH, DEPTH, SEQ, H_C, 2, DIFF_HD), new_dv.reshape(BATCH, DEPTH, SEQ, H_C, 2 * DIFF_HD))
```
